```python
import math
import jax
import jax.numpy as jnp
from jax import lax
import numpy as np

D_MODEL = 1024
BATCH = 16
SEQ = 2048
DEPTH = 4
DEC_BATCH = 8
DEC_SEQ = 16
PAST_LEN = 1024

CHUNK = 64
N_META = 16
N_MIXERS = 3
N_A = (DEPTH + N_MIXERS - 1) // N_MIXERS
N_B = (DEPTH + N_MIXERS - 2) // N_MIXERS
N_C = DEPTH // N_MIXERS
N_HEADS = 16
N_KV_HEADS = 4
HEAD_DIM = D_MODEL // N_HEADS
GROUP = N_HEADS // N_KV_HEADS
IDX_HEADS = 8
IDX_DIM = 64
TOP_K_MAX = 256
Q_BLOCK = 128
NEG_INF = -1e30
N_BUCKETS = 32
MAX_DISTANCE = 128
D_RNN = D_MODEL
LRU_BLOCKS = 4
LRU_BLOCK = D_RNN // LRU_BLOCKS
CONV_B = 4
LRU_C = 8.0
CONV_C = 3
D_FF = -(-8 * D_MODEL // (3 * 256)) * 256
RMS_EPS = 1e-6

kernel_name = "hybrid_dsa_rglru_shortconv_stream_step"


def rmsnorm(x, g):
    xf = x.astype(jnp.float32)
    y = xf * lax.rsqrt(jnp.mean(xf * xf, axis=-1, keepdims=True) + RMS_EPS)
    return (y * g.astype(jnp.float32)).astype(x.dtype)


def swiglu(x, w_gate, w_up, w_down):
    return (jax.nn.silu(x @ w_gate) * (x @ w_up)) @ w_down


def rel_bucket(rel):
    nb = N_BUCKETS // 2
    max_exact = nb // 2
    ret = jnp.where(rel > 0, nb, 0)
    n = jnp.abs(rel)
    nf = jnp.maximum(n, 1).astype(jnp.float32)
    large = max_exact + (jnp.log(nf / max_exact) / math.log(MAX_DISTANCE / max_exact)
                         * (nb - max_exact)).astype(jnp.int32)
    large = jnp.minimum(large, nb - 1)
    return ret + jnp.where(n < max_exact, n, large)


def causal_dwconv(x, buf, w):
    xe = jnp.concatenate([buf, x], axis=1)
    y = lax.conv_general_dilated(xe, w[:, None, :], window_strides=(1,), padding='VALID',
                                 dimension_numbers=('NWC', 'WIO', 'NWC'),
                                 feature_group_count=x.shape[-1])
    return y, xe[:, xe.shape[1] - (w.shape[0] - 1):]


def attend_block(qb, qib, wib, qp, qc, ks, vs, kis, k_pos, k_chunk, rel_table, n_sel):
    nq = qb.shape[0]
    s_idx = jnp.einsum('qhd,sd->qhs', qib, kis).astype(jnp.float32) * IDX_DIM ** -0.5
    score = jnp.einsum('qh,qhs->qs', wib.astype(jnp.float32), jax.nn.relu(s_idx))
    adm = k_chunk[None, :] <= qc[:, None]
    score = jnp.where(adm, score, -jnp.inf)
    _, idx = lax.top_k(score, n_sel)
    ok = jnp.take_along_axis(adm, idx, axis=1)
    kg = ks[idx]
    vg = vs[idx]
    qg = qb.reshape(nq, N_KV_HEADS, GROUP, HEAD_DIM)
    logits = jnp.einsum('qgrd,qkgd->qgrk', qg, kg).astype(jnp.float32) * HEAD_DIM ** -0.5
    bias = rel_table[rel_bucket(k_pos[idx] - qp[:, None])]
    logits = logits + bias.reshape(nq, n_sel, N_KV_HEADS, GROUP).transpose(0, 2, 3, 1).astype(jnp.float32)
    logits = jnp.where(ok[:, None, None, :], logits, NEG_INF)
    p = jax.nn.softmax(logits, axis=-1).astype(vg.dtype)
    return jnp.einsum('qgrk,qkgd->qgrd', p, vg).reshape(nq, N_HEADS * HEAD_DIM)


def mixer_a(xn, past_k, past_v, past_ki, q_pos, q_chunk, k_pos, k_chunk, n_sel,
            w_q, w_k, w_v, w_o, w_iq, w_ik, w_iw, rel_table):
    bn, t, _ = xn.shape
    q = (xn @ w_q).reshape(bn, t, N_HEADS, HEAD_DIM)
    k_new = (xn @ w_k).reshape(bn, t, N_KV_HEADS, HEAD_DIM)
    v_new = (xn @ w_v).reshape(bn, t, N_KV_HEADS, HEAD_DIM)
    qi = (xn @ w_iq).reshape(bn, t, IDX_HEADS, IDX_DIM)
    ki_new = xn @ w_ik
    wi = (xn @ w_iw) * IDX_HEADS ** -0.5
    if past_k is None:
        k_all, v_all, ki_all = k_new, v_new, ki_new
    else:
        k_all = jnp.concatenate([past_k, k_new], axis=1)
        v_all = jnp.concatenate([past_v, v_new], axis=1)
        ki_all = jnp.concatenate([past_ki, ki_new], axis=1)
    qblk = min(Q_BLOCK, t)
    nb = -(-t // qblk)
    pad = nb * qblk - t

    def to_blocks(a):
        a = jnp.pad(a, [(0, 0), (0, pad)] + [(0, 0)] * (a.ndim - 2))
        return a.reshape((bn, nb, qblk) + a.shape[2:])

    qpos_b = jnp.pad(q_pos, (0, pad), mode='edge').reshape(nb, qblk)
    qch_b = jnp.pad(q_chunk, (0, pad), mode='edge').reshape(nb, qblk)

    def per_seq(args):
        qs, qis, wis, ks, vs, kis = args
        return lax.map(lambda blk: attend_block(blk[0], blk[1], blk[2], blk[3], blk[4], ks, vs, kis,
                                                k_pos, k_chunk, rel_table, n_sel),
                       (qs, qis, wis, qpos_b, qch_b))

    o = lax.map(per_seq, (to_blocks(q), to_blocks(qi), to_blocks(wi), k_all, v_all, ki_all))
    o = o.reshape(bn, nb * qblk, N_HEADS * HEAD_DIM)[:, :t]
    return o @ w_o, k_new, v_new, ki_new


def lru_combine(left, right):
    a1, b1 = left
    a2, b2 = right
    return a1 * a2, a2 * b1 + b2


def mixer_b(xn, conv_buf, h0, w_y, b_y, w_x, b_x, conv_w, conv_b, w_r, b_r, w_i, b_i, lam, w_out, b_out):
    bn, t, _ = xn.shape
    y_branch = jax.nn.gelu(xn @ w_y + b_y)
    xb = xn @ w_x + b_x
    xc, new_buf = causal_dwconv(xb, conv_buf, conv_w)
    xc = xc + conv_b
    xblk = xc.reshape(bn, t, LRU_BLOCKS, LRU_BLOCK)
    r = jax.nn.sigmoid(jnp.einsum('btni,nij->btnj', xblk, w_r).reshape(bn, t, D_RNN) + b_r)
    ig = jax.nn.sigmoid(jnp.einsum('btni,nij->btnj', xblk, w_i).reshape(bn, t, D_RNN) + b_i)
    log_a = -LRU_C * r.astype(jnp.float32) * jax.nn.softplus(-lam.astype(jnp.float32))
    a = jnp.exp(log_a)
    u = jnp.sqrt(-jnp.expm1(2.0 * log_a)) * (ig * xc).astype(jnp.float32)
    u = u.at[:, 0].add(a[:, 0] * h0.astype(jnp.float32))
    _, h = lax.associative_scan(lru_combine, (a, u), axis=1)
    out = (h.astype(xn.dtype) * y_branch) @ w_out + b_out
    return out, h[:, -1].astype(xn.dtype), new_buf


def mixer_c(xn, conv_buf, w_in, conv_w, w_out):
    bg, cg, xx = jnp.split(xn @ w_in, 3, axis=-1)
    conv, new_buf = causal_dwconv(cg * xx, conv_buf, conv_w)
    return (bg * conv) @ w_out, new_buf


def trunk(x, past, q_pos, q_chunk, k_pos, k_chunk, n_sel, weights):
    (rel_bias, norm_mix, norm_ffn, norm_final,
     a_wq, a_wk, a_wv, a_wo, a_wiq, a_wik, a_wiw,
     b_wy, b_by, b_wx, b_bx, b_conv_w, b_conv_b, b_wr, b_br, b_wi, b_bi, b_lam, b_wo, b_bo,
     c_win, c_conv_w, c_wo, f_wg, f_wu, f_wd) = weights
    past_k, past_v, past_ki, past_h, past_cb, past_cc = past
    bn = x.shape[0]
    new_k, new_v, new_ki, new_h, new_cb, new_cc = [], [], [], [], [], []
    for l in range(DEPTH):
        s = l // N_MIXERS
        xn = rmsnorm(x, norm_mix[l])
        if l % N_MIXERS == 0:
            pk = None if past_k is None else past_k[s]
            pv = None if past_v is None else past_v[s]
            pki = None if past_ki is None else past_ki[s]
            out, kn, vn, kin = mixer_a(xn, pk, pv, pki, q_pos, q_chunk, k_pos, k_chunk, n_sel,
                                       a_wq[s], a_wk[s], a_wv[s], a_wo[s], a_wiq[s], a_wik[s], a_wiw[s],
                                       rel_bias)
            new_k.append(kn)
            new_v.append(vn)
            new_ki.append(kin)
        elif l % N_MIXERS == 1:
            h0 = jnp.zeros((bn, D_RNN), x.dtype) if past_h is None else past_h[s]
            cb = jnp.zeros((bn, CONV_B - 1, D_RNN), x.dtype) if past_cb is None else past_cb[s]
            out, hn, cbn = mixer_b(xn, cb, h0, b_wy[s], b_by[s], b_wx[s], b_bx[s], b_conv_w[s], b_conv_b[s],
                                   b_wr[s], b_br[s], b_wi[s], b_bi[s], b_lam[s], b_wo[s], b_bo[s])
            new_h.append(hn)
            new_cb.append(cbn)
        else:
            cc = jnp.zeros((bn, CONV_C - 1, D_MODEL), x.dtype) if past_cc is None else past_cc[s]
            out, ccn = mixer_c(xn, cc, c_win[s], c_conv_w[s], c_wo[s])
            new_cc.append(ccn)
        x = x + out
        x = x + swiglu(rmsnorm(x, norm_ffn[l]), f_wg[l], f_wu[l], f_wd[l])
    return (rmsnorm(x, norm_final), jnp.stack(new_k), jnp.stack(new_v), jnp.stack(new_ki),
            jnp.stack(new_h), jnp.stack(new_cb), jnp.stack(new_cc))


def setup_inputs(seed: int = 0) -> dict:
    key = jax.random.key(seed)
    ks = iter(jax.random.split(key, 48))

    def nrm(shape, scale=1.0):
        return jax.random.normal(next(ks), shape, jnp.float32) * scale

    def gain(shape):
        return 1.0 + nrm(shape, 0.02)

    d = D_MODEL
    u = jax.random.uniform(next(ks), (N_B, D_RNN), jnp.float32, minval=0.9, maxval=0.999)
    sa = u ** (1.0 / LRU_C)
    lam = jnp.log(sa) - jnp.log1p(-sa)
    return {
        "x_prompt": nrm((BATCH, SEQ, d)),
        "x_sample": nrm((DEC_BATCH, DEC_SEQ, d)),
        "cache_k": nrm((N_A, DEC_BATCH, PAST_LEN, N_KV_HEADS, HEAD_DIM)),
        "cache_v": nrm((N_A, DEC_BATCH, PAST_LEN, N_KV_HEADS, HEAD_DIM)),
        "cache_kidx": nrm((N_A, DEC_BATCH, PAST_LEN, IDX_DIM)),
        "state_h": nrm((N_B, DEC_BATCH, D_RNN), 0.5),
        "state_conv_b": nrm((N_B, DEC_BATCH, CONV_B - 1, D_RNN)),
        "state_conv_c": nrm((N_C, DEC_BATCH, CONV_C - 1, d)),
        "meta_tokens": nrm((N_META, d)),
        "rel_bias": nrm((N_BUCKETS, N_HEADS), 0.5),
        "norm_mix": gain((DEPTH, d)),
        "norm_ffn": gain((DEPTH, d)),
        "norm_final": gain((d,)),
        "a_wq": nrm((N_A, d, N_HEADS * HEAD_DIM), d ** -0.5),
        "a_wk": nrm((N_A, d, N_KV_HEADS * HEAD_DIM), d ** -0.5),
        "a_wv": nrm((N_A, d, N_KV_HEADS * HEAD_DIM), d ** -0.5),
        "a_wo": nrm((N_A, N_HEADS * HEAD_DIM, d), (N_HEADS * HEAD_DIM) ** -0.5),
        "a_wiq": nrm((N_A, d, IDX_HEADS * IDX_DIM), d ** -0.5),
        "a_wik": nrm((N_A, d, IDX_DIM), d ** -0.5),
        "a_wiw": nrm((N_A, d, IDX_HEADS), d ** -0.5),
        "b_wy": nrm((N_B, d, D_RNN), d ** -0.5),
        "b_by": nrm((N_B, D_RNN), 0.02),
        "b_wx": nrm((N_B, d, D_RNN), d ** -0.5),
        "b_bx": nrm((N_B, D_RNN), 0.02),
        "b_conv_w": nrm((N_B, CONV_B, D_RNN), CONV_B ** -0.5),
        "b_conv_b": nrm((N_B, D_RNN), 0.02),
        "b_wr": nrm((N_B, LRU_BLOCKS, LRU_BLOCK, LRU_BLOCK), LRU_BLOCK ** -0.5),
        "b_br": nrm((N_B, D_RNN), 0.02),
        "b_wi": nrm((N_B, LRU_BLOCKS, LRU_BLOCK, LRU_BLOCK), LRU_BLOCK ** -0.5),
        "b_bi": nrm((N_B, D_RNN), 0.02),
        "b_lam": lam,
        "b_wo": nrm((N_B, D_RNN, d), D_RNN ** -0.5),
        "b_bo": nrm((N_B, d), 0.02),
        "c_win": nrm((N_C, d, 3 * d), d ** -0.5),
        "c_conv_w": nrm((N_C, CONV_C, d), CONV_C ** -0.5),
        "c_wo": nrm((N_C, d, d), d ** -0.5),
        "f_wg": nrm((DEPTH, d, D_FF), d ** -0.5),
        "f_wu": nrm((DEPTH, d, D_FF), d ** -0.5),
        "f_wd": nrm((DEPTH, D_FF, d), D_FF ** -0.5),
    }


def reference(x_prompt, x_sample, cache_k, cache_v, cache_kidx, state_h, state_conv_b, state_conv_c,
              meta_tokens, rel_bias, norm_mix, norm_ffn, norm_final,
              a_wq, a_wk, a_wv, a_wo, a_wiq, a_wik, a_wiw,
              b_wy, b_by, b_wx, b_bx, b_conv_w, b_conv_b, b_wr, b_br, b_wi, b_bi, b_lam, b_wo, b_bo,
              c_win, c_conv_w, c_wo, f_wg, f_wu, f_wd):
    weights = (rel_bias, norm_mix, norm_ffn, norm_final,
               a_wq, a_wk, a_wv, a_wo, a_wiq, a_wik, a_wiw,
               b_wy, b_by, b_wx, b_bx, b_conv_w, b_conv_b, b_wr, b_br, b_wi, b_bi, b_lam, b_wo, b_bo,
               c_win, c_conv_w, c_wo, f_wg, f_wu, f_wd)
    bp, seq, _ = x_prompt.shape
    pos_p = jnp.arange(N_META + seq, dtype=jnp.int32)
    chunk_p = jnp.where(pos_p < N_META, -1, (pos_p - N_META) // CHUNK)
    meta = jnp.broadcast_to(meta_tokens.astype(x_prompt.dtype)[None], (bp, N_META, D_MODEL))
    xp = jnp.concatenate([meta, x_prompt], axis=1)
    yp, k_p, v_p, ki_p, h_p, cb_p, cc_p = trunk(xp, (None, None, None, None, None, None),
                                                pos_p, chunk_p, pos_p, chunk_p,
                                                min(TOP_K_MAX, seq // 4), weights)
    y_prompt = yp[:, N_META:]
    past_len = cache_k.shape[2]
    dec_seq = x_sample.shape[1]
    k_pos_s = jnp.arange(past_len + dec_seq, dtype=jnp.int32)
    k_chunk_s = k_pos_s // CHUNK
    q_pos_s = past_len + jnp.arange(dec_seq, dtype=jnp.int32)
    q_chunk_s = q_pos_s // CHUNK
    y_sample, k_s, v_s, ki_s, h_s, cb_s, cc_s = trunk(
        x_sample, (cache_k, cache_v, cache_kidx, state_h, state_conv_b, state_conv_c),
        q_pos_s, q_chunk_s, k_pos_s, k_chunk_s, min(TOP_K_MAX, (past_len + dec_seq) // 4), weights)
    return (y_prompt, y_sample, k_p, v_p, ki_p, h_p, cb_p, cc_p, k_s, v_s, ki_s, h_s, cb_s, cc_s)
```

```python
import functools

import jax
import jax.numpy as jnp
from jax import lax
from jax.experimental import pallas as pl
from jax.experimental.pallas import tpu as pltpu

F32 = jnp.float32
BF16 = jnp.bfloat16
I32 = jnp.int32

CHUNK = 64
N_MIXERS = 3
N_HEADS = 16
N_KV_HEADS = 4
GROUP = N_HEADS // N_KV_HEADS
HEAD_DIM = 64
IDX_HEADS = 8
IDX_DIM = 64
TOP_K_MAX = 256
NEG_INF = -1e30
N_BUCKETS = 32
LRU_BLOCKS = 4
CONV_B = 4
CONV_C = 3
LRU_C = 8.0
RMS_EPS = 1e-6
BUCKET_STEPS = (12, 16, 23, 32, 46, 64, 91)
FAR_BUCKET = N_BUCKETS // 2 - 1

LANES = 128
SUBLANES = 8
ROW_PAD = 768
TM_PROJ = 256
TM_FFN = 384
TM_SEQ = 256
QB = 128
INT_MIN = -2 ** 31
VMEM_LIMIT = 56 * 1024 * 1024


def _cparams(sem):
    return pltpu.CompilerParams(dimension_semantics=sem, vmem_limit_bytes=VMEM_LIMIT)


def _const_spec(shape):
    nd = len(shape)
    return pl.BlockSpec(shape, lambda *_: (0,) * nd, pipeline_mode=pl.Buffered(1))


def _rmsnorm(x, g):
    ms = jnp.mean(x * x, axis=-1, keepdims=True)
    return x * lax.rsqrt(ms + RMS_EPS) * g


def _dot(a, b):
    return jnp.dot(a, b, preferred_element_type=F32)


def _dot_nt(a, b):
    return lax.dot_general(a, b, (((1,), (1,)), ((), ())), preferred_element_type=F32)


def _ffn_kernel(x_ref, pre_ref, wo_ref, bo_ref, gf_ref, wg_ref, wu_ref, wd_ref, gfin_ref, out_ref, *,
                n_chunks, fc, final_norm):
    x1 = x_ref[...] + _dot(pre_ref[...], wo_ref[...]) + bo_ref[...]
    xn = _rmsnorm(x1, gf_ref[...]).astype(BF16)
    acc = x1
    for c in range(n_chunks):
        gt = _dot(xn, wg_ref[:, c * fc:(c + 1) * fc])
        up = _dot(xn, wu_ref[:, c * fc:(c + 1) * fc])
        hm = (gt * jax.nn.sigmoid(gt) * up).astype(BF16)
        acc = acc + _dot(hm, wd_ref[c * fc:(c + 1) * fc, :])
    if final_norm:
        acc = _rmsnorm(acc, gfin_ref[...])
    out_ref[...] = acc


def _ffn_call(x, pre, wo, bo, gf, wg, wu, wd, gfin, final_norm):
    n, d = x.shape
    dff = wg.shape[1]
    n_chunks = 2 if dff % (2 * LANES) == 0 else 1
    fc = dff // n_chunks
    row = lambda i: (i, 0)
    return pl.pallas_call(
        functools.partial(_ffn_kernel, n_chunks=n_chunks, fc=fc, final_norm=final_norm),
        grid=(n // TM_FFN,),
        in_specs=[pl.BlockSpec((TM_FFN, d), row), pl.BlockSpec((TM_FFN, pre.shape[1]), row),
                  _const_spec(wo.shape), _const_spec(bo.shape), _const_spec(gf.shape),
                  _const_spec(wg.shape), _const_spec(wu.shape), _const_spec(wd.shape), _const_spec(gfin.shape)],
        out_specs=pl.BlockSpec((TM_FFN, d), row),
        out_shape=jax.ShapeDtypeStruct((n, d), F32),
        compiler_params=_cparams(("parallel",)),
        name="outproj_swiglu",
    )(x, pre, wo, bo, gf, wg, wu, wd, gfin)


def _proj_kernel(x_ref, g_ref, wrow_ref, wq_ref, wt_ref, k_ref, v_ref, ki_ref, q_ref, qit_ref, wit_ref, *,
                 n_kv, n_iq):
    xn = _rmsnorm(x_ref[...], g_ref[...]).astype(BF16)
    row = _dot(xn, wrow_ref[...])
    k_ref[...] = row[:, :n_kv]
    v_ref[...] = row[:, n_kv:2 * n_kv]
    ki_ref[...] = row[:, 2 * n_kv:]
    q_ref[...] = _dot(xn, wq_ref[...]).astype(BF16)
    tt = _dot_nt(wt_ref[...], xn)
    qit_ref[...] = tt[:n_iq].astype(BF16)
    wit_ref[...] = tt[n_iq:n_iq + IDX_HEADS] * (IDX_HEADS ** -0.5)


def _proj_call(x, g, wrow, wq, wt):
    n, d = x.shape
    n_kv = N_KV_HEADS * HEAD_DIM
    n_iq = IDX_HEADS * IDX_DIM
    row = lambda i: (i, 0)
    col = lambda i: (0, i)
    return pl.pallas_call(
        functools.partial(_proj_kernel, n_kv=n_kv, n_iq=n_iq),
        grid=(n // TM_PROJ,),
        in_specs=[pl.BlockSpec((TM_PROJ, d), row), _const_spec(g.shape), _const_spec(wrow.shape),
                  _const_spec(wq.shape), _const_spec(wt.shape)],
        out_specs=[pl.BlockSpec((TM_PROJ, n_kv), row), pl.BlockSpec((TM_PROJ, n_kv), row),
                   pl.BlockSpec((TM_PROJ, IDX_DIM), row), pl.BlockSpec((TM_PROJ, wq.shape[1]), row),
                   pl.BlockSpec((n_iq, TM_PROJ), col), pl.BlockSpec((IDX_HEADS, TM_PROJ), col)],
        out_shape=[jax.ShapeDtypeStruct((n, n_kv), F32), jax.ShapeDtypeStruct((n, n_kv), F32),
                   jax.ShapeDtypeStruct((n, IDX_DIM), F32), jax.ShapeDtypeStruct((n, wq.shape[1]), BF16),
                   jax.ShapeDtypeStruct((n_iq, n), BF16), jax.ShapeDtypeStruct((IDX_HEADS, n), F32)],
        compiler_params=_cparams(("parallel",)),
        name="attn_proj",
    )(x, g, wrow, wq, wt)


def _bias_kernel(tab_ref, tb_ref):
    d = pl.program_id(0)
    h = pl.program_id(1)
    qi = lax.broadcasted_iota(I32, (QB, QB), 0)
    kj = lax.broadcasted_iota(I32, (QB, QB), 1)
    rel = (d - 1) * QB + kj - qi
    n = jnp.abs(rel)
    large = jnp.full((QB, QB), N_BUCKETS // 4, I32)
    for s in BUCKET_STEPS:
        large = large + jnp.where(n >= s, 1, 0)
    bucket = jnp.where(rel > 0, N_BUCKETS // 2, 0) + jnp.where(n < N_BUCKETS // 4, n, large)
    val = jnp.zeros((QB, QB), F32)
    for b in range(N_BUCKETS):
        val = jnp.where(bucket == b, tab_ref[b, h], val)
    tb_ref[0, 0] = val


def _bias_call(table):
    return pl.pallas_call(
        _bias_kernel,
        grid=(2, N_HEADS),
        in_specs=[pl.BlockSpec(memory_space=pltpu.SMEM)],
        out_specs=pl.BlockSpec((1, 1, QB, QB), lambda d, h: (d, h, 0, 0)),
        out_shape=jax.ShapeDtypeStruct((2, N_HEADS, QB, QB), F32),
        compiler_params=_cparams(("arbitrary", "arbitrary")),
        name="rel_bias_tiles",
    )(table)


def _attn_kernel(kend_ref, q_ref, qit_ref, wit_ref, kf_ref, vf_ref, kif_ref, tb_ref, tab_ref, o_ref,
                 key_scr, mask_scr, ls_scr, *, diag_off, kstart, k_sel, has_prev, row_bits):
    diag = pl.program_id(1) + diag_off
    nt = diag + 1
    kend = kend_ref[0]
    qi_cat = jnp.concatenate([qit_ref[h * IDX_DIM:(h + 1) * IDX_DIM, :] for h in range(IDX_HEADS)], axis=1)
    wi = wit_ref[...]
    row_iota = lax.broadcasted_iota(I32, (QB, QB), 0)

    def tile_start(t):
        return pl.multiple_of(t * QB, QB)

    def score_tile(t, carry):
        r0 = tile_start(t)
        s = _dot(kif_ref[0, pl.ds(r0, QB), :], qi_cat)
        sc = jnp.zeros((QB, QB), F32)
        for h in range(IDX_HEADS):
            sc = sc + wi[h:h + 1, :] * jnp.maximum(s[:, h * QB:(h + 1) * QB], 0.0)
        bits = lax.bitcast_convert_type(sc + 0.0, I32)
        key = jnp.where(bits >= 0, bits, bits ^ 0x7FFFFFFF)
        rows = r0 + row_iota
        adm = (rows >= kstart) & (rows < kend)
        key_scr[pl.ds(r0, QB), :] = jnp.where(adm, key, INT_MIN)
        return carry

    lax.fori_loop(0, nt, score_tile, 0)

    def count(pred):
        def body(t, acc):
            r0 = tile_start(t)
            ind = jnp.where(pred(key_scr[pl.ds(r0, QB), :], r0 + row_iota), 1, 0)
            return acc + ind.reshape(QB // SUBLANES, SUBLANES, QB).sum(axis=0)
        acc = lax.fori_loop(0, nt, body, jnp.zeros((SUBLANES, QB), I32))
        return acc.sum(axis=0, keepdims=True)

    def thr_step(b, thr):
        cand = thr + lax.shift_left(jnp.int32(1), 31 - b)
        c = count(lambda kt, rows: kt >= cand)
        return jnp.where(c >= k_sel, cand, thr)

    thr = lax.fori_loop(0, 32, thr_step, jnp.full((1, QB), INT_MIN, I32))

    c_ge = count(lambda kt, rows: kt >= thr)
    tied = (c_ge > k_sel) & (thr > INT_MIN)
    big = jnp.full((1, QB), 2 ** row_bits, I32)

    def tie_limit():
        need = k_sel - count(lambda kt, rows: kt > thr)

        def lim_step(b, lim):
            cand = lim + lax.shift_left(jnp.int32(1), row_bits - 1 - b)
            c = count(lambda kt, rows: (kt == thr) & (rows < cand))
            return jnp.where(c < need, cand, lim)

        lim = lax.fori_loop(0, row_bits, lim_step, jnp.zeros((1, QB), I32))
        return jnp.where(tied, lim, big)

    rlim = lax.cond(jnp.max(tied.astype(I32)) > 0, tie_limit, lambda: big)

    def mask_tile(t, carry):
        r0 = tile_start(t)
        kt = key_scr[pl.ds(r0, QB), :]
        sel = ((kt > thr) | ((kt == thr) & ((r0 + row_iota) <= rlim))) & (kt > INT_MIN)
        mask_scr[:, pl.ds(r0, QB)] = jnp.where(sel, 1.0, 0.0).astype(F32).T
        return carry

    lax.fori_loop(0, nt, mask_tile, 0)

    for g in range(N_KV_HEADS):
        qg = q_ref[0, 0, g]

        def logits_tile(t, m_acc, bias_of_head):
            r0 = tile_start(t)
            lg = _dot_nt(qg, kf_ref[0, g, pl.ds(r0, QB), :])
            keep = mask_scr[:, pl.ds(r0, QB)] > 0.5
            parts = []
            for r in range(GROUP):
                part = lg[r * QB:(r + 1) * QB] + bias_of_head(g * GROUP + r)
                parts.append(jnp.where(keep, part, NEG_INF))
            lg = jnp.concatenate(parts, axis=0)
            ls_scr[:, pl.ds(r0, QB)] = lg
            return jnp.maximum(m_acc, lg)

        m_acc = jnp.full((GROUP * QB, QB), NEG_INF, F32)
        n_far = diag - 1 if has_prev else diag
        m_acc = lax.fori_loop(0, n_far, lambda t, m: logits_tile(t, m, lambda h: tab_ref[FAR_BUCKET, h]), m_acc)
        if has_prev:
            m_acc = logits_tile(diag - 1, m_acc, lambda h: tb_ref[0, h])
        m_acc = logits_tile(diag, m_acc, lambda h: tb_ref[1, h])
        m = jnp.max(m_acc, axis=1, keepdims=True)

        def pv_tile(t, carry):
            l_acc, o_acc = carry
            r0 = tile_start(t)
            p = jnp.exp(ls_scr[:, pl.ds(r0, QB)] - m)
            o_acc = o_acc + _dot(p.astype(BF16), vf_ref[0, g, pl.ds(r0, QB), :])
            return l_acc + p, o_acc

        l_acc, o_acc = lax.fori_loop(0, nt, pv_tile, (jnp.zeros((GROUP * QB, QB), F32),
                                                      jnp.zeros((GROUP * QB, HEAD_DIM), F32)))
        o_ref[0, 0, g] = (o_acc / jnp.sum(l_acc, axis=1, keepdims=True)).astype(BF16)


def _attn_call(kend, qh, qit, wit, kf, vf, kif, tb, table, *, lane_blk_off, diag_off, kstart, k_sel, name):
    n_seq, n_qb = qh.shape[0], qh.shape[1]
    tkf = kf.shape[2]
    has_prev = diag_off >= 1
    row_bits = max(1, (tkf - 1).bit_length())
    n_iq = IDX_HEADS * IDX_DIM
    lane_blk = lambda s, i: (0, lane_blk_off + s * n_qb + i)
    seq_blk = lambda s, i: (s, 0, 0, 0)
    return pl.pallas_call(
        functools.partial(_attn_kernel, diag_off=diag_off, kstart=kstart, k_sel=k_sel, has_prev=has_prev,
                          row_bits=row_bits),
        grid=(n_seq, n_qb),
        in_specs=[pl.BlockSpec((1, 1, QB), lambda s, i: (i, 0, 0)),
                  pl.BlockSpec((1, 1, N_KV_HEADS, GROUP * QB, HEAD_DIM), lambda s, i: (s, i, 0, 0, 0)),
                  pl.BlockSpec((n_iq, QB), lane_blk), pl.BlockSpec((IDX_HEADS, QB), lane_blk),
                  pl.BlockSpec((1, N_KV_HEADS, tkf, HEAD_DIM), seq_blk),
                  pl.BlockSpec((1, N_KV_HEADS, tkf, HEAD_DIM), seq_blk),
                  pl.BlockSpec((1, tkf, IDX_DIM), lambda s, i: (s, 0, 0)),
                  _const_spec(tb.shape), pl.BlockSpec(memory_space=pltpu.SMEM)],
        out_specs=pl.BlockSpec((1, 1, N_KV_HEADS, GROUP * QB, HEAD_DIM), lambda s, i: (s, i, 0, 0, 0)),
        out_shape=jax.ShapeDtypeStruct(qh.shape, BF16),
        scratch_shapes=[pltpu.VMEM((tkf, QB), I32), pltpu.VMEM((QB, tkf), F32),
                        pltpu.VMEM((GROUP * QB, tkf), F32)],
        compiler_params=_cparams(("arbitrary", "arbitrary")),
        name=name,
    )(kend, qh, qit, wit, kf, vf, kif, tb, table)


def _softplus(x):
    return jnp.maximum(x, 0.0) + jnp.log1p(jnp.exp(-jnp.abs(x)))


def _mixb_kernel(x_ref, h0_ref, buf_ref, g_ref, wy_ref, by_ref, wx_ref, bx_ref, cw_ref, cb_ref, wr_ref, br_ref,
                 wi_ref, bi_ref, lam_ref, prein_ref, pre_ref, hlast_ref, tail_ref, xe_scr, h_scr, *, tm):
    del prein_ref

    @pl.when(pl.program_id(1) == 0)
    def _():
        xe_scr[0:SUBLANES, :] = buf_ref[0]
        h_scr[...] = h0_ref[0]

    xn = _rmsnorm(x_ref[...], g_ref[...]).astype(BF16)
    y = jax.nn.gelu(_dot(xn, wy_ref[...]) + by_ref[...])
    xe_scr[SUBLANES:SUBLANES + tm, :] = _dot(xn, wx_ref[...]) + bx_ref[...]
    xc = cb_ref[...] + cw_ref[0:1, :] * xe_scr[SUBLANES - 3:SUBLANES - 3 + tm, :]
    for j in range(1, CONV_B):
        xc = xc + cw_ref[j:j + 1, :] * xe_scr[SUBLANES - 3 + j:SUBLANES - 3 + j + tm, :]
    xcb = xc.astype(BF16)
    blk = xc.shape[1] // LRU_BLOCKS
    r_pre = jnp.concatenate([_dot(xcb[:, n * blk:(n + 1) * blk], wr_ref[n]) for n in range(LRU_BLOCKS)], axis=1)
    i_pre = jnp.concatenate([_dot(xcb[:, n * blk:(n + 1) * blk], wi_ref[n]) for n in range(LRU_BLOCKS)], axis=1)
    r = jax.nn.sigmoid(r_pre + br_ref[...])
    ig = jax.nn.sigmoid(i_pre + bi_ref[...])
    log_a = -LRU_C * r * _softplus(-lam_ref[...])
    a = jnp.exp(log_a)
    u = jnp.sqrt(jnp.tanh(-log_a) * (1.0 + a * a)) * (ig * xc)
    rows = lax.broadcasted_iota(I32, a.shape, 0)
    s = 1
    while s < tm:
        a_sh = jnp.where(rows >= s, pltpu.roll(a, s, 0), 1.0)
        u_sh = jnp.where(rows >= s, pltpu.roll(u, s, 0), 0.0)
        u = a * u_sh + u
        a = a * a_sh
        s *= 2
    h = a * h_scr[...] + u
    pre_ref[...] = (h * y).astype(BF16)
    h_scr[...] = h[tm - 1:tm, :]
    hlast_ref[0] = h[tm - 1:tm, :]
    tail = xe_scr[tm:tm + SUBLANES, :]
    xe_scr[0:SUBLANES, :] = tail
    tail_ref[0] = tail


def _mixc_kernel(x_ref, buf_ref, g_ref, win_ref, cw_ref, prein_ref, pre_ref, tail_ref, pe_scr, *, tm):
    del prein_ref

    @pl.when(pl.program_id(1) == 0)
    def _():
        pe_scr[0:SUBLANES, :] = buf_ref[0]

    d = x_ref.shape[1]
    xn = _rmsnorm(x_ref[...], g_ref[...]).astype(BF16)
    z = _dot(xn, win_ref[...])
    pe_scr[SUBLANES:SUBLANES + tm, :] = z[:, d:2 * d] * z[:, 2 * d:]
    conv = cw_ref[0:1, :] * pe_scr[SUBLANES - 2:SUBLANES - 2 + tm, :]
    for j in range(1, CONV_C):
        conv = conv + cw_ref[j:j + 1, :] * pe_scr[SUBLANES - 2 + j:SUBLANES - 2 + j + tm, :]
    pre_ref[...] = (z[:, :d] * conv).astype(BF16)
    tail = pe_scr[tm:tm + SUBLANES, :]
    pe_scr[0:SUBLANES, :] = tail
    tail_ref[0] = tail


def _seq_specs(n_seq, t_len, row_off, d):
    tm = min(TM_SEQ, t_len)
    n_tt = t_len // tm
    off = row_off // tm
    xrow = pl.BlockSpec((tm, d), lambda s, j: (off + s * n_tt + j, 0))
    state = lambda rows: pl.BlockSpec((1, rows, d), lambda s, j: (s, 0, 0))
    return tm, n_tt, xrow, state


def _mixb_call(x, pre_buf, h0, buf, g, wy, by, wx, bx, cw, cb, wr, br, wi, bi, lam, *, n_seq, t_len, row_off,
               name):
    d = x.shape[1]
    tm, n_tt, xrow, state = _seq_specs(n_seq, t_len, row_off, d)
    consts = (g, wy, by, wx, bx, cw, cb, wr, br, wi, bi, lam)
    return pl.pallas_call(
        functools.partial(_mixb_kernel, tm=tm),
        grid=(n_seq, n_tt),
        in_specs=[xrow, state(1), state(SUBLANES)] + [_const_spec(c.shape) for c in consts]
                 + [pl.BlockSpec(memory_space=pl.ANY)],
        out_specs=[xrow, state(1), state(SUBLANES)],
        out_shape=[jax.ShapeDtypeStruct(pre_buf.shape, BF16), jax.ShapeDtypeStruct((n_seq, 1, d), F32),
                   jax.ShapeDtypeStruct((n_seq, SUBLANES, d), F32)],
        scratch_shapes=[pltpu.VMEM((tm + SUBLANES, d), F32), pltpu.VMEM((1, d), F32)],
        input_output_aliases={3 + len(consts): 0},
        compiler_params=_cparams(("arbitrary", "arbitrary")),
        name=name,
    )(x, h0, buf, *consts, pre_buf)


def _mixc_call(x, pre_buf, buf, g, win, cw, *, n_seq, t_len, row_off, name):
    d = x.shape[1]
    tm, n_tt, xrow, state = _seq_specs(n_seq, t_len, row_off, d)
    consts = (g, win, cw)
    return pl.pallas_call(
        functools.partial(_mixc_kernel, tm=tm),
        grid=(n_seq, n_tt),
        in_specs=[xrow, state(SUBLANES)] + [_const_spec(c.shape) for c in consts]
                 + [pl.BlockSpec(memory_space=pl.ANY)],
        out_specs=[xrow, state(SUBLANES)],
        out_shape=[jax.ShapeDtypeStruct(pre_buf.shape, BF16), jax.ShapeDtypeStruct((n_seq, SUBLANES, d), F32)],
        scratch_shapes=[pltpu.VMEM((tm + SUBLANES, d), F32)],
        input_output_aliases={2 + len(consts): 0},
        compiler_params=_cparams(("arbitrary", "arbitrary")),
        name=name,
    )(x, buf, *consts, pre_buf)


def _front_pad_rows(a, rows):
    return jnp.pad(a, ((0, 0), (rows - a.shape[1], 0), (0, 0)))


def _heads_major(q_rows, n_seq, n_qb, q_len):
    q = q_rows.reshape(n_seq, q_len, N_KV_HEADS, GROUP, HEAD_DIM)
    q = jnp.pad(q, ((0, 0), (0, n_qb * QB - q_len), (0, 0), (0, 0), (0, 0)))
    q = q.reshape(n_seq, n_qb, QB, N_KV_HEADS, GROUP, HEAD_DIM).transpose(0, 1, 3, 4, 2, 5)
    return q.reshape(n_seq, n_qb, N_KV_HEADS, GROUP * QB, HEAD_DIM)


def _rows_major(o, q_len):
    n_seq, n_qb = o.shape[0], o.shape[1]
    o = o.reshape(n_seq, n_qb, N_KV_HEADS, GROUP, QB, HEAD_DIM).transpose(0, 1, 4, 2, 3, 5)
    o = o.reshape(n_seq, n_qb * QB, N_HEADS * HEAD_DIM)[:, :q_len]
    return o.reshape(n_seq * q_len, N_HEADS * HEAD_DIM)


def _kv_frame(parts):
    f = jnp.concatenate(parts, axis=1).astype(BF16)
    n_seq, rows = f.shape[0], f.shape[1]
    return f.reshape(n_seq, rows, N_KV_HEADS, HEAD_DIM).transpose(0, 2, 1, 3)


def _lane_pad_cols(a, n_seq, q_len):
    f = a.shape[0]
    a = jnp.pad(a.reshape(f, n_seq, q_len), ((0, 0), (0, 0), (0, QB - q_len)))
    return a.reshape(f, n_seq * QB)


def kernel(x_prompt, x_sample, cache_k, cache_v, cache_kidx, state_h, state_conv_b, state_conv_c, meta_tokens,
           rel_bias, norm_mix, norm_ffn, norm_final, a_wq, a_wk, a_wv, a_wo, a_wiq, a_wik, a_wiw, b_wy, b_by, b_wx,
           b_bx, b_conv_w, b_conv_b, b_wr, b_br, b_wi, b_bi, b_lam, b_wo, b_bo, c_win, c_conv_w, c_wo, f_wg, f_wu,
           f_wd):
    bp, seq, d = x_prompt.shape
    bs, dec = x_sample.shape[0], x_sample.shape[1]
    past = cache_k.shape[2]
    n_meta = meta_tokens.shape[0]
    depth = norm_mix.shape[0]
    assert seq % QB == 0 and seq % TM_SEQ == 0 and past % QB == 0 and dec <= QB and n_meta <= QB
    assert dec % SUBLANES == 0 and n_meta % SUBLANES == 0 and seq % dec == 0 and (bp * seq + bs * dec) % n_meta == 0

    n_main, n_samp = bp * seq, bs * dec
    off_s, off_m = n_main, n_main + n_samp
    n_valid = off_m + n_meta
    n_rows = -(-n_valid // ROW_PAD) * ROW_PAD
    x = jnp.concatenate([x_prompt.reshape(n_main, d), x_sample.reshape(n_samp, d), meta_tokens.astype(F32),
                         jnp.zeros((n_rows - n_valid, d), F32)], axis=0)

    k_sel_p = min(TOP_K_MAX, seq // 4)
    k_sel_s = min(TOP_K_MAX, (past + dec) // 4)
    n_qb = seq // QB
    kv_dim = N_KV_HEADS * HEAD_DIM

    tau = jnp.arange(seq, dtype=I32)
    kend_main = (QB + CHUNK * (tau // CHUNK + 1)).reshape(n_qb, 1, QB)
    lane = jnp.arange(QB, dtype=I32)
    kend_samp = jnp.minimum(past + dec, CHUNK * ((past + jnp.minimum(lane, dec - 1)) // CHUNK + 1)).reshape(1, 1, QB)
    kend_meta = jnp.full((1, 1, QB), n_meta, I32)

    tb = _bias_call(rel_bias.astype(F32))
    zeros_bias = jnp.zeros((1, d), F32)
    gfin = norm_final.reshape(1, d).astype(F32)

    new = {name: [] for name in ("k_p", "v_p", "ki_p", "h_p", "cb_p", "cc_p", "k_s", "v_s", "ki_s", "h_s", "cb_s",
                                 "cc_s")}
    for l in range(depth):
        s = l // N_MIXERS
        g_mix = norm_mix[l].reshape(1, d)
        if l % N_MIXERS == 0:
            wrow = jnp.concatenate([a_wk[s], a_wv[s], a_wik[s]], axis=1).astype(BF16)
            wq = (a_wq[s] * HEAD_DIM ** -0.5).astype(BF16)
            wt = jnp.concatenate([a_wiq[s].T * IDX_DIM ** -0.5, a_wiw[s].T,
                                  jnp.zeros((SUBLANES, d), F32)], axis=0).astype(BF16)
            k, v, ki, q, qit, wit = _proj_call(x, g_mix, wrow, wq, wt)

            k_meta, v_meta, ki_meta = (a[off_m:n_valid] for a in (k, v, ki))
            k_main, v_main, ki_main = (a[:n_main].reshape(bp, seq, -1) for a in (k, v, ki))
            k_samp, v_samp, ki_samp = (a[off_s:off_m].reshape(bs, dec, -1) for a in (k, v, ki))
            bc = lambda a: jnp.broadcast_to(a[None], (bp,) + a.shape)
            new["k_p"].append(jnp.concatenate([bc(k_meta), k_main], axis=1).reshape(bp, n_meta + seq, N_KV_HEADS,
                                                                                   HEAD_DIM))
            new["v_p"].append(jnp.concatenate([bc(v_meta), v_main], axis=1).reshape(bp, n_meta + seq, N_KV_HEADS,
                                                                                   HEAD_DIM))
            new["ki_p"].append(jnp.concatenate([bc(ki_meta), ki_main], axis=1))
            new["k_s"].append(k_samp.reshape(bs, dec, N_KV_HEADS, HEAD_DIM))
            new["v_s"].append(v_samp.reshape(bs, dec, N_KV_HEADS, HEAD_DIM))
            new["ki_s"].append(ki_samp)

            front = lambda a: jnp.pad(bc(a), ((0, 0), (QB - n_meta, 0), (0, 0)))
            o_main = _attn_call(
                kend_main, _heads_major(q[:n_main], bp, n_qb, seq), qit, wit,
                _kv_frame([front(k_meta), k_main]), _kv_frame([front(v_meta), v_main]),
                jnp.concatenate([front(ki_meta), ki_main], axis=1).astype(BF16), tb, rel_bias,
                lane_blk_off=0, diag_off=1, kstart=QB - n_meta, k_sel=k_sel_p, name="dsa_attn_prompt")
            back = lambda a: jnp.pad(a, ((0, 0), (0, QB - a.shape[1]), (0, 0)))
            o_samp = _attn_call(
                kend_samp, _heads_major(q[off_s:off_m], bs, 1, dec),
                _lane_pad_cols(qit[:, off_s:off_m], bs, dec), _lane_pad_cols(wit[:, off_s:off_m], bs, dec),
                _kv_frame([cache_k[s].reshape(bs, past, kv_dim), back(k_samp)]),
                _kv_frame([cache_v[s].reshape(bs, past, kv_dim), back(v_samp)]),
                jnp.concatenate([cache_kidx[s], back(ki_samp)], axis=1).astype(BF16), tb, rel_bias,
                lane_blk_off=0, diag_off=past // QB, kstart=0, k_sel=k_sel_s, name="dsa_attn_sample")
            o_meta = _attn_call(
                kend_meta, _heads_major(q[off_m:n_valid], 1, 1, n_meta),
                _lane_pad_cols(qit[:, off_m:n_valid], 1, n_meta), _lane_pad_cols(wit[:, off_m:n_valid], 1, n_meta),
                _kv_frame([back(k_meta[None])]), _kv_frame([back(v_meta[None])]),
                back(ki_meta[None]).astype(BF16), tb, rel_bias,
                lane_blk_off=0, diag_off=0, kstart=0, k_sel=k_sel_p, name="dsa_attn_meta")
            pre = jnp.concatenate([_rows_major(o_main, seq), _rows_major(o_samp, dec), _rows_major(o_meta, n_meta),
                                   jnp.zeros((n_rows - n_valid, d), BF16)], axis=0)
            wo, bo = a_wo[s].astype(BF16), zeros_bias
        elif l % N_MIXERS == 1:
            consts = (g_mix, b_wy[s].astype(BF16), b_by[s].reshape(1, d), b_wx[s].astype(BF16), b_bx[s].reshape(1, d),
                      jnp.pad(b_conv_w[s], ((0, SUBLANES - CONV_B), (0, 0))), b_conv_b[s].reshape(1, d),
                      b_wr[s].astype(BF16), b_br[s].reshape(1, d), b_wi[s].astype(BF16), b_bi[s].reshape(1, d),
                      b_lam[s].reshape(1, d))
            pre = jnp.zeros((n_rows, d), BF16)
            pre, h_m, tail_m = _mixb_call(x, pre, jnp.zeros((1, 1, d), F32), jnp.zeros((1, SUBLANES, d), F32),
                                          *consts, n_seq=1, t_len=n_meta, row_off=off_m, name="rglru_meta")
            pre, h_p, tail_p = _mixb_call(x, pre, jnp.broadcast_to(h_m, (bp, 1, d)),
                                          jnp.broadcast_to(tail_m, (bp, SUBLANES, d)),
                                          *consts, n_seq=bp, t_len=seq, row_off=0, name="rglru_prompt")
            pre, h_s, tail_s = _mixb_call(x, pre, state_h[s].reshape(bs, 1, d),
                                          _front_pad_rows(state_conv_b[s], SUBLANES),
                                          *consts, n_seq=bs, t_len=dec, row_off=off_s, name="rglru_sample")
            new["h_p"].append(h_p.reshape(bp, d))
            new["cb_p"].append(tail_p[:, SUBLANES - (CONV_B - 1):])
            new["h_s"].append(h_s.reshape(bs, d))
            new["cb_s"].append(tail_s[:, SUBLANES - (CONV_B - 1):])
            wo, bo = b_wo[s].astype(BF16), b_bo[s].reshape(1, d)
        else:
            consts = (g_mix, c_win[s].astype(BF16), jnp.pad(c_conv_w[s], ((0, SUBLANES - CONV_C), (0, 0))))
            pre = jnp.zeros((n_rows, d), BF16)
            pre, tail_m = _mixc_call(x, pre, jnp.zeros((1, SUBLANES, d), F32), *consts, n_seq=1, t_len=n_meta,
                                     row_off=off_m, name="sconv_meta")
            pre, tail_p = _mixc_call(x, pre, jnp.broadcast_to(tail_m, (bp, SUBLANES, d)), *consts, n_seq=bp,
                                     t_len=seq, row_off=0, name="sconv_prompt")
            pre, tail_s = _mixc_call(x, pre, _front_pad_rows(state_conv_c[s], SUBLANES), *consts, n_seq=bs,
                                     t_len=dec, row_off=off_s, name="sconv_sample")
            new["cc_p"].append(tail_p[:, SUBLANES - (CONV_C - 1):])
            new["cc_s"].append(tail_s[:, SUBLANES - (CONV_C - 1):])
            wo, bo = c_wo[s].astype(BF16), zeros_bias
        x = _ffn_call(x, pre, wo, bo, norm_ffn[l].reshape(1, d), f_wg[l].astype(BF16), f_wu[l].astype(BF16),
                      f_wd[l].astype(BF16), gfin, final_norm=(l == depth - 1))

    y_prompt = x[:n_main].reshape(bp, seq, d)
    y_sample = x[off_s:off_m].reshape(bs, dec, d)
    st = lambda name: jnp.stack(new[name])
    return (y_prompt, y_sample, st("k_p"), st("v_p"), st("ki_p"), st("h_p"), st("cb_p"), st("cc_p"),
            st("k_s"), st("v_s"), st("ki_s"), st("h_s"), st("cb_s"), st("cc_s"))
```

```python
import functools

import jax
import jax.numpy as jnp
from jax import lax
from jax.experimental import pallas as pl
from jax.experimental.pallas import tpu as pltpu

F32 = jnp.float32
BF16 = jnp.bfloat16
I32 = jnp.int32

CHUNK = 64
N_MIXERS = 3
N_HEADS = 16
N_KV_HEADS = 4
GROUP = N_HEADS // N_KV_HEADS
HEAD_DIM = 64
IDX_HEADS = 8
IDX_DIM = 64
TOP_K_MAX = 256
NEG_INF = -1e30
N_BUCKETS = 32
LRU_BLOCKS = 4
CONV_B = 4
CONV_C = 3
LRU_C = 8.0
RMS_EPS = 1e-6
BUCKET_STEPS = (12, 16, 23, 32, 46, 64, 91)
FAR_BUCKET = N_BUCKETS // 2 - 1

LANES = 128
SUBLANES = 8
ROW_PAD = 768
TM_PROJ = 256
TM_FFN = 384
TM_SEQ = 256
QB = 128
SEL_TILE = 256
INT_MIN = -2 ** 31
VMEM_LIMIT = 56 * 1024 * 1024


def _cparams(sem):
    return pltpu.CompilerParams(dimension_semantics=sem, vmem_limit_bytes=VMEM_LIMIT)


def _const_spec(shape):
    nd = len(shape)
    return pl.BlockSpec(shape, lambda *_: (0,) * nd, pipeline_mode=pl.Buffered(1))


def _rmsnorm(x, g):
    ms = jnp.mean(x * x, axis=-1, keepdims=True)
    return x * lax.rsqrt(ms + RMS_EPS) * g


def _dot(a, b):
    return jnp.dot(a, b, preferred_element_type=F32)


def _dot_nt(a, b):
    return lax.dot_general(a, b, (((1,), (1,)), ((), ())), preferred_element_type=F32)


def _ffn_kernel(x_ref, pre_ref, wo_ref, bo_ref, gf_ref, wg_ref, wu_ref, wd_ref, gfin_ref, out_ref, *,
                n_chunks, fc, final_norm):
    x1 = x_ref[...] + _dot(pre_ref[...], wo_ref[...]) + bo_ref[...]
    xn = _rmsnorm(x1, gf_ref[...]).astype(BF16)
    acc = x1
    for c in range(n_chunks):
        gt = _dot(xn, wg_ref[:, c * fc:(c + 1) * fc])
        up = _dot(xn, wu_ref[:, c * fc:(c + 1) * fc])
        hm = (gt * jax.nn.sigmoid(gt) * up).astype(BF16)
        acc = acc + _dot(hm, wd_ref[c * fc:(c + 1) * fc, :])
    if final_norm:
        acc = _rmsnorm(acc, gfin_ref[...])
    out_ref[...] = acc


def _ffn_call(x, pre, wo, bo, gf, wg, wu, wd, gfin, final_norm):
    n, d = x.shape
    dff = wg.shape[1]
    n_chunks = 2 if dff % (2 * LANES) == 0 else 1
    fc = dff // n_chunks
    row = lambda i: (i, 0)
    return pl.pallas_call(
        functools.partial(_ffn_kernel, n_chunks=n_chunks, fc=fc, final_norm=final_norm),
        grid=(n // TM_FFN,),
        in_specs=[pl.BlockSpec((TM_FFN, d), row), pl.BlockSpec((TM_FFN, pre.shape[1]), row),
                  _const_spec(wo.shape), _const_spec(bo.shape), _const_spec(gf.shape),
                  _const_spec(wg.shape), _const_spec(wu.shape), _const_spec(wd.shape), _const_spec(gfin.shape)],
        out_specs=pl.BlockSpec((TM_FFN, d), row),
        out_shape=jax.ShapeDtypeStruct((n, d), F32),
        compiler_params=_cparams(("parallel",)),
        name="outproj_swiglu",
    )(x, pre, wo, bo, gf, wg, wu, wd, gfin)


def _proj_kernel(x_ref, g_ref, wrow_ref, wt_ref, k_ref, v_ref, ki_ref, qt_ref, qit_ref, vt_ref, wit_ref, *,
                 n_kv, n_q, n_iq):
    xn = _rmsnorm(x_ref[...], g_ref[...]).astype(BF16)
    row = _dot(xn, wrow_ref[...])
    k_ref[...] = row[:, :n_kv]
    v_ref[...] = row[:, n_kv:2 * n_kv]
    ki_ref[...] = row[:, 2 * n_kv:]
    tt = _dot_nt(wt_ref[...], xn)
    qt_ref[...] = tt[:n_q].astype(BF16)
    qit_ref[...] = tt[n_q:n_q + n_iq].astype(BF16)
    vt_ref[...] = tt[n_q + n_iq:n_q + n_iq + n_kv].astype(BF16)
    wit_ref[...] = tt[n_q + n_iq + n_kv:n_q + n_iq + n_kv + IDX_HEADS] * (IDX_HEADS ** -0.5)


def _proj_call(x, g, wrow, wt):
    n, d = x.shape
    n_kv = N_KV_HEADS * HEAD_DIM
    n_q = N_HEADS * HEAD_DIM
    n_iq = IDX_HEADS * IDX_DIM
    row = lambda i: (i, 0)
    col = lambda i: (0, i)
    return pl.pallas_call(
        functools.partial(_proj_kernel, n_kv=n_kv, n_q=n_q, n_iq=n_iq),
        grid=(n // TM_PROJ,),
        in_specs=[pl.BlockSpec((TM_PROJ, d), row), _const_spec(g.shape), _const_spec(wrow.shape),
                  _const_spec(wt.shape)],
        out_specs=[pl.BlockSpec((TM_PROJ, n_kv), row), pl.BlockSpec((TM_PROJ, n_kv), row),
                   pl.BlockSpec((TM_PROJ, IDX_DIM), row), pl.BlockSpec((n_q, TM_PROJ), col),
                   pl.BlockSpec((n_iq, TM_PROJ), col), pl.BlockSpec((n_kv, TM_PROJ), col),
                   pl.BlockSpec((IDX_HEADS, TM_PROJ), col)],
        out_shape=[jax.ShapeDtypeStruct((n, n_kv), F32), jax.ShapeDtypeStruct((n, n_kv), F32),
                   jax.ShapeDtypeStruct((n, IDX_DIM), F32), jax.ShapeDtypeStruct((n_q, n), BF16),
                   jax.ShapeDtypeStruct((n_iq, n), BF16), jax.ShapeDtypeStruct((n_kv, n), BF16),
                   jax.ShapeDtypeStruct((IDX_HEADS, n), F32)],
        compiler_params=_cparams(("parallel",)),
        name="attn_proj",
    )(x, g, wrow, wt)


def _bias_kernel(tab_ref, tb_ref):
    d = pl.program_id(0)
    h = pl.program_id(1)
    kj = lax.broadcasted_iota(I32, (QB, QB), 0)
    qi = lax.broadcasted_iota(I32, (QB, QB), 1)
    rel = (d - 1) * QB + kj - qi
    n = jnp.abs(rel)
    large = jnp.full((QB, QB), N_BUCKETS // 4, I32)
    for s in BUCKET_STEPS:
        large = large + jnp.where(n >= s, 1, 0)
    bucket = jnp.where(rel > 0, N_BUCKETS // 2, 0) + jnp.where(n < N_BUCKETS // 4, n, large)
    val = jnp.zeros((QB, QB), F32)
    for b in range(N_BUCKETS):
        val = jnp.where(bucket == b, tab_ref[b, h], val)
    tb_ref[0, 0] = val


def _bias_call(table):
    return pl.pallas_call(
        _bias_kernel,
        grid=(2, N_HEADS),
        in_specs=[pl.BlockSpec(memory_space=pltpu.SMEM)],
        out_specs=pl.BlockSpec((1, 1, QB, QB), lambda d, h: (d, h, 0, 0)),
        out_shape=jax.ShapeDtypeStruct((2, N_HEADS, QB, QB), F32),
        compiler_params=_cparams(("arbitrary", "arbitrary")),
        name="rel_bias_tiles",
    )(table)


def _attn_kernel(kend_ref, qt_ref, qit_ref, wit_ref, kf_ref, vtf_ref, kif_ref, tb_ref, tab_ref, o_ref,
                 key_scr, ls_scr, oacc_scr, *, diag_off, kstart, k_sel, has_prev, row_bits):
    diag = pl.program_id(1) + diag_off
    nt = diag + 1
    nst = (nt + SEL_TILE // QB - 1) // (SEL_TILE // QB)
    kend = kend_ref[0]
    qi_cat = jnp.concatenate([qit_ref[h * IDX_DIM:(h + 1) * IDX_DIM, :] for h in range(IDX_HEADS)], axis=1)
    wi = wit_ref[...]
    sel_iota = lax.broadcasted_iota(I32, (SEL_TILE, QB), 0)
    row_iota = lax.broadcasted_iota(I32, (QB, QB), 0)

    def tile_start(t):
        return pl.multiple_of(t * QB, QB)

    def sel_start(t):
        return pl.multiple_of(t * SEL_TILE, SEL_TILE)

    def score_tile(t, carry):
        r0 = sel_start(t)
        s = _dot(kif_ref[0, pl.ds(r0, SEL_TILE), :], qi_cat)
        sc = jnp.zeros((SEL_TILE, QB), F32)
        for h in range(IDX_HEADS):
            sc = sc + wi[h:h + 1, :] * jnp.maximum(s[:, h * QB:(h + 1) * QB], 0.0)
        bits = lax.bitcast_convert_type(sc + 0.0, I32)
        key = jnp.where(bits >= 0, bits, bits ^ 0x7FFFFFFF)
        rows = r0 + sel_iota
        adm = (rows >= kstart) & (rows < kend)
        key_scr[pl.ds(r0, SEL_TILE), :] = jnp.where(adm, key, INT_MIN)
        return carry

    lax.fori_loop(0, nst, score_tile, 0)

    def count(pred):
        def body(t, acc):
            r0 = sel_start(t)
            ind = jnp.where(pred(key_scr[pl.ds(r0, SEL_TILE), :], r0 + sel_iota), 1, 0)
            return acc + ind.reshape(SEL_TILE // SUBLANES, SUBLANES, QB).sum(axis=0)
        acc = lax.fori_loop(0, nst, body, jnp.zeros((SUBLANES, QB), I32))
        return acc.sum(axis=0, keepdims=True)

    def thr_step(b, thr):
        cand = thr + lax.shift_left(jnp.int32(1), 31 - b)
        c = count(lambda kt, rows: kt >= cand)
        return jnp.where(c >= k_sel, cand, thr)

    thr = lax.fori_loop(0, 32, thr_step, jnp.full((1, QB), INT_MIN, I32))

    c_ge = count(lambda kt, rows: kt >= thr)
    tied = (c_ge > k_sel) & (thr > INT_MIN)
    big = jnp.full((1, QB), 2 ** row_bits, I32)

    def tie_limit():
        need = k_sel - count(lambda kt, rows: kt > thr)

        def lim_step(b, lim):
            cand = lim + lax.shift_left(jnp.int32(1), row_bits - 1 - b)
            c = count(lambda kt, rows: (kt == thr) & (rows < cand))
            return jnp.where(c < need, cand, lim)

        lim = lax.fori_loop(0, row_bits, lim_step, jnp.zeros((1, QB), I32))
        return jnp.where(tied, lim, big)

    rlim = lax.cond(jnp.max(tied.astype(I32)) > 0, tie_limit, lambda: big)
    rlim = jnp.where(thr > INT_MIN, rlim, -1)

    gq = GROUP * QB
    groups = QB // SUBLANES
    qg = [jnp.concatenate([qt_ref[(g * GROUP + r) * HEAD_DIM:(g * GROUP + r + 1) * HEAD_DIM, :]
                           for r in range(GROUP)], axis=1) for g in range(N_KV_HEADS)]

    def logits_tile(t, m8, bias_of_head):
        r0 = tile_start(t)
        kt = key_scr[pl.ds(r0, QB), :]
        keep = (kt > thr) | ((kt == thr) & (row_iota <= rlim - r0))
        new_m8 = []
        for g in range(N_KV_HEADS):
            lg = _dot(kf_ref[0, g, pl.ds(r0, QB), :], qg[g])
            parts = []
            for r in range(GROUP):
                part = lg[:, r * QB:(r + 1) * QB] + bias_of_head(g * GROUP + r)
                parts.append(jnp.where(keep, part, NEG_INF))
            lg = jnp.concatenate(parts, axis=1)
            ls_scr[pl.ds(r0, QB), g * gq:(g + 1) * gq] = lg
            new_m8.append(jnp.maximum(m8[g], lg.reshape(groups, SUBLANES, gq).max(axis=0)))
        return tuple(new_m8)

    m8 = tuple(jnp.full((SUBLANES, gq), NEG_INF, F32) for _ in range(N_KV_HEADS))
    n_far = diag - 1 if has_prev else diag
    m8 = lax.fori_loop(0, n_far, lambda t, m: logits_tile(t, m, lambda h: tab_ref[FAR_BUCKET, h]), m8)
    if has_prev:
        m8 = logits_tile(diag - 1, m8, lambda h: tb_ref[0, h])
    m8 = logits_tile(diag, m8, lambda h: tb_ref[1, h])
    m = [jnp.max(m8[g], axis=0, keepdims=True) for g in range(N_KV_HEADS)]

    oacc_scr[...] = jnp.zeros(oacc_scr.shape, F32)

    def pv_tile(t, l8):
        r0 = tile_start(t)
        new_l8 = []
        for g in range(N_KV_HEADS):
            p = jnp.exp(ls_scr[pl.ds(r0, QB), g * gq:(g + 1) * gq] - m[g])
            oacc_scr[g] += _dot(vtf_ref[0, g * HEAD_DIM:(g + 1) * HEAD_DIM, pl.ds(r0, QB)], p.astype(BF16))
            new_l8.append(l8[g] + p.reshape(groups, SUBLANES, gq).sum(axis=0))
        return tuple(new_l8)

    l8 = lax.fori_loop(0, nt, pv_tile, tuple(jnp.zeros((SUBLANES, gq), F32) for _ in range(N_KV_HEADS)))
    out_rows = []
    for g in range(N_KV_HEADS):
        o_g = oacc_scr[g] / jnp.sum(l8[g], axis=0, keepdims=True)
        out_rows.extend(o_g[:, r * QB:(r + 1) * QB] for r in range(GROUP))
    o_ref[...] = jnp.concatenate(out_rows, axis=0).T.astype(BF16)


def _attn_call(kend, qt, qit, wit, kf, vtf, kif, tb, table, *, n_out_rows, diag_off, kstart, k_sel, name):
    n_seq, tkf = kf.shape[0], kf.shape[2]
    assert tkf % SEL_TILE == 0 and vtf.shape[2] == tkf and kif.shape[1] == tkf
    n_qb = kend.shape[0]
    has_prev = diag_off >= 1
    row_bits = max(1, (tkf - 1).bit_length())
    lane_blk = lambda s, i: (0, s * n_qb + i)
    feat = lambda a: pl.BlockSpec((a.shape[0], QB), lane_blk)
    return pl.pallas_call(
        functools.partial(_attn_kernel, diag_off=diag_off, kstart=kstart, k_sel=k_sel, has_prev=has_prev,
                          row_bits=row_bits),
        grid=(n_seq, n_qb),
        in_specs=[pl.BlockSpec((1, 1, QB), lambda s, i: (i, 0, 0)), feat(qt), feat(qit), feat(wit),
                  pl.BlockSpec((1, N_KV_HEADS, tkf, HEAD_DIM), lambda s, i: (s, 0, 0, 0)),
                  pl.BlockSpec((1, N_KV_HEADS * HEAD_DIM, tkf), lambda s, i: (s, 0, 0)),
                  pl.BlockSpec((1, tkf, IDX_DIM), lambda s, i: (s, 0, 0)),
                  _const_spec(tb.shape), pl.BlockSpec(memory_space=pltpu.SMEM)],
        out_specs=pl.BlockSpec((QB, N_HEADS * HEAD_DIM), lambda s, i: (s * n_qb + i, 0)),
        out_shape=jax.ShapeDtypeStruct((n_out_rows, N_HEADS * HEAD_DIM), BF16),
        scratch_shapes=[pltpu.VMEM((tkf, QB), I32), pltpu.VMEM((tkf, N_HEADS * QB), F32),
                        pltpu.VMEM((N_KV_HEADS, HEAD_DIM, GROUP * QB), F32)],
        compiler_params=_cparams(("arbitrary", "arbitrary")),
        name=name,
    )(kend, qt, qit, wit, kf, vtf, kif, tb, table)


def _softplus(x):
    return jnp.maximum(x, 0.0) + jnp.log1p(jnp.exp(-jnp.abs(x)))


def _mixb_kernel(x_ref, h0_ref, buf_ref, g_ref, wy_ref, by_ref, wx_ref, bx_ref, cw_ref, cb_ref, wr_ref, br_ref,
                 wi_ref, bi_ref, lam_ref, prein_ref, pre_ref, hlast_ref, tail_ref, xe_scr, h_scr, *, tm):
    del prein_ref

    @pl.when(pl.program_id(1) == 0)
    def _():
        xe_scr[0:SUBLANES, :] = buf_ref[0]
        h_scr[...] = h0_ref[0]

    xn = _rmsnorm(x_ref[...], g_ref[...]).astype(BF16)
    y = jax.nn.gelu(_dot(xn, wy_ref[...]) + by_ref[...])
    xe_scr[SUBLANES:SUBLANES + tm, :] = _dot(xn, wx_ref[...]) + bx_ref[...]
    xc = cb_ref[...] + cw_ref[0:1, :] * xe_scr[SUBLANES - 3:SUBLANES - 3 + tm, :]
    for j in range(1, CONV_B):
        xc = xc + cw_ref[j:j + 1, :] * xe_scr[SUBLANES - 3 + j:SUBLANES - 3 + j + tm, :]
    xcb = xc.astype(BF16)
    blk = xc.shape[1] // LRU_BLOCKS
    r_pre = jnp.concatenate([_dot(xcb[:, n * blk:(n + 1) * blk], wr_ref[n]) for n in range(LRU_BLOCKS)], axis=1)
    i_pre = jnp.concatenate([_dot(xcb[:, n * blk:(n + 1) * blk], wi_ref[n]) for n in range(LRU_BLOCKS)], axis=1)
    r = jax.nn.sigmoid(r_pre + br_ref[...])
    ig = jax.nn.sigmoid(i_pre + bi_ref[...])
    log_a = -LRU_C * r * _softplus(-lam_ref[...])
    a = jnp.exp(log_a)
    u = jnp.sqrt(jnp.tanh(-log_a) * (1.0 + a * a)) * (ig * xc)
    rows = lax.broadcasted_iota(I32, a.shape, 0)
    s = 1
    while s < tm:
        a_sh = jnp.where(rows >= s, pltpu.roll(a, s, 0), 1.0)
        u_sh = jnp.where(rows >= s, pltpu.roll(u, s, 0), 0.0)
        u = a * u_sh + u
        a = a * a_sh
        s *= 2
    h = a * h_scr[...] + u
    pre_ref[...] = (h * y).astype(BF16)
    h_scr[...] = h[tm - 1:tm, :]
    hlast_ref[0] = h[tm - 1:tm, :]
    tail = xe_scr[tm:tm + SUBLANES, :]
    xe_scr[0:SUBLANES, :] = tail
    tail_ref[0] = tail


def _mixc_kernel(x_ref, buf_ref, g_ref, win_ref, cw_ref, prein_ref, pre_ref, tail_ref, pe_scr, *, tm):
    del prein_ref

    @pl.when(pl.program_id(1) == 0)
    def _():
        pe_scr[0:SUBLANES, :] = buf_ref[0]

    d = x_ref.shape[1]
    xn = _rmsnorm(x_ref[...], g_ref[...]).astype(BF16)
    z = _dot(xn, win_ref[...])
    pe_scr[SUBLANES:SUBLANES + tm, :] = z[:, d:2 * d] * z[:, 2 * d:]
    conv = cw_ref[0:1, :] * pe_scr[SUBLANES - 2:SUBLANES - 2 + tm, :]
    for j in range(1, CONV_C):
        conv = conv + cw_ref[j:j + 1, :] * pe_scr[SUBLANES - 2 + j:SUBLANES - 2 + j + tm, :]
    pre_ref[...] = (z[:, :d] * conv).astype(BF16)
    tail = pe_scr[tm:tm + SUBLANES, :]
    pe_scr[0:SUBLANES, :] = tail
    tail_ref[0] = tail


def _seq_specs(n_seq, t_len, row_off, d):
    tm = min(TM_SEQ, t_len)
    n_tt = t_len // tm
    off = row_off // tm
    xrow = pl.BlockSpec((tm, d), lambda s, j: (off + s * n_tt + j, 0))
    state = lambda rows: pl.BlockSpec((1, rows, d), lambda s, j: (s, 0, 0))
    return tm, n_tt, xrow, state


def _mixb_call(x, pre_buf, h0, buf, g, wy, by, wx, bx, cw, cb, wr, br, wi, bi, lam, *, n_seq, t_len, row_off,
               name):
    d = x.shape[1]
    tm, n_tt, xrow, state = _seq_specs(n_seq, t_len, row_off, d)
    consts = (g, wy, by, wx, bx, cw, cb, wr, br, wi, bi, lam)
    return pl.pallas_call(
        functools.partial(_mixb_kernel, tm=tm),
        grid=(n_seq, n_tt),
        in_specs=[xrow, state(1), state(SUBLANES)] + [_const_spec(c.shape) for c in consts]
                 + [pl.BlockSpec(memory_space=pl.ANY)],
        out_specs=[xrow, state(1), state(SUBLANES)],
        out_shape=[jax.ShapeDtypeStruct(pre_buf.shape, BF16), jax.ShapeDtypeStruct((n_seq, 1, d), F32),
                   jax.ShapeDtypeStruct((n_seq, SUBLANES, d), F32)],
        scratch_shapes=[pltpu.VMEM((tm + SUBLANES, d), F32), pltpu.VMEM((1, d), F32)],
        input_output_aliases={3 + len(consts): 0},
        compiler_params=_cparams(("arbitrary", "arbitrary")),
        name=name,
    )(x, h0, buf, *consts, pre_buf)


def _mixc_call(x, pre_buf, buf, g, win, cw, *, n_seq, t_len, row_off, name):
    d = x.shape[1]
    tm, n_tt, xrow, state = _seq_specs(n_seq, t_len, row_off, d)
    consts = (g, win, cw)
    return pl.pallas_call(
        functools.partial(_mixc_kernel, tm=tm),
        grid=(n_seq, n_tt),
        in_specs=[xrow, state(SUBLANES)] + [_const_spec(c.shape) for c in consts]
                 + [pl.BlockSpec(memory_space=pl.ANY)],
        out_specs=[xrow, state(SUBLANES)],
        out_shape=[jax.ShapeDtypeStruct(pre_buf.shape, BF16), jax.ShapeDtypeStruct((n_seq, SUBLANES, d), F32)],
        scratch_shapes=[pltpu.VMEM((tm + SUBLANES, d), F32)],
        input_output_aliases={2 + len(consts): 0},
        compiler_params=_cparams(("arbitrary", "arbitrary")),
        name=name,
    )(x, buf, *consts, pre_buf)


def _front_pad_rows(a, rows):
    return jnp.pad(a, ((0, 0), (rows - a.shape[1], 0), (0, 0)))


def _frame(parts, axis):
    f = jnp.concatenate(parts, axis=axis).astype(BF16)
    pad = [(0, 0)] * f.ndim
    pad[axis] = (0, -f.shape[axis] % SEL_TILE)
    return jnp.pad(f, pad)


def _k_frame(parts):
    f = _frame(parts, 1)
    n_seq, rows = f.shape[0], f.shape[1]
    return f.reshape(n_seq, rows, N_KV_HEADS, HEAD_DIM).transpose(0, 2, 1, 3)


def _seq_major(a, n_seq, q_len):
    return a.reshape(a.shape[0], n_seq, q_len).transpose(1, 0, 2)


def _lane_pad_cols(a, n_seq, q_len):
    f = a.shape[0]
    a = jnp.pad(a.reshape(f, n_seq, q_len), ((0, 0), (0, 0), (0, QB - q_len)))
    return a.reshape(f, n_seq * QB)


def kernel(x_prompt, x_sample, cache_k, cache_v, cache_kidx, state_h, state_conv_b, state_conv_c, meta_tokens,
           rel_bias, norm_mix, norm_ffn, norm_final, a_wq, a_wk, a_wv, a_wo, a_wiq, a_wik, a_wiw, b_wy, b_by, b_wx,
           b_bx, b_conv_w, b_conv_b, b_wr, b_br, b_wi, b_bi, b_lam, b_wo, b_bo, c_win, c_conv_w, c_wo, f_wg, f_wu,
           f_wd):
    bp, seq, d = x_prompt.shape
    bs, dec = x_sample.shape[0], x_sample.shape[1]
    past = cache_k.shape[2]
    n_meta = meta_tokens.shape[0]
    depth = norm_mix.shape[0]
    assert seq % QB == 0 and seq % TM_SEQ == 0 and past % QB == 0 and dec <= QB and n_meta <= QB
    assert dec % SUBLANES == 0 and n_meta % SUBLANES == 0 and seq % dec == 0 and (bp * seq + bs * dec) % n_meta == 0

    n_main, n_samp = bp * seq, bs * dec
    off_s, off_m = n_main, n_main + n_samp
    n_valid = off_m + n_meta
    n_rows = -(-n_valid // ROW_PAD) * ROW_PAD
    x = jnp.concatenate([x_prompt.reshape(n_main, d), x_sample.reshape(n_samp, d), meta_tokens.astype(F32),
                         jnp.zeros((n_rows - n_valid, d), F32)], axis=0)

    k_sel_p = min(TOP_K_MAX, seq // 4)
    k_sel_s = min(TOP_K_MAX, (past + dec) // 4)
    n_qb = seq // QB
    kv_dim = N_KV_HEADS * HEAD_DIM

    tau = jnp.arange(seq, dtype=I32)
    kend_main = (QB + CHUNK * (tau // CHUNK + 1)).reshape(n_qb, 1, QB)
    lane = jnp.arange(QB, dtype=I32)
    kend_samp = jnp.minimum(past + dec, CHUNK * ((past + jnp.minimum(lane, dec - 1)) // CHUNK + 1)).reshape(1, 1, QB)
    kend_meta = jnp.full((1, 1, QB), n_meta, I32)

    tb = _bias_call(rel_bias.astype(F32))
    zeros_bias = jnp.zeros((1, d), F32)
    gfin = norm_final.reshape(1, d).astype(F32)

    new = {name: [] for name in ("k_p", "v_p", "ki_p", "h_p", "cb_p", "cc_p", "k_s", "v_s", "ki_s", "h_s", "cb_s",
                                 "cc_s")}
    for l in range(depth):
        s = l // N_MIXERS
        g_mix = norm_mix[l].reshape(1, d)
        if l % N_MIXERS == 0:
            wrow = jnp.concatenate([a_wk[s], a_wv[s], a_wik[s]], axis=1).astype(BF16)
            wt = jnp.concatenate([a_wq[s].T * HEAD_DIM ** -0.5, a_wiq[s].T * IDX_DIM ** -0.5, a_wv[s].T, a_wiw[s].T,
                                  jnp.zeros((SUBLANES, d), F32)], axis=0).astype(BF16)
            k, v, ki, qt, qit, vt, wit = _proj_call(x, g_mix, wrow, wt)

            k_meta, v_meta, ki_meta = (a[off_m:n_valid] for a in (k, v, ki))
            k_main, v_main, ki_main = (a[:n_main].reshape(bp, seq, -1) for a in (k, v, ki))
            k_samp, v_samp, ki_samp = (a[off_s:off_m].reshape(bs, dec, -1) for a in (k, v, ki))
            bc = lambda a: jnp.broadcast_to(a[None], (bp,) + a.shape)
            new["k_p"].append(jnp.concatenate([bc(k_meta), k_main], axis=1).reshape(bp, n_meta + seq, N_KV_HEADS,
                                                                                   HEAD_DIM))
            new["v_p"].append(jnp.concatenate([bc(v_meta), v_main], axis=1).reshape(bp, n_meta + seq, N_KV_HEADS,
                                                                                   HEAD_DIM))
            new["ki_p"].append(jnp.concatenate([bc(ki_meta), ki_main], axis=1))
            new["k_s"].append(k_samp.reshape(bs, dec, N_KV_HEADS, HEAD_DIM))
            new["v_s"].append(v_samp.reshape(bs, dec, N_KV_HEADS, HEAD_DIM))
            new["ki_s"].append(ki_samp)

            front = lambda a: jnp.pad(bc(a), ((0, 0), (QB - n_meta, 0), (0, 0)))
            vt_meta = vt[:, off_m:n_valid]
            vtf_main = _frame([jnp.pad(bc(vt_meta), ((0, 0), (0, 0), (QB - n_meta, 0))),
                               _seq_major(vt[:, :n_main], bp, seq)], 2)
            pre = _attn_call(
                kend_main, qt, qit, wit, _k_frame([front(k_meta), k_main]), vtf_main,
                _frame([front(ki_meta), ki_main], 1), tb, rel_bias,
                n_out_rows=n_rows, diag_off=1, kstart=QB - n_meta, k_sel=k_sel_p, name="dsa_attn_prompt")
            vtf_samp = _frame([cache_v[s].reshape(bs, past, kv_dim).transpose(0, 2, 1),
                               _seq_major(vt[:, off_s:off_m], bs, dec)], 2)
            o_samp = _attn_call(
                kend_samp, _lane_pad_cols(qt[:, off_s:off_m], bs, dec),
                _lane_pad_cols(qit[:, off_s:off_m], bs, dec), _lane_pad_cols(wit[:, off_s:off_m], bs, dec),
                _k_frame([cache_k[s].reshape(bs, past, kv_dim), k_samp]), vtf_samp,
                _frame([cache_kidx[s], ki_samp], 1), tb, rel_bias,
                n_out_rows=bs * QB, diag_off=past // QB, kstart=0, k_sel=k_sel_s, name="dsa_attn_sample")
            o_meta = _attn_call(
                kend_meta, _lane_pad_cols(qt[:, off_m:n_valid], 1, n_meta),
                _lane_pad_cols(qit[:, off_m:n_valid], 1, n_meta), _lane_pad_cols(wit[:, off_m:n_valid], 1, n_meta),
                _k_frame([k_meta[None]]), _frame([vt_meta[None]], 2), _frame([ki_meta[None]], 1), tb, rel_bias,
                n_out_rows=QB, diag_off=0, kstart=0, k_sel=k_sel_p, name="dsa_attn_meta")
            tail_rows = jnp.concatenate([o_samp.reshape(bs, QB, d)[:, :dec].reshape(n_samp, d), o_meta[:n_meta],
                                         jnp.zeros((n_rows - n_valid, d), BF16)], axis=0)
            pre = lax.dynamic_update_slice(pre, tail_rows, (off_s, 0))
            wo, bo = a_wo[s].astype(BF16), zeros_bias
        elif l % N_MIXERS == 1:
            consts = (g_mix, b_wy[s].astype(BF16), b_by[s].reshape(1, d), b_wx[s].astype(BF16), b_bx[s].reshape(1, d),
                      jnp.pad(b_conv_w[s], ((0, SUBLANES - CONV_B), (0, 0))), b_conv_b[s].reshape(1, d),
                      b_wr[s].astype(BF16), b_br[s].reshape(1, d), b_wi[s].astype(BF16), b_bi[s].reshape(1, d),
                      b_lam[s].reshape(1, d))
            pre = jnp.zeros((n_rows, d), BF16)
            pre, h_m, tail_m = _mixb_call(x, pre, jnp.zeros((1, 1, d), F32), jnp.zeros((1, SUBLANES, d), F32),
                                          *consts, n_seq=1, t_len=n_meta, row_off=off_m, name="rglru_meta")
            pre, h_p, tail_p = _mixb_call(x, pre, jnp.broadcast_to(h_m, (bp, 1, d)),
                                          jnp.broadcast_to(tail_m, (bp, SUBLANES, d)),
                                          *consts, n_seq=bp, t_len=seq, row_off=0, name="rglru_prompt")
            pre, h_s, tail_s = _mixb_call(x, pre, state_h[s].reshape(bs, 1, d),
                                          _front_pad_rows(state_conv_b[s], SUBLANES),
                                          *consts, n_seq=bs, t_len=dec, row_off=off_s, name="rglru_sample")
            new["h_p"].append(h_p.reshape(bp, d))
            new["cb_p"].append(tail_p[:, SUBLANES - (CONV_B - 1):])
            new["h_s"].append(h_s.reshape(bs, d))
            new["cb_s"].append(tail_s[:, SUBLANES - (CONV_B - 1):])
            wo, bo = b_wo[s].astype(BF16), b_bo[s].reshape(1, d)
        else:
            consts = (g_mix, c_win[s].astype(BF16), jnp.pad(c_conv_w[s], ((0, SUBLANES - CONV_C), (0, 0))))
            pre = jnp.zeros((n_rows, d), BF16)
            pre, tail_m = _mixc_call(x, pre, jnp.zeros((1, SUBLANES, d), F32), *consts, n_seq=1, t_len=n_meta,
                                     row_off=off_m, name="sconv_meta")
            pre, tail_p = _mixc_call(x, pre, jnp.broadcast_to(tail_m, (bp, SUBLANES, d)), *consts, n_seq=bp,
                                     t_len=seq, row_off=0, name="sconv_prompt")
            pre, tail_s = _mixc_call(x, pre, _front_pad_rows(state_conv_c[s], SUBLANES), *consts, n_seq=bs,
                                     t_len=dec, row_off=off_s, name="sconv_sample")
            new["cc_p"].append(tail_p[:, SUBLANES - (CONV_C - 1):])
            new["cc_s"].append(tail_s[:, SUBLANES - (CONV_C - 1):])
            wo, bo = c_wo[s].astype(BF16), zeros_bias
        x = _ffn_call(x, pre, wo, bo, norm_ffn[l].reshape(1, d), f_wg[l].astype(BF16), f_wu[l].astype(BF16),
                      f_wd[l].astype(BF16), gfin, final_norm=(l == depth - 1))

    y_prompt = x[:n_main].reshape(bp, seq, d)
    y_sample = x[off_s:off_m].reshape(bs, dec, d)
    st = lambda name: jnp.stack(new[name])
    return (y_prompt, y_sample, st("k_p"), st("v_p"), st("ki_p"), st("h_p"), st("cb_p"), st("cc_p"),
            st("k_s"), st("v_s"), st("ki_s"), st("h_s"), st("cb_s"), st("cc_s"))
```

```python
import functools

import jax
import jax.numpy as jnp
from jax import lax
from jax.experimental import pallas as pl
from jax.experimental.pallas import tpu as pltpu

F32 = jnp.float32
BF16 = jnp.bfloat16
I32 = jnp.int32
I16 = jnp.int16

CHUNK = 64
N_MIXERS = 3
N_HEADS = 16
N_KV_HEADS = 4
GROUP = N_HEADS // N_KV_HEADS
HEAD_DIM = 64
IDX_HEADS = 8
IDX_DIM = 64
TOP_K_MAX = 256
NEG_INF = -1e30
N_BUCKETS = 32
LRU_BLOCKS = 4
CONV_B = 4
CONV_C = 3
LRU_C = 8.0
RMS_EPS = 1e-6
BUCKET_STEPS = (12, 16, 23, 32, 46, 64, 91)
FAR_BUCKET = N_BUCKETS // 2 - 1

LANES = 128
SUBLANES = 8
ROW_PAD = 768
TM_PROJ = 256
TM_FFN = 384
TM_SEQ = 256
QB = 128
SEL_TILE = 2 * QB
INT_MIN = -2 ** 31
I16_MIN = -2 ** 15
PACK16 = 2 * SUBLANES
VMEM_LIMIT = 56 * 1024 * 1024


def _cparams(sem):
    return pltpu.CompilerParams(dimension_semantics=sem, vmem_limit_bytes=VMEM_LIMIT)


def _const_spec(shape):
    nd = len(shape)
    return pl.BlockSpec(shape, lambda *_: (0,) * nd, pipeline_mode=pl.Buffered(1))


def _rmsnorm(x, g):
    ms = jnp.mean(x * x, axis=-1, keepdims=True)
    return x * lax.rsqrt(ms + RMS_EPS) * g


def _dot(a, b):
    return jnp.dot(a, b, preferred_element_type=F32)


def _dot_nt(a, b):
    return lax.dot_general(a, b, (((1,), (1,)), ((), ())), preferred_element_type=F32)


def _ffn_kernel(x_ref, pre_ref, wo_ref, bo_ref, gf_ref, wg_ref, wu_ref, wd_ref, gfin_ref, out_ref, *,
                n_chunks, fc, final_norm):
    x1 = x_ref[...] + _dot(pre_ref[...], wo_ref[...]) + bo_ref[...]
    xn = _rmsnorm(x1, gf_ref[...]).astype(BF16)
    acc = x1
    for c in range(n_chunks):
        gt = _dot(xn, wg_ref[:, c * fc:(c + 1) * fc])
        up = _dot(xn, wu_ref[:, c * fc:(c + 1) * fc])
        hm = (gt * jax.nn.sigmoid(gt) * up).astype(BF16)
        acc = acc + _dot(hm, wd_ref[c * fc:(c + 1) * fc, :])
    if final_norm:
        acc = _rmsnorm(acc, gfin_ref[...])
    out_ref[...] = acc


def _ffn_call(x, pre, wo, bo, gf, wg, wu, wd, gfin, final_norm):
    n, d = x.shape
    dff = wg.shape[1]
    n_chunks = 2 if dff % (2 * LANES) == 0 else 1
    fc = dff // n_chunks
    row = lambda i: (i, 0)
    return pl.pallas_call(
        functools.partial(_ffn_kernel, n_chunks=n_chunks, fc=fc, final_norm=final_norm),
        grid=(n // TM_FFN,),
        in_specs=[pl.BlockSpec((TM_FFN, d), row), pl.BlockSpec((TM_FFN, pre.shape[1]), row),
                  _const_spec(wo.shape), _const_spec(bo.shape), _const_spec(gf.shape),
                  _const_spec(wg.shape), _const_spec(wu.shape), _const_spec(wd.shape), _const_spec(gfin.shape)],
        out_specs=pl.BlockSpec((TM_FFN, d), row),
        out_shape=jax.ShapeDtypeStruct((n, d), F32),
        compiler_params=_cparams(("parallel",)),
        name="outproj_swiglu",
    )(x, pre, wo, bo, gf, wg, wu, wd, gfin)


def _proj_kernel(x_ref, g_ref, wrow_ref, wt_ref, k_ref, v_ref, ki_ref, qt_ref, qit_ref, vt_ref, wit_ref, *,
                 n_kv, n_q, n_iq):
    xn = _rmsnorm(x_ref[...], g_ref[...]).astype(BF16)
    row = _dot(xn, wrow_ref[...])
    k_ref[...] = row[:, :n_kv]
    v_ref[...] = row[:, n_kv:2 * n_kv]
    ki_ref[...] = row[:, 2 * n_kv:]
    tt = _dot_nt(wt_ref[...], xn)
    qt_ref[...] = tt[:n_q].astype(BF16)
    qit_ref[...] = tt[n_q:n_q + n_iq].astype(BF16)
    vt_ref[...] = tt[n_q + n_iq:n_q + n_iq + n_kv].astype(BF16)
    wit_ref[...] = tt[n_q + n_iq + n_kv:n_q + n_iq + n_kv + IDX_HEADS] * (IDX_HEADS ** -0.5)


def _proj_call(x, g, wrow, wt):
    n, d = x.shape
    n_kv = N_KV_HEADS * HEAD_DIM
    n_q = N_HEADS * HEAD_DIM
    n_iq = IDX_HEADS * IDX_DIM
    row = lambda i: (i, 0)
    col = lambda i: (0, i)
    return pl.pallas_call(
        functools.partial(_proj_kernel, n_kv=n_kv, n_q=n_q, n_iq=n_iq),
        grid=(n // TM_PROJ,),
        in_specs=[pl.BlockSpec((TM_PROJ, d), row), _const_spec(g.shape), _const_spec(wrow.shape),
                  _const_spec(wt.shape)],
        out_specs=[pl.BlockSpec((TM_PROJ, n_kv), row), pl.BlockSpec((TM_PROJ, n_kv), row),
                   pl.BlockSpec((TM_PROJ, IDX_DIM), row), pl.BlockSpec((n_q, TM_PROJ), col),
                   pl.BlockSpec((n_iq, TM_PROJ), col), pl.BlockSpec((n_kv, TM_PROJ), col),
                   pl.BlockSpec((IDX_HEADS, TM_PROJ), col)],
        out_shape=[jax.ShapeDtypeStruct((n, n_kv), F32), jax.ShapeDtypeStruct((n, n_kv), F32),
                   jax.ShapeDtypeStruct((n, IDX_DIM), F32), jax.ShapeDtypeStruct((n_q, n), BF16),
                   jax.ShapeDtypeStruct((n_iq, n), BF16), jax.ShapeDtypeStruct((n_kv, n), BF16),
                   jax.ShapeDtypeStruct((IDX_HEADS, n), F32)],
        compiler_params=_cparams(("parallel",)),
        name="attn_proj",
    )(x, g, wrow, wt)


def _bias_kernel(tab_ref, tb_ref):
    h = pl.program_id(0)
    d = pl.program_id(1)
    kj = lax.broadcasted_iota(I32, (QB, QB), 0)
    qi = lax.broadcasted_iota(I32, (QB, QB), 1)
    rel = (d - 1) * QB + kj - qi
    n = jnp.abs(rel)
    large = jnp.full((QB, QB), N_BUCKETS // 4, I32)
    for s in BUCKET_STEPS:
        large = large + jnp.where(n >= s, 1, 0)
    bucket = jnp.where(rel > 0, N_BUCKETS // 2, 0) + jnp.where(n < N_BUCKETS // 4, n, large)
    val = jnp.zeros((QB, QB), F32)
    for b in range(N_BUCKETS):
        val = jnp.where(bucket == b, tab_ref[b, h], val)
    tb_ref[0] = val


def _bias_call(table):
    return pl.pallas_call(
        _bias_kernel,
        grid=(N_HEADS, 2),
        in_specs=[pl.BlockSpec(memory_space=pltpu.SMEM)],
        out_specs=pl.BlockSpec((1, QB, QB), lambda h, d: (h, d, 0)),
        out_shape=jax.ShapeDtypeStruct((N_HEADS, 2 * QB, QB), F32),
        compiler_params=_cparams(("arbitrary", "arbitrary")),
        name="rel_bias_tiles",
    )(table)


def _attn_kernel(kend_ref, qt_ref, qit_ref, wit_ref, kf_ref, vtf_ref, kif_ref, tb_ref, tab_ref, o_ref,
                 key_scr, hi_scr, lo_scr, lo2_scr, ls_scr, oacc_scr, *, diag_off, kstart, k_sel, has_prev,
                 row_bits):
    diag = pl.program_id(1) + diag_off
    nt = diag + 1
    nst = (nt + SEL_TILE // QB - 1) // (SEL_TILE // QB)
    kend = kend_ref[0]
    qi_cat = jnp.concatenate([qit_ref[h * IDX_DIM:(h + 1) * IDX_DIM, :] for h in range(IDX_HEADS)], axis=1)
    wi = wit_ref[...]
    sel_iota = lax.broadcasted_iota(I32, (SEL_TILE, QB), 0)

    def tile_start(t):
        return pl.multiple_of(t * QB, QB)

    def sel_start(t):
        return pl.multiple_of(t * SEL_TILE, SEL_TILE)

    def score_tile(t, carry):
        r0 = sel_start(t)
        s = _dot(kif_ref[0, pl.ds(r0, SEL_TILE), :], qi_cat)
        sc = jnp.zeros((SEL_TILE, QB), F32)
        for h in range(IDX_HEADS):
            sc = sc + wi[h:h + 1, :] * jnp.maximum(s[:, h * QB:(h + 1) * QB], 0.0)
        bits = lax.bitcast_convert_type(sc + 0.0, I32)
        key = jnp.where(bits >= 0, bits, bits ^ 0x7FFFFFFF)
        rows = r0 + sel_iota
        adm = (rows >= kstart) & (rows < kend)
        key = jnp.where(adm, key, INT_MIN)
        key_scr[pl.ds(r0, SEL_TILE), :] = key
        hi_scr[pl.ds(r0, SEL_TILE), :] = (key >> 16).astype(I16)
        lo_scr[pl.ds(r0, SEL_TILE), :] = ((key & 0xFFFF) + I16_MIN).astype(I16)
        return carry

    lax.fori_loop(0, nst, score_tile, 0)

    @pl.when(nst % 2 == 1)
    def _():
        blank = jnp.full((SEL_TILE, QB), I16_MIN, I16)
        hi_scr[pl.ds(sel_start(nst), SEL_TILE), :] = blank
        lo_scr[pl.ds(sel_start(nst), SEL_TILE), :] = blank

    n_pair = (nst + 1) // 2
    pair_vregs = 2 * SEL_TILE // PACK16

    def pair_start(t):
        return pl.multiple_of(t * 2 * SEL_TILE, 2 * SEL_TILE)

    def as_packed(v):
        return jnp.broadcast_to(v, (PACK16, QB)).astype(I16)

    def count16(src_scr, cand):
        c16 = as_packed(cand)[None]

        def body(t, acc):
            v = src_scr[pl.ds(pair_start(t), 2 * SEL_TILE), :].reshape(pair_vregs, PACK16, QB)
            ind = jnp.where(v >= c16, jnp.ones((), BF16), jnp.zeros((), BF16))
            parts = [ind[i] for i in range(pair_vregs)]
            while len(parts) > 1:
                parts = [parts[i] + parts[i + 1] for i in range(0, len(parts), 2)]
            return acc + parts[0]

        acc = lax.fori_loop(0, n_pair, body, jnp.zeros((PACK16, QB), BF16))
        return acc.astype(F32).sum(axis=0, keepdims=True)

    def search16(src_scr, base, c_start):
        def step(b, carry):
            t_acc, c_acc = carry
            cand = t_acc + lax.shift_left(jnp.int32(1), 15 - b)
            c = base + count16(src_scr, cand)
            ok = c >= k_sel
            return jnp.where(ok, cand, t_acc), jnp.where(ok, c, c_acc)

        return lax.fori_loop(0, 16, step, (jnp.full((1, QB), I16_MIN, I32), c_start))

    def count(pred):
        def body(t, acc):
            r0 = sel_start(t)
            ind = jnp.where(pred(key_scr[pl.ds(r0, SEL_TILE), :], r0 + sel_iota), 1, 0)
            return acc + ind.reshape(SEL_TILE // SUBLANES, SUBLANES, QB).sum(axis=0)
        acc = lax.fori_loop(0, nst, body, jnp.zeros((SUBLANES, QB), I32))
        return acc.sum(axis=0, keepdims=True)

    n_all = (n_pair * 2 * SEL_TILE).astype(F32)
    thr_hi, c_hi = search16(hi_scr, 0.0, jnp.full((1, QB), n_all, F32))
    above = jnp.where(thr_hi == -I16_MIN - 1, 0.0, count16(hi_scr, thr_hi + 1))
    hi16 = as_packed(thr_hi)[None]

    def low_tile(t, carry):
        rows = pl.ds(pair_start(t), 2 * SEL_TILE)
        hi = hi_scr[rows, :].reshape(pair_vregs, PACK16, QB)
        lo = lo_scr[rows, :].reshape(pair_vregs, PACK16, QB)
        lo2_scr[rows, :] = jnp.where(hi == hi16, lo, jnp.full((), I16_MIN, I16)).reshape(2 * SEL_TILE, QB)
        return carry

    lax.fori_loop(0, n_pair, low_tile, 0)
    thr_lo, c_ge = search16(lo2_scr, above, c_hi)
    thr = thr_hi * 65536 + (thr_lo - I16_MIN)

    tied = (c_ge > k_sel) & (thr > INT_MIN)
    big = jnp.full((1, QB), 2 ** row_bits, I32)

    def tie_limit():
        need = k_sel - count(lambda kt, rows: kt > thr)

        def lim_step(b, lim):
            cand = lim + lax.shift_left(jnp.int32(1), row_bits - 1 - b)
            c = count(lambda kt, rows: (kt == thr) & (rows < cand))
            return jnp.where(c < need, cand, lim)

        lim = lax.fori_loop(0, row_bits, lim_step, jnp.zeros((1, QB), I32))
        return jnp.where(tied, lim, big)

    rlim = lax.cond(jnp.max(tied.astype(I32)) > 0, tie_limit, lambda: big)
    rlim = jnp.where(thr > INT_MIN, rlim, -1)

    gq = GROUP * QB
    qg = [jnp.concatenate([qt_ref[(g * GROUP + r) * HEAD_DIM:(g * GROUP + r + 1) * HEAD_DIM, :]
                           for r in range(GROUP)], axis=1) for g in range(N_KV_HEADS)]

    def logits_rows(r0, rows, m8, bias_of_head):
        kt = key_scr[pl.ds(r0, rows), :]
        keep = (kt > thr) | ((kt == thr) & (sel_iota[:rows] <= rlim - r0))
        new_m8 = []
        for g in range(N_KV_HEADS):
            lg = _dot(kf_ref[0, g, pl.ds(r0, rows), :], qg[g])
            parts = []
            for r in range(GROUP):
                part = lg[:, r * QB:(r + 1) * QB] + bias_of_head(g * GROUP + r)
                parts.append(jnp.where(keep, part, NEG_INF))
            lg = jnp.concatenate(parts, axis=1)
            ls_scr[pl.ds(r0, rows), g * gq:(g + 1) * gq] = lg
            new_m8.append(jnp.maximum(m8[g], lg.reshape(rows // SUBLANES, SUBLANES, gq).max(axis=0)))
        return tuple(new_m8)

    far_bias = lambda h: tab_ref[FAR_BUCKET, h]
    m8 = tuple(jnp.full((SUBLANES, gq), NEG_INF, F32) for _ in range(N_KV_HEADS))
    n_far = diag - 1 if has_prev else diag
    m8 = lax.fori_loop(0, n_far // 2, lambda t, m: logits_rows(sel_start(t), SEL_TILE, m, far_bias), m8)
    m8 = lax.cond(n_far % 2 == 1, lambda m: logits_rows(tile_start(n_far - 1), QB, m, far_bias), lambda m: m, m8)
    if has_prev:
        m8 = logits_rows(tile_start(diag - 1), 2 * QB, m8, lambda h: tb_ref[h])
    else:
        m8 = logits_rows(tile_start(diag), QB, m8, lambda h: tb_ref[h, QB:, :])
    m = [jnp.max(m8[g], axis=0, keepdims=True) for g in range(N_KV_HEADS)]

    @pl.when(nt % 2 == 1)
    def _():
        ls_scr[pl.ds(tile_start(nt), QB), :] = jnp.full((QB, N_HEADS * QB), NEG_INF, F32)

    oacc_scr[...] = jnp.zeros(oacc_scr.shape, F32)

    def pv_tile(t, l8):
        r0 = sel_start(t)
        new_l8 = []
        for g in range(N_KV_HEADS):
            p = jnp.exp(ls_scr[pl.ds(r0, SEL_TILE), g * gq:(g + 1) * gq] - m[g])
            oacc_scr[g] += _dot(vtf_ref[0, g * HEAD_DIM:(g + 1) * HEAD_DIM, pl.ds(r0, SEL_TILE)], p.astype(BF16))
            new_l8.append(l8[g] + p.reshape(SEL_TILE // SUBLANES, SUBLANES, gq).sum(axis=0))
        return tuple(new_l8)

    l8 = lax.fori_loop(0, nst, pv_tile, tuple(jnp.zeros((SUBLANES, gq), F32) for _ in range(N_KV_HEADS)))
    out_rows = []
    for g in range(N_KV_HEADS):
        o_g = oacc_scr[g] / jnp.sum(l8[g], axis=0, keepdims=True)
        out_rows.extend(o_g[:, r * QB:(r + 1) * QB] for r in range(GROUP))
    o_ref[...] = jnp.concatenate(out_rows, axis=0).T.astype(BF16)


def _attn_call(kend, qt, qit, wit, kf, vtf, kif, tb, table, *, n_out_rows, diag_off, kstart, k_sel, name):
    n_seq, tkf = kf.shape[0], kf.shape[2]
    assert tkf % SEL_TILE == 0 and vtf.shape[2] == tkf and kif.shape[1] == tkf
    rows16 = -(-tkf // (2 * SEL_TILE)) * 2 * SEL_TILE
    assert rows16 // PACK16 <= 256, "per-slot bf16 counts must stay exact"
    n_qb = kend.shape[0]
    has_prev = diag_off >= 1
    row_bits = max(1, (tkf - 1).bit_length())
    lane_blk = lambda s, i: (0, s * n_qb + i)
    feat = lambda a: pl.BlockSpec((a.shape[0], QB), lane_blk)
    return pl.pallas_call(
        functools.partial(_attn_kernel, diag_off=diag_off, kstart=kstart, k_sel=k_sel, has_prev=has_prev,
                          row_bits=row_bits),
        grid=(n_seq, n_qb),
        in_specs=[pl.BlockSpec((1, 1, QB), lambda s, i: (i, 0, 0)), feat(qt), feat(qit), feat(wit),
                  pl.BlockSpec((1, N_KV_HEADS, tkf, HEAD_DIM), lambda s, i: (s, 0, 0, 0)),
                  pl.BlockSpec((1, N_KV_HEADS * HEAD_DIM, tkf), lambda s, i: (s, 0, 0)),
                  pl.BlockSpec((1, tkf, IDX_DIM), lambda s, i: (s, 0, 0)),
                  _const_spec(tb.shape), pl.BlockSpec(memory_space=pltpu.SMEM)],
        out_specs=pl.BlockSpec((QB, N_HEADS * HEAD_DIM), lambda s, i: (s * n_qb + i, 0)),
        out_shape=jax.ShapeDtypeStruct((n_out_rows, N_HEADS * HEAD_DIM), BF16),
        scratch_shapes=[pltpu.VMEM((tkf, QB), I32)] + [pltpu.VMEM((rows16, QB), I16)] * 3
                       + [pltpu.VMEM((tkf, N_HEADS * QB), F32), pltpu.VMEM((N_KV_HEADS, HEAD_DIM, GROUP * QB), F32)],
        compiler_params=_cparams(("arbitrary", "arbitrary")),
        name=name,
    )(kend, qt, qit, wit, kf, vtf, kif, tb, table)


def _softplus(x):
    return jnp.maximum(x, 0.0) + jnp.log1p(jnp.exp(-jnp.abs(x)))


def _mixb_kernel(x_ref, h0_ref, buf_ref, g_ref, wy_ref, by_ref, wx_ref, bx_ref, cw_ref, cb_ref, wr_ref, br_ref,
                 wi_ref, bi_ref, lam_ref, prein_ref, pre_ref, hlast_ref, tail_ref, xe_scr, h_scr, *, tm):
    del prein_ref

    @pl.when(pl.program_id(1) == 0)
    def _():
        xe_scr[0:SUBLANES, :] = buf_ref[0]
        h_scr[...] = h0_ref[0]

    xn = _rmsnorm(x_ref[...], g_ref[...]).astype(BF16)
    y = jax.nn.gelu(_dot(xn, wy_ref[...]) + by_ref[...])
    xe_scr[SUBLANES:SUBLANES + tm, :] = _dot(xn, wx_ref[...]) + bx_ref[...]
    xc = cb_ref[...] + cw_ref[0:1, :] * xe_scr[SUBLANES - 3:SUBLANES - 3 + tm, :]
    for j in range(1, CONV_B):
        xc = xc + cw_ref[j:j + 1, :] * xe_scr[SUBLANES - 3 + j:SUBLANES - 3 + j + tm, :]
    xcb = xc.astype(BF16)
    blk = xc.shape[1] // LRU_BLOCKS
    r_pre = jnp.concatenate([_dot(xcb[:, n * blk:(n + 1) * blk], wr_ref[n]) for n in range(LRU_BLOCKS)], axis=1)
    i_pre = jnp.concatenate([_dot(xcb[:, n * blk:(n + 1) * blk], wi_ref[n]) for n in range(LRU_BLOCKS)], axis=1)
    r = jax.nn.sigmoid(r_pre + br_ref[...])
    ig = jax.nn.sigmoid(i_pre + bi_ref[...])
    log_a = -LRU_C * r * _softplus(-lam_ref[...])
    a = jnp.exp(log_a)
    u = jnp.sqrt(jnp.tanh(-log_a) * (1.0 + a * a)) * (ig * xc)
    rows = lax.broadcasted_iota(I32, a.shape, 0)
    s = 1
    while s < tm:
        a_sh = jnp.where(rows >= s, pltpu.roll(a, s, 0), 1.0)
        u_sh = jnp.where(rows >= s, pltpu.roll(u, s, 0), 0.0)
        u = a * u_sh + u
        a = a * a_sh
        s *= 2
    h = a * h_scr[...] + u
    pre_ref[...] = (h * y).astype(BF16)
    h_scr[...] = h[tm - 1:tm, :]
    hlast_ref[0] = h[tm - 1:tm, :]
    tail = xe_scr[tm:tm + SUBLANES, :]
    xe_scr[0:SUBLANES, :] = tail
    tail_ref[0] = tail


def _mixc_kernel(x_ref, buf_ref, g_ref, win_ref, cw_ref, prein_ref, pre_ref, tail_ref, pe_scr, *, tm):
    del prein_ref

    @pl.when(pl.program_id(1) == 0)
    def _():
        pe_scr[0:SUBLANES, :] = buf_ref[0]

    d = x_ref.shape[1]
    xn = _rmsnorm(x_ref[...], g_ref[...]).astype(BF16)
    z = _dot(xn, win_ref[...])
    pe_scr[SUBLANES:SUBLANES + tm, :] = z[:, d:2 * d] * z[:, 2 * d:]
    conv = cw_ref[0:1, :] * pe_scr[SUBLANES - 2:SUBLANES - 2 + tm, :]
    for j in range(1, CONV_C):
        conv = conv + cw_ref[j:j + 1, :] * pe_scr[SUBLANES - 2 + j:SUBLANES - 2 + j + tm, :]
    pre_ref[...] = (z[:, :d] * conv).astype(BF16)
    tail = pe_scr[tm:tm + SUBLANES, :]
    pe_scr[0:SUBLANES, :] = tail
    tail_ref[0] = tail


def _seq_specs(n_seq, t_len, row_off, d):
    tm = min(TM_SEQ, t_len)
    n_tt = t_len // tm
    off = row_off // tm
    xrow = pl.BlockSpec((tm, d), lambda s, j: (off + s * n_tt + j, 0))
    state = lambda rows: pl.BlockSpec((1, rows, d), lambda s, j: (s, 0, 0))
    return tm, n_tt, xrow, state


def _mixb_call(x, pre_buf, h0, buf, g, wy, by, wx, bx, cw, cb, wr, br, wi, bi, lam, *, n_seq, t_len, row_off,
               name):
    d = x.shape[1]
    tm, n_tt, xrow, state = _seq_specs(n_seq, t_len, row_off, d)
    consts = (g, wy, by, wx, bx, cw, cb, wr, br, wi, bi, lam)
    return pl.pallas_call(
        functools.partial(_mixb_kernel, tm=tm),
        grid=(n_seq, n_tt),
        in_specs=[xrow, state(1), state(SUBLANES)] + [_const_spec(c.shape) for c in consts]
                 + [pl.BlockSpec(memory_space=pl.ANY)],
        out_specs=[xrow, state(1), state(SUBLANES)],
        out_shape=[jax.ShapeDtypeStruct(pre_buf.shape, BF16), jax.ShapeDtypeStruct((n_seq, 1, d), F32),
                   jax.ShapeDtypeStruct((n_seq, SUBLANES, d), F32)],
        scratch_shapes=[pltpu.VMEM((tm + SUBLANES, d), F32), pltpu.VMEM((1, d), F32)],
        input_output_aliases={3 + len(consts): 0},
        compiler_params=_cparams(("arbitrary", "arbitrary")),
        name=name,
    )(x, h0, buf, *consts, pre_buf)


def _mixc_call(x, pre_buf, buf, g, win, cw, *, n_seq, t_len, row_off, name):
    d = x.shape[1]
    tm, n_tt, xrow, state = _seq_specs(n_seq, t_len, row_off, d)
    consts = (g, win, cw)
    return pl.pallas_call(
        functools.partial(_mixc_kernel, tm=tm),
        grid=(n_seq, n_tt),
        in_specs=[xrow, state(SUBLANES)] + [_const_spec(c.shape) for c in consts]
                 + [pl.BlockSpec(memory_space=pl.ANY)],
        out_specs=[xrow, state(SUBLANES)],
        out_shape=[jax.ShapeDtypeStruct(pre_buf.shape, BF16), jax.ShapeDtypeStruct((n_seq, SUBLANES, d), F32)],
        scratch_shapes=[pltpu.VMEM((tm + SUBLANES, d), F32)],
        input_output_aliases={2 + len(consts): 0},
        compiler_params=_cparams(("arbitrary", "arbitrary")),
        name=name,
    )(x, buf, *consts, pre_buf)


def _front_pad_rows(a, rows):
    return jnp.pad(a, ((0, 0), (rows - a.shape[1], 0), (0, 0)))


def _frame(parts, axis):
    f = jnp.concatenate(parts, axis=axis).astype(BF16)
    pad = [(0, 0)] * f.ndim
    pad[axis] = (0, -f.shape[axis] % SEL_TILE)
    return jnp.pad(f, pad)


def _k_frame(parts):
    f = _frame(parts, 1)
    n_seq, rows = f.shape[0], f.shape[1]
    return f.reshape(n_seq, rows, N_KV_HEADS, HEAD_DIM).transpose(0, 2, 1, 3)


def _seq_major(a, n_seq, q_len):
    return a.reshape(a.shape[0], n_seq, q_len).transpose(1, 0, 2)


def _lane_pad_cols(a, n_seq, q_len):
    f = a.shape[0]
    a = jnp.pad(a.reshape(f, n_seq, q_len), ((0, 0), (0, 0), (0, QB - q_len)))
    return a.reshape(f, n_seq * QB)


def kernel(x_prompt, x_sample, cache_k, cache_v, cache_kidx, state_h, state_conv_b, state_conv_c, meta_tokens,
           rel_bias, norm_mix, norm_ffn, norm_final, a_wq, a_wk, a_wv, a_wo, a_wiq, a_wik, a_wiw, b_wy, b_by, b_wx,
           b_bx, b_conv_w, b_conv_b, b_wr, b_br, b_wi, b_bi, b_lam, b_wo, b_bo, c_win, c_conv_w, c_wo, f_wg, f_wu,
           f_wd):
    bp, seq, d = x_prompt.shape
    bs, dec = x_sample.shape[0], x_sample.shape[1]
    past = cache_k.shape[2]
    n_meta = meta_tokens.shape[0]
    depth = norm_mix.shape[0]
    assert seq % QB == 0 and seq % TM_SEQ == 0 and past % QB == 0 and dec <= QB and n_meta <= QB
    assert dec % SUBLANES == 0 and n_meta % SUBLANES == 0 and seq % dec == 0 and (bp * seq + bs * dec) % n_meta == 0

    n_main, n_samp = bp * seq, bs * dec
    off_s, off_m = n_main, n_main + n_samp
    n_valid = off_m + n_meta
    n_rows = -(-n_valid // ROW_PAD) * ROW_PAD
    x = jnp.concatenate([x_prompt.reshape(n_main, d), x_sample.reshape(n_samp, d), meta_tokens.astype(F32),
                         jnp.zeros((n_rows - n_valid, d), F32)], axis=0)

    k_sel_p = min(TOP_K_MAX, seq // 4)
    k_sel_s = min(TOP_K_MAX, (past + dec) // 4)
    n_qb = seq // QB
    kv_dim = N_KV_HEADS * HEAD_DIM

    tau = jnp.arange(seq, dtype=I32)
    kend_main = (QB + CHUNK * (tau // CHUNK + 1)).reshape(n_qb, 1, QB)
    lane = jnp.arange(QB, dtype=I32)
    kend_samp = jnp.minimum(past + dec, CHUNK * ((past + jnp.minimum(lane, dec - 1)) // CHUNK + 1)).reshape(1, 1, QB)
    kend_meta = jnp.full((1, 1, QB), n_meta, I32)

    tb = _bias_call(rel_bias.astype(F32))
    zeros_bias = jnp.zeros((1, d), F32)
    gfin = norm_final.reshape(1, d).astype(F32)

    new = {name: [] for name in ("k_p", "v_p", "ki_p", "h_p", "cb_p", "cc_p", "k_s", "v_s", "ki_s", "h_s", "cb_s",
                                 "cc_s")}
    for l in range(depth):
        s = l // N_MIXERS
        g_mix = norm_mix[l].reshape(1, d)
        if l % N_MIXERS == 0:
            wrow = jnp.concatenate([a_wk[s], a_wv[s], a_wik[s]], axis=1).astype(BF16)
            wt = jnp.concatenate([a_wq[s].T * HEAD_DIM ** -0.5, a_wiq[s].T * IDX_DIM ** -0.5, a_wv[s].T, a_wiw[s].T,
                                  jnp.zeros((SUBLANES, d), F32)], axis=0).astype(BF16)
            k, v, ki, qt, qit, vt, wit = _proj_call(x, g_mix, wrow, wt)

            k_meta, v_meta, ki_meta = (a[off_m:n_valid] for a in (k, v, ki))
            k_main, v_main, ki_main = (a[:n_main].reshape(bp, seq, -1) for a in (k, v, ki))
            k_samp, v_samp, ki_samp = (a[off_s:off_m].reshape(bs, dec, -1) for a in (k, v, ki))
            bc = lambda a: jnp.broadcast_to(a[None], (bp,) + a.shape)
            new["k_p"].append(jnp.concatenate([bc(k_meta), k_main], axis=1).reshape(bp, n_meta + seq, N_KV_HEADS,
                                                                                   HEAD_DIM))
            new["v_p"].append(jnp.concatenate([bc(v_meta), v_main], axis=1).reshape(bp, n_meta + seq, N_KV_HEADS,
                                                                                   HEAD_DIM))
            new["ki_p"].append(jnp.concatenate([bc(ki_meta), ki_main], axis=1))
            new["k_s"].append(k_samp.reshape(bs, dec, N_KV_HEADS, HEAD_DIM))
            new["v_s"].append(v_samp.reshape(bs, dec, N_KV_HEADS, HEAD_DIM))
            new["ki_s"].append(ki_samp)

            front = lambda a: jnp.pad(bc(a), ((0, 0), (QB - n_meta, 0), (0, 0)))
            vt_meta = vt[:, off_m:n_valid]
            vtf_main = _frame([jnp.pad(bc(vt_meta), ((0, 0), (0, 0), (QB - n_meta, 0))),
                               _seq_major(vt[:, :n_main], bp, seq)], 2)
            pre = _attn_call(
                kend_main, qt, qit, wit, _k_frame([front(k_meta), k_main]), vtf_main,
                _frame([front(ki_meta), ki_main], 1), tb, rel_bias,
                n_out_rows=n_rows, diag_off=1, kstart=QB - n_meta, k_sel=k_sel_p, name="dsa_attn_prompt")
            vtf_samp = _frame([cache_v[s].reshape(bs, past, kv_dim).transpose(0, 2, 1),
                               _seq_major(vt[:, off_s:off_m], bs, dec)], 2)
            o_samp = _attn_call(
                kend_samp, _lane_pad_cols(qt[:, off_s:off_m], bs, dec),
                _lane_pad_cols(qit[:, off_s:off_m], bs, dec), _lane_pad_cols(wit[:, off_s:off_m], bs, dec),
                _k_frame([cache_k[s].reshape(bs, past, kv_dim), k_samp]), vtf_samp,
                _frame([cache_kidx[s], ki_samp], 1), tb, rel_bias,
                n_out_rows=bs * QB, diag_off=past // QB, kstart=0, k_sel=k_sel_s, name="dsa_attn_sample")
            o_meta = _attn_call(
                kend_meta, _lane_pad_cols(qt[:, off_m:n_valid], 1, n_meta),
                _lane_pad_cols(qit[:, off_m:n_valid], 1, n_meta), _lane_pad_cols(wit[:, off_m:n_valid], 1, n_meta),
                _k_frame([k_meta[None]]), _frame([vt_meta[None]], 2), _frame([ki_meta[None]], 1), tb, rel_bias,
                n_out_rows=QB, diag_off=0, kstart=0, k_sel=k_sel_p, name="dsa_attn_meta")
            tail_rows = jnp.concatenate([o_samp.reshape(bs, QB, d)[:, :dec].reshape(n_samp, d), o_meta[:n_meta],
                                         jnp.zeros((n_rows - n_valid, d), BF16)], axis=0)
            pre = lax.dynamic_update_slice(pre, tail_rows, (off_s, 0))
            wo, bo = a_wo[s].astype(BF16), zeros_bias
        elif l % N_MIXERS == 1:
            consts = (g_mix, b_wy[s].astype(BF16), b_by[s].reshape(1, d), b_wx[s].astype(BF16), b_bx[s].reshape(1, d),
                      jnp.pad(b_conv_w[s], ((0, SUBLANES - CONV_B), (0, 0))), b_conv_b[s].reshape(1, d),
                      b_wr[s].astype(BF16), b_br[s].reshape(1, d), b_wi[s].astype(BF16), b_bi[s].reshape(1, d),
                      b_lam[s].reshape(1, d))
            pre = jnp.zeros((n_rows, d), BF16)
            pre, h_m, tail_m = _mixb_call(x, pre, jnp.zeros((1, 1, d), F32), jnp.zeros((1, SUBLANES, d), F32),
                                          *consts, n_seq=1, t_len=n_meta, row_off=off_m, name="rglru_meta")
            pre, h_p, tail_p = _mixb_call(x, pre, jnp.broadcast_to(h_m, (bp, 1, d)),
                                          jnp.broadcast_to(tail_m, (bp, SUBLANES, d)),
                                          *consts, n_seq=bp, t_len=seq, row_off=0, name="rglru_prompt")
            pre, h_s, tail_s = _mixb_call(x, pre, state_h[s].reshape(bs, 1, d),
                                          _front_pad_rows(state_conv_b[s], SUBLANES),
                                          *consts, n_seq=bs, t_len=dec, row_off=off_s, name="rglru_sample")
            new["h_p"].append(h_p.reshape(bp, d))
            new["cb_p"].append(tail_p[:, SUBLANES - (CONV_B - 1):])
            new["h_s"].append(h_s.reshape(bs, d))
            new["cb_s"].append(tail_s[:, SUBLANES - (CONV_B - 1):])
            wo, bo = b_wo[s].astype(BF16), b_bo[s].reshape(1, d)
        else:
            consts = (g_mix, c_win[s].astype(BF16), jnp.pad(c_conv_w[s], ((0, SUBLANES - CONV_C), (0, 0))))
            pre = jnp.zeros((n_rows, d), BF16)
            pre, tail_m = _mixc_call(x, pre, jnp.zeros((1, SUBLANES, d), F32), *consts, n_seq=1, t_len=n_meta,
                                     row_off=off_m, name="sconv_meta")
            pre, tail_p = _mixc_call(x, pre, jnp.broadcast_to(tail_m, (bp, SUBLANES, d)), *consts, n_seq=bp,
                                     t_len=seq, row_off=0, name="sconv_prompt")
            pre, tail_s = _mixc_call(x, pre, _front_pad_rows(state_conv_c[s], SUBLANES), *consts, n_seq=bs,
                                     t_len=dec, row_off=off_s, name="sconv_sample")
            new["cc_p"].append(tail_p[:, SUBLANES - (CONV_C - 1):])
            new["cc_s"].append(tail_s[:, SUBLANES - (CONV_C - 1):])
            wo, bo = c_wo[s].astype(BF16), zeros_bias
        x = _ffn_call(x, pre, wo, bo, norm_ffn[l].reshape(1, d), f_wg[l].astype(BF16), f_wu[l].astype(BF16),
                      f_wd[l].astype(BF16), gfin, final_norm=(l == depth - 1))

    y_prompt = x[:n_main].reshape(bp, seq, d)
    y_sample = x[off_s:off_m].reshape(bs, dec, d)
    st = lambda name: jnp.stack(new[name])
    return (y_prompt, y_sample, st("k_p"), st("v_p"), st("ki_p"), st("h_p"), st("cb_p"), st("cc_p"),
            st("k_s"), st("v_s"), st("ki_s"), st("h_s"), st("cb_s"), st("cc_s"))
```

```python
import functools

import jax
import jax.numpy as jnp
from jax import lax
from jax.experimental import pallas as pl
from jax.experimental.pallas import tpu as pltpu

F32 = jnp.float32
BF16 = jnp.bfloat16
I32 = jnp.int32
I16 = jnp.int16

CHUNK = 64
N_MIXERS = 3
N_HEADS = 16
N_KV_HEADS = 4
GROUP = N_HEADS // N_KV_HEADS
HEAD_DIM = 64
IDX_HEADS = 8
IDX_DIM = 64
TOP_K_MAX = 256
NEG_INF = -1e30
N_BUCKETS = 32
LRU_BLOCKS = 4
CONV_B = 4
CONV_C = 3
LRU_C = 8.0
RMS_EPS = 1e-6
BUCKET_STEPS = (12, 16, 23, 32, 46, 64, 91)
FAR_BUCKET = N_BUCKETS // 2 - 1

LANES = 128
SUBLANES = 8
ROW_PAD = 768
TM_PROJ = 256
TM_FFN = 384
TM_SEQ = 256
QB = 128
SEL_TILE = 2 * QB
INT_MIN = -2 ** 31
I16_MIN = -2 ** 15
PACK16 = 2 * SUBLANES
VMEM_LIMIT = 56 * 1024 * 1024


def _cparams(sem):
    return pltpu.CompilerParams(dimension_semantics=sem, vmem_limit_bytes=VMEM_LIMIT)


def _const_spec(shape):
    nd = len(shape)
    return pl.BlockSpec(shape, lambda *_: (0,) * nd, pipeline_mode=pl.Buffered(1))


def _rmsnorm(x, g):
    ms = jnp.mean(x * x, axis=-1, keepdims=True)
    return x * lax.rsqrt(ms + RMS_EPS) * g


def _dot(a, b):
    return jnp.dot(a, b, preferred_element_type=F32)


def _dot_nt(a, b):
    return lax.dot_general(a, b, (((1,), (1,)), ((), ())), preferred_element_type=F32)


def _ffn_kernel(x_ref, pre_ref, wo_ref, bo_ref, gf_ref, wg_ref, wu_ref, wd_ref, gfin_ref, out_ref, *,
                n_chunks, fc, final_norm):
    x1 = x_ref[...] + _dot(pre_ref[...], wo_ref[...]) + bo_ref[...]
    xn = _rmsnorm(x1, gf_ref[...]).astype(BF16)
    acc = x1
    for c in range(n_chunks):
        gt = _dot(xn, wg_ref[:, c * fc:(c + 1) * fc])
        up = _dot(xn, wu_ref[:, c * fc:(c + 1) * fc])
        hm = (gt * jax.nn.sigmoid(gt) * up).astype(BF16)
        acc = acc + _dot(hm, wd_ref[c * fc:(c + 1) * fc, :])
    if final_norm:
        acc = _rmsnorm(acc, gfin_ref[...])
    out_ref[...] = acc


def _ffn_call(x, pre, wo, bo, gf, wg, wu, wd, gfin, final_norm):
    n, d = x.shape
    dff = wg.shape[1]
    n_chunks = 2 if dff % (2 * LANES) == 0 else 1
    fc = dff // n_chunks
    row = lambda i: (i, 0)
    return pl.pallas_call(
        functools.partial(_ffn_kernel, n_chunks=n_chunks, fc=fc, final_norm=final_norm),
        grid=(n // TM_FFN,),
        in_specs=[pl.BlockSpec((TM_FFN, d), row), pl.BlockSpec((TM_FFN, pre.shape[1]), row),
                  _const_spec(wo.shape), _const_spec(bo.shape), _const_spec(gf.shape),
                  _const_spec(wg.shape), _const_spec(wu.shape), _const_spec(wd.shape), _const_spec(gfin.shape)],
        out_specs=pl.BlockSpec((TM_FFN, d), row),
        out_shape=jax.ShapeDtypeStruct((n, d), F32),
        compiler_params=_cparams(("parallel",)),
        name="outproj_swiglu",
    )(x, pre, wo, bo, gf, wg, wu, wd, gfin)


def _proj_kernel(x_ref, g_ref, wrow_ref, wt_ref, k_ref, v_ref, ki_ref, qt_ref, qit_ref, vt_ref, wit_ref, *,
                 n_kv, n_q, n_iq):
    xn = _rmsnorm(x_ref[...], g_ref[...]).astype(BF16)
    row = _dot(xn, wrow_ref[...])
    k_ref[...] = row[:, :n_kv]
    v_ref[...] = row[:, n_kv:2 * n_kv]
    ki_ref[...] = row[:, 2 * n_kv:]
    tt = _dot_nt(wt_ref[...], xn)
    for b in range(TM_PROJ // QB):
        blk = tt[:, b * QB:(b + 1) * QB]
        qt_ref[b] = blk[:n_q].astype(BF16)
        qit_ref[b] = blk[n_q:n_q + n_iq].astype(BF16)
        vt_ref[b] = blk[n_q + n_iq:n_q + n_iq + n_kv].astype(BF16)
        wit_ref[b] = blk[n_q + n_iq + n_kv:n_q + n_iq + n_kv + IDX_HEADS] * (IDX_HEADS ** -0.5)


def _proj_call(x, g, wrow, wt):
    n, d = x.shape
    n_kv = N_KV_HEADS * HEAD_DIM
    n_q = N_HEADS * HEAD_DIM
    n_iq = IDX_HEADS * IDX_DIM
    row = lambda i: (i, 0)
    blocks = TM_PROJ // QB
    feat_spec = lambda f: pl.BlockSpec((blocks, f, QB), lambda i: (i, 0, 0))
    feat_shape = lambda f, dt: jax.ShapeDtypeStruct((n // QB, f, QB), dt)
    return pl.pallas_call(
        functools.partial(_proj_kernel, n_kv=n_kv, n_q=n_q, n_iq=n_iq),
        grid=(n // TM_PROJ,),
        in_specs=[pl.BlockSpec((TM_PROJ, d), row), _const_spec(g.shape), _const_spec(wrow.shape),
                  _const_spec(wt.shape)],
        out_specs=[pl.BlockSpec((TM_PROJ, n_kv), row), pl.BlockSpec((TM_PROJ, n_kv), row),
                   pl.BlockSpec((TM_PROJ, IDX_DIM), row), feat_spec(n_q), feat_spec(n_iq), feat_spec(n_kv),
                   feat_spec(IDX_HEADS)],
        out_shape=[jax.ShapeDtypeStruct((n, n_kv), F32), jax.ShapeDtypeStruct((n, n_kv), F32),
                   jax.ShapeDtypeStruct((n, IDX_DIM), F32), feat_shape(n_q, BF16), feat_shape(n_iq, BF16),
                   feat_shape(n_kv, BF16), feat_shape(IDX_HEADS, F32)],
        compiler_params=_cparams(("parallel",)),
        name="attn_proj",
    )(x, g, wrow, wt)


def _bias_kernel(tab_ref, tb_ref):
    h = pl.program_id(0)
    d = pl.program_id(1)
    kj = lax.broadcasted_iota(I32, (QB, QB), 0)
    qi = lax.broadcasted_iota(I32, (QB, QB), 1)
    rel = (d - 1) * QB + kj - qi
    n = jnp.abs(rel)
    large = jnp.full((QB, QB), N_BUCKETS // 4, I32)
    for s in BUCKET_STEPS:
        large = large + jnp.where(n >= s, 1, 0)
    bucket = jnp.where(rel > 0, N_BUCKETS // 2, 0) + jnp.where(n < N_BUCKETS // 4, n, large)
    val = jnp.zeros((QB, QB), F32)
    for b in range(N_BUCKETS):
        val = jnp.where(bucket == b, tab_ref[b, h], val)
    tb_ref[0] = val


def _bias_call(table):
    return pl.pallas_call(
        _bias_kernel,
        grid=(N_HEADS, 2),
        in_specs=[pl.BlockSpec(memory_space=pltpu.SMEM)],
        out_specs=pl.BlockSpec((1, QB, QB), lambda h, d: (h, d, 0)),
        out_shape=jax.ShapeDtypeStruct((N_HEADS, 2 * QB, QB), F32),
        compiler_params=_cparams(("arbitrary", "arbitrary")),
        name="rel_bias_tiles",
    )(table)


def _attn_kernel(kend_ref, qt_ref, qit_ref, wit_ref, kf_ref, vtf_ref, kif_ref, tb_ref, tab_ref, o_ref,
                 key_scr, hi_scr, lo_scr, lo2_scr, ls_scr, oacc_scr, *, diag_off, kstart, k_sel, has_prev,
                 row_bits):
    diag = pl.program_id(1) + diag_off
    nt = diag + 1
    nst = (nt + SEL_TILE // QB - 1) // (SEL_TILE // QB)
    kend = kend_ref[0]
    qi_cat = jnp.concatenate([qit_ref[0, h * IDX_DIM:(h + 1) * IDX_DIM, :] for h in range(IDX_HEADS)], axis=1)
    wi = wit_ref[0]
    sel_iota = lax.broadcasted_iota(I32, (SEL_TILE, QB), 0)

    def tile_start(t):
        return pl.multiple_of(t * QB, QB)

    def sel_start(t):
        return pl.multiple_of(t * SEL_TILE, SEL_TILE)

    def score_tile(t, carry):
        r0 = sel_start(t)
        s = _dot(kif_ref[0, pl.ds(r0, SEL_TILE), :], qi_cat)
        sc = jnp.zeros((SEL_TILE, QB), F32)
        for h in range(IDX_HEADS):
            sc = sc + wi[h:h + 1, :] * jnp.maximum(s[:, h * QB:(h + 1) * QB], 0.0)
        bits = lax.bitcast_convert_type(sc + 0.0, I32)
        key = jnp.where(bits >= 0, bits, bits ^ 0x7FFFFFFF)
        rows = r0 + sel_iota
        adm = (rows >= kstart) & (rows < kend)
        key = jnp.where(adm, key, INT_MIN)
        key_scr[pl.ds(r0, SEL_TILE), :] = key
        hi_scr[pl.ds(r0, SEL_TILE), :] = (key >> 16).astype(I16)
        lo_scr[pl.ds(r0, SEL_TILE), :] = ((key & 0xFFFF) + I16_MIN).astype(I16)
        return carry

    lax.fori_loop(0, nst, score_tile, 0)

    @pl.when(nst % 2 == 1)
    def _():
        blank = jnp.full((SEL_TILE, QB), I16_MIN, I16)
        hi_scr[pl.ds(sel_start(nst), SEL_TILE), :] = blank
        lo_scr[pl.ds(sel_start(nst), SEL_TILE), :] = blank

    n_pair = (nst + 1) // 2
    pair_vregs = 2 * SEL_TILE // PACK16

    def pair_start(t):
        return pl.multiple_of(t * 2 * SEL_TILE, 2 * SEL_TILE)

    def as_packed(v):
        return jnp.broadcast_to(v, (PACK16, QB)).astype(I16)

    def count16(src_scr, cand):
        c16 = as_packed(cand)[None]

        def body(t, acc):
            v = src_scr[pl.ds(pair_start(t), 2 * SEL_TILE), :].reshape(pair_vregs, PACK16, QB)
            ind = jnp.where(v >= c16, jnp.ones((), BF16), jnp.zeros((), BF16))
            parts = [ind[i] for i in range(pair_vregs)]
            while len(parts) > 1:
                parts = [parts[i] + parts[i + 1] for i in range(0, len(parts), 2)]
            return acc + parts[0]

        acc = lax.fori_loop(0, n_pair, body, jnp.zeros((PACK16, QB), BF16))
        return acc.astype(F32).sum(axis=0, keepdims=True)

    def search16(src_scr, base, c_start):
        def step(b, carry):
            t_acc, c_acc = carry
            cand = t_acc + lax.shift_left(jnp.int32(1), 15 - b)
            c = base + count16(src_scr, cand)
            ok = c >= k_sel
            return jnp.where(ok, cand, t_acc), jnp.where(ok, c, c_acc)

        return lax.fori_loop(0, 16, step, (jnp.full((1, QB), I16_MIN, I32), c_start))

    def count(pred):
        def body(t, acc):
            r0 = sel_start(t)
            ind = jnp.where(pred(key_scr[pl.ds(r0, SEL_TILE), :], r0 + sel_iota), 1, 0)
            return acc + ind.reshape(SEL_TILE // SUBLANES, SUBLANES, QB).sum(axis=0)
        acc = lax.fori_loop(0, nst, body, jnp.zeros((SUBLANES, QB), I32))
        return acc.sum(axis=0, keepdims=True)

    n_all = (n_pair * 2 * SEL_TILE).astype(F32)
    thr_hi, c_hi = search16(hi_scr, 0.0, jnp.full((1, QB), n_all, F32))
    above = jnp.where(thr_hi == -I16_MIN - 1, 0.0, count16(hi_scr, thr_hi + 1))
    hi16 = as_packed(thr_hi)[None]

    def low_tile(t, carry):
        rows = pl.ds(pair_start(t), 2 * SEL_TILE)
        hi = hi_scr[rows, :].reshape(pair_vregs, PACK16, QB)
        lo = lo_scr[rows, :].reshape(pair_vregs, PACK16, QB)
        lo2_scr[rows, :] = jnp.where(hi == hi16, lo, jnp.full((), I16_MIN, I16)).reshape(2 * SEL_TILE, QB)
        return carry

    lax.fori_loop(0, n_pair, low_tile, 0)
    thr_lo, c_ge = search16(lo2_scr, above, c_hi)
    thr = thr_hi * 65536 + (thr_lo - I16_MIN)

    tied = (c_ge > k_sel) & (thr > INT_MIN)
    big = jnp.full((1, QB), 2 ** row_bits, I32)

    def tie_limit():
        need = k_sel - count(lambda kt, rows: kt > thr)

        def lim_step(b, lim):
            cand = lim + lax.shift_left(jnp.int32(1), row_bits - 1 - b)
            c = count(lambda kt, rows: (kt == thr) & (rows < cand))
            return jnp.where(c < need, cand, lim)

        lim = lax.fori_loop(0, row_bits, lim_step, jnp.zeros((1, QB), I32))
        return jnp.where(tied, lim, big)

    rlim = lax.cond(jnp.max(tied.astype(I32)) > 0, tie_limit, lambda: big)
    rlim = jnp.where(thr > INT_MIN, rlim, -1)

    gq = GROUP * QB
    qg = [jnp.concatenate([qt_ref[0, (g * GROUP + r) * HEAD_DIM:(g * GROUP + r + 1) * HEAD_DIM, :]
                           for r in range(GROUP)], axis=1) for g in range(N_KV_HEADS)]

    def logits_rows(r0, rows, m8, bias_of_head):
        kt = key_scr[pl.ds(r0, rows), :]
        keep = (kt > thr) | ((kt == thr) & (sel_iota[:rows] <= rlim - r0))
        new_m8 = []
        for g in range(N_KV_HEADS):
            lg = _dot(kf_ref[0, g, pl.ds(r0, rows), :], qg[g])
            parts = []
            for r in range(GROUP):
                part = lg[:, r * QB:(r + 1) * QB] + bias_of_head(g * GROUP + r)
                parts.append(jnp.where(keep, part, NEG_INF))
            lg = jnp.concatenate(parts, axis=1)
            ls_scr[pl.ds(r0, rows), g * gq:(g + 1) * gq] = lg
            new_m8.append(jnp.maximum(m8[g], lg.reshape(rows // SUBLANES, SUBLANES, gq).max(axis=0)))
        return tuple(new_m8)

    far_bias = lambda h: tab_ref[FAR_BUCKET, h]
    m8 = tuple(jnp.full((SUBLANES, gq), NEG_INF, F32) for _ in range(N_KV_HEADS))
    n_far = diag - 1 if has_prev else diag
    m8 = lax.fori_loop(0, n_far // 2, lambda t, m: logits_rows(sel_start(t), SEL_TILE, m, far_bias), m8)
    m8 = lax.cond(n_far % 2 == 1, lambda m: logits_rows(tile_start(n_far - 1), QB, m, far_bias), lambda m: m, m8)
    if has_prev:
        m8 = logits_rows(tile_start(diag - 1), 2 * QB, m8, lambda h: tb_ref[h])
    else:
        m8 = logits_rows(tile_start(diag), QB, m8, lambda h: tb_ref[h, QB:, :])
    m = [jnp.max(m8[g], axis=0, keepdims=True) for g in range(N_KV_HEADS)]

    @pl.when(nt % 2 == 1)
    def _():
        ls_scr[pl.ds(tile_start(nt), QB), :] = jnp.full((QB, N_HEADS * QB), NEG_INF, F32)

    oacc_scr[...] = jnp.zeros(oacc_scr.shape, F32)

    def pv_tile(t, l8):
        r0 = sel_start(t)
        new_l8 = []
        for g in range(N_KV_HEADS):
            p = jnp.exp(ls_scr[pl.ds(r0, SEL_TILE), g * gq:(g + 1) * gq] - m[g])
            oacc_scr[g] += _dot(vtf_ref[0, g * HEAD_DIM:(g + 1) * HEAD_DIM, pl.ds(r0, SEL_TILE)], p.astype(BF16))
            new_l8.append(l8[g] + p.reshape(SEL_TILE // SUBLANES, SUBLANES, gq).sum(axis=0))
        return tuple(new_l8)

    l8 = lax.fori_loop(0, nst, pv_tile, tuple(jnp.zeros((SUBLANES, gq), F32) for _ in range(N_KV_HEADS)))
    out_rows = []
    for g in range(N_KV_HEADS):
        o_g = oacc_scr[g] / jnp.sum(l8[g], axis=0, keepdims=True)
        out_rows.extend(o_g[:, r * QB:(r + 1) * QB] for r in range(GROUP))
    o_ref[...] = jnp.concatenate(out_rows, axis=0).T.astype(BF16)


def _attn_call(kend, qt, qit, wit, kf, vtf, kif, tb, table, *, n_out_rows, diag_off, kstart, k_sel, name):
    n_seq, tkf = kf.shape[0], kf.shape[2]
    assert tkf % SEL_TILE == 0 and vtf.shape[2] == tkf and kif.shape[1] == tkf
    rows16 = -(-tkf // (2 * SEL_TILE)) * 2 * SEL_TILE
    assert rows16 // PACK16 <= 256, "per-slot bf16 counts must stay exact"
    n_qb = kend.shape[0]
    has_prev = diag_off >= 1
    row_bits = max(1, (tkf - 1).bit_length())
    feat = lambda a: pl.BlockSpec((1, a.shape[1], QB), lambda s, i: (s * n_qb + i, 0, 0))
    return pl.pallas_call(
        functools.partial(_attn_kernel, diag_off=diag_off, kstart=kstart, k_sel=k_sel, has_prev=has_prev,
                          row_bits=row_bits),
        grid=(n_seq, n_qb),
        in_specs=[pl.BlockSpec((1, 1, QB), lambda s, i: (i, 0, 0)), feat(qt), feat(qit), feat(wit),
                  pl.BlockSpec((1, N_KV_HEADS, tkf, HEAD_DIM), lambda s, i: (s, 0, 0, 0)),
                  pl.BlockSpec((1, N_KV_HEADS * HEAD_DIM, tkf), lambda s, i: (s, 0, 0)),
                  pl.BlockSpec((1, tkf, IDX_DIM), lambda s, i: (s, 0, 0)),
                  _const_spec(tb.shape), pl.BlockSpec(memory_space=pltpu.SMEM)],
        out_specs=pl.BlockSpec((QB, N_HEADS * HEAD_DIM), lambda s, i: (s * n_qb + i, 0)),
        out_shape=jax.ShapeDtypeStruct((n_out_rows, N_HEADS * HEAD_DIM), BF16),
        scratch_shapes=[pltpu.VMEM((tkf, QB), I32)] + [pltpu.VMEM((rows16, QB), I16)] * 3
                       + [pltpu.VMEM((tkf, N_HEADS * QB), F32), pltpu.VMEM((N_KV_HEADS, HEAD_DIM, GROUP * QB), F32)],
        compiler_params=_cparams(("arbitrary", "arbitrary")),
        name=name,
    )(kend, qt, qit, wit, kf, vtf, kif, tb, table)


def _softplus(x):
    return jnp.maximum(x, 0.0) + jnp.log1p(jnp.exp(-jnp.abs(x)))


def _mixb_kernel(x_ref, h0_ref, buf_ref, g_ref, wy_ref, by_ref, wx_ref, bx_ref, cw_ref, cb_ref, wr_ref, br_ref,
                 wi_ref, bi_ref, lam_ref, prein_ref, pre_ref, hlast_ref, tail_ref, xe_scr, h_scr, *, tm):
    del prein_ref

    @pl.when(pl.program_id(1) == 0)
    def _():
        xe_scr[0:SUBLANES, :] = buf_ref[0]
        h_scr[...] = h0_ref[0]

    xn = _rmsnorm(x_ref[...], g_ref[...]).astype(BF16)
    y = jax.nn.gelu(_dot(xn, wy_ref[...]) + by_ref[...])
    xe_scr[SUBLANES:SUBLANES + tm, :] = _dot(xn, wx_ref[...]) + bx_ref[...]
    xc = cb_ref[...] + cw_ref[0:1, :] * xe_scr[SUBLANES - 3:SUBLANES - 3 + tm, :]
    for j in range(1, CONV_B):
        xc = xc + cw_ref[j:j + 1, :] * xe_scr[SUBLANES - 3 + j:SUBLANES - 3 + j + tm, :]
    xcb = xc.astype(BF16)
    blk = xc.shape[1] // LRU_BLOCKS
    r_pre = jnp.concatenate([_dot(xcb[:, n * blk:(n + 1) * blk], wr_ref[n]) for n in range(LRU_BLOCKS)], axis=1)
    i_pre = jnp.concatenate([_dot(xcb[:, n * blk:(n + 1) * blk], wi_ref[n]) for n in range(LRU_BLOCKS)], axis=1)
    r = jax.nn.sigmoid(r_pre + br_ref[...])
    ig = jax.nn.sigmoid(i_pre + bi_ref[...])
    log_a = -LRU_C * r * _softplus(-lam_ref[...])
    a = jnp.exp(log_a)
    u = jnp.sqrt(jnp.tanh(-log_a) * (1.0 + a * a)) * (ig * xc)
    rows = lax.broadcasted_iota(I32, a.shape, 0)
    s = 1
    while s < tm:
        a_sh = jnp.where(rows >= s, pltpu.roll(a, s, 0), 1.0)
        u_sh = jnp.where(rows >= s, pltpu.roll(u, s, 0), 0.0)
        u = a * u_sh + u
        a = a * a_sh
        s *= 2
    h = a * h_scr[...] + u
    pre_ref[...] = (h * y).astype(BF16)
    h_scr[...] = h[tm - 1:tm, :]
    hlast_ref[0] = h[tm - 1:tm, :]
    tail = xe_scr[tm:tm + SUBLANES, :]
    xe_scr[0:SUBLANES, :] = tail
    tail_ref[0] = tail


def _mixc_kernel(x_ref, buf_ref, g_ref, win_ref, cw_ref, prein_ref, pre_ref, tail_ref, pe_scr, *, tm):
    del prein_ref

    @pl.when(pl.program_id(1) == 0)
    def _():
        pe_scr[0:SUBLANES, :] = buf_ref[0]

    d = x_ref.shape[1]
    xn = _rmsnorm(x_ref[...], g_ref[...]).astype(BF16)
    z = _dot(xn, win_ref[...])
    pe_scr[SUBLANES:SUBLANES + tm, :] = z[:, d:2 * d] * z[:, 2 * d:]
    conv = cw_ref[0:1, :] * pe_scr[SUBLANES - 2:SUBLANES - 2 + tm, :]
    for j in range(1, CONV_C):
        conv = conv + cw_ref[j:j + 1, :] * pe_scr[SUBLANES - 2 + j:SUBLANES - 2 + j + tm, :]
    pre_ref[...] = (z[:, :d] * conv).astype(BF16)
    tail = pe_scr[tm:tm + SUBLANES, :]
    pe_scr[0:SUBLANES, :] = tail
    tail_ref[0] = tail


def _seq_specs(n_seq, t_len, row_off, d):
    tm = min(TM_SEQ, t_len)
    n_tt = t_len // tm
    off = row_off // tm
    xrow = pl.BlockSpec((tm, d), lambda s, j: (off + s * n_tt + j, 0))
    state = lambda rows: pl.BlockSpec((1, rows, d), lambda s, j: (s, 0, 0))
    return tm, n_tt, xrow, state


def _mixb_call(x, pre_buf, h0, buf, g, wy, by, wx, bx, cw, cb, wr, br, wi, bi, lam, *, n_seq, t_len, row_off,
               name):
    d = x.shape[1]
    tm, n_tt, xrow, state = _seq_specs(n_seq, t_len, row_off, d)
    consts = (g, wy, by, wx, bx, cw, cb, wr, br, wi, bi, lam)
    return pl.pallas_call(
        functools.partial(_mixb_kernel, tm=tm),
        grid=(n_seq, n_tt),
        in_specs=[xrow, state(1), state(SUBLANES)] + [_const_spec(c.shape) for c in consts]
                 + [pl.BlockSpec(memory_space=pl.ANY)],
        out_specs=[xrow, state(1), state(SUBLANES)],
        out_shape=[jax.ShapeDtypeStruct(pre_buf.shape, BF16), jax.ShapeDtypeStruct((n_seq, 1, d), F32),
                   jax.ShapeDtypeStruct((n_seq, SUBLANES, d), F32)],
        scratch_shapes=[pltpu.VMEM((tm + SUBLANES, d), F32), pltpu.VMEM((1, d), F32)],
        input_output_aliases={3 + len(consts): 0},
        compiler_params=_cparams(("arbitrary", "arbitrary")),
        name=name,
    )(x, h0, buf, *consts, pre_buf)


def _mixc_call(x, pre_buf, buf, g, win, cw, *, n_seq, t_len, row_off, name):
    d = x.shape[1]
    tm, n_tt, xrow, state = _seq_specs(n_seq, t_len, row_off, d)
    consts = (g, win, cw)
    return pl.pallas_call(
        functools.partial(_mixc_kernel, tm=tm),
        grid=(n_seq, n_tt),
        in_specs=[xrow, state(SUBLANES)] + [_const_spec(c.shape) for c in consts]
                 + [pl.BlockSpec(memory_space=pl.ANY)],
        out_specs=[xrow, state(SUBLANES)],
        out_shape=[jax.ShapeDtypeStruct(pre_buf.shape, BF16), jax.ShapeDtypeStruct((n_seq, SUBLANES, d), F32)],
        scratch_shapes=[pltpu.VMEM((tm + SUBLANES, d), F32)],
        input_output_aliases={2 + len(consts): 0},
        compiler_params=_cparams(("arbitrary", "arbitrary")),
        name=name,
    )(x, buf, *consts, pre_buf)


def _front_pad_rows(a, rows):
    return jnp.pad(a, ((0, 0), (rows - a.shape[1], 0), (0, 0)))


def _frame(parts, axis):
    f = jnp.concatenate(parts, axis=axis).astype(BF16)
    pad = [(0, 0)] * f.ndim
    pad[axis] = (0, -f.shape[axis] % SEL_TILE)
    return jnp.pad(f, pad)


def _k_frame(parts):
    f = _frame(parts, 1)
    n_seq, rows = f.shape[0], f.shape[1]
    return f.reshape(n_seq, rows, N_KV_HEADS, HEAD_DIM).transpose(0, 2, 1, 3)


def _cols(a, start, n):
    b0, b1 = start // QB, -(-(start + n) // QB)
    c = a[b0:b1].transpose(1, 0, 2).reshape(a.shape[1], (b1 - b0) * QB)
    return c[:, start - b0 * QB:start - b0 * QB + n]


def _seq_major(a, n_seq, q_len):
    return a.reshape(a.shape[0], n_seq, q_len).transpose(1, 0, 2)


def _lane_pad_cols(a, n_seq, q_len):
    return jnp.pad(_seq_major(a, n_seq, q_len), ((0, 0), (0, 0), (0, QB - q_len)))


def kernel(x_prompt, x_sample, cache_k, cache_v, cache_kidx, state_h, state_conv_b, state_conv_c, meta_tokens,
           rel_bias, norm_mix, norm_ffn, norm_final, a_wq, a_wk, a_wv, a_wo, a_wiq, a_wik, a_wiw, b_wy, b_by, b_wx,
           b_bx, b_conv_w, b_conv_b, b_wr, b_br, b_wi, b_bi, b_lam, b_wo, b_bo, c_win, c_conv_w, c_wo, f_wg, f_wu,
           f_wd):
    bp, seq, d = x_prompt.shape
    bs, dec = x_sample.shape[0], x_sample.shape[1]
    past = cache_k.shape[2]
    n_meta = meta_tokens.shape[0]
    depth = norm_mix.shape[0]
    assert seq % QB == 0 and seq % TM_SEQ == 0 and past % QB == 0 and dec <= QB and n_meta <= QB
    assert dec % SUBLANES == 0 and n_meta % SUBLANES == 0 and seq % dec == 0 and (bp * seq + bs * dec) % n_meta == 0

    n_main, n_samp = bp * seq, bs * dec
    off_s, off_m = n_main, n_main + n_samp
    n_valid = off_m + n_meta
    n_rows = -(-n_valid // ROW_PAD) * ROW_PAD
    x = jnp.concatenate([x_prompt.reshape(n_main, d), x_sample.reshape(n_samp, d), meta_tokens.astype(F32),
                         jnp.zeros((n_rows - n_valid, d), F32)], axis=0)

    k_sel_p = min(TOP_K_MAX, seq // 4)
    k_sel_s = min(TOP_K_MAX, (past + dec) // 4)
    n_qb = seq // QB
    kv_dim = N_KV_HEADS * HEAD_DIM

    tau = jnp.arange(seq, dtype=I32)
    kend_main = (QB + CHUNK * (tau // CHUNK + 1)).reshape(n_qb, 1, QB)
    lane = jnp.arange(QB, dtype=I32)
    kend_samp = jnp.minimum(past + dec, CHUNK * ((past + jnp.minimum(lane, dec - 1)) // CHUNK + 1)).reshape(1, 1, QB)
    kend_meta = jnp.full((1, 1, QB), n_meta, I32)

    tb = _bias_call(rel_bias.astype(F32))
    zeros_bias = jnp.zeros((1, d), F32)
    gfin = norm_final.reshape(1, d).astype(F32)

    new = {name: [] for name in ("k_p", "v_p", "ki_p", "h_p", "cb_p", "cc_p", "k_s", "v_s", "ki_s", "h_s", "cb_s",
                                 "cc_s")}
    for l in range(depth):
        s = l // N_MIXERS
        g_mix = norm_mix[l].reshape(1, d)
        if l % N_MIXERS == 0:
            wrow = jnp.concatenate([a_wk[s], a_wv[s], a_wik[s]], axis=1).astype(BF16)
            wt = jnp.concatenate([a_wq[s].T * HEAD_DIM ** -0.5, a_wiq[s].T * IDX_DIM ** -0.5, a_wv[s].T, a_wiw[s].T,
                                  jnp.zeros((SUBLANES, d), F32)], axis=0).astype(BF16)
            k, v, ki, qt, qit, vt, wit = _proj_call(x, g_mix, wrow, wt)

            k_meta, v_meta, ki_meta = (a[off_m:n_valid] for a in (k, v, ki))
            k_main, v_main, ki_main = (a[:n_main].reshape(bp, seq, -1) for a in (k, v, ki))
            k_samp, v_samp, ki_samp = (a[off_s:off_m].reshape(bs, dec, -1) for a in (k, v, ki))
            bc = lambda a: jnp.broadcast_to(a[None], (bp,) + a.shape)
            new["k_p"].append(jnp.concatenate([bc(k_meta), k_main], axis=1).reshape(bp, n_meta + seq, N_KV_HEADS,
                                                                                   HEAD_DIM))
            new["v_p"].append(jnp.concatenate([bc(v_meta), v_main], axis=1).reshape(bp, n_meta + seq, N_KV_HEADS,
                                                                                   HEAD_DIM))
            new["ki_p"].append(jnp.concatenate([bc(ki_meta), ki_main], axis=1))
            new["k_s"].append(k_samp.reshape(bs, dec, N_KV_HEADS, HEAD_DIM))
            new["v_s"].append(v_samp.reshape(bs, dec, N_KV_HEADS, HEAD_DIM))
            new["ki_s"].append(ki_samp)

            front = lambda a: jnp.pad(bc(a), ((0, 0), (QB - n_meta, 0), (0, 0)))
            samp_cols = lambda a: _cols(a, off_s, n_samp)
            meta_cols = lambda a: _cols(a, off_m, n_meta)
            vt_meta = meta_cols(vt)
            vt_main = vt[:n_main // QB].reshape(bp, n_qb, kv_dim, QB).transpose(0, 2, 1, 3).reshape(bp, kv_dim, seq)
            vtf_main = _frame([jnp.pad(bc(vt_meta), ((0, 0), (0, 0), (QB - n_meta, 0))), vt_main], 2)
            pre = _attn_call(
                kend_main, qt, qit, wit, _k_frame([front(k_meta), k_main]), vtf_main,
                _frame([front(ki_meta), ki_main], 1), tb, rel_bias,
                n_out_rows=n_rows, diag_off=1, kstart=QB - n_meta, k_sel=k_sel_p, name="dsa_attn_prompt")
            vtf_samp = _frame([cache_v[s].reshape(bs, past, kv_dim).transpose(0, 2, 1),
                               _seq_major(samp_cols(vt), bs, dec)], 2)
            o_samp = _attn_call(
                kend_samp, _lane_pad_cols(samp_cols(qt), bs, dec),
                _lane_pad_cols(samp_cols(qit), bs, dec), _lane_pad_cols(samp_cols(wit), bs, dec),
                _k_frame([cache_k[s].reshape(bs, past, kv_dim), k_samp]), vtf_samp,
                _frame([cache_kidx[s], ki_samp], 1), tb, rel_bias,
                n_out_rows=bs * QB, diag_off=past // QB, kstart=0, k_sel=k_sel_s, name="dsa_attn_sample")
            o_meta = _attn_call(
                kend_meta, _lane_pad_cols(meta_cols(qt), 1, n_meta),
                _lane_pad_cols(meta_cols(qit), 1, n_meta), _lane_pad_cols(meta_cols(wit), 1, n_meta),
                _k_frame([k_meta[None]]), _frame([vt_meta[None]], 2), _frame([ki_meta[None]], 1), tb, rel_bias,
                n_out_rows=QB, diag_off=0, kstart=0, k_sel=k_sel_p, name="dsa_attn_meta")
            tail_rows = jnp.concatenate([o_samp.reshape(bs, QB, d)[:, :dec].reshape(n_samp, d), o_meta[:n_meta],
                                         jnp.zeros((n_rows - n_valid, d), BF16)], axis=0)
            pre = lax.dynamic_update_slice(pre, tail_rows, (off_s, 0))
            wo, bo = a_wo[s].astype(BF16), zeros_bias
        elif l % N_MIXERS == 1:
            consts = (g_mix, b_wy[s].astype(BF16), b_by[s].reshape(1, d), b_wx[s].astype(BF16), b_bx[s].reshape(1, d),
                      jnp.pad(b_conv_w[s], ((0, SUBLANES - CONV_B), (0, 0))), b_conv_b[s].reshape(1, d),
                      b_wr[s].astype(BF16), b_br[s].reshape(1, d), b_wi[s].astype(BF16), b_bi[s].reshape(1, d),
                      b_lam[s].reshape(1, d))
            pre = jnp.zeros((n_rows, d), BF16)
            pre, h_m, tail_m = _mixb_call(x, pre, jnp.zeros((1, 1, d), F32), jnp.zeros((1, SUBLANES, d), F32),
                                          *consts, n_seq=1, t_len=n_meta, row_off=off_m, name="rglru_meta")
            pre, h_p, tail_p = _mixb_call(x, pre, jnp.broadcast_to(h_m, (bp, 1, d)),
                                          jnp.broadcast_to(tail_m, (bp, SUBLANES, d)),
                                          *consts, n_seq=bp, t_len=seq, row_off=0, name="rglru_prompt")
            pre, h_s, tail_s = _mixb_call(x, pre, state_h[s].reshape(bs, 1, d),
                                          _front_pad_rows(state_conv_b[s], SUBLANES),
                                          *consts, n_seq=bs, t_len=dec, row_off=off_s, name="rglru_sample")
            new["h_p"].append(h_p.reshape(bp, d))
            new["cb_p"].append(tail_p[:, SUBLANES - (CONV_B - 1):])
            new["h_s"].append(h_s.reshape(bs, d))
            new["cb_s"].append(tail_s[:, SUBLANES - (CONV_B - 1):])
            wo, bo = b_wo[s].astype(BF16), b_bo[s].reshape(1, d)
        else:
            consts = (g_mix, c_win[s].astype(BF16), jnp.pad(c_conv_w[s], ((0, SUBLANES - CONV_C), (0, 0))))
            pre = jnp.zeros((n_rows, d), BF16)
            pre, tail_m = _mixc_call(x, pre, jnp.zeros((1, SUBLANES, d), F32), *consts, n_seq=1, t_len=n_meta,
                                     row_off=off_m, name="sconv_meta")
            pre, tail_p = _mixc_call(x, pre, jnp.broadcast_to(tail_m, (bp, SUBLANES, d)), *consts, n_seq=bp,
                                     t_len=seq, row_off=0, name="sconv_prompt")
            pre, tail_s = _mixc_call(x, pre, _front_pad_rows(state_conv_c[s], SUBLANES), *consts, n_seq=bs,
                                     t_len=dec, row_off=off_s, name="sconv_sample")
            new["cc_p"].append(tail_p[:, SUBLANES - (CONV_C - 1):])
            new["cc_s"].append(tail_s[:, SUBLANES - (CONV_C - 1):])
            wo, bo = c_wo[s].astype(BF16), zeros_bias
        x = _ffn_call(x, pre, wo, bo, norm_ffn[l].reshape(1, d), f_wg[l].astype(BF16), f_wu[l].astype(BF16),
                      f_wd[l].astype(BF16), gfin, final_norm=(l == depth - 1))

    y_prompt = x[:n_main].reshape(bp, seq, d)
    y_sample = x[off_s:off_m].reshape(bs, dec, d)
    st = lambda name: jnp.stack(new[name])
    return (y_prompt, y_sample, st("k_p"), st("v_p"), st("ki_p"), st("h_p"), st("cb_p"), st("cc_p"),
            st("k_s"), st("v_s"), st("ki_s"), st("h_s"), st("cb_s"), st("cc_s"))
```

```python
import functools

import jax
import jax.numpy as jnp
from jax import lax
from jax.experimental import pallas as pl
from jax.experimental.pallas import tpu as pltpu

F32 = jnp.float32
BF16 = jnp.bfloat16
I32 = jnp.int32
I16 = jnp.int16

CHUNK = 64
N_MIXERS = 3
N_HEADS = 16
N_KV_HEADS = 4
GROUP = N_HEADS // N_KV_HEADS
HEAD_DIM = 64
IDX_HEADS = 8
IDX_DIM = 64
TOP_K_MAX = 256
NEG_INF = -1e30
N_BUCKETS = 32
LRU_BLOCKS = 4
CONV_B = 4
CONV_C = 3
LRU_C = 8.0
RMS_EPS = 1e-6
BUCKET_STEPS = (12, 16, 23, 32, 46, 64, 91)
FAR_BUCKET = N_BUCKETS // 2 - 1

LANES = 128
SUBLANES = 8
ROW_PAD = 768
TM_PROJ = 256
TM_FFN = 384
TM_SEQ = 256
QB = 128
SEL_TILE = 2 * QB
INT_MIN = -2 ** 31
I16_MIN = -2 ** 15
PACK16 = 2 * SUBLANES
V_ROWS = HEAD_DIM + PACK16
LOG2E = 1.4426950408889634
VMEM_LIMIT = 56 * 1024 * 1024


def _cparams(sem):
    return pltpu.CompilerParams(dimension_semantics=sem, vmem_limit_bytes=VMEM_LIMIT)


def _const_spec(shape):
    nd = len(shape)
    return pl.BlockSpec(shape, lambda *_: (0,) * nd, pipeline_mode=pl.Buffered(1))


def _rmsnorm(x, g):
    ms = jnp.mean(x * x, axis=-1, keepdims=True)
    return x * lax.rsqrt(ms + RMS_EPS) * g


def _dot(a, b):
    return jnp.dot(a, b, preferred_element_type=F32)


def _dot_nt(a, b):
    return lax.dot_general(a, b, (((1,), (1,)), ((), ())), preferred_element_type=F32)


def _ffn_kernel(x_ref, pre_ref, wo_ref, bo_ref, gf_ref, wg_ref, wu_ref, wd_ref, gfin_ref, out_ref, *,
                n_chunks, fc, final_norm):
    x1 = x_ref[...] + _dot(pre_ref[...], wo_ref[...]) + bo_ref[...]
    xn = _rmsnorm(x1, gf_ref[...]).astype(BF16)
    acc = x1
    for c in range(n_chunks):
        gt = _dot(xn, wg_ref[:, c * fc:(c + 1) * fc])
        up = _dot(xn, wu_ref[:, c * fc:(c + 1) * fc])
        hm = (gt * jax.nn.sigmoid(gt) * up).astype(BF16)
        acc = acc + _dot(hm, wd_ref[c * fc:(c + 1) * fc, :])
    if final_norm:
        acc = _rmsnorm(acc, gfin_ref[...])
    out_ref[...] = acc


def _ffn_call(x, pre, wo, bo, gf, wg, wu, wd, gfin, final_norm):
    n, d = x.shape
    dff = wg.shape[1]
    n_chunks = 2 if dff % (2 * LANES) == 0 else 1
    fc = dff // n_chunks
    row = lambda i: (i, 0)
    return pl.pallas_call(
        functools.partial(_ffn_kernel, n_chunks=n_chunks, fc=fc, final_norm=final_norm),
        grid=(n // TM_FFN,),
        in_specs=[pl.BlockSpec((TM_FFN, d), row), pl.BlockSpec((TM_FFN, pre.shape[1]), row),
                  _const_spec(wo.shape), _const_spec(bo.shape), _const_spec(gf.shape),
                  _const_spec(wg.shape), _const_spec(wu.shape), _const_spec(wd.shape), _const_spec(gfin.shape)],
        out_specs=pl.BlockSpec((TM_FFN, d), row),
        out_shape=jax.ShapeDtypeStruct((n, d), F32),
        compiler_params=_cparams(("parallel",)),
        name="outproj_swiglu",
    )(x, pre, wo, bo, gf, wg, wu, wd, gfin)


def _proj_kernel(x_ref, g_ref, wrow_ref, wt_ref, k_ref, v_ref, ki_ref, qt_ref, qit_ref, vt_ref, wit_ref, *,
                 n_kv, n_q, n_iq):
    xn = _rmsnorm(x_ref[...], g_ref[...]).astype(BF16)
    row = _dot(xn, wrow_ref[...])
    k_ref[...] = row[:, :n_kv]
    v_ref[...] = row[:, n_kv:2 * n_kv]
    ki_ref[...] = row[:, 2 * n_kv:]
    tt = _dot_nt(wt_ref[...], xn)
    for b in range(TM_PROJ // QB):
        blk = tt[:, b * QB:(b + 1) * QB]
        qt_ref[b] = blk[:n_q].astype(BF16)
        qit_ref[b] = blk[n_q:n_q + n_iq].astype(BF16)
        vt_ref[b] = blk[n_q + n_iq:n_q + n_iq + n_kv].astype(BF16)
        wit_ref[b] = blk[n_q + n_iq + n_kv:n_q + n_iq + n_kv + IDX_HEADS] * (IDX_HEADS ** -0.5)


def _proj_call(x, g, wrow, wt):
    n, d = x.shape
    n_kv = N_KV_HEADS * HEAD_DIM
    n_q = N_HEADS * HEAD_DIM
    n_iq = IDX_HEADS * IDX_DIM
    row = lambda i: (i, 0)
    blocks = TM_PROJ // QB
    feat_spec = lambda f: pl.BlockSpec((blocks, f, QB), lambda i: (i, 0, 0))
    feat_shape = lambda f, dt: jax.ShapeDtypeStruct((n // QB, f, QB), dt)
    return pl.pallas_call(
        functools.partial(_proj_kernel, n_kv=n_kv, n_q=n_q, n_iq=n_iq),
        grid=(n // TM_PROJ,),
        in_specs=[pl.BlockSpec((TM_PROJ, d), row), _const_spec(g.shape), _const_spec(wrow.shape),
                  _const_spec(wt.shape)],
        out_specs=[pl.BlockSpec((TM_PROJ, n_kv), row), pl.BlockSpec((TM_PROJ, n_kv), row),
                   pl.BlockSpec((TM_PROJ, IDX_DIM), row), feat_spec(n_q), feat_spec(n_iq), feat_spec(n_kv),
                   feat_spec(IDX_HEADS)],
        out_shape=[jax.ShapeDtypeStruct((n, n_kv), F32), jax.ShapeDtypeStruct((n, n_kv), F32),
                   jax.ShapeDtypeStruct((n, IDX_DIM), F32), feat_shape(n_q, BF16), feat_shape(n_iq, BF16),
                   feat_shape(n_kv, BF16), feat_shape(IDX_HEADS, F32)],
        compiler_params=_cparams(("parallel",)),
        name="attn_proj",
    )(x, g, wrow, wt)


def _bias_kernel(tab_ref, tb_ref):
    h = pl.program_id(0)
    d = pl.program_id(1)
    kj = lax.broadcasted_iota(I32, (QB, QB), 0)
    qi = lax.broadcasted_iota(I32, (QB, QB), 1)
    rel = (d - 1) * QB + kj - qi
    n = jnp.abs(rel)
    large = jnp.full((QB, QB), N_BUCKETS // 4, I32)
    for s in BUCKET_STEPS:
        large = large + jnp.where(n >= s, 1, 0)
    bucket = jnp.where(rel > 0, N_BUCKETS // 2, 0) + jnp.where(n < N_BUCKETS // 4, n, large)
    val = jnp.zeros((QB, QB), F32)
    for b in range(N_BUCKETS):
        val = jnp.where(bucket == b, tab_ref[b, h], val)
    tb_ref[0] = val


def _bias_call(table):
    return pl.pallas_call(
        _bias_kernel,
        grid=(N_HEADS, 2),
        in_specs=[pl.BlockSpec(memory_space=pltpu.SMEM)],
        out_specs=pl.BlockSpec((1, QB, QB), lambda h, d: (h, d, 0)),
        out_shape=jax.ShapeDtypeStruct((N_HEADS, 2 * QB, QB), F32),
        compiler_params=_cparams(("arbitrary", "arbitrary")),
        name="rel_bias_tiles",
    )(table)


def _attn_kernel(kend_ref, qt_ref, qit_ref, wit_ref, kf_ref, vtf_ref, kif_ref, tb_ref, tab_ref, o_ref,
                 key_scr, hi_scr, lo_scr, lo2_scr, ls_scr, oacc_scr, *, diag_off, kstart, k_sel, has_prev,
                 row_bits):
    diag = pl.program_id(1) + diag_off
    nt = diag + 1
    nst = (nt + SEL_TILE // QB - 1) // (SEL_TILE // QB)
    kend = kend_ref[0]
    qi_cat = jnp.concatenate([qit_ref[0, h * IDX_DIM:(h + 1) * IDX_DIM, :] for h in range(IDX_HEADS)], axis=1)
    wi = wit_ref[0]
    sel_iota = lax.broadcasted_iota(I32, (SEL_TILE, QB), 0)

    def tile_start(t):
        return pl.multiple_of(t * QB, QB)

    def sel_start(t):
        return pl.multiple_of(t * SEL_TILE, SEL_TILE)

    def score_tile(t, carry):
        r0 = sel_start(t)
        s = _dot(kif_ref[0, pl.ds(r0, SEL_TILE), :], qi_cat)
        sc = jnp.zeros((SEL_TILE, QB), F32)
        for h in range(IDX_HEADS):
            sc = sc + wi[h:h + 1, :] * jnp.maximum(s[:, h * QB:(h + 1) * QB], 0.0)
        bits = lax.bitcast_convert_type(sc + 0.0, I32)
        key = jnp.where(bits >= 0, bits, bits ^ 0x7FFFFFFF)
        rows = r0 + sel_iota
        adm = (rows >= kstart) & (rows < kend)
        key = jnp.where(adm, key, INT_MIN)
        key_scr[pl.ds(r0, SEL_TILE), :] = key
        hi_scr[pl.ds(r0, SEL_TILE), :] = (key >> 16).astype(I16)
        lo_scr[pl.ds(r0, SEL_TILE), :] = ((key & 0xFFFF) + I16_MIN).astype(I16)
        return carry

    lax.fori_loop(0, nst, score_tile, 0)

    @pl.when(nst % 2 == 1)
    def _():
        blank = jnp.full((SEL_TILE, QB), I16_MIN, I16)
        hi_scr[pl.ds(sel_start(nst), SEL_TILE), :] = blank
        lo_scr[pl.ds(sel_start(nst), SEL_TILE), :] = blank

    n_pair = (nst + 1) // 2
    pair_vregs = 2 * SEL_TILE // PACK16

    def pair_start(t):
        return pl.multiple_of(t * 2 * SEL_TILE, 2 * SEL_TILE)

    def as_packed(v):
        return jnp.broadcast_to(v, (PACK16, QB)).astype(I16)

    def count16(src_scr, cand):
        c16 = as_packed(cand)[None]

        def body(t, acc):
            v = src_scr[pl.ds(pair_start(t), 2 * SEL_TILE), :].reshape(pair_vregs, PACK16, QB)
            ind = jnp.where(v >= c16, jnp.ones((), BF16), jnp.zeros((), BF16))
            parts = [ind[i] for i in range(pair_vregs)]
            while len(parts) > 1:
                parts = [parts[i] + parts[i + 1] for i in range(0, len(parts), 2)]
            return acc + parts[0]

        acc = lax.fori_loop(0, n_pair, body, jnp.zeros((PACK16, QB), BF16))
        return acc.astype(F32).sum(axis=0, keepdims=True)

    def search16(src_scr, base, c_start):
        def step(b, carry):
            t_acc, c_acc = carry
            cand = t_acc + lax.shift_left(jnp.int32(1), 15 - b)
            c = base + count16(src_scr, cand)
            ok = c >= k_sel
            return jnp.where(ok, cand, t_acc), jnp.where(ok, c, c_acc)

        return lax.fori_loop(0, 16, step, (jnp.full((1, QB), I16_MIN, I32), c_start))

    def count(pred):
        def body(t, acc):
            r0 = sel_start(t)
            ind = jnp.where(pred(key_scr[pl.ds(r0, SEL_TILE), :], r0 + sel_iota), 1, 0)
            return acc + ind.reshape(SEL_TILE // SUBLANES, SUBLANES, QB).sum(axis=0)
        acc = lax.fori_loop(0, nst, body, jnp.zeros((SUBLANES, QB), I32))
        return acc.sum(axis=0, keepdims=True)

    n_all = (n_pair * 2 * SEL_TILE).astype(F32)
    thr_hi, c_hi = search16(hi_scr, 0.0, jnp.full((1, QB), n_all, F32))
    above = jnp.where(thr_hi == -I16_MIN - 1, 0.0, count16(hi_scr, thr_hi + 1))
    hi16 = as_packed(thr_hi)[None]

    def low_tile(t, carry):
        rows = pl.ds(pair_start(t), 2 * SEL_TILE)
        hi = hi_scr[rows, :].reshape(pair_vregs, PACK16, QB)
        lo = lo_scr[rows, :].reshape(pair_vregs, PACK16, QB)
        lo2_scr[rows, :] = jnp.where(hi == hi16, lo, jnp.full((), I16_MIN, I16)).reshape(2 * SEL_TILE, QB)
        return carry

    lax.fori_loop(0, n_pair, low_tile, 0)
    thr_lo, c_ge = search16(lo2_scr, above, c_hi)
    thr = thr_hi * 65536 + (thr_lo - I16_MIN)

    tied = (c_ge > k_sel) & (thr > INT_MIN)
    big = jnp.full((1, QB), 2 ** row_bits, I32)

    def tie_limit():
        need = k_sel - count(lambda kt, rows: kt > thr)

        def lim_step(b, lim):
            cand = lim + lax.shift_left(jnp.int32(1), row_bits - 1 - b)
            c = count(lambda kt, rows: (kt == thr) & (rows < cand))
            return jnp.where(c < need, cand, lim)

        lim = lax.fori_loop(0, row_bits, lim_step, jnp.zeros((1, QB), I32))
        return jnp.where(tied, lim, big)

    rlim = lax.cond(jnp.max(tied.astype(I32)) > 0, tie_limit, lambda: big)
    rlim = jnp.where(thr > INT_MIN, rlim, -1)

    gq = GROUP * QB
    qg = [jnp.concatenate([qt_ref[0, (g * GROUP + r) * HEAD_DIM:(g * GROUP + r + 1) * HEAD_DIM, :]
                           for r in range(GROUP)], axis=1) for g in range(N_KV_HEADS)]

    def logits_rows(r0, rows, m8, bias_of_head):
        kt = key_scr[pl.ds(r0, rows), :]
        keep = (kt > thr) | ((kt == thr) & (sel_iota[:rows] <= rlim - r0))
        new_m8 = []
        for g in range(N_KV_HEADS):
            lg = _dot(kf_ref[0, g, pl.ds(r0, rows), :], qg[g])
            parts = []
            for r in range(GROUP):
                part = (lg[:, r * QB:(r + 1) * QB] + bias_of_head(g * GROUP + r)) * LOG2E
                parts.append(jnp.where(keep, part, NEG_INF))
            lg = jnp.concatenate(parts, axis=1)
            ls_scr[pl.ds(r0, rows), g * gq:(g + 1) * gq] = lg
            new_m8.append(jnp.maximum(m8[g], lg.reshape(rows // SUBLANES, SUBLANES, gq).max(axis=0)))
        return tuple(new_m8)

    far_bias = lambda h: tab_ref[FAR_BUCKET, h]
    m8 = tuple(jnp.full((SUBLANES, gq), NEG_INF, F32) for _ in range(N_KV_HEADS))
    n_far = diag - 1 if has_prev else diag
    m8 = lax.fori_loop(0, n_far // 2, lambda t, m: logits_rows(sel_start(t), SEL_TILE, m, far_bias), m8)
    m8 = lax.cond(n_far % 2 == 1, lambda m: logits_rows(tile_start(n_far - 1), QB, m, far_bias), lambda m: m, m8)
    if has_prev:
        m8 = logits_rows(tile_start(diag - 1), 2 * QB, m8, lambda h: tb_ref[h])
    else:
        m8 = logits_rows(tile_start(diag), QB, m8, lambda h: tb_ref[h, QB:, :])
    m = [jnp.max(m8[g], axis=0, keepdims=True) for g in range(N_KV_HEADS)]

    @pl.when(nt % 2 == 1)
    def _():
        ls_scr[pl.ds(tile_start(nt), QB), :] = jnp.full((QB, N_HEADS * QB), NEG_INF, F32)

    oacc_scr[...] = jnp.zeros(oacc_scr.shape, F32)

    def pv_tile(t, carry):
        r0 = sel_start(t)
        for g in range(N_KV_HEADS):
            p = jnp.exp2(ls_scr[pl.ds(r0, SEL_TILE), g * gq:(g + 1) * gq] - m[g])
            oacc_scr[g] += _dot(vtf_ref[0, g * V_ROWS:(g + 1) * V_ROWS, pl.ds(r0, SEL_TILE)], p.astype(BF16))
        return carry

    lax.fori_loop(0, nst, pv_tile, 0)
    out_rows = []
    for g in range(N_KV_HEADS):
        o_g = oacc_scr[g, :HEAD_DIM] / oacc_scr[g, HEAD_DIM:HEAD_DIM + 1]
        out_rows.extend(o_g[:, r * QB:(r + 1) * QB] for r in range(GROUP))
    o_ref[...] = jnp.concatenate(out_rows, axis=0).T.astype(BF16)


def _attn_call(kend, qt, qit, wit, kf, vtf, kif, tb, table, *, n_out_rows, diag_off, kstart, k_sel, name):
    n_seq, tkf = kf.shape[0], kf.shape[2]
    assert tkf % SEL_TILE == 0 and vtf.shape[2] == tkf and kif.shape[1] == tkf
    rows16 = -(-tkf // (2 * SEL_TILE)) * 2 * SEL_TILE
    assert rows16 // PACK16 <= 256, "per-slot bf16 counts must stay exact"
    n_qb = kend.shape[0]
    has_prev = diag_off >= 1
    row_bits = max(1, (tkf - 1).bit_length())
    feat = lambda a: pl.BlockSpec((1, a.shape[1], QB), lambda s, i: (s * n_qb + i, 0, 0))
    return pl.pallas_call(
        functools.partial(_attn_kernel, diag_off=diag_off, kstart=kstart, k_sel=k_sel, has_prev=has_prev,
                          row_bits=row_bits),
        grid=(n_seq, n_qb),
        in_specs=[pl.BlockSpec((1, 1, QB), lambda s, i: (i, 0, 0)), feat(qt), feat(qit), feat(wit),
                  pl.BlockSpec((1, N_KV_HEADS, tkf, HEAD_DIM), lambda s, i: (s, 0, 0, 0)),
                  pl.BlockSpec((1, N_KV_HEADS * V_ROWS, tkf), lambda s, i: (s, 0, 0)),
                  pl.BlockSpec((1, tkf, IDX_DIM), lambda s, i: (s, 0, 0)),
                  _const_spec(tb.shape), pl.BlockSpec(memory_space=pltpu.SMEM)],
        out_specs=pl.BlockSpec((QB, N_HEADS * HEAD_DIM), lambda s, i: (s * n_qb + i, 0)),
        out_shape=jax.ShapeDtypeStruct((n_out_rows, N_HEADS * HEAD_DIM), BF16),
        scratch_shapes=[pltpu.VMEM((tkf, QB), I32)] + [pltpu.VMEM((rows16, QB), I16)] * 3
                       + [pltpu.VMEM((tkf, N_HEADS * QB), F32), pltpu.VMEM((N_KV_HEADS, V_ROWS, GROUP * QB), F32)],
        compiler_params=_cparams(("arbitrary", "arbitrary")),
        name=name,
    )(kend, qt, qit, wit, kf, vtf, kif, tb, table)


def _softplus(x):
    return jnp.maximum(x, 0.0) + jnp.log1p(jnp.exp(-jnp.abs(x)))


def _mixb_kernel(x_ref, h0_ref, buf_ref, g_ref, wy_ref, by_ref, wx_ref, bx_ref, cw_ref, cb_ref, wr_ref, br_ref,
                 wi_ref, bi_ref, lam_ref, prein_ref, pre_ref, hlast_ref, tail_ref, xe_scr, h_scr, *, tm):
    del prein_ref

    @pl.when(pl.program_id(1) == 0)
    def _():
        xe_scr[0:SUBLANES, :] = buf_ref[0]
        h_scr[...] = h0_ref[0]

    xn = _rmsnorm(x_ref[...], g_ref[...]).astype(BF16)
    y = jax.nn.gelu(_dot(xn, wy_ref[...]) + by_ref[...])
    xe_scr[SUBLANES:SUBLANES + tm, :] = _dot(xn, wx_ref[...]) + bx_ref[...]
    xc = cb_ref[...] + cw_ref[0:1, :] * xe_scr[SUBLANES - 3:SUBLANES - 3 + tm, :]
    for j in range(1, CONV_B):
        xc = xc + cw_ref[j:j + 1, :] * xe_scr[SUBLANES - 3 + j:SUBLANES - 3 + j + tm, :]
    xcb = xc.astype(BF16)
    d_rnn = xc.shape[1]
    blk = d_rnn // LRU_BLOCKS
    r_pre = jnp.concatenate([_dot(xcb[:, n * blk:(n + 1) * blk], wr_ref[n]) for n in range(LRU_BLOCKS)], axis=1)
    i_pre = jnp.concatenate([_dot(xcb[:, n * blk:(n + 1) * blk], wi_ref[n]) for n in range(LRU_BLOCKS)], axis=1)
    r = jax.nn.sigmoid(r_pre + br_ref[...])
    ig = jax.nn.sigmoid(i_pre + bi_ref[...])
    log_a = -LRU_C * r * _softplus(-lam_ref[...])
    a = jnp.exp(log_a)
    u = jnp.sqrt(jnp.tanh(-log_a) * (1.0 + a * a)) * (ig * xc)
    row_in_group = lax.broadcasted_iota(I32, a.shape, 0) % SUBLANES
    s = 1
    while s < SUBLANES:
        a_sh = jnp.where(row_in_group >= s, pltpu.roll(a, s, 0), 1.0)
        u_sh = jnp.where(row_in_group >= s, pltpu.roll(u, s, 0), 0.0)
        u = a * u_sh + u
        a = a * a_sh
        s *= 2
    a = a.reshape(tm // SUBLANES, SUBLANES, d_rnn)
    u = u.reshape(tm // SUBLANES, SUBLANES, d_rnn)
    carry = h_scr[...]
    h_groups = []
    for grp in range(tm // SUBLANES):
        h_grp = a[grp] * carry + u[grp]
        carry = h_grp[SUBLANES - 1:SUBLANES, :]
        h_groups.append(h_grp)
    h = jnp.concatenate(h_groups, axis=0)
    pre_ref[...] = (h * y).astype(BF16)
    h_scr[...] = carry
    hlast_ref[0] = carry
    tail = xe_scr[tm:tm + SUBLANES, :]
    xe_scr[0:SUBLANES, :] = tail
    tail_ref[0] = tail


def _mixc_kernel(x_ref, buf_ref, g_ref, win_ref, cw_ref, prein_ref, pre_ref, tail_ref, pe_scr, *, tm):
    del prein_ref

    @pl.when(pl.program_id(1) == 0)
    def _():
        pe_scr[0:SUBLANES, :] = buf_ref[0]

    d = x_ref.shape[1]
    xn = _rmsnorm(x_ref[...], g_ref[...]).astype(BF16)
    z = _dot(xn, win_ref[...])
    pe_scr[SUBLANES:SUBLANES + tm, :] = z[:, d:2 * d] * z[:, 2 * d:]
    conv = cw_ref[0:1, :] * pe_scr[SUBLANES - 2:SUBLANES - 2 + tm, :]
    for j in range(1, CONV_C):
        conv = conv + cw_ref[j:j + 1, :] * pe_scr[SUBLANES - 2 + j:SUBLANES - 2 + j + tm, :]
    pre_ref[...] = (z[:, :d] * conv).astype(BF16)
    tail = pe_scr[tm:tm + SUBLANES, :]
    pe_scr[0:SUBLANES, :] = tail
    tail_ref[0] = tail


def _seq_specs(n_seq, t_len, row_off, d):
    tm = min(TM_SEQ, t_len)
    n_tt = t_len // tm
    off = row_off // tm
    xrow = pl.BlockSpec((tm, d), lambda s, j: (off + s * n_tt + j, 0))
    state = lambda rows: pl.BlockSpec((1, rows, d), lambda s, j: (s, 0, 0))
    return tm, n_tt, xrow, state


def _mixb_call(x, pre_buf, h0, buf, g, wy, by, wx, bx, cw, cb, wr, br, wi, bi, lam, *, n_seq, t_len, row_off,
               name):
    d = x.shape[1]
    tm, n_tt, xrow, state = _seq_specs(n_seq, t_len, row_off, d)
    consts = (g, wy, by, wx, bx, cw, cb, wr, br, wi, bi, lam)
    return pl.pallas_call(
        functools.partial(_mixb_kernel, tm=tm),
        grid=(n_seq, n_tt),
        in_specs=[xrow, state(1), state(SUBLANES)] + [_const_spec(c.shape) for c in consts]
                 + [pl.BlockSpec(memory_space=pl.ANY)],
        out_specs=[xrow, state(1), state(SUBLANES)],
        out_shape=[jax.ShapeDtypeStruct(pre_buf.shape, BF16), jax.ShapeDtypeStruct((n_seq, 1, d), F32),
                   jax.ShapeDtypeStruct((n_seq, SUBLANES, d), F32)],
        scratch_shapes=[pltpu.VMEM((tm + SUBLANES, d), F32), pltpu.VMEM((1, d), F32)],
        input_output_aliases={3 + len(consts): 0},
        compiler_params=_cparams(("arbitrary", "arbitrary")),
        name=name,
    )(x, h0, buf, *consts, pre_buf)


def _mixc_call(x, pre_buf, buf, g, win, cw, *, n_seq, t_len, row_off, name):
    d = x.shape[1]
    tm, n_tt, xrow, state = _seq_specs(n_seq, t_len, row_off, d)
    consts = (g, win, cw)
    return pl.pallas_call(
        functools.partial(_mixc_kernel, tm=tm),
        grid=(n_seq, n_tt),
        in_specs=[xrow, state(SUBLANES)] + [_const_spec(c.shape) for c in consts]
                 + [pl.BlockSpec(memory_space=pl.ANY)],
        out_specs=[xrow, state(SUBLANES)],
        out_shape=[jax.ShapeDtypeStruct(pre_buf.shape, BF16), jax.ShapeDtypeStruct((n_seq, SUBLANES, d), F32)],
        scratch_shapes=[pltpu.VMEM((tm + SUBLANES, d), F32)],
        input_output_aliases={2 + len(consts): 0},
        compiler_params=_cparams(("arbitrary", "arbitrary")),
        name=name,
    )(x, buf, *consts, pre_buf)


def _front_pad_rows(a, rows):
    return jnp.pad(a, ((0, 0), (rows - a.shape[1], 0), (0, 0)))


def _frame(parts, axis):
    f = jnp.concatenate(parts, axis=axis).astype(BF16)
    pad = [(0, 0)] * f.ndim
    pad[axis] = (0, -f.shape[axis] % SEL_TILE)
    return jnp.pad(f, pad)


def _k_frame(parts):
    f = _frame(parts, 1)
    n_seq, rows = f.shape[0], f.shape[1]
    return f.reshape(n_seq, rows, N_KV_HEADS, HEAD_DIM).transpose(0, 2, 1, 3)


def _v_frame(parts):
    f = _frame(parts, 2)
    n_seq, tkf = f.shape[0], f.shape[2]
    extra = jnp.zeros((n_seq, N_KV_HEADS, V_ROWS - HEAD_DIM, tkf), BF16).at[:, :, 0].set(1)
    f = jnp.concatenate([f.reshape(n_seq, N_KV_HEADS, HEAD_DIM, tkf), extra], axis=2)
    return f.reshape(n_seq, N_KV_HEADS * V_ROWS, tkf)


def _cols(a, start, n):
    b0, b1 = start // QB, -(-(start + n) // QB)
    c = a[b0:b1].transpose(1, 0, 2).reshape(a.shape[1], (b1 - b0) * QB)
    return c[:, start - b0 * QB:start - b0 * QB + n]


def _seq_major(a, n_seq, q_len):
    return a.reshape(a.shape[0], n_seq, q_len).transpose(1, 0, 2)


def _lane_pad_cols(a, n_seq, q_len):
    return jnp.pad(_seq_major(a, n_seq, q_len), ((0, 0), (0, 0), (0, QB - q_len)))


def kernel(x_prompt, x_sample, cache_k, cache_v, cache_kidx, state_h, state_conv_b, state_conv_c, meta_tokens,
           rel_bias, norm_mix, norm_ffn, norm_final, a_wq, a_wk, a_wv, a_wo, a_wiq, a_wik, a_wiw, b_wy, b_by, b_wx,
           b_bx, b_conv_w, b_conv_b, b_wr, b_br, b_wi, b_bi, b_lam, b_wo, b_bo, c_win, c_conv_w, c_wo, f_wg, f_wu,
           f_wd):
    bp, seq, d = x_prompt.shape
    bs, dec = x_sample.shape[0], x_sample.shape[1]
    past = cache_k.shape[2]
    n_meta = meta_tokens.shape[0]
    depth = norm_mix.shape[0]
    assert seq % QB == 0 and seq % TM_SEQ == 0 and past % QB == 0 and dec <= QB and n_meta <= QB
    assert dec % SUBLANES == 0 and n_meta % SUBLANES == 0 and seq % dec == 0 and (bp * seq + bs * dec) % n_meta == 0

    n_main, n_samp = bp * seq, bs * dec
    off_s, off_m = n_main, n_main + n_samp
    n_valid = off_m + n_meta
    n_rows = -(-n_valid // ROW_PAD) * ROW_PAD
    x = jnp.concatenate([x_prompt.reshape(n_main, d), x_sample.reshape(n_samp, d), meta_tokens.astype(F32),
                         jnp.zeros((n_rows - n_valid, d), F32)], axis=0)

    k_sel_p = min(TOP_K_MAX, seq // 4)
    k_sel_s = min(TOP_K_MAX, (past + dec) // 4)
    n_qb = seq // QB
    kv_dim = N_KV_HEADS * HEAD_DIM

    tau = jnp.arange(seq, dtype=I32)
    kend_main = (QB + CHUNK * (tau // CHUNK + 1)).reshape(n_qb, 1, QB)
    lane = jnp.arange(QB, dtype=I32)
    kend_samp = jnp.minimum(past + dec, CHUNK * ((past + jnp.minimum(lane, dec - 1)) // CHUNK + 1)).reshape(1, 1, QB)
    kend_meta = jnp.full((1, 1, QB), n_meta, I32)

    tb = _bias_call(rel_bias.astype(F32))
    zeros_bias = jnp.zeros((1, d), F32)
    gfin = norm_final.reshape(1, d).astype(F32)

    new = {name: [] for name in ("k_p", "v_p", "ki_p", "h_p", "cb_p", "cc_p", "k_s", "v_s", "ki_s", "h_s", "cb_s",
                                 "cc_s")}
    for l in range(depth):
        s = l // N_MIXERS
        g_mix = norm_mix[l].reshape(1, d)
        if l % N_MIXERS == 0:
            wrow = jnp.concatenate([a_wk[s], a_wv[s], a_wik[s]], axis=1).astype(BF16)
            wt = jnp.concatenate([a_wq[s].T * HEAD_DIM ** -0.5, a_wiq[s].T * IDX_DIM ** -0.5, a_wv[s].T, a_wiw[s].T,
                                  jnp.zeros((SUBLANES, d), F32)], axis=0).astype(BF16)
            k, v, ki, qt, qit, vt, wit = _proj_call(x, g_mix, wrow, wt)

            k_meta, v_meta, ki_meta = (a[off_m:n_valid] for a in (k, v, ki))
            k_main, v_main, ki_main = (a[:n_main].reshape(bp, seq, -1) for a in (k, v, ki))
            k_samp, v_samp, ki_samp = (a[off_s:off_m].reshape(bs, dec, -1) for a in (k, v, ki))
            bc = lambda a: jnp.broadcast_to(a[None], (bp,) + a.shape)
            new["k_p"].append(jnp.concatenate([bc(k_meta), k_main], axis=1).reshape(bp, n_meta + seq, N_KV_HEADS,
                                                                                   HEAD_DIM))
            new["v_p"].append(jnp.concatenate([bc(v_meta), v_main], axis=1).reshape(bp, n_meta + seq, N_KV_HEADS,
                                                                                   HEAD_DIM))
            new["ki_p"].append(jnp.concatenate([bc(ki_meta), ki_main], axis=1))
            new["k_s"].append(k_samp.reshape(bs, dec, N_KV_HEADS, HEAD_DIM))
            new["v_s"].append(v_samp.reshape(bs, dec, N_KV_HEADS, HEAD_DIM))
            new["ki_s"].append(ki_samp)

            front = lambda a: jnp.pad(bc(a), ((0, 0), (QB - n_meta, 0), (0, 0)))
            samp_cols = lambda a: _cols(a, off_s, n_samp)
            meta_cols = lambda a: _cols(a, off_m, n_meta)
            vt_meta = meta_cols(vt)
            vt_main = vt[:n_main // QB].reshape(bp, n_qb, kv_dim, QB).transpose(0, 2, 1, 3).reshape(bp, kv_dim, seq)
            vtf_main = _v_frame([jnp.pad(bc(vt_meta), ((0, 0), (0, 0), (QB - n_meta, 0))), vt_main])
            pre = _attn_call(
                kend_main, qt, qit, wit, _k_frame([front(k_meta), k_main]), vtf_main,
                _frame([front(ki_meta), ki_main], 1), tb, rel_bias,
                n_out_rows=n_rows, diag_off=1, kstart=QB - n_meta, k_sel=k_sel_p, name="dsa_attn_prompt")
            vtf_samp = _v_frame([cache_v[s].reshape(bs, past, kv_dim).transpose(0, 2, 1),
                                 _seq_major(samp_cols(vt), bs, dec)])
            o_samp = _attn_call(
                kend_samp, _lane_pad_cols(samp_cols(qt), bs, dec),
                _lane_pad_cols(samp_cols(qit), bs, dec), _lane_pad_cols(samp_cols(wit), bs, dec),
                _k_frame([cache_k[s].reshape(bs, past, kv_dim), k_samp]), vtf_samp,
                _frame([cache_kidx[s], ki_samp], 1), tb, rel_bias,
                n_out_rows=bs * QB, diag_off=past // QB, kstart=0, k_sel=k_sel_s, name="dsa_attn_sample")
            o_meta = _attn_call(
                kend_meta, _lane_pad_cols(meta_cols(qt), 1, n_meta),
                _lane_pad_cols(meta_cols(qit), 1, n_meta), _lane_pad_cols(meta_cols(wit), 1, n_meta),
                _k_frame([k_meta[None]]), _v_frame([vt_meta[None]]), _frame([ki_meta[None]], 1), tb, rel_bias,
                n_out_rows=QB, diag_off=0, kstart=0, k_sel=k_sel_p, name="dsa_attn_meta")
            tail_rows = jnp.concatenate([o_samp.reshape(bs, QB, d)[:, :dec].reshape(n_samp, d), o_meta[:n_meta],
                                         jnp.zeros((n_rows - n_valid, d), BF16)], axis=0)
            pre = lax.dynamic_update_slice(pre, tail_rows, (off_s, 0))
            wo, bo = a_wo[s].astype(BF16), zeros_bias
        elif l % N_MIXERS == 1:
            consts = (g_mix, b_wy[s].astype(BF16), b_by[s].reshape(1, d), b_wx[s].astype(BF16), b_bx[s].reshape(1, d),
                      jnp.pad(b_conv_w[s], ((0, SUBLANES - CONV_B), (0, 0))), b_conv_b[s].reshape(1, d),
                      b_wr[s].astype(BF16), b_br[s].reshape(1, d), b_wi[s].astype(BF16), b_bi[s].reshape(1, d),
                      b_lam[s].reshape(1, d))
            pre = jnp.zeros((n_rows, d), BF16)
            pre, h_m, tail_m = _mixb_call(x, pre, jnp.zeros((1, 1, d), F32), jnp.zeros((1, SUBLANES, d), F32),
                                          *consts, n_seq=1, t_len=n_meta, row_off=off_m, name="rglru_meta")
            pre, h_p, tail_p = _mixb_call(x, pre, jnp.broadcast_to(h_m, (bp, 1, d)),
                                          jnp.broadcast_to(tail_m, (bp, SUBLANES, d)),
                                          *consts, n_seq=bp, t_len=seq, row_off=0, name="rglru_prompt")
            pre, h_s, tail_s = _mixb_call(x, pre, state_h[s].reshape(bs, 1, d),
                                          _front_pad_rows(state_conv_b[s], SUBLANES),
                                          *consts, n_seq=bs, t_len=dec, row_off=off_s, name="rglru_sample")
            new["h_p"].append(h_p.reshape(bp, d))
            new["cb_p"].append(tail_p[:, SUBLANES - (CONV_B - 1):])
            new["h_s"].append(h_s.reshape(bs, d))
            new["cb_s"].append(tail_s[:, SUBLANES - (CONV_B - 1):])
            wo, bo = b_wo[s].astype(BF16), b_bo[s].reshape(1, d)
        else:
            consts = (g_mix, c_win[s].astype(BF16), jnp.pad(c_conv_w[s], ((0, SUBLANES - CONV_C), (0, 0))))
            pre = jnp.zeros((n_rows, d), BF16)
            pre, tail_m = _mixc_call(x, pre, jnp.zeros((1, SUBLANES, d), F32), *consts, n_seq=1, t_len=n_meta,
                                     row_off=off_m, name="sconv_meta")
            pre, tail_p = _mixc_call(x, pre, jnp.broadcast_to(tail_m, (bp, SUBLANES, d)), *consts, n_seq=bp,
                                     t_len=seq, row_off=0, name="sconv_prompt")
            pre, tail_s = _mixc_call(x, pre, _front_pad_rows(state_conv_c[s], SUBLANES), *consts, n_seq=bs,
                                     t_len=dec, row_off=off_s, name="sconv_sample")
            new["cc_p"].append(tail_p[:, SUBLANES - (CONV_C - 1):])
            new["cc_s"].append(tail_s[:, SUBLANES - (CONV_C - 1):])
            wo, bo = c_wo[s].astype(BF16), zeros_bias
        x = _ffn_call(x, pre, wo, bo, norm_ffn[l].reshape(1, d), f_wg[l].astype(BF16), f_wu[l].astype(BF16),
                      f_wd[l].astype(BF16), gfin, final_norm=(l == depth - 1))

    y_prompt = x[:n_main].reshape(bp, seq, d)
    y_sample = x[off_s:off_m].reshape(bs, dec, d)
    st = lambda name: jnp.stack(new[name])
    return (y_prompt, y_sample, st("k_p"), st("v_p"), st("ki_p"), st("h_p"), st("cb_p"), st("cc_p"),
            st("k_s"), st("v_s"), st("ki_s"), st("h_s"), st("cb_s"), st("cc_s"))
```

```python
import functools

import jax
import jax.numpy as jnp
from jax import lax
from jax.experimental import pallas as pl
from jax.experimental.pallas import tpu as pltpu

F32 = jnp.float32
BF16 = jnp.bfloat16
I32 = jnp.int32
I16 = jnp.int16

CHUNK = 64
N_MIXERS = 3
N_HEADS = 16
N_KV_HEADS = 4
GROUP = N_HEADS // N_KV_HEADS
HEAD_DIM = 64
IDX_HEADS = 8
IDX_DIM = 64
TOP_K_MAX = 256
NEG_INF = -1e30
N_BUCKETS = 32
LRU_BLOCKS = 4
CONV_B = 4
CONV_C = 3
LRU_C = 8.0
RMS_EPS = 1e-6
BUCKET_STEPS = (12, 16, 23, 32, 46, 64, 91)
FAR_BUCKET = N_BUCKETS // 2 - 1

LANES = 128
SUBLANES = 8
TM_PROJ = 256
TM_FFN = 512
TM_SEQ = 256
QB = 128
SEL_TILE = 2 * QB
INT_MIN = -2 ** 31
I16_MIN = -2 ** 15
PACK16 = 2 * SUBLANES
V_ROWS = HEAD_DIM + PACK16
LOG2E = 1.4426950408889634
VMEM_LIMIT = 56 * 1024 * 1024


def _cparams(sem):
    return pltpu.CompilerParams(dimension_semantics=sem, vmem_limit_bytes=VMEM_LIMIT)


def _const_spec(shape):
    nd = len(shape)
    return pl.BlockSpec(shape, lambda *_: (0,) * nd, pipeline_mode=pl.Buffered(1))


def _rmsnorm(x, g):
    ms = jnp.mean(x * x, axis=-1, keepdims=True)
    return x * lax.rsqrt(ms + RMS_EPS) * g


def _dot(a, b):
    return jnp.dot(a, b, preferred_element_type=F32)


def _dot_nt(a, b):
    return lax.dot_general(a, b, (((1,), (1,)), ((), ())), preferred_element_type=F32)


def _ffn_kernel(x_ref, pre_ref, wo_ref, bo_ref, gf_ref, wg_ref, wu_ref, wd_ref, gfin_ref, out_ref, *,
                n_chunks, fc, final_norm):
    x1 = x_ref[...] + _dot(pre_ref[...], wo_ref[...]) + bo_ref[...]
    xn = _rmsnorm(x1, gf_ref[...]).astype(BF16)
    acc = x1
    for c in range(n_chunks):
        gt = _dot(xn, wg_ref[:, c * fc:(c + 1) * fc])
        up = _dot(xn, wu_ref[:, c * fc:(c + 1) * fc])
        hm = (gt * jax.nn.sigmoid(gt) * up).astype(BF16)
        acc = acc + _dot(hm, wd_ref[c * fc:(c + 1) * fc, :])
    if final_norm:
        acc = _rmsnorm(acc, gfin_ref[...])
    out_ref[...] = acc


def _ffn_call(x, pre, wo, bo, gf, wg, wu, wd, gfin, final_norm):
    n, d = x.shape
    dff = wg.shape[1]
    n_chunks = 2 if dff % (2 * LANES) == 0 else 1
    fc = dff // n_chunks
    row = lambda i: (i, 0)
    tm = TM_FFN if n % TM_FFN == 0 else TM_PROJ
    return pl.pallas_call(
        functools.partial(_ffn_kernel, n_chunks=n_chunks, fc=fc, final_norm=final_norm),
        grid=(n // tm,),
        in_specs=[pl.BlockSpec((tm, d), row), pl.BlockSpec((tm, pre.shape[1]), row),
                  _const_spec(wo.shape), _const_spec(bo.shape), _const_spec(gf.shape),
                  _const_spec(wg.shape), _const_spec(wu.shape), _const_spec(wd.shape), _const_spec(gfin.shape)],
        out_specs=pl.BlockSpec((tm, d), row),
        out_shape=jax.ShapeDtypeStruct((n, d), F32),
        compiler_params=_cparams(("parallel",)),
        name="outproj_swiglu",
    )(x, pre, wo, bo, gf, wg, wu, wd, gfin)


def _proj_kernel(x_ref, g_ref, wrow_ref, wt_ref, k_ref, v_ref, ki_ref, qt_ref, qit_ref, vt_ref, wit_ref, *,
                 n_kv, n_q, n_iq):
    xn = _rmsnorm(x_ref[...], g_ref[...]).astype(BF16)
    row = _dot(xn, wrow_ref[...])
    k_ref[...] = row[:, :n_kv]
    v_ref[...] = row[:, n_kv:2 * n_kv]
    ki_ref[...] = row[:, 2 * n_kv:]
    tt = _dot_nt(wt_ref[...], xn)
    for b in range(TM_PROJ // QB):
        blk = tt[:, b * QB:(b + 1) * QB]
        qt_ref[b] = blk[:n_q].astype(BF16)
        qit_ref[b] = blk[n_q:n_q + n_iq].astype(BF16)
        vt_ref[b] = blk[n_q + n_iq:n_q + n_iq + n_kv].astype(BF16)
        wit_ref[b] = blk[n_q + n_iq + n_kv:n_q + n_iq + n_kv + IDX_HEADS] * (IDX_HEADS ** -0.5)


def _proj_call(x, g, wrow, wt):
    n, d = x.shape
    n_kv = N_KV_HEADS * HEAD_DIM
    n_q = N_HEADS * HEAD_DIM
    n_iq = IDX_HEADS * IDX_DIM
    row = lambda i: (i, 0)
    blocks = TM_PROJ // QB
    feat_spec = lambda f: pl.BlockSpec((blocks, f, QB), lambda i: (i, 0, 0))
    feat_shape = lambda f, dt: jax.ShapeDtypeStruct((n // QB, f, QB), dt)
    return pl.pallas_call(
        functools.partial(_proj_kernel, n_kv=n_kv, n_q=n_q, n_iq=n_iq),
        grid=(n // TM_PROJ,),
        in_specs=[pl.BlockSpec((TM_PROJ, d), row), _const_spec(g.shape), _const_spec(wrow.shape),
                  _const_spec(wt.shape)],
        out_specs=[pl.BlockSpec((TM_PROJ, n_kv), row), pl.BlockSpec((TM_PROJ, n_kv), row),
                   pl.BlockSpec((TM_PROJ, IDX_DIM), row), feat_spec(n_q), feat_spec(n_iq), feat_spec(n_kv),
                   feat_spec(IDX_HEADS)],
        out_shape=[jax.ShapeDtypeStruct((n, n_kv), F32), jax.ShapeDtypeStruct((n, n_kv), F32),
                   jax.ShapeDtypeStruct((n, IDX_DIM), F32), feat_shape(n_q, BF16), feat_shape(n_iq, BF16),
                   feat_shape(n_kv, BF16), feat_shape(IDX_HEADS, F32)],
        compiler_params=_cparams(("parallel",)),
        name="attn_proj",
    )(x, g, wrow, wt)


def _bias_kernel(tab_ref, tb_ref):
    h = pl.program_id(0)
    d = pl.program_id(1)
    kj = lax.broadcasted_iota(I32, (QB, QB), 0)
    qi = lax.broadcasted_iota(I32, (QB, QB), 1)
    rel = (d - 1) * QB + kj - qi
    n = jnp.abs(rel)
    large = jnp.full((QB, QB), N_BUCKETS // 4, I32)
    for s in BUCKET_STEPS:
        large = large + jnp.where(n >= s, 1, 0)
    bucket = jnp.where(rel > 0, N_BUCKETS // 2, 0) + jnp.where(n < N_BUCKETS // 4, n, large)
    val = jnp.zeros((QB, QB), F32)
    for b in range(N_BUCKETS):
        val = jnp.where(bucket == b, tab_ref[b, h], val)
    tb_ref[0] = val


def _bias_call(table):
    return pl.pallas_call(
        _bias_kernel,
        grid=(N_HEADS, 2),
        in_specs=[pl.BlockSpec(memory_space=pltpu.SMEM)],
        out_specs=pl.BlockSpec((1, QB, QB), lambda h, d: (h, d, 0)),
        out_shape=jax.ShapeDtypeStruct((N_HEADS, 2 * QB, QB), F32),
        compiler_params=_cparams(("arbitrary", "arbitrary")),
        name="rel_bias_tiles",
    )(table)


def _attn_kernel(kend_ref, qt_ref, qit_ref, wit_ref, kf_ref, vtf_ref, kif_ref, tb_ref, tab_ref, o_ref,
                 key_scr, hi_scr, lo_scr, lo2_scr, ls_scr, oacc_scr, *, diag_off, kstart, k_sel, has_prev,
                 row_bits):
    diag = pl.program_id(1) + diag_off
    nt = diag + 1
    nst = (nt + SEL_TILE // QB - 1) // (SEL_TILE // QB)
    kend = kend_ref[0]
    qi_cat = jnp.concatenate([qit_ref[0, h * IDX_DIM:(h + 1) * IDX_DIM, :] for h in range(IDX_HEADS)], axis=1)
    wi = wit_ref[0]
    sel_iota = lax.broadcasted_iota(I32, (SEL_TILE, QB), 0)

    def tile_start(t):
        return pl.multiple_of(t * QB, QB)

    def sel_start(t):
        return pl.multiple_of(t * SEL_TILE, SEL_TILE)

    def score_tile(t, carry):
        r0 = sel_start(t)
        s = _dot(kif_ref[0, pl.ds(r0, SEL_TILE), :], qi_cat)
        sc = jnp.zeros((SEL_TILE, QB), F32)
        for h in range(IDX_HEADS):
            sc = sc + wi[h:h + 1, :] * jnp.maximum(s[:, h * QB:(h + 1) * QB], 0.0)
        bits = lax.bitcast_convert_type(sc + 0.0, I32)
        key = jnp.where(bits >= 0, bits, bits ^ 0x7FFFFFFF)
        rows = r0 + sel_iota
        adm = (rows >= kstart) & (rows < kend)
        key = jnp.where(adm, key, INT_MIN)
        key_scr[pl.ds(r0, SEL_TILE), :] = key
        hi_scr[pl.ds(r0, SEL_TILE), :] = (key >> 16).astype(I16)
        lo_scr[pl.ds(r0, SEL_TILE), :] = ((key & 0xFFFF) + I16_MIN).astype(I16)
        return carry

    lax.fori_loop(0, nst, score_tile, 0)

    @pl.when(nst % 2 == 1)
    def _():
        blank = jnp.full((SEL_TILE, QB), I16_MIN, I16)
        hi_scr[pl.ds(sel_start(nst), SEL_TILE), :] = blank
        lo_scr[pl.ds(sel_start(nst), SEL_TILE), :] = blank

    n_pair = (nst + 1) // 2
    pair_vregs = 2 * SEL_TILE // PACK16

    def pair_start(t):
        return pl.multiple_of(t * 2 * SEL_TILE, 2 * SEL_TILE)

    def as_packed(v):
        return jnp.broadcast_to(v, (PACK16, QB)).astype(I16)

    def count16(src_scr, cands):
        c16 = [as_packed(c)[None] for c in cands]

        def body(t, accs):
            v = src_scr[pl.ds(pair_start(t), 2 * SEL_TILE), :].reshape(pair_vregs, PACK16, QB)
            new_accs = []
            for c, acc in zip(c16, accs):
                ind = jnp.where(v >= c, jnp.ones((), BF16), jnp.zeros((), BF16))
                parts = [ind[i] for i in range(pair_vregs)]
                while len(parts) > 1:
                    parts = [parts[i] + parts[i + 1] for i in range(0, len(parts), 2)]
                new_accs.append(acc + parts[0])
            return tuple(new_accs)

        accs = lax.fori_loop(0, n_pair, body, tuple(jnp.zeros((PACK16, QB), BF16) for _ in cands))
        return [acc.astype(F32).sum(axis=0, keepdims=True) for acc in accs]

    def search16(src_scr, base, c_start):
        def step(b, carry):
            t_acc, c_acc = carry
            unit = lax.shift_left(jnp.int32(1), 14 - 2 * b)
            cands = [t_acc + j * unit for j in (1, 2, 3)]
            for cand, c in zip(cands, count16(src_scr, cands)):
                ok = base + c >= k_sel
                t_acc, c_acc = jnp.where(ok, cand, t_acc), jnp.where(ok, base + c, c_acc)
            return t_acc, c_acc

        return lax.fori_loop(0, 8, step, (jnp.full((1, QB), I16_MIN, I32), c_start))

    def count(pred):
        def body(t, acc):
            r0 = sel_start(t)
            ind = jnp.where(pred(key_scr[pl.ds(r0, SEL_TILE), :], r0 + sel_iota), 1, 0)
            return acc + ind.reshape(SEL_TILE // SUBLANES, SUBLANES, QB).sum(axis=0)
        acc = lax.fori_loop(0, nst, body, jnp.zeros((SUBLANES, QB), I32))
        return acc.sum(axis=0, keepdims=True)

    n_all = (n_pair * 2 * SEL_TILE).astype(F32)
    thr_hi, c_hi = search16(hi_scr, 0.0, jnp.full((1, QB), n_all, F32))
    above = jnp.where(thr_hi == -I16_MIN - 1, 0.0, count16(hi_scr, [thr_hi + 1])[0])
    hi16 = as_packed(thr_hi)[None]

    def low_tile(t, carry):
        rows = pl.ds(pair_start(t), 2 * SEL_TILE)
        hi = hi_scr[rows, :].reshape(pair_vregs, PACK16, QB)
        lo = lo_scr[rows, :].reshape(pair_vregs, PACK16, QB)
        lo2_scr[rows, :] = jnp.where(hi == hi16, lo, jnp.full((), I16_MIN, I16)).reshape(2 * SEL_TILE, QB)
        return carry

    lax.fori_loop(0, n_pair, low_tile, 0)
    thr_lo, c_ge = search16(lo2_scr, above, c_hi)
    thr = thr_hi * 65536 + (thr_lo - I16_MIN)

    tied = (c_ge > k_sel) & (thr > INT_MIN)
    big = jnp.full((1, QB), 2 ** row_bits, I32)

    def tie_limit():
        need = k_sel - count(lambda kt, rows: kt > thr)

        def lim_step(b, lim):
            cand = lim + lax.shift_left(jnp.int32(1), row_bits - 1 - b)
            c = count(lambda kt, rows: (kt == thr) & (rows < cand))
            return jnp.where(c < need, cand, lim)

        lim = lax.fori_loop(0, row_bits, lim_step, jnp.zeros((1, QB), I32))
        return jnp.where(tied, lim, big)

    rlim = lax.cond(jnp.max(tied.astype(I32)) > 0, tie_limit, lambda: big)
    rlim = jnp.where(thr > INT_MIN, rlim, -1)

    gq = GROUP * QB
    qg = [jnp.concatenate([qt_ref[0, (g * GROUP + r) * HEAD_DIM:(g * GROUP + r + 1) * HEAD_DIM, :]
                           for r in range(GROUP)], axis=1) for g in range(N_KV_HEADS)]

    def logits_rows(r0, rows, m8, bias_of_head):
        kt = key_scr[pl.ds(r0, rows), :]
        keep = (kt > thr) | ((kt == thr) & (sel_iota[:rows] <= rlim - r0))
        new_m8 = []
        for g in range(N_KV_HEADS):
            lg = _dot(kf_ref[0, g, pl.ds(r0, rows), :], qg[g])
            parts = []
            for r in range(GROUP):
                part = (lg[:, r * QB:(r + 1) * QB] + bias_of_head(g * GROUP + r)) * LOG2E
                parts.append(jnp.where(keep, part, NEG_INF))
            lg = jnp.concatenate(parts, axis=1)
            ls_scr[pl.ds(r0, rows), g * gq:(g + 1) * gq] = lg
            new_m8.append(jnp.maximum(m8[g], lg.reshape(rows // SUBLANES, SUBLANES, gq).max(axis=0)))
        return tuple(new_m8)

    far_bias = lambda h: tab_ref[FAR_BUCKET, h]
    m8 = tuple(jnp.full((SUBLANES, gq), NEG_INF, F32) for _ in range(N_KV_HEADS))
    n_far = diag - 1 if has_prev else diag
    m8 = lax.fori_loop(0, n_far // 2, lambda t, m: logits_rows(sel_start(t), SEL_TILE, m, far_bias), m8)
    m8 = lax.cond(n_far % 2 == 1, lambda m: logits_rows(tile_start(n_far - 1), QB, m, far_bias), lambda m: m, m8)
    if has_prev:
        m8 = logits_rows(tile_start(diag - 1), 2 * QB, m8, lambda h: tb_ref[h])
    else:
        m8 = logits_rows(tile_start(diag), QB, m8, lambda h: tb_ref[h, QB:, :])
    m = [jnp.max(m8[g], axis=0, keepdims=True) for g in range(N_KV_HEADS)]

    @pl.when(nt % 2 == 1)
    def _():
        ls_scr[pl.ds(tile_start(nt), QB), :] = jnp.full((QB, N_HEADS * QB), NEG_INF, F32)

    oacc_scr[...] = jnp.zeros(oacc_scr.shape, F32)

    def pv_tile(t, carry):
        r0 = sel_start(t)
        for g in range(N_KV_HEADS):
            p = jnp.exp2(ls_scr[pl.ds(r0, SEL_TILE), g * gq:(g + 1) * gq] - m[g])
            oacc_scr[g] += _dot(vtf_ref[0, g * V_ROWS:(g + 1) * V_ROWS, pl.ds(r0, SEL_TILE)], p.astype(BF16))
        return carry

    lax.fori_loop(0, nst, pv_tile, 0)
    out_rows = []
    for g in range(N_KV_HEADS):
        o_g = oacc_scr[g, :HEAD_DIM] / oacc_scr[g, HEAD_DIM:HEAD_DIM + 1]
        out_rows.extend(o_g[:, r * QB:(r + 1) * QB] for r in range(GROUP))
    o_ref[...] = jnp.concatenate(out_rows, axis=0).T.astype(BF16)


def _attn_call(kend, qt, qit, wit, kf, vtf, kif, tb, table, *, n_out_rows, diag_off, kstart, k_sel, name):
    n_seq, tkf = kf.shape[0], kf.shape[2]
    assert tkf % SEL_TILE == 0 and vtf.shape[2] == tkf and kif.shape[1] == tkf
    rows16 = -(-tkf // (2 * SEL_TILE)) * 2 * SEL_TILE
    assert rows16 // PACK16 <= 256, "per-slot bf16 counts must stay exact"
    n_qb = kend.shape[0]
    has_prev = diag_off >= 1
    row_bits = max(1, (tkf - 1).bit_length())
    feat = lambda a: pl.BlockSpec((1, a.shape[1], QB), lambda s, i: (s * n_qb + i, 0, 0))
    return pl.pallas_call(
        functools.partial(_attn_kernel, diag_off=diag_off, kstart=kstart, k_sel=k_sel, has_prev=has_prev,
                          row_bits=row_bits),
        grid=(n_seq, n_qb),
        in_specs=[pl.BlockSpec((1, 1, QB), lambda s, i: (i, 0, 0)), feat(qt), feat(qit), feat(wit),
                  pl.BlockSpec((1, N_KV_HEADS, tkf, HEAD_DIM), lambda s, i: (s, 0, 0, 0)),
                  pl.BlockSpec((1, N_KV_HEADS * V_ROWS, tkf), lambda s, i: (s, 0, 0)),
                  pl.BlockSpec((1, tkf, IDX_DIM), lambda s, i: (s, 0, 0)),
                  _const_spec(tb.shape), pl.BlockSpec(memory_space=pltpu.SMEM)],
        out_specs=pl.BlockSpec((QB, N_HEADS * HEAD_DIM), lambda s, i: (s * n_qb + i, 0)),
        out_shape=jax.ShapeDtypeStruct((n_out_rows, N_HEADS * HEAD_DIM), BF16),
        scratch_shapes=[pltpu.VMEM((tkf, QB), I32)] + [pltpu.VMEM((rows16, QB), I16)] * 3
                       + [pltpu.VMEM((tkf, N_HEADS * QB), F32), pltpu.VMEM((N_KV_HEADS, V_ROWS, GROUP * QB), F32)],
        compiler_params=_cparams(("arbitrary", "arbitrary")),
        name=name,
    )(kend, qt, qit, wit, kf, vtf, kif, tb, table)


def _softplus(x):
    return jnp.maximum(x, 0.0) + jnp.log1p(jnp.exp(-jnp.abs(x)))


def _mixb_kernel(x_ref, h0_ref, buf_ref, g_ref, wy_ref, by_ref, wx_ref, bx_ref, cw_ref, cb_ref, wr_ref, br_ref,
                 wi_ref, bi_ref, lam_ref, *rest, tm):
    pre_ref, hlast_ref, tail_ref, xe_scr, h_scr = rest[-5:]

    @pl.when(pl.program_id(1) == 0)
    def _():
        xe_scr[0:SUBLANES, :] = buf_ref[0]
        h_scr[...] = h0_ref[0]

    xn = _rmsnorm(x_ref[...], g_ref[...]).astype(BF16)
    y = jax.nn.gelu(_dot(xn, wy_ref[...]) + by_ref[...])
    xe_scr[SUBLANES:SUBLANES + tm, :] = _dot(xn, wx_ref[...]) + bx_ref[...]
    xc = cb_ref[...] + cw_ref[0:1, :] * xe_scr[SUBLANES - 3:SUBLANES - 3 + tm, :]
    for j in range(1, CONV_B):
        xc = xc + cw_ref[j:j + 1, :] * xe_scr[SUBLANES - 3 + j:SUBLANES - 3 + j + tm, :]
    xcb = xc.astype(BF16)
    d_rnn = xc.shape[1]
    blk = d_rnn // LRU_BLOCKS
    r_pre = jnp.concatenate([_dot(xcb[:, n * blk:(n + 1) * blk], wr_ref[n]) for n in range(LRU_BLOCKS)], axis=1)
    i_pre = jnp.concatenate([_dot(xcb[:, n * blk:(n + 1) * blk], wi_ref[n]) for n in range(LRU_BLOCKS)], axis=1)
    r = jax.nn.sigmoid(r_pre + br_ref[...])
    ig = jax.nn.sigmoid(i_pre + bi_ref[...])
    log_a = -LRU_C * r * _softplus(-lam_ref[...])
    a = jnp.exp(log_a)
    u = jnp.sqrt(jnp.tanh(-log_a) * (1.0 + a * a)) * (ig * xc)
    row_in_group = lax.broadcasted_iota(I32, a.shape, 0) % SUBLANES
    s = 1
    while s < SUBLANES:
        a_sh = jnp.where(row_in_group >= s, pltpu.roll(a, s, 0), 1.0)
        u_sh = jnp.where(row_in_group >= s, pltpu.roll(u, s, 0), 0.0)
        u = a * u_sh + u
        a = a * a_sh
        s *= 2
    a = a.reshape(tm // SUBLANES, SUBLANES, d_rnn)
    u = u.reshape(tm // SUBLANES, SUBLANES, d_rnn)
    carry = h_scr[...]
    h_groups = []
    for grp in range(tm // SUBLANES):
        h_grp = a[grp] * carry + u[grp]
        carry = h_grp[SUBLANES - 1:SUBLANES, :]
        h_groups.append(h_grp)
    h = jnp.concatenate(h_groups, axis=0)
    pre_ref[...] = (h * y).astype(BF16)
    h_scr[...] = carry
    hlast_ref[0] = carry
    tail = xe_scr[tm:tm + SUBLANES, :]
    xe_scr[0:SUBLANES, :] = tail
    tail_ref[0] = tail


def _mixc_kernel(x_ref, buf_ref, g_ref, win_ref, cw_ref, *rest, tm):
    pre_ref, tail_ref, pe_scr = rest[-3:]

    @pl.when(pl.program_id(1) == 0)
    def _():
        pe_scr[0:SUBLANES, :] = buf_ref[0]

    d = x_ref.shape[1]
    xn = _rmsnorm(x_ref[...], g_ref[...]).astype(BF16)
    z = _dot(xn, win_ref[...])
    pe_scr[SUBLANES:SUBLANES + tm, :] = z[:, d:2 * d] * z[:, 2 * d:]
    conv = cw_ref[0:1, :] * pe_scr[SUBLANES - 2:SUBLANES - 2 + tm, :]
    for j in range(1, CONV_C):
        conv = conv + cw_ref[j:j + 1, :] * pe_scr[SUBLANES - 2 + j:SUBLANES - 2 + j + tm, :]
    pre_ref[...] = (z[:, :d] * conv).astype(BF16)
    tail = pe_scr[tm:tm + SUBLANES, :]
    pe_scr[0:SUBLANES, :] = tail
    tail_ref[0] = tail


def _shared_out(buf, index):
    if buf is None:
        return {"specs": [], "args": (), "aliases": {}}
    return {"specs": [pl.BlockSpec(memory_space=pl.ANY)], "args": (buf,), "aliases": {index: 0}}


def _seq_specs(n_seq, t_len, row_off, d):
    tm = min(TM_SEQ, t_len)
    n_tt = t_len // tm
    off = row_off // tm
    xrow = pl.BlockSpec((tm, d), lambda s, j: (off + s * n_tt + j, 0))
    state = lambda rows: pl.BlockSpec((1, rows, d), lambda s, j: (s, 0, 0))
    return tm, n_tt, xrow, state


def _mixb_call(x, pre_buf, h0, buf, g, wy, by, wx, bx, cw, cb, wr, br, wi, bi, lam, *, n_seq, t_len, row_off,
               name):
    d = x.shape[1]
    tm, n_tt, xrow, state = _seq_specs(n_seq, t_len, row_off, d)
    consts = (g, wy, by, wx, bx, cw, cb, wr, br, wi, bi, lam)
    shared = _shared_out(pre_buf, 3 + len(consts))
    return pl.pallas_call(
        functools.partial(_mixb_kernel, tm=tm),
        grid=(n_seq, n_tt),
        in_specs=[xrow, state(1), state(SUBLANES)] + [_const_spec(c.shape) for c in consts] + shared["specs"],
        out_specs=[xrow, state(1), state(SUBLANES)],
        out_shape=[jax.ShapeDtypeStruct((x.shape[0], d), BF16), jax.ShapeDtypeStruct((n_seq, 1, d), F32),
                   jax.ShapeDtypeStruct((n_seq, SUBLANES, d), F32)],
        scratch_shapes=[pltpu.VMEM((tm + SUBLANES, d), F32), pltpu.VMEM((1, d), F32)],
        input_output_aliases=shared["aliases"],
        compiler_params=_cparams(("arbitrary", "arbitrary")),
        name=name,
    )(x, h0, buf, *consts, *shared["args"])


def _mixc_call(x, pre_buf, buf, g, win, cw, *, n_seq, t_len, row_off, name):
    d = x.shape[1]
    tm, n_tt, xrow, state = _seq_specs(n_seq, t_len, row_off, d)
    consts = (g, win, cw)
    shared = _shared_out(pre_buf, 2 + len(consts))
    return pl.pallas_call(
        functools.partial(_mixc_kernel, tm=tm),
        grid=(n_seq, n_tt),
        in_specs=[xrow, state(SUBLANES)] + [_const_spec(c.shape) for c in consts] + shared["specs"],
        out_specs=[xrow, state(SUBLANES)],
        out_shape=[jax.ShapeDtypeStruct((x.shape[0], d), BF16), jax.ShapeDtypeStruct((n_seq, SUBLANES, d), F32)],
        scratch_shapes=[pltpu.VMEM((tm + SUBLANES, d), F32)],
        input_output_aliases=shared["aliases"],
        compiler_params=_cparams(("arbitrary", "arbitrary")),
        name=name,
    )(x, buf, *consts, *shared["args"])


def _front_pad_rows(a, rows):
    return jnp.pad(a, ((0, 0), (rows - a.shape[1], 0), (0, 0)))


def _frame(parts, axis):
    f = jnp.concatenate(parts, axis=axis).astype(BF16)
    pad = [(0, 0)] * f.ndim
    pad[axis] = (0, -f.shape[axis] % SEL_TILE)
    return jnp.pad(f, pad)


def _k_frame(parts):
    f = _frame(parts, 1)
    n_seq, rows = f.shape[0], f.shape[1]
    return f.reshape(n_seq, rows, N_KV_HEADS, HEAD_DIM).transpose(0, 2, 1, 3)


def _v_frame(parts):
    f = _frame(parts, 2)
    n_seq, tkf = f.shape[0], f.shape[2]
    extra = jnp.zeros((n_seq, N_KV_HEADS, V_ROWS - HEAD_DIM, tkf), BF16).at[:, :, 0].set(1)
    f = jnp.concatenate([f.reshape(n_seq, N_KV_HEADS, HEAD_DIM, tkf), extra], axis=2)
    return f.reshape(n_seq, N_KV_HEADS * V_ROWS, tkf)


def _cols(a, start, n):
    b0, b1 = start // QB, -(-(start + n) // QB)
    c = a[b0:b1].transpose(1, 0, 2).reshape(a.shape[1], (b1 - b0) * QB)
    return c[:, start - b0 * QB:start - b0 * QB + n]


def _seq_major(a, n_seq, q_len):
    return a.reshape(a.shape[0], n_seq, q_len).transpose(1, 0, 2)


def _lane_pad_cols(a, n_seq, q_len):
    return jnp.pad(_seq_major(a, n_seq, q_len), ((0, 0), (0, 0), (0, QB - q_len)))


def kernel(x_prompt, x_sample, cache_k, cache_v, cache_kidx, state_h, state_conv_b, state_conv_c, meta_tokens,
           rel_bias, norm_mix, norm_ffn, norm_final, a_wq, a_wk, a_wv, a_wo, a_wiq, a_wik, a_wiw, b_wy, b_by, b_wx,
           b_bx, b_conv_w, b_conv_b, b_wr, b_br, b_wi, b_bi, b_lam, b_wo, b_bo, c_win, c_conv_w, c_wo, f_wg, f_wu,
           f_wd):
    bp, seq, d = x_prompt.shape
    bs, dec = x_sample.shape[0], x_sample.shape[1]
    past = cache_k.shape[2]
    n_meta = meta_tokens.shape[0]
    depth = norm_mix.shape[0]
    assert seq % QB == 0 and seq % TM_SEQ == 0 and past % QB == 0 and dec <= QB and n_meta <= QB
    assert dec % PACK16 == 0 and n_meta % PACK16 == 0 and (bs * dec) % n_meta == 0

    n_main, n_samp = bp * seq, bs * dec
    n_tail = -(-(n_samp + n_meta) // TM_PROJ) * TM_PROJ
    x_main = x_prompt.reshape(n_main, d)
    x_tail = jnp.concatenate([x_sample.reshape(n_samp, d), meta_tokens.astype(F32),
                              jnp.zeros((n_tail - n_samp - n_meta, d), F32)], axis=0)
    tail_fill = jnp.zeros((n_tail - n_samp - n_meta, d), BF16)

    k_sel_p = min(TOP_K_MAX, seq // 4)
    k_sel_s = min(TOP_K_MAX, (past + dec) // 4)
    n_qb = seq // QB
    kv_dim = N_KV_HEADS * HEAD_DIM

    tau = jnp.arange(seq, dtype=I32)
    kend_main = (QB + CHUNK * (tau // CHUNK + 1)).reshape(n_qb, 1, QB)
    lane = jnp.arange(QB, dtype=I32)
    kend_samp = jnp.minimum(past + dec, CHUNK * ((past + jnp.minimum(lane, dec - 1)) // CHUNK + 1)).reshape(1, 1, QB)
    kend_meta = jnp.full((1, 1, QB), n_meta, I32)

    tb = _bias_call(rel_bias.astype(F32))
    zeros_bias = jnp.zeros((1, d), F32)
    gfin = norm_final.reshape(1, d).astype(F32)

    new = {name: [] for name in ("k_p", "v_p", "ki_p", "h_p", "cb_p", "cc_p", "k_s", "v_s", "ki_s", "h_s", "cb_s",
                                 "cc_s")}
    for l in range(depth):
        s = l // N_MIXERS
        g_mix = norm_mix[l].reshape(1, d)
        if l % N_MIXERS == 0:
            wrow = jnp.concatenate([a_wk[s], a_wv[s], a_wik[s]], axis=1).astype(BF16)
            wt = jnp.concatenate([a_wq[s].T * HEAD_DIM ** -0.5, a_wiq[s].T * IDX_DIM ** -0.5, a_wv[s].T, a_wiw[s].T,
                                  jnp.zeros((SUBLANES, d), F32)], axis=0).astype(BF16)
            k, v, ki, qt, qit, vt, wit = _proj_call(x_main, g_mix, wrow, wt)
            k_t, v_t, ki_t, qt_t, qit_t, vt_t, wit_t = _proj_call(x_tail, g_mix, wrow, wt)

            k_meta, v_meta, ki_meta = (a[n_samp:n_samp + n_meta] for a in (k_t, v_t, ki_t))
            k_main, v_main, ki_main = (a.reshape(bp, seq, -1) for a in (k, v, ki))
            k_samp, v_samp, ki_samp = (a[:n_samp].reshape(bs, dec, -1) for a in (k_t, v_t, ki_t))
            bc = lambda a: jnp.broadcast_to(a[None], (bp,) + a.shape)
            new["k_p"].append(jnp.concatenate([bc(k_meta), k_main], axis=1).reshape(bp, n_meta + seq, N_KV_HEADS,
                                                                                   HEAD_DIM))
            new["v_p"].append(jnp.concatenate([bc(v_meta), v_main], axis=1).reshape(bp, n_meta + seq, N_KV_HEADS,
                                                                                   HEAD_DIM))
            new["ki_p"].append(jnp.concatenate([bc(ki_meta), ki_main], axis=1))
            new["k_s"].append(k_samp.reshape(bs, dec, N_KV_HEADS, HEAD_DIM))
            new["v_s"].append(v_samp.reshape(bs, dec, N_KV_HEADS, HEAD_DIM))
            new["ki_s"].append(ki_samp)

            front = lambda a: jnp.pad(bc(a), ((0, 0), (QB - n_meta, 0), (0, 0)))
            samp_cols = lambda a: _cols(a, 0, n_samp)
            meta_cols = lambda a: _cols(a, n_samp, n_meta)
            vt_meta = meta_cols(vt_t)
            vt_main = vt.reshape(bp, n_qb, kv_dim, QB).transpose(0, 2, 1, 3).reshape(bp, kv_dim, seq)
            vtf_main = _v_frame([jnp.pad(bc(vt_meta), ((0, 0), (0, 0), (QB - n_meta, 0))), vt_main])
            pre_main = _attn_call(
                kend_main, qt, qit, wit, _k_frame([front(k_meta), k_main]), vtf_main,
                _frame([front(ki_meta), ki_main], 1), tb, rel_bias,
                n_out_rows=n_main, diag_off=1, kstart=QB - n_meta, k_sel=k_sel_p, name="dsa_attn_prompt")
            vtf_samp = _v_frame([cache_v[s].reshape(bs, past, kv_dim).transpose(0, 2, 1),
                                 _seq_major(samp_cols(vt_t), bs, dec)])
            o_samp = _attn_call(
                kend_samp, _lane_pad_cols(samp_cols(qt_t), bs, dec),
                _lane_pad_cols(samp_cols(qit_t), bs, dec), _lane_pad_cols(samp_cols(wit_t), bs, dec),
                _k_frame([cache_k[s].reshape(bs, past, kv_dim), k_samp]), vtf_samp,
                _frame([cache_kidx[s], ki_samp], 1), tb, rel_bias,
                n_out_rows=bs * QB, diag_off=past // QB, kstart=0, k_sel=k_sel_s, name="dsa_attn_sample")
            o_meta = _attn_call(
                kend_meta, _lane_pad_cols(meta_cols(qt_t), 1, n_meta),
                _lane_pad_cols(meta_cols(qit_t), 1, n_meta), _lane_pad_cols(meta_cols(wit_t), 1, n_meta),
                _k_frame([k_meta[None]]), _v_frame([vt_meta[None]]), _frame([ki_meta[None]], 1), tb, rel_bias,
                n_out_rows=QB, diag_off=0, kstart=0, k_sel=k_sel_p, name="dsa_attn_meta")
            pre_tail = jnp.concatenate([o_samp.reshape(bs, QB, d)[:, :dec].reshape(n_samp, d), o_meta[:n_meta],
                                        tail_fill], axis=0)
            wo, bo = a_wo[s].astype(BF16), zeros_bias
        elif l % N_MIXERS == 1:
            consts = (g_mix, b_wy[s].astype(BF16), b_by[s].reshape(1, d), b_wx[s].astype(BF16), b_bx[s].reshape(1, d),
                      jnp.pad(b_conv_w[s], ((0, SUBLANES - CONV_B), (0, 0))), b_conv_b[s].reshape(1, d),
                      b_wr[s].astype(BF16), b_br[s].reshape(1, d), b_wi[s].astype(BF16), b_bi[s].reshape(1, d),
                      b_lam[s].reshape(1, d))
            pre_tail = jnp.zeros((n_tail, d), BF16)
            pre_tail, h_m, tail_m = _mixb_call(x_tail, pre_tail, jnp.zeros((1, 1, d), F32),
                                               jnp.zeros((1, SUBLANES, d), F32), *consts, n_seq=1, t_len=n_meta,
                                               row_off=n_samp, name="rglru_meta")
            pre_tail, h_s, tail_s = _mixb_call(x_tail, pre_tail, state_h[s].reshape(bs, 1, d),
                                               _front_pad_rows(state_conv_b[s], SUBLANES), *consts, n_seq=bs,
                                               t_len=dec, row_off=0, name="rglru_sample")
            pre_main, h_p, tail_p = _mixb_call(x_main, None, jnp.broadcast_to(h_m, (bp, 1, d)),
                                               jnp.broadcast_to(tail_m, (bp, SUBLANES, d)), *consts, n_seq=bp,
                                               t_len=seq, row_off=0, name="rglru_prompt")
            new["h_p"].append(h_p.reshape(bp, d))
            new["cb_p"].append(tail_p[:, SUBLANES - (CONV_B - 1):])
            new["h_s"].append(h_s.reshape(bs, d))
            new["cb_s"].append(tail_s[:, SUBLANES - (CONV_B - 1):])
            wo, bo = b_wo[s].astype(BF16), b_bo[s].reshape(1, d)
        else:
            consts = (g_mix, c_win[s].astype(BF16), jnp.pad(c_conv_w[s], ((0, SUBLANES - CONV_C), (0, 0))))
            pre_tail = jnp.zeros((n_tail, d), BF16)
            pre_tail, tail_m = _mixc_call(x_tail, pre_tail, jnp.zeros((1, SUBLANES, d), F32), *consts, n_seq=1,
                                          t_len=n_meta, row_off=n_samp, name="sconv_meta")
            pre_tail, tail_s = _mixc_call(x_tail, pre_tail, _front_pad_rows(state_conv_c[s], SUBLANES), *consts,
                                          n_seq=bs, t_len=dec, row_off=0, name="sconv_sample")
            pre_main, tail_p = _mixc_call(x_main, None, jnp.broadcast_to(tail_m, (bp, SUBLANES, d)), *consts,
                                          n_seq=bp, t_len=seq, row_off=0, name="sconv_prompt")
            new["cc_p"].append(tail_p[:, SUBLANES - (CONV_C - 1):])
            new["cc_s"].append(tail_s[:, SUBLANES - (CONV_C - 1):])
            wo, bo = c_wo[s].astype(BF16), zeros_bias
        ffn = (wo, bo, norm_ffn[l].reshape(1, d), f_wg[l].astype(BF16), f_wu[l].astype(BF16), f_wd[l].astype(BF16),
               gfin, l == depth - 1)
        x_main = _ffn_call(x_main, pre_main, *ffn)
        x_tail = _ffn_call(x_tail, pre_tail, *ffn)

    y_prompt = x_main.reshape(bp, seq, d)
    y_sample = x_tail[:n_samp].reshape(bs, dec, d)
    st = lambda name: jnp.stack(new[name])
    return (y_prompt, y_sample, st("k_p"), st("v_p"), st("ki_p"), st("h_p"), st("cb_p"), st("cc_p"),
            st("k_s"), st("v_s"), st("ki_s"), st("h_s"), st("cb_s"), st("cc_s"))
```

```python
import functools

import jax
import jax.numpy as jnp
from jax import lax
from jax.experimental import pallas as pl
from jax.experimental.pallas import tpu as pltpu

F32 = jnp.float32
BF16 = jnp.bfloat16
I32 = jnp.int32
I16 = jnp.int16

CHUNK = 64
N_MIXERS = 3
N_HEADS = 16
N_KV_HEADS = 4
GROUP = N_HEADS // N_KV_HEADS
HEAD_DIM = 64
IDX_HEADS = 8
IDX_DIM = 64
TOP_K_MAX = 256
NEG_INF = -1e30
N_BUCKETS = 32
LRU_BLOCKS = 4
CONV_B = 4
CONV_C = 3
LRU_C = 8.0
RMS_EPS = 1e-6
BUCKET_STEPS = (12, 16, 23, 32, 46, 64, 91)
FAR_BUCKET = N_BUCKETS // 2 - 1

LANES = 128
SUBLANES = 8
TM_PROJ = 256
TM_FFN = 512
TM_SEQ = 256
QB = 128
SEL_TILE = 2 * QB
INT_MIN = -2 ** 31
I16_MIN = -2 ** 15
PACK16 = 2 * SUBLANES
V_ROWS = HEAD_DIM + PACK16
LOG2E = 1.4426950408889634
VMEM_LIMIT = 56 * 1024 * 1024


def _cparams(sem):
    return pltpu.CompilerParams(dimension_semantics=sem, vmem_limit_bytes=VMEM_LIMIT)


def _const_spec(shape):
    nd = len(shape)
    return pl.BlockSpec(shape, lambda *_: (0,) * nd, pipeline_mode=pl.Buffered(1))


def _rmsnorm(x, g):
    ms = jnp.mean(x * x, axis=-1, keepdims=True)
    return x * lax.rsqrt(ms + RMS_EPS) * g


def _dot(a, b):
    return jnp.dot(a, b, preferred_element_type=F32)


def _dot_nt(a, b):
    return lax.dot_general(a, b, (((1,), (1,)), ((), ())), preferred_element_type=F32)


def _ffn_kernel(x_ref, pre_ref, wo_ref, bo_ref, gf_ref, wg_ref, wu_ref, wd_ref, gfin_ref, out_ref, *,
                n_chunks, fc, final_norm):
    x1 = x_ref[...] + _dot(pre_ref[...], wo_ref[...]) + bo_ref[...]
    xn = _rmsnorm(x1, gf_ref[...]).astype(BF16)
    acc = x1
    for c in range(n_chunks):
        gt = _dot(xn, wg_ref[:, c * fc:(c + 1) * fc])
        up = _dot(xn, wu_ref[:, c * fc:(c + 1) * fc])
        hm = (gt * jax.nn.sigmoid(gt) * up).astype(BF16)
        acc = acc + _dot(hm, wd_ref[c * fc:(c + 1) * fc, :])
    if final_norm:
        acc = _rmsnorm(acc, gfin_ref[...])
    out_ref[...] = acc


def _ffn_call(x, pre, wo, bo, gf, wg, wu, wd, gfin, final_norm):
    n, d = x.shape
    dff = wg.shape[1]
    n_chunks = 2 if dff % (2 * LANES) == 0 else 1
    fc = dff // n_chunks
    row = lambda i: (i, 0)
    tm = TM_FFN if n % TM_FFN == 0 else TM_PROJ
    return pl.pallas_call(
        functools.partial(_ffn_kernel, n_chunks=n_chunks, fc=fc, final_norm=final_norm),
        grid=(n // tm,),
        in_specs=[pl.BlockSpec((tm, d), row), pl.BlockSpec((tm, pre.shape[1]), row),
                  _const_spec(wo.shape), _const_spec(bo.shape), _const_spec(gf.shape),
                  _const_spec(wg.shape), _const_spec(wu.shape), _const_spec(wd.shape), _const_spec(gfin.shape)],
        out_specs=pl.BlockSpec((tm, d), row),
        out_shape=jax.ShapeDtypeStruct((n, d), F32),
        compiler_params=_cparams(("parallel",)),
        name="outproj_swiglu",
    )(x, pre, wo, bo, gf, wg, wu, wd, gfin)


def _proj_kernel(x_ref, g_ref, wrow_ref, wt_ref, k_ref, v_ref, ki_ref, qt_ref, qit_ref, vt_ref, wit_ref, *,
                 n_kv, n_q, n_iq):
    xn = _rmsnorm(x_ref[...], g_ref[...]).astype(BF16)
    row = _dot(xn, wrow_ref[...])
    k_ref[...] = row[:, :n_kv]
    v_ref[...] = row[:, n_kv:2 * n_kv]
    ki_ref[...] = row[:, 2 * n_kv:]
    tt = _dot_nt(wt_ref[...], xn)
    for b in range(TM_PROJ // QB):
        blk = tt[:, b * QB:(b + 1) * QB]
        qt_ref[b] = blk[:n_q].astype(BF16)
        qit_ref[b] = blk[n_q:n_q + n_iq].astype(BF16)
        vt_ref[b] = blk[n_q + n_iq:n_q + n_iq + n_kv].astype(BF16)
        wit_ref[b] = blk[n_q + n_iq + n_kv:n_q + n_iq + n_kv + IDX_HEADS] * (IDX_HEADS ** -0.5)


def _proj_call(x, g, wrow, wt):
    n, d = x.shape
    n_kv = N_KV_HEADS * HEAD_DIM
    n_q = N_HEADS * HEAD_DIM
    n_iq = IDX_HEADS * IDX_DIM
    row = lambda i: (i, 0)
    blocks = TM_PROJ // QB
    feat_spec = lambda f: pl.BlockSpec((blocks, f, QB), lambda i: (i, 0, 0))
    feat_shape = lambda f, dt: jax.ShapeDtypeStruct((n // QB, f, QB), dt)
    return pl.pallas_call(
        functools.partial(_proj_kernel, n_kv=n_kv, n_q=n_q, n_iq=n_iq),
        grid=(n // TM_PROJ,),
        in_specs=[pl.BlockSpec((TM_PROJ, d), row), _const_spec(g.shape), _const_spec(wrow.shape),
                  _const_spec(wt.shape)],
        out_specs=[pl.BlockSpec((TM_PROJ, n_kv), row), pl.BlockSpec((TM_PROJ, n_kv), row),
                   pl.BlockSpec((TM_PROJ, IDX_DIM), row), feat_spec(n_q), feat_spec(n_iq), feat_spec(n_kv),
                   feat_spec(IDX_HEADS)],
        out_shape=[jax.ShapeDtypeStruct((n, n_kv), F32), jax.ShapeDtypeStruct((n, n_kv), F32),
                   jax.ShapeDtypeStruct((n, IDX_DIM), F32), feat_shape(n_q, BF16), feat_shape(n_iq, BF16),
                   feat_shape(n_kv, BF16), feat_shape(IDX_HEADS, F32)],
        compiler_params=_cparams(("parallel",)),
        name="attn_proj",
    )(x, g, wrow, wt)


def _bias_kernel(tab_ref, tb_ref):
    h = pl.program_id(0)
    d = pl.program_id(1)
    kj = lax.broadcasted_iota(I32, (QB, QB), 0)
    qi = lax.broadcasted_iota(I32, (QB, QB), 1)
    rel = (d - 1) * QB + kj - qi
    n = jnp.abs(rel)
    large = jnp.full((QB, QB), N_BUCKETS // 4, I32)
    for s in BUCKET_STEPS:
        large = large + jnp.where(n >= s, 1, 0)
    bucket = jnp.where(rel > 0, N_BUCKETS // 2, 0) + jnp.where(n < N_BUCKETS // 4, n, large)
    val = jnp.zeros((QB, QB), F32)
    for b in range(N_BUCKETS):
        val = jnp.where(bucket == b, tab_ref[b, h], val)
    tb_ref[0] = val


def _bias_call(table):
    return pl.pallas_call(
        _bias_kernel,
        grid=(N_HEADS, 2),
        in_specs=[pl.BlockSpec(memory_space=pltpu.SMEM)],
        out_specs=pl.BlockSpec((1, QB, QB), lambda h, d: (h, d, 0)),
        out_shape=jax.ShapeDtypeStruct((N_HEADS, 2 * QB, QB), F32),
        compiler_params=_cparams(("arbitrary", "arbitrary")),
        name="rel_bias_tiles",
    )(table)


def _attn_kernel(kend_ref, qt_ref, qit_ref, wit_ref, kf_ref, vtf_ref, kif_ref, tb_ref, tab_ref, o_ref,
                 key_scr, hi_scr, lo_scr, lo2_scr, ls_scr, oacc_scr, *, diag_off, kstart, k_sel, has_prev,
                 row_bits):
    diag = pl.program_id(1) + diag_off
    nt = diag + 1
    nst = (nt + SEL_TILE // QB - 1) // (SEL_TILE // QB)
    kend = kend_ref[0]
    qi_cat = jnp.concatenate([qit_ref[0, h * IDX_DIM:(h + 1) * IDX_DIM, :] for h in range(IDX_HEADS)], axis=1)
    wi = wit_ref[0]
    sel_iota = lax.broadcasted_iota(I32, (SEL_TILE, QB), 0)

    def tile_start(t):
        return pl.multiple_of(t * QB, QB)

    def sel_start(t):
        return pl.multiple_of(t * SEL_TILE, SEL_TILE)

    def score_rows(r0):
        s = _dot(kif_ref[0, pl.ds(r0, SEL_TILE), :], qi_cat)
        sc = jnp.zeros((SEL_TILE, QB), F32)
        for h in range(IDX_HEADS):
            sc = sc + wi[h:h + 1, :] * jnp.maximum(s[:, h * QB:(h + 1) * QB], 0.0)
        bits = lax.bitcast_convert_type(sc + 0.0, I32)
        key = jnp.where(bits >= 0, bits, bits ^ 0x7FFFFFFF)
        rows = r0 + sel_iota
        adm = (rows >= kstart) & (rows < kend)
        key = jnp.where(adm, key, INT_MIN)
        key_scr[pl.ds(r0, SEL_TILE), :] = key
        hi_scr[pl.ds(r0, SEL_TILE), :] = (key >> 16).astype(I16)
        lo_scr[pl.ds(r0, SEL_TILE), :] = ((key & 0xFFFF) + I16_MIN).astype(I16)

    pair_rows = 2 * SEL_TILE

    def score_pair(t, carry):
        r0 = pl.multiple_of(t * pair_rows, pair_rows)
        score_rows(r0)
        score_rows(r0 + SEL_TILE)
        return carry

    lax.fori_loop(0, (nst + 1) // 2, score_pair, 0)

    pair_vregs = pair_rows // PACK16

    def as_packed(v):
        return jnp.broadcast_to(v, (PACK16, QB)).astype(I16)

    def threshold(n_pair):
        def count16(src_scr, cand):
            c16 = as_packed(cand)[None]
            acc = jnp.zeros((PACK16, QB), BF16)
            for t in range(n_pair):
                v = src_scr[t * pair_rows:(t + 1) * pair_rows, :].reshape(pair_vregs, PACK16, QB)
                ind = jnp.where(v >= c16, jnp.ones((), BF16), jnp.zeros((), BF16))
                parts = [ind[i] for i in range(pair_vregs)]
                while len(parts) > 1:
                    parts = [parts[i] + parts[i + 1] for i in range(0, len(parts), 2)]
                acc = acc + parts[0]
            return acc.astype(F32).sum(axis=0, keepdims=True)

        def search16(src_scr, base, c_start):
            def step(b, carry):
                t_acc, c_acc = carry
                cand = t_acc + lax.shift_left(jnp.int32(1), 15 - b)
                c = base + count16(src_scr, cand)
                ok = c >= k_sel
                return jnp.where(ok, cand, t_acc), jnp.where(ok, c, c_acc)

            return lax.fori_loop(0, 16, step, (jnp.full((1, QB), I16_MIN, I32), c_start))

        thr_hi, c_hi = search16(hi_scr, 0.0, jnp.full((1, QB), n_pair * pair_rows, F32))
        above = jnp.where(thr_hi == -I16_MIN - 1, 0.0, count16(hi_scr, thr_hi + 1))
        hi16 = as_packed(thr_hi)[None]
        for t in range(n_pair):
            rows = slice(t * pair_rows, (t + 1) * pair_rows)
            hi = hi_scr[rows, :].reshape(pair_vregs, PACK16, QB)
            lo = lo_scr[rows, :].reshape(pair_vregs, PACK16, QB)
            lo2_scr[rows, :] = jnp.where(hi == hi16, lo, jnp.full((), I16_MIN, I16)).reshape(pair_rows, QB)
        thr_lo, c_ge = search16(lo2_scr, above, c_hi)
        return thr_hi * 65536 + (thr_lo - I16_MIN), c_ge

    def count(pred):
        def body(t, acc):
            r0 = sel_start(t)
            ind = jnp.where(pred(key_scr[pl.ds(r0, SEL_TILE), :], r0 + sel_iota), 1, 0)
            return acc + ind.reshape(SEL_TILE // SUBLANES, SUBLANES, QB).sum(axis=0)
        acc = lax.fori_loop(0, nst, body, jnp.zeros((SUBLANES, QB), I32))
        return acc.sum(axis=0, keepdims=True)

    max_pairs = hi_scr.shape[0] // pair_rows
    thr, c_ge = lax.switch((nst + 1) // 2 - 1, [functools.partial(threshold, n) for n in range(1, max_pairs + 1)])

    tied = (c_ge > k_sel) & (thr > INT_MIN)
    big = jnp.full((1, QB), 2 ** row_bits, I32)

    def tie_limit():
        need = k_sel - count(lambda kt, rows: kt > thr)

        def lim_step(b, lim):
            cand = lim + lax.shift_left(jnp.int32(1), row_bits - 1 - b)
            c = count(lambda kt, rows: (kt == thr) & (rows < cand))
            return jnp.where(c < need, cand, lim)

        lim = lax.fori_loop(0, row_bits, lim_step, jnp.zeros((1, QB), I32))
        return jnp.where(tied, lim, big)

    rlim = lax.cond(jnp.max(tied.astype(I32)) > 0, tie_limit, lambda: big)
    rlim = jnp.where(thr > INT_MIN, rlim, -1)

    gq = GROUP * QB
    qg = [jnp.concatenate([qt_ref[0, (g * GROUP + r) * HEAD_DIM:(g * GROUP + r + 1) * HEAD_DIM, :]
                           for r in range(GROUP)], axis=1) for g in range(N_KV_HEADS)]

    def logits_rows(r0, rows, m8, bias_of_head):
        kt = key_scr[pl.ds(r0, rows), :]
        keep = (kt > thr) | ((kt == thr) & (sel_iota[:rows] <= rlim - r0))
        new_m8 = []
        for g in range(N_KV_HEADS):
            lg = _dot(kf_ref[0, g, pl.ds(r0, rows), :], qg[g])
            parts = []
            for r in range(GROUP):
                part = (lg[:, r * QB:(r + 1) * QB] + bias_of_head(g * GROUP + r)) * LOG2E
                parts.append(jnp.where(keep, part, NEG_INF))
            lg = jnp.concatenate(parts, axis=1)
            ls_scr[pl.ds(r0, rows), g * gq:(g + 1) * gq] = lg
            new_m8.append(jnp.maximum(m8[g], lg.reshape(rows // SUBLANES, SUBLANES, gq).max(axis=0)))
        return tuple(new_m8)

    far_bias = lambda h: tab_ref[FAR_BUCKET, h]
    m8 = tuple(jnp.full((SUBLANES, gq), NEG_INF, F32) for _ in range(N_KV_HEADS))
    n_far = diag - 1 if has_prev else diag
    def far_pair(t, m8):
        r0 = pl.multiple_of(t * pair_rows, pair_rows)
        return logits_rows(r0 + SEL_TILE, SEL_TILE, logits_rows(r0, SEL_TILE, m8, far_bias), far_bias)

    m8 = lax.fori_loop(0, n_far // 4, far_pair, m8)
    m8 = lax.cond(n_far % 4 >= 2, lambda m: logits_rows(sel_start(n_far // 4 * 2), SEL_TILE, m, far_bias),
                  lambda m: m, m8)
    m8 = lax.cond(n_far % 2 == 1, lambda m: logits_rows(tile_start(n_far - 1), QB, m, far_bias), lambda m: m, m8)
    if has_prev:
        m8 = logits_rows(tile_start(diag - 1), 2 * QB, m8, lambda h: tb_ref[h])
    else:
        m8 = logits_rows(tile_start(diag), QB, m8, lambda h: tb_ref[h, QB:, :])
    m = [jnp.max(m8[g], axis=0, keepdims=True) for g in range(N_KV_HEADS)]

    @pl.when(nt % 2 == 1)
    def _():
        ls_scr[pl.ds(tile_start(nt), QB), :] = jnp.full((QB, N_HEADS * QB), NEG_INF, F32)

    oacc_scr[...] = jnp.zeros(oacc_scr.shape, F32)

    def pv_tile(t, carry):
        r0 = sel_start(t)
        for g in range(N_KV_HEADS):
            p = jnp.exp2(ls_scr[pl.ds(r0, SEL_TILE), g * gq:(g + 1) * gq] - m[g])
            oacc_scr[g] += _dot(vtf_ref[0, g * V_ROWS:(g + 1) * V_ROWS, pl.ds(r0, SEL_TILE)], p.astype(BF16))
        return carry

    lax.fori_loop(0, nst, pv_tile, 0)
    out_rows = []
    for g in range(N_KV_HEADS):
        o_g = oacc_scr[g, :HEAD_DIM] / oacc_scr[g, HEAD_DIM:HEAD_DIM + 1]
        out_rows.extend(o_g[:, r * QB:(r + 1) * QB] for r in range(GROUP))
    o_ref[...] = jnp.concatenate(out_rows, axis=0).T.astype(BF16)


def _attn_call(kend, qt, qit, wit, kf, vtf, kif, tb, table, *, n_out_rows, diag_off, kstart, k_sel, name):
    n_seq, tkf = kf.shape[0], kf.shape[2]
    assert tkf % (2 * SEL_TILE) == 0 and vtf.shape[2] == tkf and kif.shape[1] == tkf
    rows16 = tkf
    assert rows16 // PACK16 <= 256, "per-slot bf16 counts must stay exact"
    n_qb = kend.shape[0]
    has_prev = diag_off >= 1
    row_bits = max(1, (tkf - 1).bit_length())
    feat = lambda a: pl.BlockSpec((1, a.shape[1], QB), lambda s, i: (s * n_qb + i, 0, 0))
    return pl.pallas_call(
        functools.partial(_attn_kernel, diag_off=diag_off, kstart=kstart, k_sel=k_sel, has_prev=has_prev,
                          row_bits=row_bits),
        grid=(n_seq, n_qb),
        in_specs=[pl.BlockSpec((1, 1, QB), lambda s, i: (i, 0, 0)), feat(qt), feat(qit), feat(wit),
                  pl.BlockSpec((1, N_KV_HEADS, tkf, HEAD_DIM), lambda s, i: (s, 0, 0, 0)),
                  pl.BlockSpec((1, N_KV_HEADS * V_ROWS, tkf), lambda s, i: (s, 0, 0)),
                  pl.BlockSpec((1, tkf, IDX_DIM), lambda s, i: (s, 0, 0)),
                  _const_spec(tb.shape), pl.BlockSpec(memory_space=pltpu.SMEM)],
        out_specs=pl.BlockSpec((QB, N_HEADS * HEAD_DIM), lambda s, i: (s * n_qb + i, 0)),
        out_shape=jax.ShapeDtypeStruct((n_out_rows, N_HEADS * HEAD_DIM), BF16),
        scratch_shapes=[pltpu.VMEM((tkf, QB), I32)] + [pltpu.VMEM((rows16, QB), I16)] * 3
                       + [pltpu.VMEM((tkf, N_HEADS * QB), F32), pltpu.VMEM((N_KV_HEADS, V_ROWS, GROUP * QB), F32)],
        compiler_params=_cparams(("arbitrary", "arbitrary")),
        name=name,
    )(kend, qt, qit, wit, kf, vtf, kif, tb, table)


def _softplus(x):
    return jnp.maximum(x, 0.0) + jnp.log1p(jnp.exp(-jnp.abs(x)))


def _mixb_kernel(x_ref, h0_ref, buf_ref, g_ref, wy_ref, by_ref, wx_ref, bx_ref, cw_ref, cb_ref, wr_ref, br_ref,
                 wi_ref, bi_ref, lam_ref, *rest, tm):
    pre_ref, hlast_ref, tail_ref, xe_scr, h_scr = rest[-5:]

    @pl.when(pl.program_id(1) == 0)
    def _():
        xe_scr[0:SUBLANES, :] = buf_ref[0]
        h_scr[...] = h0_ref[0]

    xn = _rmsnorm(x_ref[...], g_ref[...]).astype(BF16)
    y = jax.nn.gelu(_dot(xn, wy_ref[...]) + by_ref[...])
    xe_scr[SUBLANES:SUBLANES + tm, :] = _dot(xn, wx_ref[...]) + bx_ref[...]
    xc = cb_ref[...] + cw_ref[0:1, :] * xe_scr[SUBLANES - 3:SUBLANES - 3 + tm, :]
    for j in range(1, CONV_B):
        xc = xc + cw_ref[j:j + 1, :] * xe_scr[SUBLANES - 3 + j:SUBLANES - 3 + j + tm, :]
    xcb = xc.astype(BF16)
    d_rnn = xc.shape[1]
    blk = d_rnn // LRU_BLOCKS
    r_pre = jnp.concatenate([_dot(xcb[:, n * blk:(n + 1) * blk], wr_ref[n]) for n in range(LRU_BLOCKS)], axis=1)
    i_pre = jnp.concatenate([_dot(xcb[:, n * blk:(n + 1) * blk], wi_ref[n]) for n in range(LRU_BLOCKS)], axis=1)
    r = jax.nn.sigmoid(r_pre + br_ref[...])
    ig = jax.nn.sigmoid(i_pre + bi_ref[...])
    log_a = -LRU_C * r * _softplus(-lam_ref[...])
    a = jnp.exp(log_a)
    u = jnp.sqrt(jnp.tanh(-log_a) * (1.0 + a * a)) * (ig * xc)
    row_in_group = lax.broadcasted_iota(I32, a.shape, 0) % SUBLANES
    s = 1
    while s < SUBLANES:
        a_sh = jnp.where(row_in_group >= s, pltpu.roll(a, s, 0), 1.0)
        u_sh = jnp.where(row_in_group >= s, pltpu.roll(u, s, 0), 0.0)
        u = a * u_sh + u
        a = a * a_sh
        s *= 2
    a = a.reshape(tm // SUBLANES, SUBLANES, d_rnn)
    u = u.reshape(tm // SUBLANES, SUBLANES, d_rnn)
    carry = h_scr[...]
    h_groups = []
    for grp in range(tm // SUBLANES):
        h_grp = a[grp] * carry + u[grp]
        carry = h_grp[SUBLANES - 1:SUBLANES, :]
        h_groups.append(h_grp)
    h = jnp.concatenate(h_groups, axis=0)
    pre_ref[...] = (h * y).astype(BF16)
    h_scr[...] = carry
    hlast_ref[0] = carry
    tail = xe_scr[tm:tm + SUBLANES, :]
    xe_scr[0:SUBLANES, :] = tail
    tail_ref[0] = tail


def _mixc_kernel(x_ref, buf_ref, g_ref, win_ref, cw_ref, *rest, tm):
    pre_ref, tail_ref, pe_scr = rest[-3:]

    @pl.when(pl.program_id(1) == 0)
    def _():
        pe_scr[0:SUBLANES, :] = buf_ref[0]

    d = x_ref.shape[1]
    xn = _rmsnorm(x_ref[...], g_ref[...]).astype(BF16)
    z = _dot(xn, win_ref[...])
    pe_scr[SUBLANES:SUBLANES + tm, :] = z[:, d:2 * d] * z[:, 2 * d:]
    conv = cw_ref[0:1, :] * pe_scr[SUBLANES - 2:SUBLANES - 2 + tm, :]
    for j in range(1, CONV_C):
        conv = conv + cw_ref[j:j + 1, :] * pe_scr[SUBLANES - 2 + j:SUBLANES - 2 + j + tm, :]
    pre_ref[...] = (z[:, :d] * conv).astype(BF16)
    tail = pe_scr[tm:tm + SUBLANES, :]
    pe_scr[0:SUBLANES, :] = tail
    tail_ref[0] = tail


def _shared_out(buf, index):
    if buf is None:
        return {"specs": [], "args": (), "aliases": {}}
    return {"specs": [pl.BlockSpec(memory_space=pl.ANY)], "args": (buf,), "aliases": {index: 0}}


def _seq_specs(n_seq, t_len, row_off, d):
    tm = min(TM_SEQ, t_len)
    n_tt = t_len // tm
    off = row_off // tm
    xrow = pl.BlockSpec((tm, d), lambda s, j: (off + s * n_tt + j, 0))
    state = lambda rows: pl.BlockSpec((1, rows, d), lambda s, j: (s, 0, 0))
    return tm, n_tt, xrow, state


def _mixb_call(x, pre_buf, h0, buf, g, wy, by, wx, bx, cw, cb, wr, br, wi, bi, lam, *, n_seq, t_len, row_off,
               name):
    d = x.shape[1]
    tm, n_tt, xrow, state = _seq_specs(n_seq, t_len, row_off, d)
    consts = (g, wy, by, wx, bx, cw, cb, wr, br, wi, bi, lam)
    shared = _shared_out(pre_buf, 3 + len(consts))
    return pl.pallas_call(
        functools.partial(_mixb_kernel, tm=tm),
        grid=(n_seq, n_tt),
        in_specs=[xrow, state(1), state(SUBLANES)] + [_const_spec(c.shape) for c in consts] + shared["specs"],
        out_specs=[xrow, state(1), state(SUBLANES)],
        out_shape=[jax.ShapeDtypeStruct((x.shape[0], d), BF16), jax.ShapeDtypeStruct((n_seq, 1, d), F32),
                   jax.ShapeDtypeStruct((n_seq, SUBLANES, d), F32)],
        scratch_shapes=[pltpu.VMEM((tm + SUBLANES, d), F32), pltpu.VMEM((1, d), F32)],
        input_output_aliases=shared["aliases"],
        compiler_params=_cparams(("arbitrary", "arbitrary")),
        name=name,
    )(x, h0, buf, *consts, *shared["args"])


def _mixc_call(x, pre_buf, buf, g, win, cw, *, n_seq, t_len, row_off, name):
    d = x.shape[1]
    tm, n_tt, xrow, state = _seq_specs(n_seq, t_len, row_off, d)
    consts = (g, win, cw)
    shared = _shared_out(pre_buf, 2 + len(consts))
    return pl.pallas_call(
        functools.partial(_mixc_kernel, tm=tm),
        grid=(n_seq, n_tt),
        in_specs=[xrow, state(SUBLANES)] + [_const_spec(c.shape) for c in consts] + shared["specs"],
        out_specs=[xrow, state(SUBLANES)],
        out_shape=[jax.ShapeDtypeStruct((x.shape[0], d), BF16), jax.ShapeDtypeStruct((n_seq, SUBLANES, d), F32)],
        scratch_shapes=[pltpu.VMEM((tm + SUBLANES, d), F32)],
        input_output_aliases=shared["aliases"],
        compiler_params=_cparams(("arbitrary", "arbitrary")),
        name=name,
    )(x, buf, *consts, *shared["args"])


def _front_pad_rows(a, rows):
    return jnp.pad(a, ((0, 0), (rows - a.shape[1], 0), (0, 0)))


def _frame(parts, axis):
    f = jnp.concatenate(parts, axis=axis).astype(BF16)
    pad = [(0, 0)] * f.ndim
    pad[axis] = (0, -f.shape[axis] % (2 * SEL_TILE))
    return jnp.pad(f, pad)


def _k_frame(parts):
    f = _frame(parts, 1)
    n_seq, rows = f.shape[0], f.shape[1]
    return f.reshape(n_seq, rows, N_KV_HEADS, HEAD_DIM).transpose(0, 2, 1, 3)


def _v_frame(parts):
    f = _frame(parts, 2)
    n_seq, tkf = f.shape[0], f.shape[2]
    extra = jnp.zeros((n_seq, N_KV_HEADS, V_ROWS - HEAD_DIM, tkf), BF16).at[:, :, 0].set(1)
    f = jnp.concatenate([f.reshape(n_seq, N_KV_HEADS, HEAD_DIM, tkf), extra], axis=2)
    return f.reshape(n_seq, N_KV_HEADS * V_ROWS, tkf)


def _cols(a, start, n):
    b0, b1 = start // QB, -(-(start + n) // QB)
    c = a[b0:b1].transpose(1, 0, 2).reshape(a.shape[1], (b1 - b0) * QB)
    return c[:, start - b0 * QB:start - b0 * QB + n]


def _seq_major(a, n_seq, q_len):
    return a.reshape(a.shape[0], n_seq, q_len).transpose(1, 0, 2)


def _lane_pad_cols(a, n_seq, q_len):
    return jnp.pad(_seq_major(a, n_seq, q_len), ((0, 0), (0, 0), (0, QB - q_len)))


def kernel(x_prompt, x_sample, cache_k, cache_v, cache_kidx, state_h, state_conv_b, state_conv_c, meta_tokens,
           rel_bias, norm_mix, norm_ffn, norm_final, a_wq, a_wk, a_wv, a_wo, a_wiq, a_wik, a_wiw, b_wy, b_by, b_wx,
           b_bx, b_conv_w, b_conv_b, b_wr, b_br, b_wi, b_bi, b_lam, b_wo, b_bo, c_win, c_conv_w, c_wo, f_wg, f_wu,
           f_wd):
    bp, seq, d = x_prompt.shape
    bs, dec = x_sample.shape[0], x_sample.shape[1]
    past = cache_k.shape[2]
    n_meta = meta_tokens.shape[0]
    depth = norm_mix.shape[0]
    assert seq % QB == 0 and seq % TM_SEQ == 0 and past % QB == 0 and dec <= QB and n_meta <= QB
    assert dec % PACK16 == 0 and n_meta % PACK16 == 0 and (bs * dec) % n_meta == 0

    n_main, n_samp = bp * seq, bs * dec
    n_tail = -(-(n_samp + n_meta) // TM_PROJ) * TM_PROJ
    x_main = x_prompt.reshape(n_main, d)
    x_tail = jnp.concatenate([x_sample.reshape(n_samp, d), meta_tokens.astype(F32),
                              jnp.zeros((n_tail - n_samp - n_meta, d), F32)], axis=0)
    tail_fill = jnp.zeros((n_tail - n_samp - n_meta, d), BF16)

    k_sel_p = min(TOP_K_MAX, seq // 4)
    k_sel_s = min(TOP_K_MAX, (past + dec) // 4)
    n_qb = seq // QB
    kv_dim = N_KV_HEADS * HEAD_DIM

    tau = jnp.arange(seq, dtype=I32)
    kend_main = (QB + CHUNK * (tau // CHUNK + 1)).reshape(n_qb, 1, QB)
    lane = jnp.arange(QB, dtype=I32)
    kend_samp = jnp.minimum(past + dec, CHUNK * ((past + jnp.minimum(lane, dec - 1)) // CHUNK + 1)).reshape(1, 1, QB)
    kend_meta = jnp.full((1, 1, QB), n_meta, I32)

    tb = _bias_call(rel_bias.astype(F32))
    zeros_bias = jnp.zeros((1, d), F32)
    gfin = norm_final.reshape(1, d).astype(F32)

    new = {name: [] for name in ("k_p", "v_p", "ki_p", "h_p", "cb_p", "cc_p", "k_s", "v_s", "ki_s", "h_s", "cb_s",
                                 "cc_s")}
    for l in range(depth):
        s = l // N_MIXERS
        g_mix = norm_mix[l].reshape(1, d)
        if l % N_MIXERS == 0:
            wrow = jnp.concatenate([a_wk[s], a_wv[s], a_wik[s]], axis=1).astype(BF16)
            wt = jnp.concatenate([a_wq[s].T * HEAD_DIM ** -0.5, a_wiq[s].T * IDX_DIM ** -0.5, a_wv[s].T, a_wiw[s].T,
                                  jnp.zeros((SUBLANES, d), F32)], axis=0).astype(BF16)
            k, v, ki, qt, qit, vt, wit = _proj_call(x_main, g_mix, wrow, wt)
            k_t, v_t, ki_t, qt_t, qit_t, vt_t, wit_t = _proj_call(x_tail, g_mix, wrow, wt)

            k_meta, v_meta, ki_meta = (a[n_samp:n_samp + n_meta] for a in (k_t, v_t, ki_t))
            k_main, v_main, ki_main = (a.reshape(bp, seq, -1) for a in (k, v, ki))
            k_samp, v_samp, ki_samp = (a[:n_samp].reshape(bs, dec, -1) for a in (k_t, v_t, ki_t))
            bc = lambda a: jnp.broadcast_to(a[None], (bp,) + a.shape)
            new["k_p"].append(jnp.concatenate([bc(k_meta), k_main], axis=1).reshape(bp, n_meta + seq, N_KV_HEADS,
                                                                                   HEAD_DIM))
            new["v_p"].append(jnp.concatenate([bc(v_meta), v_main], axis=1).reshape(bp, n_meta + seq, N_KV_HEADS,
                                                                                   HEAD_DIM))
            new["ki_p"].append(jnp.concatenate([bc(ki_meta), ki_main], axis=1))
            new["k_s"].append(k_samp.reshape(bs, dec, N_KV_HEADS, HEAD_DIM))
            new["v_s"].append(v_samp.reshape(bs, dec, N_KV_HEADS, HEAD_DIM))
            new["ki_s"].append(ki_samp)

            front = lambda a: jnp.pad(bc(a), ((0, 0), (QB - n_meta, 0), (0, 0)))
            samp_cols = lambda a: _cols(a, 0, n_samp)
            meta_cols = lambda a: _cols(a, n_samp, n_meta)
            vt_meta = meta_cols(vt_t)
            vt_main = vt.reshape(bp, n_qb, kv_dim, QB).transpose(0, 2, 1, 3).reshape(bp, kv_dim, seq)
            vtf_main = _v_frame([jnp.pad(bc(vt_meta), ((0, 0), (0, 0), (QB - n_meta, 0))), vt_main])
            pre_main = _attn_call(
                kend_main, qt, qit, wit, _k_frame([front(k_meta), k_main]), vtf_main,
                _frame([front(ki_meta), ki_main], 1), tb, rel_bias,
                n_out_rows=n_main, diag_off=1, kstart=QB - n_meta, k_sel=k_sel_p, name="dsa_attn_prompt")
            vtf_samp = _v_frame([cache_v[s].reshape(bs, past, kv_dim).transpose(0, 2, 1),
                                 _seq_major(samp_cols(vt_t), bs, dec)])
            o_samp = _attn_call(
                kend_samp, _lane_pad_cols(samp_cols(qt_t), bs, dec),
                _lane_pad_cols(samp_cols(qit_t), bs, dec), _lane_pad_cols(samp_cols(wit_t), bs, dec),
                _k_frame([cache_k[s].reshape(bs, past, kv_dim), k_samp]), vtf_samp,
                _frame([cache_kidx[s], ki_samp], 1), tb, rel_bias,
                n_out_rows=bs * QB, diag_off=past // QB, kstart=0, k_sel=k_sel_s, name="dsa_attn_sample")
            o_meta = _attn_call(
                kend_meta, _lane_pad_cols(meta_cols(qt_t), 1, n_meta),
                _lane_pad_cols(meta_cols(qit_t), 1, n_meta), _lane_pad_cols(meta_cols(wit_t), 1, n_meta),
                _k_frame([k_meta[None]]), _v_frame([vt_meta[None]]), _frame([ki_meta[None]], 1), tb, rel_bias,
                n_out_rows=QB, diag_off=0, kstart=0, k_sel=k_sel_p, name="dsa_attn_meta")
            pre_tail = jnp.concatenate([o_samp.reshape(bs, QB, d)[:, :dec].reshape(n_samp, d), o_meta[:n_meta],
                                        tail_fill], axis=0)
            wo, bo = a_wo[s].astype(BF16), zeros_bias
        elif l % N_MIXERS == 1:
            consts = (g_mix, b_wy[s].astype(BF16), b_by[s].reshape(1, d), b_wx[s].astype(BF16), b_bx[s].reshape(1, d),
                      jnp.pad(b_conv_w[s], ((0, SUBLANES - CONV_B), (0, 0))), b_conv_b[s].reshape(1, d),
                      b_wr[s].astype(BF16), b_br[s].reshape(1, d), b_wi[s].astype(BF16), b_bi[s].reshape(1, d),
                      b_lam[s].reshape(1, d))
            pre_tail = jnp.zeros((n_tail, d), BF16)
            pre_tail, h_m, tail_m = _mixb_call(x_tail, pre_tail, jnp.zeros((1, 1, d), F32),
                                               jnp.zeros((1, SUBLANES, d), F32), *consts, n_seq=1, t_len=n_meta,
                                               row_off=n_samp, name="rglru_meta")
            pre_tail, h_s, tail_s = _mixb_call(x_tail, pre_tail, state_h[s].reshape(bs, 1, d),
                                               _front_pad_rows(state_conv_b[s], SUBLANES), *consts, n_seq=bs,
                                               t_len=dec, row_off=0, name="rglru_sample")
            pre_main, h_p, tail_p = _mixb_call(x_main, None, jnp.broadcast_to(h_m, (bp, 1, d)),
                                               jnp.broadcast_to(tail_m, (bp, SUBLANES, d)), *consts, n_seq=bp,
                                               t_len=seq, row_off=0, name="rglru_prompt")
            new["h_p"].append(h_p.reshape(bp, d))
            new["cb_p"].append(tail_p[:, SUBLANES - (CONV_B - 1):])
            new["h_s"].append(h_s.reshape(bs, d))
            new["cb_s"].append(tail_s[:, SUBLANES - (CONV_B - 1):])
            wo, bo = b_wo[s].astype(BF16), b_bo[s].reshape(1, d)
        else:
            consts = (g_mix, c_win[s].astype(BF16), jnp.pad(c_conv_w[s], ((0, SUBLANES - CONV_C), (0, 0))))
            pre_tail = jnp.zeros((n_tail, d), BF16)
            pre_tail, tail_m = _mixc_call(x_tail, pre_tail, jnp.zeros((1, SUBLANES, d), F32), *consts, n_seq=1,
                                          t_len=n_meta, row_off=n_samp, name="sconv_meta")
            pre_tail, tail_s = _mixc_call(x_tail, pre_tail, _front_pad_rows(state_conv_c[s], SUBLANES), *consts,
                                          n_seq=bs, t_len=dec, row_off=0, name="sconv_sample")
            pre_main, tail_p = _mixc_call(x_main, None, jnp.broadcast_to(tail_m, (bp, SUBLANES, d)), *consts,
                                          n_seq=bp, t_len=seq, row_off=0, name="sconv_prompt")
            new["cc_p"].append(tail_p[:, SUBLANES - (CONV_C - 1):])
            new["cc_s"].append(tail_s[:, SUBLANES - (CONV_C - 1):])
            wo, bo = c_wo[s].astype(BF16), zeros_bias
        ffn = (wo, bo, norm_ffn[l].reshape(1, d), f_wg[l].astype(BF16), f_wu[l].astype(BF16), f_wd[l].astype(BF16),
               gfin, l == depth - 1)
        x_main = _ffn_call(x_main, pre_main, *ffn)
        x_tail = _ffn_call(x_tail, pre_tail, *ffn)

    y_prompt = x_main.reshape(bp, seq, d)
    y_sample = x_tail[:n_samp].reshape(bs, dec, d)
    st = lambda name: jnp.stack(new[name])
    return (y_prompt, y_sample, st("k_p"), st("v_p"), st("ki_p"), st("h_p"), st("cb_p"), st("cc_p"),
            st("k_s"), st("v_s"), st("ki_s"), st("h_s"), st("cb_s"), st("cc_s"))
```

```python
import functools

import jax
import jax.numpy as jnp
from jax import lax
from jax.experimental import pallas as pl
from jax.experimental.pallas import tpu as pltpu

F32 = jnp.float32
BF16 = jnp.bfloat16
I32 = jnp.int32
I16 = jnp.int16

CHUNK = 64
N_MIXERS = 3
N_HEADS = 16
N_KV_HEADS = 4
GROUP = N_HEADS // N_KV_HEADS
HEAD_DIM = 64
IDX_HEADS = 8
IDX_DIM = 64
TOP_K_MAX = 256
NEG_INF = -1e30
N_BUCKETS = 32
LRU_BLOCKS = 4
CONV_B = 4
CONV_C = 3
LRU_C = 8.0
RMS_EPS = 1e-6
BUCKET_STEPS = (12, 16, 23, 32, 46, 64, 91)
FAR_BUCKET = N_BUCKETS // 2 - 1

LANES = 128
SUBLANES = 8
TM_PROJ = 256
TM_FFN = 512
TM_SEQ = 256
QB = 128
SEL_TILE = 2 * QB
INT_MIN = -2 ** 31
I16_MIN = -2 ** 15
PACK16 = 2 * SUBLANES
V_ROWS = HEAD_DIM + PACK16
LOG2E = 1.4426950408889634
VMEM_LIMIT = 56 * 1024 * 1024


def _cparams(sem):
    return pltpu.CompilerParams(dimension_semantics=sem, vmem_limit_bytes=VMEM_LIMIT)


def _const_spec(shape):
    nd = len(shape)
    return pl.BlockSpec(shape, lambda *_: (0,) * nd, pipeline_mode=pl.Buffered(1))


def _rmsnorm(x, g):
    ms = jnp.mean(x * x, axis=-1, keepdims=True)
    return x * lax.rsqrt(ms + RMS_EPS) * g


def _dot(a, b):
    return jnp.dot(a, b, preferred_element_type=F32)


def _dot_nt(a, b):
    return lax.dot_general(a, b, (((1,), (1,)), ((), ())), preferred_element_type=F32)


def _ffn_kernel(x_ref, pre_ref, wo_ref, bo_ref, gf_ref, wg_ref, wu_ref, wd_ref, gfin_ref, out_ref, *,
                n_chunks, fc, final_norm):
    x1 = x_ref[...] + _dot(pre_ref[...], wo_ref[...]) + bo_ref[...]
    xn = _rmsnorm(x1, gf_ref[...]).astype(BF16)
    acc = x1
    for c in range(n_chunks):
        gt = _dot(xn, wg_ref[:, c * fc:(c + 1) * fc])
        up = _dot(xn, wu_ref[:, c * fc:(c + 1) * fc])
        hm = (gt * jax.nn.sigmoid(gt) * up).astype(BF16)
        acc = acc + _dot(hm, wd_ref[c * fc:(c + 1) * fc, :])
    if final_norm:
        acc = _rmsnorm(acc, gfin_ref[...])
    out_ref[...] = acc


def _ffn_call(x, pre, wo, bo, gf, wg, wu, wd, gfin, final_norm):
    n, d = x.shape
    dff = wg.shape[1]
    n_chunks = 2 if dff % (2 * LANES) == 0 else 1
    fc = dff // n_chunks
    row = lambda i: (i, 0)
    tm = TM_FFN if n % TM_FFN == 0 else TM_PROJ
    return pl.pallas_call(
        functools.partial(_ffn_kernel, n_chunks=n_chunks, fc=fc, final_norm=final_norm),
        grid=(n // tm,),
        in_specs=[pl.BlockSpec((tm, d), row), pl.BlockSpec((tm, pre.shape[1]), row),
                  _const_spec(wo.shape), _const_spec(bo.shape), _const_spec(gf.shape),
                  _const_spec(wg.shape), _const_spec(wu.shape), _const_spec(wd.shape), _const_spec(gfin.shape)],
        out_specs=pl.BlockSpec((tm, d), row),
        out_shape=jax.ShapeDtypeStruct((n, d), F32),
        compiler_params=_cparams(("parallel",)),
        name="outproj_swiglu",
    )(x, pre, wo, bo, gf, wg, wu, wd, gfin)


def _proj_kernel(x_ref, g_ref, wrow_ref, wt_ref, k_ref, v_ref, ki_ref, qt_ref, qit_ref, vt_ref, wit_ref, *,
                 n_kv, n_q, n_iq):
    xn = _rmsnorm(x_ref[...], g_ref[...]).astype(BF16)
    row = _dot(xn, wrow_ref[...])
    k_ref[...] = row[:, :n_kv]
    v_ref[...] = row[:, n_kv:2 * n_kv]
    ki_ref[...] = row[:, 2 * n_kv:]
    tt = _dot_nt(wt_ref[...], xn)
    for b in range(TM_PROJ // QB):
        blk = tt[:, b * QB:(b + 1) * QB]
        qt_ref[b] = blk[:n_q].astype(BF16)
        qit_ref[b] = blk[n_q:n_q + n_iq].astype(BF16)
        vt_ref[b] = blk[n_q + n_iq:n_q + n_iq + n_kv].astype(BF16)
        wit_ref[b] = blk[n_q + n_iq + n_kv:n_q + n_iq + n_kv + IDX_HEADS] * (IDX_HEADS ** -0.5)


def _proj_call(x, g, wrow, wt):
    n, d = x.shape
    n_kv = N_KV_HEADS * HEAD_DIM
    n_q = N_HEADS * HEAD_DIM
    n_iq = IDX_HEADS * IDX_DIM
    row = lambda i: (i, 0)
    blocks = TM_PROJ // QB
    feat_spec = lambda f: pl.BlockSpec((blocks, f, QB), lambda i: (i, 0, 0))
    feat_shape = lambda f, dt: jax.ShapeDtypeStruct((n // QB, f, QB), dt)
    return pl.pallas_call(
        functools.partial(_proj_kernel, n_kv=n_kv, n_q=n_q, n_iq=n_iq),
        grid=(n // TM_PROJ,),
        in_specs=[pl.BlockSpec((TM_PROJ, d), row), _const_spec(g.shape), _const_spec(wrow.shape),
                  _const_spec(wt.shape)],
        out_specs=[pl.BlockSpec((TM_PROJ, n_kv), row), pl.BlockSpec((TM_PROJ, n_kv), row),
                   pl.BlockSpec((TM_PROJ, IDX_DIM), row), feat_spec(n_q), feat_spec(n_iq), feat_spec(n_kv),
                   feat_spec(IDX_HEADS)],
        out_shape=[jax.ShapeDtypeStruct((n, n_kv), F32), jax.ShapeDtypeStruct((n, n_kv), F32),
                   jax.ShapeDtypeStruct((n, IDX_DIM), F32), feat_shape(n_q, BF16), feat_shape(n_iq, BF16),
                   feat_shape(n_kv, BF16), feat_shape(IDX_HEADS, F32)],
        compiler_params=_cparams(("parallel",)),
        name="attn_proj",
    )(x, g, wrow, wt)


def _bias_kernel(tab_ref, tb_ref):
    h = pl.program_id(0)
    d = pl.program_id(1)
    kj = lax.broadcasted_iota(I32, (QB, QB), 0)
    qi = lax.broadcasted_iota(I32, (QB, QB), 1)
    rel = (d - 1) * QB + kj - qi
    n = jnp.abs(rel)
    large = jnp.full((QB, QB), N_BUCKETS // 4, I32)
    for s in BUCKET_STEPS:
        large = large + jnp.where(n >= s, 1, 0)
    bucket = jnp.where(rel > 0, N_BUCKETS // 2, 0) + jnp.where(n < N_BUCKETS // 4, n, large)
    val = jnp.zeros((QB, QB), F32)
    for b in range(N_BUCKETS):
        val = jnp.where(bucket == b, tab_ref[b, h], val)
    tb_ref[0] = val


def _bias_call(table):
    return pl.pallas_call(
        _bias_kernel,
        grid=(N_HEADS, 2),
        in_specs=[pl.BlockSpec(memory_space=pltpu.SMEM)],
        out_specs=pl.BlockSpec((1, QB, QB), lambda h, d: (h, d, 0)),
        out_shape=jax.ShapeDtypeStruct((N_HEADS, 2 * QB, QB), F32),
        compiler_params=_cparams(("arbitrary", "arbitrary")),
        name="rel_bias_tiles",
    )(table)


def _attn_kernel(kend_ref, qt_ref, qit_ref, wit_ref, kf_ref, vtf_ref, kif_ref, tb_ref, tab_ref, o_ref,
                 key_scr, hi_scr, lo_scr, lo2_scr, ls_scr, oacc_scr, *, diag_off, kstart, k_sel, has_prev,
                 row_bits):
    diag = pl.program_id(1) + diag_off
    nt = diag + 1
    nst = (nt + SEL_TILE // QB - 1) // (SEL_TILE // QB)
    kend = kend_ref[0]
    qi_cat = jnp.concatenate([qit_ref[0, h * IDX_DIM:(h + 1) * IDX_DIM, :] for h in range(IDX_HEADS)], axis=1)
    wi = wit_ref[0]
    sel_iota = lax.broadcasted_iota(I32, (SEL_TILE, QB), 0)

    def tile_start(t):
        return pl.multiple_of(t * QB, QB)

    def sel_start(t):
        return pl.multiple_of(t * SEL_TILE, SEL_TILE)

    def score_rows(r0):
        s = _dot(kif_ref[0, pl.ds(r0, SEL_TILE), :], qi_cat)
        sc = jnp.zeros((SEL_TILE, QB), F32)
        for h in range(IDX_HEADS):
            sc = sc + wi[h:h + 1, :] * jnp.maximum(s[:, h * QB:(h + 1) * QB], 0.0)
        bits = lax.bitcast_convert_type(sc + 0.0, I32)
        key = jnp.where(bits >= 0, bits, bits ^ 0x7FFFFFFF)
        rows = r0 + sel_iota
        adm = (rows >= kstart) & (rows < kend)
        key = jnp.where(adm, key, INT_MIN)
        key_scr[pl.ds(r0, SEL_TILE), :] = key
        hi_scr[pl.ds(r0, SEL_TILE), :] = (key >> 16).astype(I16)
        lo_scr[pl.ds(r0, SEL_TILE), :] = ((key & 0xFFFF) + I16_MIN).astype(I16)

    pair_rows = 2 * SEL_TILE

    def score_pair(t, carry):
        r0 = pl.multiple_of(t * pair_rows, pair_rows)
        score_rows(r0)
        score_rows(r0 + SEL_TILE)
        return carry

    lax.fori_loop(0, (nst + 1) // 2, score_pair, 0)

    pair_vregs = pair_rows // PACK16

    def as_packed(v):
        return jnp.broadcast_to(v, (PACK16, QB)).astype(I16)

    def threshold(n_pair):
        def count16(src_scr, cand):
            c16 = as_packed(cand)[None]
            acc = jnp.zeros((PACK16, QB), BF16)
            for t in range(n_pair):
                v = src_scr[t * pair_rows:(t + 1) * pair_rows, :].reshape(pair_vregs, PACK16, QB)
                ind = jnp.where(v >= c16, jnp.ones((), BF16), jnp.zeros((), BF16))
                parts = [ind[i] for i in range(pair_vregs)]
                while len(parts) > 1:
                    parts = [parts[i] + parts[i + 1] for i in range(0, len(parts), 2)]
                acc = acc + parts[0]
            return acc.astype(F32).sum(axis=0, keepdims=True)

        def search16(src_scr, base, c_start):
            def step(b, carry):
                t_acc, c_acc = carry
                cand = t_acc + lax.shift_left(jnp.int32(1), 15 - b)
                c = base + count16(src_scr, cand)
                ok = c >= k_sel
                return jnp.where(ok, cand, t_acc), jnp.where(ok, c, c_acc)

            return lax.fori_loop(0, 16, step, (jnp.full((1, QB), I16_MIN, I32), c_start))

        thr_hi, c_hi = search16(hi_scr, 0.0, jnp.full((1, QB), n_pair * pair_rows, F32))
        above = jnp.where(thr_hi == -I16_MIN - 1, 0.0, count16(hi_scr, thr_hi + 1))
        hi16 = as_packed(thr_hi)[None]
        for t in range(n_pair):
            rows = slice(t * pair_rows, (t + 1) * pair_rows)
            hi = hi_scr[rows, :].reshape(pair_vregs, PACK16, QB)
            lo = lo_scr[rows, :].reshape(pair_vregs, PACK16, QB)
            lo2_scr[rows, :] = jnp.where(hi == hi16, lo, jnp.full((), I16_MIN, I16)).reshape(pair_rows, QB)
        thr_lo, c_ge = search16(lo2_scr, above, c_hi)
        return thr_hi * 65536 + (thr_lo - I16_MIN), c_ge

    def count(pred):
        def body(t, acc):
            r0 = sel_start(t)
            ind = jnp.where(pred(key_scr[pl.ds(r0, SEL_TILE), :], r0 + sel_iota), 1, 0)
            return acc + ind.reshape(SEL_TILE // SUBLANES, SUBLANES, QB).sum(axis=0)
        acc = lax.fori_loop(0, nst, body, jnp.zeros((SUBLANES, QB), I32))
        return acc.sum(axis=0, keepdims=True)

    max_pairs = hi_scr.shape[0] // pair_rows
    thr, c_ge = lax.switch((nst + 1) // 2 - 1, [functools.partial(threshold, n) for n in range(1, max_pairs + 1)])

    tied = (c_ge > k_sel) & (thr > INT_MIN)
    big = jnp.full((1, QB), 2 ** row_bits, I32)

    def tie_limit():
        need = k_sel - count(lambda kt, rows: kt > thr)

        def lim_step(b, lim):
            cand = lim + lax.shift_left(jnp.int32(1), row_bits - 1 - b)
            c = count(lambda kt, rows: (kt == thr) & (rows < cand))
            return jnp.where(c < need, cand, lim)

        lim = lax.fori_loop(0, row_bits, lim_step, jnp.zeros((1, QB), I32))
        return jnp.where(tied, lim, big)

    rlim = lax.cond(jnp.max(tied.astype(I32)) > 0, tie_limit, lambda: big)
    rlim = jnp.where(thr > INT_MIN, rlim, -1)

    gq = GROUP * QB
    qg = [jnp.concatenate([qt_ref[0, (g * GROUP + r) * HEAD_DIM:(g * GROUP + r + 1) * HEAD_DIM, :]
                           for r in range(GROUP)], axis=1) for g in range(N_KV_HEADS)]

    def logits_rows(r0, rows, m8, bias_of_head):
        kt = key_scr[pl.ds(r0, rows), :]
        keep = (kt > thr) | ((kt == thr) & (sel_iota[:rows] <= rlim - r0))
        new_m8 = []
        for g in range(N_KV_HEADS):
            lg = _dot(kf_ref[0, g, pl.ds(r0, rows), :], qg[g])
            parts = []
            for r in range(GROUP):
                part = (lg[:, r * QB:(r + 1) * QB] + bias_of_head(g * GROUP + r)) * LOG2E
                parts.append(jnp.where(keep, part, NEG_INF))
            lg = jnp.concatenate(parts, axis=1)
            ls_scr[pl.ds(r0, rows), g * gq:(g + 1) * gq] = lg
            new_m8.append(jnp.maximum(m8[g], lg.reshape(rows // SUBLANES, SUBLANES, gq).max(axis=0)))
        return tuple(new_m8)

    far_bias = lambda h: tab_ref[FAR_BUCKET, h]
    m8 = tuple(jnp.full((SUBLANES, gq), NEG_INF, F32) for _ in range(N_KV_HEADS))
    n_far = diag - 1 if has_prev else diag
    def far_pair(t, m8):
        r0 = pl.multiple_of(t * pair_rows, pair_rows)
        return logits_rows(r0 + SEL_TILE, SEL_TILE, logits_rows(r0, SEL_TILE, m8, far_bias), far_bias)

    m8 = lax.fori_loop(0, n_far // 4, far_pair, m8)
    m8 = lax.cond(n_far % 4 >= 2, lambda m: logits_rows(sel_start(n_far // 4 * 2), SEL_TILE, m, far_bias),
                  lambda m: m, m8)
    m8 = lax.cond(n_far % 2 == 1, lambda m: logits_rows(tile_start(n_far - 1), QB, m, far_bias), lambda m: m, m8)
    if has_prev:
        m8 = logits_rows(tile_start(diag - 1), 2 * QB, m8, lambda h: tb_ref[h])
    else:
        m8 = logits_rows(tile_start(diag), QB, m8, lambda h: tb_ref[h, QB:, :])
    m = [jnp.max(m8[g], axis=0, keepdims=True) for g in range(N_KV_HEADS)]

    @pl.when(nt % 2 == 1)
    def _():
        ls_scr[pl.ds(tile_start(nt), QB), :] = jnp.full((QB, N_HEADS * QB), NEG_INF, F32)

    oacc_scr[...] = jnp.zeros(oacc_scr.shape, F32)

    def pv_rows(r0):
        for g in range(N_KV_HEADS):
            p = jnp.exp2(ls_scr[pl.ds(r0, SEL_TILE), g * gq:(g + 1) * gq] - m[g])
            oacc_scr[g] += _dot(vtf_ref[0, g * V_ROWS:(g + 1) * V_ROWS, pl.ds(r0, SEL_TILE)], p.astype(BF16))

    def pv_pair(t, carry):
        r0 = pl.multiple_of(t * pair_rows, pair_rows)
        pv_rows(r0)
        pv_rows(r0 + SEL_TILE)
        return carry

    lax.fori_loop(0, nst // 2, pv_pair, 0)

    @pl.when(nst % 2 == 1)
    def _():
        pv_rows(sel_start(nst - 1))
    out_rows = []
    for g in range(N_KV_HEADS):
        o_g = oacc_scr[g, :HEAD_DIM] / oacc_scr[g, HEAD_DIM:HEAD_DIM + 1]
        out_rows.extend(o_g[:, r * QB:(r + 1) * QB] for r in range(GROUP))
    o_ref[...] = jnp.concatenate(out_rows, axis=0).T.astype(BF16)


def _attn_call(kend, qt, qit, wit, kf, vtf, kif, tb, table, *, n_out_rows, diag_off, kstart, k_sel, name):
    n_seq, tkf = kf.shape[0], kf.shape[2]
    assert tkf % (2 * SEL_TILE) == 0 and vtf.shape[2] == tkf and kif.shape[1] == tkf
    rows16 = tkf
    assert rows16 // PACK16 <= 256, "per-slot bf16 counts must stay exact"
    n_qb = kend.shape[0]
    has_prev = diag_off >= 1
    row_bits = max(1, (tkf - 1).bit_length())
    feat = lambda a: pl.BlockSpec((1, a.shape[1], QB), lambda s, i: (s * n_qb + i, 0, 0))
    return pl.pallas_call(
        functools.partial(_attn_kernel, diag_off=diag_off, kstart=kstart, k_sel=k_sel, has_prev=has_prev,
                          row_bits=row_bits),
        grid=(n_seq, n_qb),
        in_specs=[pl.BlockSpec((1, 1, QB), lambda s, i: (i, 0, 0)), feat(qt), feat(qit), feat(wit),
                  pl.BlockSpec((1, N_KV_HEADS, tkf, HEAD_DIM), lambda s, i: (s, 0, 0, 0)),
                  pl.BlockSpec((1, N_KV_HEADS * V_ROWS, tkf), lambda s, i: (s, 0, 0)),
                  pl.BlockSpec((1, tkf, IDX_DIM), lambda s, i: (s, 0, 0)),
                  _const_spec(tb.shape), pl.BlockSpec(memory_space=pltpu.SMEM)],
        out_specs=pl.BlockSpec((QB, N_HEADS * HEAD_DIM), lambda s, i: (s * n_qb + i, 0)),
        out_shape=jax.ShapeDtypeStruct((n_out_rows, N_HEADS * HEAD_DIM), BF16),
        scratch_shapes=[pltpu.VMEM((tkf, QB), I32)] + [pltpu.VMEM((rows16, QB), I16)] * 3
                       + [pltpu.VMEM((tkf, N_HEADS * QB), F32), pltpu.VMEM((N_KV_HEADS, V_ROWS, GROUP * QB), F32)],
        compiler_params=_cparams(("arbitrary", "arbitrary")),
        name=name,
    )(kend, qt, qit, wit, kf, vtf, kif, tb, table)


def _softplus(x):
    return jnp.maximum(x, 0.0) + jnp.log1p(jnp.exp(-jnp.abs(x)))


def _mixb_kernel(x_ref, h0_ref, buf_ref, g_ref, wy_ref, by_ref, wx_ref, bx_ref, cw_ref, cb_ref, wr_ref, br_ref,
                 wi_ref, bi_ref, lam_ref, *rest, tm):
    pre_ref, hlast_ref, tail_ref, xe_scr, h_scr = rest[-5:]

    @pl.when(pl.program_id(1) == 0)
    def _():
        xe_scr[0:SUBLANES, :] = buf_ref[0]
        h_scr[...] = h0_ref[0]

    xn = _rmsnorm(x_ref[...], g_ref[...]).astype(BF16)
    y = jax.nn.gelu(_dot(xn, wy_ref[...]) + by_ref[...])
    xe_scr[SUBLANES:SUBLANES + tm, :] = _dot(xn, wx_ref[...]) + bx_ref[...]
    xc = cb_ref[...] + cw_ref[0:1, :] * xe_scr[SUBLANES - 3:SUBLANES - 3 + tm, :]
    for j in range(1, CONV_B):
        xc = xc + cw_ref[j:j + 1, :] * xe_scr[SUBLANES - 3 + j:SUBLANES - 3 + j + tm, :]
    xcb = xc.astype(BF16)
    d_rnn = xc.shape[1]
    blk = d_rnn // LRU_BLOCKS
    r_pre = jnp.concatenate([_dot(xcb[:, n * blk:(n + 1) * blk], wr_ref[n]) for n in range(LRU_BLOCKS)], axis=1)
    i_pre = jnp.concatenate([_dot(xcb[:, n * blk:(n + 1) * blk], wi_ref[n]) for n in range(LRU_BLOCKS)], axis=1)
    r = jax.nn.sigmoid(r_pre + br_ref[...])
    ig = jax.nn.sigmoid(i_pre + bi_ref[...])
    log_a = -LRU_C * r * _softplus(-lam_ref[...])
    a = jnp.exp(log_a)
    u = jnp.sqrt(jnp.tanh(-log_a) * (1.0 + a * a)) * (ig * xc)
    a = a.reshape(tm // SUBLANES, SUBLANES, d_rnn)
    u = u.reshape(tm // SUBLANES, SUBLANES, d_rnn)
    row_in_group = lax.broadcasted_iota(I32, a.shape, 1)
    s = 1
    while s < SUBLANES:
        a_sh = jnp.where(row_in_group >= s, pltpu.roll(a, s, 1), 1.0)
        u_sh = jnp.where(row_in_group >= s, pltpu.roll(u, s, 1), 0.0)
        u = a * u_sh + u
        a = a * a_sh
        s *= 2
    carry = h_scr[...]
    h_groups = []
    for grp in range(tm // SUBLANES):
        h_grp = a[grp] * carry + u[grp]
        carry = h_grp[SUBLANES - 1:SUBLANES, :]
        h_groups.append(h_grp)
    h = jnp.concatenate(h_groups, axis=0)
    pre_ref[...] = (h * y).astype(BF16)
    h_scr[...] = carry
    hlast_ref[0] = carry
    tail = xe_scr[tm:tm + SUBLANES, :]
    xe_scr[0:SUBLANES, :] = tail
    tail_ref[0] = tail


def _mixc_kernel(x_ref, buf_ref, g_ref, win_ref, cw_ref, *rest, tm):
    pre_ref, tail_ref, pe_scr = rest[-3:]

    @pl.when(pl.program_id(1) == 0)
    def _():
        pe_scr[0:SUBLANES, :] = buf_ref[0]

    d = x_ref.shape[1]
    xn = _rmsnorm(x_ref[...], g_ref[...]).astype(BF16)
    z = _dot(xn, win_ref[...])
    pe_scr[SUBLANES:SUBLANES + tm, :] = z[:, d:2 * d] * z[:, 2 * d:]
    conv = cw_ref[0:1, :] * pe_scr[SUBLANES - 2:SUBLANES - 2 + tm, :]
    for j in range(1, CONV_C):
        conv = conv + cw_ref[j:j + 1, :] * pe_scr[SUBLANES - 2 + j:SUBLANES - 2 + j + tm, :]
    pre_ref[...] = (z[:, :d] * conv).astype(BF16)
    tail = pe_scr[tm:tm + SUBLANES, :]
    pe_scr[0:SUBLANES, :] = tail
    tail_ref[0] = tail


def _shared_out(buf, index):
    if buf is None:
        return {"specs": [], "args": (), "aliases": {}}
    return {"specs": [pl.BlockSpec(memory_space=pl.ANY)], "args": (buf,), "aliases": {index: 0}}


def _seq_specs(n_seq, t_len, row_off, d):
    tm = min(TM_SEQ, t_len)
    n_tt = t_len // tm
    off = row_off // tm
    xrow = pl.BlockSpec((tm, d), lambda s, j: (off + s * n_tt + j, 0))
    state = lambda rows: pl.BlockSpec((1, rows, d), lambda s, j: (s, 0, 0))
    return tm, n_tt, xrow, state


def _mixb_call(x, pre_buf, h0, buf, g, wy, by, wx, bx, cw, cb, wr, br, wi, bi, lam, *, n_seq, t_len, row_off,
               name):
    d = x.shape[1]
    tm, n_tt, xrow, state = _seq_specs(n_seq, t_len, row_off, d)
    consts = (g, wy, by, wx, bx, cw, cb, wr, br, wi, bi, lam)
    shared = _shared_out(pre_buf, 3 + len(consts))
    return pl.pallas_call(
        functools.partial(_mixb_kernel, tm=tm),
        grid=(n_seq, n_tt),
        in_specs=[xrow, state(1), state(SUBLANES)] + [_const_spec(c.shape) for c in consts] + shared["specs"],
        out_specs=[xrow, state(1), state(SUBLANES)],
        out_shape=[jax.ShapeDtypeStruct((x.shape[0], d), BF16), jax.ShapeDtypeStruct((n_seq, 1, d), F32),
                   jax.ShapeDtypeStruct((n_seq, SUBLANES, d), F32)],
        scratch_shapes=[pltpu.VMEM((tm + SUBLANES, d), F32), pltpu.VMEM((1, d), F32)],
        input_output_aliases=shared["aliases"],
        compiler_params=_cparams(("arbitrary", "arbitrary")),
        name=name,
    )(x, h0, buf, *consts, *shared["args"])


def _mixc_call(x, pre_buf, buf, g, win, cw, *, n_seq, t_len, row_off, name):
    d = x.shape[1]
    tm, n_tt, xrow, state = _seq_specs(n_seq, t_len, row_off, d)
    consts = (g, win, cw)
    shared = _shared_out(pre_buf, 2 + len(consts))
    return pl.pallas_call(
        functools.partial(_mixc_kernel, tm=tm),
        grid=(n_seq, n_tt),
        in_specs=[xrow, state(SUBLANES)] + [_const_spec(c.shape) for c in consts] + shared["specs"],
        out_specs=[xrow, state(SUBLANES)],
        out_shape=[jax.ShapeDtypeStruct((x.shape[0], d), BF16), jax.ShapeDtypeStruct((n_seq, SUBLANES, d), F32)],
        scratch_shapes=[pltpu.VMEM((tm + SUBLANES, d), F32)],
        input_output_aliases=shared["aliases"],
        compiler_params=_cparams(("arbitrary", "arbitrary")),
        name=name,
    )(x, buf, *consts, *shared["args"])


def _front_pad_rows(a, rows):
    return jnp.pad(a, ((0, 0), (rows - a.shape[1], 0), (0, 0)))


def _frame(parts, axis):
    f = jnp.concatenate(parts, axis=axis).astype(BF16)
    pad = [(0, 0)] * f.ndim
    pad[axis] = (0, -f.shape[axis] % (2 * SEL_TILE))
    return jnp.pad(f, pad)


def _k_frame(parts):
    f = _frame(parts, 1)
    n_seq, rows = f.shape[0], f.shape[1]
    return f.reshape(n_seq, rows, N_KV_HEADS, HEAD_DIM).transpose(0, 2, 1, 3)


def _v_frame(parts):
    f = _frame(parts, 2)
    n_seq, tkf = f.shape[0], f.shape[2]
    extra = jnp.zeros((n_seq, N_KV_HEADS, V_ROWS - HEAD_DIM, tkf), BF16).at[:, :, 0].set(1)
    f = jnp.concatenate([f.reshape(n_seq, N_KV_HEADS, HEAD_DIM, tkf), extra], axis=2)
    return f.reshape(n_seq, N_KV_HEADS * V_ROWS, tkf)


def _cols(a, start, n):
    b0, b1 = start // QB, -(-(start + n) // QB)
    c = a[b0:b1].transpose(1, 0, 2).reshape(a.shape[1], (b1 - b0) * QB)
    return c[:, start - b0 * QB:start - b0 * QB + n]


def _seq_major(a, n_seq, q_len):
    return a.reshape(a.shape[0], n_seq, q_len).transpose(1, 0, 2)


def _lane_pad_cols(a, n_seq, q_len):
    return jnp.pad(_seq_major(a, n_seq, q_len), ((0, 0), (0, 0), (0, QB - q_len)))


def kernel(x_prompt, x_sample, cache_k, cache_v, cache_kidx, state_h, state_conv_b, state_conv_c, meta_tokens,
           rel_bias, norm_mix, norm_ffn, norm_final, a_wq, a_wk, a_wv, a_wo, a_wiq, a_wik, a_wiw, b_wy, b_by, b_wx,
           b_bx, b_conv_w, b_conv_b, b_wr, b_br, b_wi, b_bi, b_lam, b_wo, b_bo, c_win, c_conv_w, c_wo, f_wg, f_wu,
           f_wd):
    bp, seq, d = x_prompt.shape
    bs, dec = x_sample.shape[0], x_sample.shape[1]
    past = cache_k.shape[2]
    n_meta = meta_tokens.shape[0]
    depth = norm_mix.shape[0]
    assert seq % QB == 0 and seq % TM_SEQ == 0 and past % QB == 0 and dec <= QB and n_meta <= QB
    assert dec % PACK16 == 0 and n_meta % PACK16 == 0 and (bs * dec) % n_meta == 0

    n_main, n_samp = bp * seq, bs * dec
    n_tail = -(-(n_samp + n_meta) // TM_PROJ) * TM_PROJ
    x_main = x_prompt.reshape(n_main, d)
    x_tail = jnp.concatenate([x_sample.reshape(n_samp, d), meta_tokens.astype(F32),
                              jnp.zeros((n_tail - n_samp - n_meta, d), F32)], axis=0)
    tail_fill = jnp.zeros((n_tail - n_samp - n_meta, d), BF16)

    k_sel_p = min(TOP_K_MAX, seq // 4)
    k_sel_s = min(TOP_K_MAX, (past + dec) // 4)
    n_qb = seq // QB
    kv_dim = N_KV_HEADS * HEAD_DIM

    tau = jnp.arange(seq, dtype=I32)
    kend_main = (QB + CHUNK * (tau // CHUNK + 1)).reshape(n_qb, 1, QB)
    lane = jnp.arange(QB, dtype=I32)
    kend_samp = jnp.minimum(past + dec, CHUNK * ((past + jnp.minimum(lane, dec - 1)) // CHUNK + 1)).reshape(1, 1, QB)
    kend_meta = jnp.full((1, 1, QB), n_meta, I32)

    tb = _bias_call(rel_bias.astype(F32))
    zeros_bias = jnp.zeros((1, d), F32)
    gfin = norm_final.reshape(1, d).astype(F32)

    new = {name: [] for name in ("k_p", "v_p", "ki_p", "h_p", "cb_p", "cc_p", "k_s", "v_s", "ki_s", "h_s", "cb_s",
                                 "cc_s")}
    for l in range(depth):
        s = l // N_MIXERS
        g_mix = norm_mix[l].reshape(1, d)
        if l % N_MIXERS == 0:
            wrow = jnp.concatenate([a_wk[s], a_wv[s], a_wik[s]], axis=1).astype(BF16)
            wt = jnp.concatenate([a_wq[s].T * HEAD_DIM ** -0.5, a_wiq[s].T * IDX_DIM ** -0.5, a_wv[s].T, a_wiw[s].T,
                                  jnp.zeros((SUBLANES, d), F32)], axis=0).astype(BF16)
            k, v, ki, qt, qit, vt, wit = _proj_call(x_main, g_mix, wrow, wt)
            k_t, v_t, ki_t, qt_t, qit_t, vt_t, wit_t = _proj_call(x_tail, g_mix, wrow, wt)

            k_meta, v_meta, ki_meta = (a[n_samp:n_samp + n_meta] for a in (k_t, v_t, ki_t))
            k_main, v_main, ki_main = (a.reshape(bp, seq, -1) for a in (k, v, ki))
            k_samp, v_samp, ki_samp = (a[:n_samp].reshape(bs, dec, -1) for a in (k_t, v_t, ki_t))
            bc = lambda a: jnp.broadcast_to(a[None], (bp,) + a.shape)
            new["k_p"].append(jnp.concatenate([bc(k_meta), k_main], axis=1).reshape(bp, n_meta + seq, N_KV_HEADS,
                                                                                   HEAD_DIM))
            new["v_p"].append(jnp.concatenate([bc(v_meta), v_main], axis=1).reshape(bp, n_meta + seq, N_KV_HEADS,
                                                                                   HEAD_DIM))
            new["ki_p"].append(jnp.concatenate([bc(ki_meta), ki_main], axis=1))
            new["k_s"].append(k_samp.reshape(bs, dec, N_KV_HEADS, HEAD_DIM))
            new["v_s"].append(v_samp.reshape(bs, dec, N_KV_HEADS, HEAD_DIM))
            new["ki_s"].append(ki_samp)

            front = lambda a: jnp.pad(bc(a), ((0, 0), (QB - n_meta, 0), (0, 0)))
            samp_cols = lambda a: _cols(a, 0, n_samp)
            meta_cols = lambda a: _cols(a, n_samp, n_meta)
            vt_meta = meta_cols(vt_t)
            vt_main = vt.reshape(bp, n_qb, kv_dim, QB).transpose(0, 2, 1, 3).reshape(bp, kv_dim, seq)
            vtf_main = _v_frame([jnp.pad(bc(vt_meta), ((0, 0), (0, 0), (QB - n_meta, 0))), vt_main])
            pre_main = _attn_call(
                kend_main, qt, qit, wit, _k_frame([front(k_meta), k_main]), vtf_main,
                _frame([front(ki_meta), ki_main], 1), tb, rel_bias,
                n_out_rows=n_main, diag_off=1, kstart=QB - n_meta, k_sel=k_sel_p, name="dsa_attn_prompt")
            vtf_samp = _v_frame([cache_v[s].reshape(bs, past, kv_dim).transpose(0, 2, 1),
                                 _seq_major(samp_cols(vt_t), bs, dec)])
            o_samp = _attn_call(
                kend_samp, _lane_pad_cols(samp_cols(qt_t), bs, dec),
                _lane_pad_cols(samp_cols(qit_t), bs, dec), _lane_pad_cols(samp_cols(wit_t), bs, dec),
                _k_frame([cache_k[s].reshape(bs, past, kv_dim), k_samp]), vtf_samp,
                _frame([cache_kidx[s], ki_samp], 1), tb, rel_bias,
                n_out_rows=bs * QB, diag_off=past // QB, kstart=0, k_sel=k_sel_s, name="dsa_attn_sample")
            o_meta = _attn_call(
                kend_meta, _lane_pad_cols(meta_cols(qt_t), 1, n_meta),
                _lane_pad_cols(meta_cols(qit_t), 1, n_meta), _lane_pad_cols(meta_cols(wit_t), 1, n_meta),
                _k_frame([k_meta[None]]), _v_frame([vt_meta[None]]), _frame([ki_meta[None]], 1), tb, rel_bias,
                n_out_rows=QB, diag_off=0, kstart=0, k_sel=k_sel_p, name="dsa_attn_meta")
            pre_tail = jnp.concatenate([o_samp.reshape(bs, QB, d)[:, :dec].reshape(n_samp, d), o_meta[:n_meta],
                                        tail_fill], axis=0)
            wo, bo = a_wo[s].astype(BF16), zeros_bias
        elif l % N_MIXERS == 1:
            consts = (g_mix, b_wy[s].astype(BF16), b_by[s].reshape(1, d), b_wx[s].astype(BF16), b_bx[s].reshape(1, d),
                      jnp.pad(b_conv_w[s], ((0, SUBLANES - CONV_B), (0, 0))), b_conv_b[s].reshape(1, d),
                      b_wr[s].astype(BF16), b_br[s].reshape(1, d), b_wi[s].astype(BF16), b_bi[s].reshape(1, d),
                      b_lam[s].reshape(1, d))
            pre_tail = jnp.zeros((n_tail, d), BF16)
            pre_tail, h_m, tail_m = _mixb_call(x_tail, pre_tail, jnp.zeros((1, 1, d), F32),
                                               jnp.zeros((1, SUBLANES, d), F32), *consts, n_seq=1, t_len=n_meta,
                                               row_off=n_samp, name="rglru_meta")
            pre_tail, h_s, tail_s = _mixb_call(x_tail, pre_tail, state_h[s].reshape(bs, 1, d),
                                               _front_pad_rows(state_conv_b[s], SUBLANES), *consts, n_seq=bs,
                                               t_len=dec, row_off=0, name="rglru_sample")
            pre_main, h_p, tail_p = _mixb_call(x_main, None, jnp.broadcast_to(h_m, (bp, 1, d)),
                                               jnp.broadcast_to(tail_m, (bp, SUBLANES, d)), *consts, n_seq=bp,
                                               t_len=seq, row_off=0, name="rglru_prompt")
            new["h_p"].append(h_p.reshape(bp, d))
            new["cb_p"].append(tail_p[:, SUBLANES - (CONV_B - 1):])
            new["h_s"].append(h_s.reshape(bs, d))
            new["cb_s"].append(tail_s[:, SUBLANES - (CONV_B - 1):])
            wo, bo = b_wo[s].astype(BF16), b_bo[s].reshape(1, d)
        else:
            consts = (g_mix, c_win[s].astype(BF16), jnp.pad(c_conv_w[s], ((0, SUBLANES - CONV_C), (0, 0))))
            pre_tail = jnp.zeros((n_tail, d), BF16)
            pre_tail, tail_m = _mixc_call(x_tail, pre_tail, jnp.zeros((1, SUBLANES, d), F32), *consts, n_seq=1,
                                          t_len=n_meta, row_off=n_samp, name="sconv_meta")
            pre_tail, tail_s = _mixc_call(x_tail, pre_tail, _front_pad_rows(state_conv_c[s], SUBLANES), *consts,
                                          n_seq=bs, t_len=dec, row_off=0, name="sconv_sample")
            pre_main, tail_p = _mixc_call(x_main, None, jnp.broadcast_to(tail_m, (bp, SUBLANES, d)), *consts,
                                          n_seq=bp, t_len=seq, row_off=0, name="sconv_prompt")
            new["cc_p"].append(tail_p[:, SUBLANES - (CONV_C - 1):])
            new["cc_s"].append(tail_s[:, SUBLANES - (CONV_C - 1):])
            wo, bo = c_wo[s].astype(BF16), zeros_bias
        ffn = (wo, bo, norm_ffn[l].reshape(1, d), f_wg[l].astype(BF16), f_wu[l].astype(BF16), f_wd[l].astype(BF16),
               gfin, l == depth - 1)
        x_main = _ffn_call(x_main, pre_main, *ffn)
        x_tail = _ffn_call(x_tail, pre_tail, *ffn)

    y_prompt = x_main.reshape(bp, seq, d)
    y_sample = x_tail[:n_samp].reshape(bs, dec, d)
    st = lambda name: jnp.stack(new[name])
    return (y_prompt, y_sample, st("k_p"), st("v_p"), st("ki_p"), st("h_p"), st("cb_p"), st("cc_p"),
            st("k_s"), st("v_s"), st("ki_s"), st("h_s"), st("cb_s"), st("cc_s"))
```

```python
import functools

import jax
import jax.numpy as jnp
from jax import lax
from jax.experimental import pallas as pl
from jax.experimental.pallas import tpu as pltpu

F32 = jnp.float32
BF16 = jnp.bfloat16
I32 = jnp.int32
I16 = jnp.int16

CHUNK = 64
N_MIXERS = 3
N_HEADS = 16
N_KV_HEADS = 4
GROUP = N_HEADS // N_KV_HEADS
HEAD_DIM = 64
IDX_HEADS = 8
IDX_DIM = 64
TOP_K_MAX = 256
NEG_INF = -1e30
N_BUCKETS = 32
LRU_BLOCKS = 4
CONV_B = 4
CONV_C = 3
LRU_C = 8.0
RMS_EPS = 1e-6
BUCKET_STEPS = (12, 16, 23, 32, 46, 64, 91)
FAR_BUCKET = N_BUCKETS // 2 - 1

LANES = 128
SUBLANES = 8
TM_PROJ = 256
TM_FFN = 512
TM_SEQ = 256
QB = 128
SEL_TILE = 2 * QB
INT_MIN = -2 ** 31
I16_MIN = -2 ** 15
PACK16 = 2 * SUBLANES
V_ROWS = HEAD_DIM + PACK16
LOG2E = 1.4426950408889634
VMEM_LIMIT = 56 * 1024 * 1024


def _cparams(sem):
    return pltpu.CompilerParams(dimension_semantics=sem, vmem_limit_bytes=VMEM_LIMIT)


def _const_spec(shape):
    nd = len(shape)
    return pl.BlockSpec(shape, lambda *_: (0,) * nd, pipeline_mode=pl.Buffered(1))


def _rmsnorm(x, g):
    ms = jnp.mean(x * x, axis=-1, keepdims=True)
    return x * lax.rsqrt(ms + RMS_EPS) * g


def _dot(a, b):
    return jnp.dot(a, b, preferred_element_type=F32)


def _dot_nt(a, b):
    return lax.dot_general(a, b, (((1,), (1,)), ((), ())), preferred_element_type=F32)


def _ffn_kernel(x_ref, pre_ref, wo_ref, bo_ref, gf_ref, wg_ref, wu_ref, wd_ref, gfin_ref, out_ref, *,
                n_chunks, fc, final_norm):
    x1 = x_ref[...] + _dot(pre_ref[...], wo_ref[...]) + bo_ref[...]
    xn = _rmsnorm(x1, gf_ref[...]).astype(BF16)
    acc = x1
    for c in range(n_chunks):
        gt = _dot(xn, wg_ref[:, c * fc:(c + 1) * fc])
        up = _dot(xn, wu_ref[:, c * fc:(c + 1) * fc])
        hm = (gt * jax.nn.sigmoid(gt) * up).astype(BF16)
        acc = acc + _dot(hm, wd_ref[c * fc:(c + 1) * fc, :])
    if final_norm:
        acc = _rmsnorm(acc, gfin_ref[...])
    out_ref[...] = acc


def _ffn_call(x, pre, wo, bo, gf, wg, wu, wd, gfin, final_norm):
    n, d = x.shape
    dff = wg.shape[1]
    n_chunks = 2 if dff % (2 * LANES) == 0 else 1
    fc = dff // n_chunks
    row = lambda i: (i, 0)
    tm = TM_FFN if n % TM_FFN == 0 else TM_PROJ
    return pl.pallas_call(
        functools.partial(_ffn_kernel, n_chunks=n_chunks, fc=fc, final_norm=final_norm),
        grid=(n // tm,),
        in_specs=[pl.BlockSpec((tm, d), row), pl.BlockSpec((tm, pre.shape[1]), row),
                  _const_spec(wo.shape), _const_spec(bo.shape), _const_spec(gf.shape),
                  _const_spec(wg.shape), _const_spec(wu.shape), _const_spec(wd.shape), _const_spec(gfin.shape)],
        out_specs=pl.BlockSpec((tm, d), row),
        out_shape=jax.ShapeDtypeStruct((n, d), F32),
        compiler_params=_cparams(("parallel",)),
        name="outproj_swiglu",
    )(x, pre, wo, bo, gf, wg, wu, wd, gfin)


def _proj_kernel(x_ref, g_ref, wrow_ref, wt_ref, k_ref, v_ref, ki_ref, kb_ref, kib_ref, qt_ref, qit_ref, vt_ref,
                 wit_ref, *, n_kv, n_q, n_iq):
    xn = _rmsnorm(x_ref[...], g_ref[...]).astype(BF16)
    row = _dot(xn, wrow_ref[...])
    k_ref[...] = row[:, :n_kv]
    v_ref[...] = row[:, n_kv:2 * n_kv]
    ki_ref[...] = row[:, 2 * n_kv:]
    kb_ref[...] = row[:, :n_kv].astype(BF16)
    kib_ref[...] = row[:, 2 * n_kv:].astype(BF16)
    tt = _dot_nt(wt_ref[...], xn)
    for b in range(TM_PROJ // QB):
        blk = tt[:, b * QB:(b + 1) * QB]
        qt_ref[b] = blk[:n_q].astype(BF16)
        qit_ref[b] = blk[n_q:n_q + n_iq].astype(BF16)
        vt_ref[b] = blk[n_q + n_iq:n_q + n_iq + n_kv].astype(BF16)
        wit_ref[b] = blk[n_q + n_iq + n_kv:n_q + n_iq + n_kv + IDX_HEADS] * (IDX_HEADS ** -0.5)


def _proj_call(x, g, wrow, wt):
    n, d = x.shape
    n_kv = N_KV_HEADS * HEAD_DIM
    n_q = N_HEADS * HEAD_DIM
    n_iq = IDX_HEADS * IDX_DIM
    row = lambda i: (i, 0)
    blocks = TM_PROJ // QB
    feat_spec = lambda f: pl.BlockSpec((blocks, f, QB), lambda i: (i, 0, 0))
    feat_shape = lambda f, dt: jax.ShapeDtypeStruct((n // QB, f, QB), dt)
    return pl.pallas_call(
        functools.partial(_proj_kernel, n_kv=n_kv, n_q=n_q, n_iq=n_iq),
        grid=(n // TM_PROJ,),
        in_specs=[pl.BlockSpec((TM_PROJ, d), row), _const_spec(g.shape), _const_spec(wrow.shape),
                  _const_spec(wt.shape)],
        out_specs=[pl.BlockSpec((TM_PROJ, n_kv), row), pl.BlockSpec((TM_PROJ, n_kv), row),
                   pl.BlockSpec((TM_PROJ, IDX_DIM), row), pl.BlockSpec((TM_PROJ, n_kv), row),
                   pl.BlockSpec((TM_PROJ, IDX_DIM), row), feat_spec(n_q), feat_spec(n_iq), feat_spec(n_kv),
                   feat_spec(IDX_HEADS)],
        out_shape=[jax.ShapeDtypeStruct((n, n_kv), F32), jax.ShapeDtypeStruct((n, n_kv), F32),
                   jax.ShapeDtypeStruct((n, IDX_DIM), F32), jax.ShapeDtypeStruct((n, n_kv), BF16),
                   jax.ShapeDtypeStruct((n, IDX_DIM), BF16), feat_shape(n_q, BF16), feat_shape(n_iq, BF16),
                   feat_shape(n_kv, BF16), feat_shape(IDX_HEADS, F32)],
        compiler_params=_cparams(("parallel",)),
        name="attn_proj",
    )(x, g, wrow, wt)


def _bias_kernel(tab_ref, tb_ref):
    h = pl.program_id(0)
    d = pl.program_id(1)
    kj = lax.broadcasted_iota(I32, (QB, QB), 0)
    qi = lax.broadcasted_iota(I32, (QB, QB), 1)
    rel = (d - 1) * QB + kj - qi
    n = jnp.abs(rel)
    large = jnp.full((QB, QB), N_BUCKETS // 4, I32)
    for s in BUCKET_STEPS:
        large = large + jnp.where(n >= s, 1, 0)
    bucket = jnp.where(rel > 0, N_BUCKETS // 2, 0) + jnp.where(n < N_BUCKETS // 4, n, large)
    val = jnp.zeros((QB, QB), F32)
    for b in range(N_BUCKETS):
        val = jnp.where(bucket == b, tab_ref[b, h], val)
    tb_ref[0] = val


def _bias_call(table):
    return pl.pallas_call(
        _bias_kernel,
        grid=(N_HEADS, 2),
        in_specs=[pl.BlockSpec(memory_space=pltpu.SMEM)],
        out_specs=pl.BlockSpec((1, QB, QB), lambda h, d: (h, d, 0)),
        out_shape=jax.ShapeDtypeStruct((N_HEADS, 2 * QB, QB), F32),
        compiler_params=_cparams(("arbitrary", "arbitrary")),
        name="rel_bias_tiles",
    )(table)


def _attn_kernel(kend_ref, qt_ref, qit_ref, wit_ref, *refs, diag_off, kstart, k_sel, has_prev, row_bits, build):
    if build:
        (kb_ref, vt_ref, kib_ref, k0_ref, v0_ref, ki0_ref, tb_ref, tab_ref, o_ref, key_scr, hi_scr, lo_scr, lo2_scr,
         ls_scr, oacc_scr, kf_scr, vtf_scr, kif_scr) = refs
        seq = kb_ref.shape[0]
        tkf = kf_scr.shape[1]

        @pl.when(pl.program_id(1) == 0)
        def _():
            kif_scr[0:QB, :] = ki0_ref[...]
            kif_scr[QB:QB + seq, :] = kib_ref[...]
            kif_scr[QB + seq:tkf, :] = jnp.zeros((tkf - QB - seq, IDX_DIM), BF16)
            for g in range(N_KV_HEADS):
                cols = slice(g * HEAD_DIM, (g + 1) * HEAD_DIM)
                kf_scr[g, 0:QB, :] = k0_ref[:, cols]
                for c in range(seq // SEL_TILE):
                    kf_scr[g, QB + c * SEL_TILE:QB + (c + 1) * SEL_TILE, :] = kb_ref[c * SEL_TILE:(c + 1) * SEL_TILE,
                                                                                     cols]
                kf_scr[g, QB + seq:tkf, :] = jnp.zeros((tkf - QB - seq, HEAD_DIM), BF16)
                vrows = slice(g * V_ROWS, g * V_ROWS + HEAD_DIM)
                vtf_scr[vrows, 0:QB] = v0_ref[cols, :]
                for j in range(seq // QB):
                    vtf_scr[vrows, QB + j * QB:QB + (j + 1) * QB] = vt_ref[j, cols, :]
                vtf_scr[vrows, QB + seq:tkf] = jnp.zeros((HEAD_DIM, tkf - QB - seq), BF16)
                extra = lax.broadcasted_iota(I32, (V_ROWS - HEAD_DIM, tkf), 0) == 0
                vtf_scr[g * V_ROWS + HEAD_DIM:(g + 1) * V_ROWS, :] = jnp.where(extra, 1.0, 0.0).astype(BF16)

        k_rows = lambda g, r0, rows: kf_scr[g, pl.ds(r0, rows), :]
        v_cols = lambda g, r0, rows: vtf_scr[g * V_ROWS:(g + 1) * V_ROWS, pl.ds(r0, rows)]
        ki_rows = lambda r0, rows: kif_scr[pl.ds(r0, rows), :]
    else:
        (kf_ref, vtf_ref, kif_ref, tb_ref, tab_ref, o_ref, key_scr, hi_scr, lo_scr, lo2_scr, ls_scr,
         oacc_scr) = refs
        k_rows = lambda g, r0, rows: kf_ref[0, g, pl.ds(r0, rows), :]
        v_cols = lambda g, r0, rows: vtf_ref[0, g * V_ROWS:(g + 1) * V_ROWS, pl.ds(r0, rows)]
        ki_rows = lambda r0, rows: kif_ref[0, pl.ds(r0, rows), :]
    diag = pl.program_id(1) + diag_off
    nt = diag + 1
    nst = (nt + SEL_TILE // QB - 1) // (SEL_TILE // QB)
    kend = kend_ref[0]
    qi_cat = jnp.concatenate([qit_ref[0, h * IDX_DIM:(h + 1) * IDX_DIM, :] for h in range(IDX_HEADS)], axis=1)
    wi = wit_ref[0]
    sel_iota = lax.broadcasted_iota(I32, (SEL_TILE, QB), 0)

    def tile_start(t):
        return pl.multiple_of(t * QB, QB)

    def sel_start(t):
        return pl.multiple_of(t * SEL_TILE, SEL_TILE)

    def score_rows(r0):
        s = _dot(ki_rows(r0, SEL_TILE), qi_cat)
        sc = jnp.zeros((SEL_TILE, QB), F32)
        for h in range(IDX_HEADS):
            sc = sc + wi[h:h + 1, :] * jnp.maximum(s[:, h * QB:(h + 1) * QB], 0.0)
        bits = lax.bitcast_convert_type(sc + 0.0, I32)
        key = jnp.where(bits >= 0, bits, bits ^ 0x7FFFFFFF)
        rows = r0 + sel_iota
        adm = (rows >= kstart) & (rows < kend)
        key = jnp.where(adm, key, INT_MIN)
        key_scr[pl.ds(r0, SEL_TILE), :] = key
        hi_scr[pl.ds(r0, SEL_TILE), :] = (key >> 16).astype(I16)
        lo_scr[pl.ds(r0, SEL_TILE), :] = ((key & 0xFFFF) + I16_MIN).astype(I16)

    pair_rows = 2 * SEL_TILE

    def score_pair(t, carry):
        r0 = pl.multiple_of(t * pair_rows, pair_rows)
        score_rows(r0)
        score_rows(r0 + SEL_TILE)
        return carry

    lax.fori_loop(0, (nst + 1) // 2, score_pair, 0)

    pair_vregs = pair_rows // PACK16

    def as_packed(v):
        return jnp.broadcast_to(v, (PACK16, QB)).astype(I16)

    def threshold(n_pair):
        def count16(src_scr, cand):
            c16 = as_packed(cand)[None]
            acc = jnp.zeros((PACK16, QB), BF16)
            for t in range(n_pair):
                v = src_scr[t * pair_rows:(t + 1) * pair_rows, :].reshape(pair_vregs, PACK16, QB)
                ind = jnp.where(v >= c16, jnp.ones((), BF16), jnp.zeros((), BF16))
                parts = [ind[i] for i in range(pair_vregs)]
                while len(parts) > 1:
                    parts = [parts[i] + parts[i + 1] for i in range(0, len(parts), 2)]
                acc = acc + parts[0]
            return acc.astype(F32).sum(axis=0, keepdims=True)

        def search16(src_scr, base, c_start):
            def step(b, carry):
                t_acc, c_acc = carry
                cand = t_acc + lax.shift_left(jnp.int32(1), 15 - b)
                c = base + count16(src_scr, cand)
                ok = c >= k_sel
                return jnp.where(ok, cand, t_acc), jnp.where(ok, c, c_acc)

            return lax.fori_loop(0, 16, step, (jnp.full((1, QB), I16_MIN, I32), c_start))

        thr_hi, c_hi = search16(hi_scr, 0.0, jnp.full((1, QB), n_pair * pair_rows, F32))
        above = jnp.where(thr_hi == -I16_MIN - 1, 0.0, count16(hi_scr, thr_hi + 1))
        hi16 = as_packed(thr_hi)[None]
        for t in range(n_pair):
            rows = slice(t * pair_rows, (t + 1) * pair_rows)
            hi = hi_scr[rows, :].reshape(pair_vregs, PACK16, QB)
            lo = lo_scr[rows, :].reshape(pair_vregs, PACK16, QB)
            lo2_scr[rows, :] = jnp.where(hi == hi16, lo, jnp.full((), I16_MIN, I16)).reshape(pair_rows, QB)
        thr_lo, c_ge = search16(lo2_scr, above, c_hi)
        return thr_hi * 65536 + (thr_lo - I16_MIN), c_ge

    def count(pred):
        def body(t, acc):
            r0 = sel_start(t)
            ind = jnp.where(pred(key_scr[pl.ds(r0, SEL_TILE), :], r0 + sel_iota), 1, 0)
            return acc + ind.reshape(SEL_TILE // SUBLANES, SUBLANES, QB).sum(axis=0)
        acc = lax.fori_loop(0, nst, body, jnp.zeros((SUBLANES, QB), I32))
        return acc.sum(axis=0, keepdims=True)

    max_pairs = hi_scr.shape[0] // pair_rows
    thr, c_ge = lax.switch((nst + 1) // 2 - 1, [functools.partial(threshold, n) for n in range(1, max_pairs + 1)])

    tied = (c_ge > k_sel) & (thr > INT_MIN)
    big = jnp.full((1, QB), 2 ** row_bits, I32)

    def tie_limit():
        need = k_sel - count(lambda kt, rows: kt > thr)

        def lim_step(b, lim):
            cand = lim + lax.shift_left(jnp.int32(1), row_bits - 1 - b)
            c = count(lambda kt, rows: (kt == thr) & (rows < cand))
            return jnp.where(c < need, cand, lim)

        lim = lax.fori_loop(0, row_bits, lim_step, jnp.zeros((1, QB), I32))
        return jnp.where(tied, lim, big)

    rlim = lax.cond(jnp.max(tied.astype(I32)) > 0, tie_limit, lambda: big)
    rlim = jnp.where(thr > INT_MIN, rlim, -1)

    gq = GROUP * QB
    qg = [jnp.concatenate([qt_ref[0, (g * GROUP + r) * HEAD_DIM:(g * GROUP + r + 1) * HEAD_DIM, :]
                           for r in range(GROUP)], axis=1) for g in range(N_KV_HEADS)]

    def logits_rows(r0, rows, m8, bias_of_head):
        kt = key_scr[pl.ds(r0, rows), :]
        keep = (kt > thr) | ((kt == thr) & (sel_iota[:rows] <= rlim - r0))
        new_m8 = []
        for g in range(N_KV_HEADS):
            lg = _dot(k_rows(g, r0, rows), qg[g])
            parts = []
            for r in range(GROUP):
                part = (lg[:, r * QB:(r + 1) * QB] + bias_of_head(g * GROUP + r)) * LOG2E
                parts.append(jnp.where(keep, part, NEG_INF))
            lg = jnp.concatenate(parts, axis=1)
            ls_scr[pl.ds(r0, rows), g * gq:(g + 1) * gq] = lg
            new_m8.append(jnp.maximum(m8[g], lg.reshape(rows // SUBLANES, SUBLANES, gq).max(axis=0)))
        return tuple(new_m8)

    far_bias = lambda h: tab_ref[FAR_BUCKET, h]
    m8 = tuple(jnp.full((SUBLANES, gq), NEG_INF, F32) for _ in range(N_KV_HEADS))
    n_far = diag - 1 if has_prev else diag
    def far_pair(t, m8):
        r0 = pl.multiple_of(t * pair_rows, pair_rows)
        return logits_rows(r0 + SEL_TILE, SEL_TILE, logits_rows(r0, SEL_TILE, m8, far_bias), far_bias)

    m8 = lax.fori_loop(0, n_far // 4, far_pair, m8)
    m8 = lax.cond(n_far % 4 >= 2, lambda m: logits_rows(sel_start(n_far // 4 * 2), SEL_TILE, m, far_bias),
                  lambda m: m, m8)
    m8 = lax.cond(n_far % 2 == 1, lambda m: logits_rows(tile_start(n_far - 1), QB, m, far_bias), lambda m: m, m8)
    if has_prev:
        m8 = logits_rows(tile_start(diag - 1), 2 * QB, m8, lambda h: tb_ref[h])
    else:
        m8 = logits_rows(tile_start(diag), QB, m8, lambda h: tb_ref[h, QB:, :])
    m = [jnp.max(m8[g], axis=0, keepdims=True) for g in range(N_KV_HEADS)]

    @pl.when(nt % 2 == 1)
    def _():
        ls_scr[pl.ds(tile_start(nt), QB), :] = jnp.full((QB, N_HEADS * QB), NEG_INF, F32)

    oacc_scr[...] = jnp.zeros(oacc_scr.shape, F32)

    def pv_rows(r0):
        for g in range(N_KV_HEADS):
            p = jnp.exp2(ls_scr[pl.ds(r0, SEL_TILE), g * gq:(g + 1) * gq] - m[g])
            oacc_scr[g] += _dot(v_cols(g, r0, SEL_TILE), p.astype(BF16))

    def pv_pair(t, carry):
        r0 = pl.multiple_of(t * pair_rows, pair_rows)
        pv_rows(r0)
        pv_rows(r0 + SEL_TILE)
        return carry

    lax.fori_loop(0, nst // 2, pv_pair, 0)

    @pl.when(nst % 2 == 1)
    def _():
        pv_rows(sel_start(nst - 1))
    out_rows = []
    for g in range(N_KV_HEADS):
        o_g = oacc_scr[g, :HEAD_DIM] / oacc_scr[g, HEAD_DIM:HEAD_DIM + 1]
        out_rows.extend(o_g[:, r * QB:(r + 1) * QB] for r in range(GROUP))
    o_ref[...] = jnp.concatenate(out_rows, axis=0).T.astype(BF16)


def _attn_call(kend, qt, qit, wit, frames, tb, table, *, n_seq, build, n_out_rows, diag_off, kstart, k_sel, name):
    n_qb = kend.shape[0]
    kv_dim = N_KV_HEADS * HEAD_DIM
    if build:
        seq = frames[0].shape[0] // n_seq
        tkf = -(-(QB + seq) // (2 * SEL_TILE)) * 2 * SEL_TILE
        frame_specs = [pl.BlockSpec((seq, kv_dim), lambda s, i: (s, 0)),
                       pl.BlockSpec((seq // QB, kv_dim, QB), lambda s, i: (s, 0, 0)),
                       pl.BlockSpec((seq, IDX_DIM), lambda s, i: (s, 0))] + [_const_spec(a.shape) for a in frames[3:]]
        frame_scratch = [pltpu.VMEM((N_KV_HEADS, tkf, HEAD_DIM), BF16), pltpu.VMEM((N_KV_HEADS * V_ROWS, tkf), BF16),
                         pltpu.VMEM((tkf, IDX_DIM), BF16)]
    else:
        kf, vtf, kif = frames
        tkf = kf.shape[2]
        assert vtf.shape[2] == tkf and kif.shape[1] == tkf
        frame_specs = [pl.BlockSpec((1, N_KV_HEADS, tkf, HEAD_DIM), lambda s, i: (s, 0, 0, 0)),
                       pl.BlockSpec((1, N_KV_HEADS * V_ROWS, tkf), lambda s, i: (s, 0, 0)),
                       pl.BlockSpec((1, tkf, IDX_DIM), lambda s, i: (s, 0, 0))]
        frame_scratch = []
    assert tkf % (2 * SEL_TILE) == 0
    assert tkf // PACK16 <= 256, "per-slot bf16 counts must stay exact"
    has_prev = diag_off >= 1
    row_bits = max(1, (tkf - 1).bit_length())
    feat = lambda a: pl.BlockSpec((1, a.shape[1], QB), lambda s, i: (s * n_qb + i, 0, 0))
    return pl.pallas_call(
        functools.partial(_attn_kernel, diag_off=diag_off, kstart=kstart, k_sel=k_sel, has_prev=has_prev,
                          row_bits=row_bits, build=build),
        grid=(n_seq, n_qb),
        in_specs=[pl.BlockSpec((1, 1, QB), lambda s, i: (i, 0, 0)), feat(qt), feat(qit), feat(wit)] + frame_specs
                 + [_const_spec(tb.shape), pl.BlockSpec(memory_space=pltpu.SMEM)],
        out_specs=pl.BlockSpec((QB, N_HEADS * HEAD_DIM), lambda s, i: (s * n_qb + i, 0)),
        out_shape=jax.ShapeDtypeStruct((n_out_rows, N_HEADS * HEAD_DIM), BF16),
        scratch_shapes=[pltpu.VMEM((tkf, QB), I32)] + [pltpu.VMEM((tkf, QB), I16)] * 3
                       + [pltpu.VMEM((tkf, N_HEADS * QB), F32), pltpu.VMEM((N_KV_HEADS, V_ROWS, GROUP * QB), F32)]
                       + frame_scratch,
        compiler_params=_cparams(("arbitrary", "arbitrary")),
        name=name,
    )(kend, qt, qit, wit, *frames, tb, table)


def _softplus(x):
    return jnp.maximum(x, 0.0) + jnp.log1p(jnp.exp(-jnp.abs(x)))


def _mixb_kernel(x_ref, h0_ref, buf_ref, g_ref, wy_ref, by_ref, wx_ref, bx_ref, cw_ref, cb_ref, wr_ref, br_ref,
                 wi_ref, bi_ref, lam_ref, *rest, tm):
    pre_ref, hlast_ref, tail_ref, xe_scr, h_scr = rest[-5:]

    @pl.when(pl.program_id(1) == 0)
    def _():
        xe_scr[0:SUBLANES, :] = buf_ref[0]
        h_scr[...] = h0_ref[0]

    xn = _rmsnorm(x_ref[...], g_ref[...]).astype(BF16)
    y = jax.nn.gelu(_dot(xn, wy_ref[...]) + by_ref[...])
    xe_scr[SUBLANES:SUBLANES + tm, :] = _dot(xn, wx_ref[...]) + bx_ref[...]
    xc = cb_ref[...] + cw_ref[0:1, :] * xe_scr[SUBLANES - 3:SUBLANES - 3 + tm, :]
    for j in range(1, CONV_B):
        xc = xc + cw_ref[j:j + 1, :] * xe_scr[SUBLANES - 3 + j:SUBLANES - 3 + j + tm, :]
    xcb = xc.astype(BF16)
    d_rnn = xc.shape[1]
    blk = d_rnn // LRU_BLOCKS
    r_pre = jnp.concatenate([_dot(xcb[:, n * blk:(n + 1) * blk], wr_ref[n]) for n in range(LRU_BLOCKS)], axis=1)
    i_pre = jnp.concatenate([_dot(xcb[:, n * blk:(n + 1) * blk], wi_ref[n]) for n in range(LRU_BLOCKS)], axis=1)
    r = jax.nn.sigmoid(r_pre + br_ref[...])
    ig = jax.nn.sigmoid(i_pre + bi_ref[...])
    log_a = -LRU_C * r * _softplus(-lam_ref[...])
    a = jnp.exp(log_a)
    u = jnp.sqrt(jnp.tanh(-log_a) * (1.0 + a * a)) * (ig * xc)
    a = a.reshape(tm // SUBLANES, SUBLANES, d_rnn)
    u = u.reshape(tm // SUBLANES, SUBLANES, d_rnn)
    row_in_group = lax.broadcasted_iota(I32, a.shape, 1)
    s = 1
    while s < SUBLANES:
        a_sh = jnp.where(row_in_group >= s, pltpu.roll(a, s, 1), 1.0)
        u_sh = jnp.where(row_in_group >= s, pltpu.roll(u, s, 1), 0.0)
        u = a * u_sh + u
        a = a * a_sh
        s *= 2
    carry = h_scr[...]
    h_groups = []
    for grp in range(tm // SUBLANES):
        h_grp = a[grp] * carry + u[grp]
        carry = h_grp[SUBLANES - 1:SUBLANES, :]
        h_groups.append(h_grp)
    h = jnp.concatenate(h_groups, axis=0)
    pre_ref[...] = (h * y).astype(BF16)
    h_scr[...] = carry
    hlast_ref[0] = carry
    tail = xe_scr[tm:tm + SUBLANES, :]
    xe_scr[0:SUBLANES, :] = tail
    tail_ref[0] = tail


def _mixc_kernel(x_ref, buf_ref, g_ref, win_ref, cw_ref, *rest, tm):
    pre_ref, tail_ref, pe_scr = rest[-3:]

    @pl.when(pl.program_id(1) == 0)
    def _():
        pe_scr[0:SUBLANES, :] = buf_ref[0]

    d = x_ref.shape[1]
    xn = _rmsnorm(x_ref[...], g_ref[...]).astype(BF16)
    z = _dot(xn, win_ref[...])
    pe_scr[SUBLANES:SUBLANES + tm, :] = z[:, d:2 * d] * z[:, 2 * d:]
    conv = cw_ref[0:1, :] * pe_scr[SUBLANES - 2:SUBLANES - 2 + tm, :]
    for j in range(1, CONV_C):
        conv = conv + cw_ref[j:j + 1, :] * pe_scr[SUBLANES - 2 + j:SUBLANES - 2 + j + tm, :]
    pre_ref[...] = (z[:, :d] * conv).astype(BF16)
    tail = pe_scr[tm:tm + SUBLANES, :]
    pe_scr[0:SUBLANES, :] = tail
    tail_ref[0] = tail


def _shared_out(buf, index):
    if buf is None:
        return {"specs": [], "args": (), "aliases": {}}
    return {"specs": [pl.BlockSpec(memory_space=pl.ANY)], "args": (buf,), "aliases": {index: 0}}


def _seq_specs(n_seq, t_len, row_off, d):
    tm = min(TM_SEQ, t_len)
    n_tt = t_len // tm
    off = row_off // tm
    xrow = pl.BlockSpec((tm, d), lambda s, j: (off + s * n_tt + j, 0))
    state = lambda rows: pl.BlockSpec((1, rows, d), lambda s, j: (s, 0, 0))
    return tm, n_tt, xrow, state


def _mixb_call(x, pre_buf, h0, buf, g, wy, by, wx, bx, cw, cb, wr, br, wi, bi, lam, *, n_seq, t_len, row_off,
               name):
    d = x.shape[1]
    tm, n_tt, xrow, state = _seq_specs(n_seq, t_len, row_off, d)
    consts = (g, wy, by, wx, bx, cw, cb, wr, br, wi, bi, lam)
    shared = _shared_out(pre_buf, 3 + len(consts))
    return pl.pallas_call(
        functools.partial(_mixb_kernel, tm=tm),
        grid=(n_seq, n_tt),
        in_specs=[xrow, state(1), state(SUBLANES)] + [_const_spec(c.shape) for c in consts] + shared["specs"],
        out_specs=[xrow, state(1), state(SUBLANES)],
        out_shape=[jax.ShapeDtypeStruct((x.shape[0], d), BF16), jax.ShapeDtypeStruct((n_seq, 1, d), F32),
                   jax.ShapeDtypeStruct((n_seq, SUBLANES, d), F32)],
        scratch_shapes=[pltpu.VMEM((tm + SUBLANES, d), F32), pltpu.VMEM((1, d), F32)],
        input_output_aliases=shared["aliases"],
        compiler_params=_cparams(("arbitrary", "arbitrary")),
        name=name,
    )(x, h0, buf, *consts, *shared["args"])


def _mixc_call(x, pre_buf, buf, g, win, cw, *, n_seq, t_len, row_off, name):
    d = x.shape[1]
    tm, n_tt, xrow, state = _seq_specs(n_seq, t_len, row_off, d)
    consts = (g, win, cw)
    shared = _shared_out(pre_buf, 2 + len(consts))
    return pl.pallas_call(
        functools.partial(_mixc_kernel, tm=tm),
        grid=(n_seq, n_tt),
        in_specs=[xrow, state(SUBLANES)] + [_const_spec(c.shape) for c in consts] + shared["specs"],
        out_specs=[xrow, state(SUBLANES)],
        out_shape=[jax.ShapeDtypeStruct((x.shape[0], d), BF16), jax.ShapeDtypeStruct((n_seq, SUBLANES, d), F32)],
        scratch_shapes=[pltpu.VMEM((tm + SUBLANES, d), F32)],
        input_output_aliases=shared["aliases"],
        compiler_params=_cparams(("arbitrary", "arbitrary")),
        name=name,
    )(x, buf, *consts, *shared["args"])


def _front_pad_rows(a, rows):
    return jnp.pad(a, ((0, 0), (rows - a.shape[1], 0), (0, 0)))


def _frame(parts, axis):
    f = jnp.concatenate(parts, axis=axis).astype(BF16)
    pad = [(0, 0)] * f.ndim
    pad[axis] = (0, -f.shape[axis] % (2 * SEL_TILE))
    return jnp.pad(f, pad)


def _k_frame(parts):
    f = _frame(parts, 1)
    n_seq, rows = f.shape[0], f.shape[1]
    return f.reshape(n_seq, rows, N_KV_HEADS, HEAD_DIM).transpose(0, 2, 1, 3)


def _v_frame(parts):
    f = _frame(parts, 2)
    n_seq, tkf = f.shape[0], f.shape[2]
    extra = jnp.zeros((n_seq, N_KV_HEADS, V_ROWS - HEAD_DIM, tkf), BF16).at[:, :, 0].set(1)
    f = jnp.concatenate([f.reshape(n_seq, N_KV_HEADS, HEAD_DIM, tkf), extra], axis=2)
    return f.reshape(n_seq, N_KV_HEADS * V_ROWS, tkf)


def _cols(a, start, n):
    b0, b1 = start // QB, -(-(start + n) // QB)
    c = a[b0:b1].transpose(1, 0, 2).reshape(a.shape[1], (b1 - b0) * QB)
    return c[:, start - b0 * QB:start - b0 * QB + n]


def _seq_major(a, n_seq, q_len):
    return a.reshape(a.shape[0], n_seq, q_len).transpose(1, 0, 2)


def _lane_pad_cols(a, n_seq, q_len):
    return jnp.pad(_seq_major(a, n_seq, q_len), ((0, 0), (0, 0), (0, QB - q_len)))


def kernel(x_prompt, x_sample, cache_k, cache_v, cache_kidx, state_h, state_conv_b, state_conv_c, meta_tokens,
           rel_bias, norm_mix, norm_ffn, norm_final, a_wq, a_wk, a_wv, a_wo, a_wiq, a_wik, a_wiw, b_wy, b_by, b_wx,
           b_bx, b_conv_w, b_conv_b, b_wr, b_br, b_wi, b_bi, b_lam, b_wo, b_bo, c_win, c_conv_w, c_wo, f_wg, f_wu,
           f_wd):
    bp, seq, d = x_prompt.shape
    bs, dec = x_sample.shape[0], x_sample.shape[1]
    past = cache_k.shape[2]
    n_meta = meta_tokens.shape[0]
    depth = norm_mix.shape[0]
    assert seq % QB == 0 and seq % TM_SEQ == 0 and past % QB == 0 and dec <= QB and n_meta <= QB
    assert dec % PACK16 == 0 and n_meta % PACK16 == 0 and (bs * dec) % n_meta == 0

    n_main, n_samp = bp * seq, bs * dec
    n_tail = -(-(n_samp + n_meta) // TM_PROJ) * TM_PROJ
    x_main = x_prompt.reshape(n_main, d)
    x_tail = jnp.concatenate([x_sample.reshape(n_samp, d), meta_tokens.astype(F32),
                              jnp.zeros((n_tail - n_samp - n_meta, d), F32)], axis=0)
    tail_fill = jnp.zeros((n_tail - n_samp - n_meta, d), BF16)

    k_sel_p = min(TOP_K_MAX, seq // 4)
    k_sel_s = min(TOP_K_MAX, (past + dec) // 4)
    n_qb = seq // QB
    kv_dim = N_KV_HEADS * HEAD_DIM

    tau = jnp.arange(seq, dtype=I32)
    kend_main = (QB + CHUNK * (tau // CHUNK + 1)).reshape(n_qb, 1, QB)
    lane = jnp.arange(QB, dtype=I32)
    kend_samp = jnp.minimum(past + dec, CHUNK * ((past + jnp.minimum(lane, dec - 1)) // CHUNK + 1)).reshape(1, 1, QB)
    kend_meta = jnp.full((1, 1, QB), n_meta, I32)

    tb = _bias_call(rel_bias.astype(F32))
    zeros_bias = jnp.zeros((1, d), F32)
    gfin = norm_final.reshape(1, d).astype(F32)

    new = {name: [] for name in ("k_p", "v_p", "ki_p", "h_p", "cb_p", "cc_p", "k_s", "v_s", "ki_s", "h_s", "cb_s",
                                 "cc_s")}
    for l in range(depth):
        s = l // N_MIXERS
        g_mix = norm_mix[l].reshape(1, d)
        if l % N_MIXERS == 0:
            wrow = jnp.concatenate([a_wk[s], a_wv[s], a_wik[s]], axis=1).astype(BF16)
            wt = jnp.concatenate([a_wq[s].T * HEAD_DIM ** -0.5, a_wiq[s].T * IDX_DIM ** -0.5, a_wv[s].T, a_wiw[s].T,
                                  jnp.zeros((SUBLANES, d), F32)], axis=0).astype(BF16)
            k, v, ki, kb, kib, qt, qit, vt, wit = _proj_call(x_main, g_mix, wrow, wt)
            k_t, v_t, ki_t, _, _, qt_t, qit_t, vt_t, wit_t = _proj_call(x_tail, g_mix, wrow, wt)

            k_meta, v_meta, ki_meta = (a[n_samp:n_samp + n_meta] for a in (k_t, v_t, ki_t))
            k_main, v_main, ki_main = (a.reshape(bp, seq, -1) for a in (k, v, ki))
            k_samp, v_samp, ki_samp = (a[:n_samp].reshape(bs, dec, -1) for a in (k_t, v_t, ki_t))
            bc = lambda a: jnp.broadcast_to(a[None], (bp,) + a.shape)
            new["k_p"].append(jnp.concatenate([bc(k_meta), k_main], axis=1).reshape(bp, n_meta + seq, N_KV_HEADS,
                                                                                   HEAD_DIM))
            new["v_p"].append(jnp.concatenate([bc(v_meta), v_main], axis=1).reshape(bp, n_meta + seq, N_KV_HEADS,
                                                                                   HEAD_DIM))
            new["ki_p"].append(jnp.concatenate([bc(ki_meta), ki_main], axis=1))
            new["k_s"].append(k_samp.reshape(bs, dec, N_KV_HEADS, HEAD_DIM))
            new["v_s"].append(v_samp.reshape(bs, dec, N_KV_HEADS, HEAD_DIM))
            new["ki_s"].append(ki_samp)

            samp_cols = lambda a: _cols(a, 0, n_samp)
            meta_cols = lambda a: _cols(a, n_samp, n_meta)
            vt_meta = meta_cols(vt_t)
            lead = lambda a, axis: jnp.pad(a, [(QB - n_meta, 0) if ax == axis else (0, 0) for ax in range(2)]).astype(BF16)
            pre_main = _attn_call(
                kend_main, qt, qit, wit, (kb, vt, kib, lead(k_meta, 0), lead(vt_meta, 1), lead(ki_meta, 0)), tb,
                rel_bias, n_seq=bp, build=True,
                n_out_rows=n_main, diag_off=1, kstart=QB - n_meta, k_sel=k_sel_p, name="dsa_attn_prompt")
            vtf_samp = _v_frame([cache_v[s].reshape(bs, past, kv_dim).transpose(0, 2, 1),
                                 _seq_major(samp_cols(vt_t), bs, dec)])
            o_samp = _attn_call(
                kend_samp, _lane_pad_cols(samp_cols(qt_t), bs, dec),
                _lane_pad_cols(samp_cols(qit_t), bs, dec), _lane_pad_cols(samp_cols(wit_t), bs, dec),
                (_k_frame([cache_k[s].reshape(bs, past, kv_dim), k_samp]), vtf_samp,
                 _frame([cache_kidx[s], ki_samp], 1)), tb, rel_bias, n_seq=bs, build=False,
                n_out_rows=bs * QB, diag_off=past // QB, kstart=0, k_sel=k_sel_s, name="dsa_attn_sample")
            o_meta = _attn_call(
                kend_meta, _lane_pad_cols(meta_cols(qt_t), 1, n_meta),
                _lane_pad_cols(meta_cols(qit_t), 1, n_meta), _lane_pad_cols(meta_cols(wit_t), 1, n_meta),
                (_k_frame([k_meta[None]]), _v_frame([vt_meta[None]]), _frame([ki_meta[None]], 1)), tb, rel_bias,
                n_seq=1, build=False, n_out_rows=QB, diag_off=0, kstart=0, k_sel=k_sel_p, name="dsa_attn_meta")
            pre_tail = jnp.concatenate([o_samp.reshape(bs, QB, d)[:, :dec].reshape(n_samp, d), o_meta[:n_meta],
                                        tail_fill], axis=0)
            wo, bo = a_wo[s].astype(BF16), zeros_bias
        elif l % N_MIXERS == 1:
            consts = (g_mix, b_wy[s].astype(BF16), b_by[s].reshape(1, d), b_wx[s].astype(BF16), b_bx[s].reshape(1, d),
                      jnp.pad(b_conv_w[s], ((0, SUBLANES - CONV_B), (0, 0))), b_conv_b[s].reshape(1, d),
                      b_wr[s].astype(BF16), b_br[s].reshape(1, d), b_wi[s].astype(BF16), b_bi[s].reshape(1, d),
                      b_lam[s].reshape(1, d))
            pre_tail = jnp.zeros((n_tail, d), BF16)
            pre_tail, h_m, tail_m = _mixb_call(x_tail, pre_tail, jnp.zeros((1, 1, d), F32),
                                               jnp.zeros((1, SUBLANES, d), F32), *consts, n_seq=1, t_len=n_meta,
                                               row_off=n_samp, name="rglru_meta")
            pre_tail, h_s, tail_s = _mixb_call(x_tail, pre_tail, state_h[s].reshape(bs, 1, d),
                                               _front_pad_rows(state_conv_b[s], SUBLANES), *consts, n_seq=bs,
                                               t_len=dec, row_off=0, name="rglru_sample")
            pre_main, h_p, tail_p = _mixb_call(x_main, None, jnp.broadcast_to(h_m, (bp, 1, d)),
                                               jnp.broadcast_to(tail_m, (bp, SUBLANES, d)), *consts, n_seq=bp,
                                               t_len=seq, row_off=0, name="rglru_prompt")
            new["h_p"].append(h_p.reshape(bp, d))
            new["cb_p"].append(tail_p[:, SUBLANES - (CONV_B - 1):])
            new["h_s"].append(h_s.reshape(bs, d))
            new["cb_s"].append(tail_s[:, SUBLANES - (CONV_B - 1):])
            wo, bo = b_wo[s].astype(BF16), b_bo[s].reshape(1, d)
        else:
            consts = (g_mix, c_win[s].astype(BF16), jnp.pad(c_conv_w[s], ((0, SUBLANES - CONV_C), (0, 0))))
            pre_tail = jnp.zeros((n_tail, d), BF16)
            pre_tail, tail_m = _mixc_call(x_tail, pre_tail, jnp.zeros((1, SUBLANES, d), F32), *consts, n_seq=1,
                                          t_len=n_meta, row_off=n_samp, name="sconv_meta")
            pre_tail, tail_s = _mixc_call(x_tail, pre_tail, _front_pad_rows(state_conv_c[s], SUBLANES), *consts,
                                          n_seq=bs, t_len=dec, row_off=0, name="sconv_sample")
            pre_main, tail_p = _mixc_call(x_main, None, jnp.broadcast_to(tail_m, (bp, SUBLANES, d)), *consts,
                                          n_seq=bp, t_len=seq, row_off=0, name="sconv_prompt")
            new["cc_p"].append(tail_p[:, SUBLANES - (CONV_C - 1):])
            new["cc_s"].append(tail_s[:, SUBLANES - (CONV_C - 1):])
            wo, bo = c_wo[s].astype(BF16), zeros_bias
        ffn = (wo, bo, norm_ffn[l].reshape(1, d), f_wg[l].astype(BF16), f_wu[l].astype(BF16), f_wd[l].astype(BF16),
               gfin, l == depth - 1)
        x_main = _ffn_call(x_main, pre_main, *ffn)
        x_tail = _ffn_call(x_tail, pre_tail, *ffn)

    y_prompt = x_main.reshape(bp, seq, d)
    y_sample = x_tail[:n_samp].reshape(bs, dec, d)
    st = lambda name: jnp.stack(new[name])
    return (y_prompt, y_sample, st("k_p"), st("v_p"), st("ki_p"), st("h_p"), st("cb_p"), st("cc_p"),
            st("k_s"), st("v_s"), st("ki_s"), st("h_s"), st("cb_s"), st("cc_s"))
```

```python
import functools

import jax
import jax.numpy as jnp
from jax import lax
from jax.experimental import pallas as pl
from jax.experimental.pallas import tpu as pltpu

F32 = jnp.float32
BF16 = jnp.bfloat16
I32 = jnp.int32
I16 = jnp.int16

CHUNK = 64
N_MIXERS = 3
N_HEADS = 16
N_KV_HEADS = 4
GROUP = N_HEADS // N_KV_HEADS
HEAD_DIM = 64
IDX_HEADS = 8
IDX_DIM = 64
TOP_K_MAX = 256
NEG_INF = -1e30
N_BUCKETS = 32
LRU_BLOCKS = 4
CONV_B = 4
CONV_C = 3
LRU_C = 8.0
RMS_EPS = 1e-6
BUCKET_STEPS = (12, 16, 23, 32, 46, 64, 91)
FAR_BUCKET = N_BUCKETS // 2 - 1

LANES = 128
SUBLANES = 8
TM_PROJ = 256
TM_FFN = 512
TM_SEQ = 256
TM_CONV = 512
QB = 128
SEL_TILE = 2 * QB
INT_MIN = -2 ** 31
I16_MIN = -2 ** 15
PACK16 = 2 * SUBLANES
V_ROWS = HEAD_DIM + PACK16
LOG2E = 1.4426950408889634
VMEM_LIMIT = 56 * 1024 * 1024


def _cparams(sem):
    return pltpu.CompilerParams(dimension_semantics=sem, vmem_limit_bytes=VMEM_LIMIT)


def _const_spec(shape):
    nd = len(shape)
    return pl.BlockSpec(shape, lambda *_: (0,) * nd, pipeline_mode=pl.Buffered(1))


def _rmsnorm(x, g):
    ms = jnp.mean(x * x, axis=-1, keepdims=True)
    return x * lax.rsqrt(ms + RMS_EPS) * g


def _dot(a, b):
    return jnp.dot(a, b, preferred_element_type=F32)


def _dot_nt(a, b):
    return lax.dot_general(a, b, (((1,), (1,)), ((), ())), preferred_element_type=F32)


def _ffn_kernel(x_ref, pre_ref, wo_ref, bo_ref, gf_ref, wg_ref, wu_ref, wd_ref, gfin_ref, out_ref, *,
                n_chunks, fc, final_norm):
    x1 = x_ref[...] + _dot(pre_ref[...], wo_ref[...]) + bo_ref[...]
    xn = _rmsnorm(x1, gf_ref[...]).astype(BF16)
    acc = x1
    for c in range(n_chunks):
        gt = _dot(xn, wg_ref[:, c * fc:(c + 1) * fc])
        up = _dot(xn, wu_ref[:, c * fc:(c + 1) * fc])
        hm = (gt * jax.nn.sigmoid(gt) * up).astype(BF16)
        acc = acc + _dot(hm, wd_ref[c * fc:(c + 1) * fc, :])
    if final_norm:
        acc = _rmsnorm(acc, gfin_ref[...])
    out_ref[...] = acc


def _ffn_call(x, pre, wo, bo, gf, wg, wu, wd, gfin, final_norm):
    n, d = x.shape
    dff = wg.shape[1]
    n_chunks = 2 if dff % (2 * LANES) == 0 else 1
    fc = dff // n_chunks
    row = lambda i: (i, 0)
    tm = TM_FFN if n % TM_FFN == 0 else TM_PROJ
    return pl.pallas_call(
        functools.partial(_ffn_kernel, n_chunks=n_chunks, fc=fc, final_norm=final_norm),
        grid=(n // tm,),
        in_specs=[pl.BlockSpec((tm, d), row), pl.BlockSpec((tm, pre.shape[1]), row),
                  _const_spec(wo.shape), _const_spec(bo.shape), _const_spec(gf.shape),
                  _const_spec(wg.shape), _const_spec(wu.shape), _const_spec(wd.shape), _const_spec(gfin.shape)],
        out_specs=pl.BlockSpec((tm, d), row),
        out_shape=jax.ShapeDtypeStruct((n, d), F32),
        compiler_params=_cparams(("parallel",)),
        name="outproj_swiglu",
    )(x, pre, wo, bo, gf, wg, wu, wd, gfin)


def _proj_kernel(x_ref, g_ref, wrow_ref, wt_ref, k_ref, v_ref, ki_ref, kb_ref, kib_ref, qt_ref, qit_ref, vt_ref,
                 wit_ref, *, n_kv, n_q, n_iq):
    xn = _rmsnorm(x_ref[...], g_ref[...]).astype(BF16)
    row = _dot(xn, wrow_ref[...])
    k_ref[...] = row[:, :n_kv]
    v_ref[...] = row[:, n_kv:2 * n_kv]
    ki_ref[...] = row[:, 2 * n_kv:]
    kb_ref[...] = row[:, :n_kv].astype(BF16)
    kib_ref[...] = row[:, 2 * n_kv:].astype(BF16)
    tt = _dot_nt(wt_ref[...], xn)
    for b in range(x_ref.shape[0] // QB):
        blk = tt[:, b * QB:(b + 1) * QB]
        qt_ref[b] = blk[:n_q].astype(BF16)
        qit_ref[b] = blk[n_q:n_q + n_iq].astype(BF16)
        vt_ref[b] = blk[n_q + n_iq:n_q + n_iq + n_kv].astype(BF16)
        wit_ref[b] = blk[n_q + n_iq + n_kv:n_q + n_iq + n_kv + IDX_HEADS] * (IDX_HEADS ** -0.5)


def _proj_call(x, g, wrow, wt):
    n, d = x.shape
    n_kv = N_KV_HEADS * HEAD_DIM
    n_q = N_HEADS * HEAD_DIM
    n_iq = IDX_HEADS * IDX_DIM
    row = lambda i: (i, 0)
    tm = TM_FFN if n % TM_FFN == 0 else TM_PROJ
    feat_spec = lambda f: pl.BlockSpec((tm // QB, f, QB), lambda i: (i, 0, 0))
    feat_shape = lambda f, dt: jax.ShapeDtypeStruct((n // QB, f, QB), dt)
    return pl.pallas_call(
        functools.partial(_proj_kernel, n_kv=n_kv, n_q=n_q, n_iq=n_iq),
        grid=(n // tm,),
        in_specs=[pl.BlockSpec((tm, d), row), _const_spec(g.shape), _const_spec(wrow.shape),
                  _const_spec(wt.shape)],
        out_specs=[pl.BlockSpec((tm, n_kv), row), pl.BlockSpec((tm, n_kv), row),
                   pl.BlockSpec((tm, IDX_DIM), row), pl.BlockSpec((tm, n_kv), row),
                   pl.BlockSpec((tm, IDX_DIM), row), feat_spec(n_q), feat_spec(n_iq), feat_spec(n_kv),
                   feat_spec(IDX_HEADS)],
        out_shape=[jax.ShapeDtypeStruct((n, n_kv), F32), jax.ShapeDtypeStruct((n, n_kv), F32),
                   jax.ShapeDtypeStruct((n, IDX_DIM), F32), jax.ShapeDtypeStruct((n, n_kv), BF16),
                   jax.ShapeDtypeStruct((n, IDX_DIM), BF16), feat_shape(n_q, BF16), feat_shape(n_iq, BF16),
                   feat_shape(n_kv, BF16), feat_shape(IDX_HEADS, F32)],
        compiler_params=_cparams(("parallel",)),
        name="attn_proj",
    )(x, g, wrow, wt)


def _bias_kernel(tab_ref, tb_ref):
    h = pl.program_id(0)
    d = pl.program_id(1)
    kj = lax.broadcasted_iota(I32, (QB, QB), 0)
    qi = lax.broadcasted_iota(I32, (QB, QB), 1)
    rel = (d - 1) * QB + kj - qi
    n = jnp.abs(rel)
    large = jnp.full((QB, QB), N_BUCKETS // 4, I32)
    for s in BUCKET_STEPS:
        large = large + jnp.where(n >= s, 1, 0)
    bucket = jnp.where(rel > 0, N_BUCKETS // 2, 0) + jnp.where(n < N_BUCKETS // 4, n, large)
    val = jnp.zeros((QB, QB), F32)
    for b in range(N_BUCKETS):
        val = jnp.where(bucket == b, tab_ref[b, h], val)
    tb_ref[0] = val


def _bias_call(table):
    return pl.pallas_call(
        _bias_kernel,
        grid=(N_HEADS, 2),
        in_specs=[pl.BlockSpec(memory_space=pltpu.SMEM)],
        out_specs=pl.BlockSpec((1, QB, QB), lambda h, d: (h, d, 0)),
        out_shape=jax.ShapeDtypeStruct((N_HEADS, 2 * QB, QB), F32),
        compiler_params=_cparams(("arbitrary", "arbitrary")),
        name="rel_bias_tiles",
    )(table)


def _attn_kernel(kend_ref, qt_ref, qit_ref, wit_ref, *refs, diag_off, kstart, k_sel, has_prev, row_bits, build):
    if build:
        (kb_ref, vt_ref, kib_ref, k0_ref, v0_ref, ki0_ref, tb_ref, tab_ref, o_ref, key_scr, hi_scr, lo_scr, lo2_scr,
         ls_scr, oacc_scr, kf_scr, vtf_scr, kif_scr) = refs
        seq = kb_ref.shape[0]
        tkf = kf_scr.shape[1]

        @pl.when(pl.program_id(1) == 0)
        def _():
            kif_scr[0:QB, :] = ki0_ref[...]
            kif_scr[QB:QB + seq, :] = kib_ref[...]
            kif_scr[QB + seq:tkf, :] = jnp.zeros((tkf - QB - seq, IDX_DIM), BF16)
            for g in range(N_KV_HEADS):
                cols = slice(g * HEAD_DIM, (g + 1) * HEAD_DIM)
                kf_scr[g, 0:QB, :] = k0_ref[:, cols]
                for c in range(seq // SEL_TILE):
                    kf_scr[g, QB + c * SEL_TILE:QB + (c + 1) * SEL_TILE, :] = kb_ref[c * SEL_TILE:(c + 1) * SEL_TILE,
                                                                                     cols]
                kf_scr[g, QB + seq:tkf, :] = jnp.zeros((tkf - QB - seq, HEAD_DIM), BF16)
                vrows = slice(g * V_ROWS, g * V_ROWS + HEAD_DIM)
                vtf_scr[vrows, 0:QB] = v0_ref[cols, :]
                for j in range(seq // QB):
                    vtf_scr[vrows, QB + j * QB:QB + (j + 1) * QB] = vt_ref[j, cols, :]
                vtf_scr[vrows, QB + seq:tkf] = jnp.zeros((HEAD_DIM, tkf - QB - seq), BF16)
                extra = lax.broadcasted_iota(I32, (V_ROWS - HEAD_DIM, tkf), 0) == 0
                vtf_scr[g * V_ROWS + HEAD_DIM:(g + 1) * V_ROWS, :] = jnp.where(extra, 1.0, 0.0).astype(BF16)

        k_rows = lambda g, r0, rows: kf_scr[g, pl.ds(r0, rows), :]
        v_cols = lambda g, r0, rows: vtf_scr[g * V_ROWS:(g + 1) * V_ROWS, pl.ds(r0, rows)]
        ki_rows = lambda r0, rows: kif_scr[pl.ds(r0, rows), :]
    else:
        (kf_ref, vtf_ref, kif_ref, tb_ref, tab_ref, o_ref, key_scr, hi_scr, lo_scr, lo2_scr, ls_scr,
         oacc_scr) = refs
        k_rows = lambda g, r0, rows: kf_ref[0, g, pl.ds(r0, rows), :]
        v_cols = lambda g, r0, rows: vtf_ref[0, g * V_ROWS:(g + 1) * V_ROWS, pl.ds(r0, rows)]
        ki_rows = lambda r0, rows: kif_ref[0, pl.ds(r0, rows), :]
    diag = pl.program_id(1) + diag_off
    nt = diag + 1
    nst = (nt + SEL_TILE // QB - 1) // (SEL_TILE // QB)
    kend = kend_ref[0]
    qi_cat = jnp.concatenate([qit_ref[0, h * IDX_DIM:(h + 1) * IDX_DIM, :] for h in range(IDX_HEADS)], axis=1)
    wi = wit_ref[0]
    sel_iota = lax.broadcasted_iota(I32, (SEL_TILE, QB), 0)

    def tile_start(t):
        return pl.multiple_of(t * QB, QB)

    def sel_start(t):
        return pl.multiple_of(t * SEL_TILE, SEL_TILE)

    def score_rows(r0):
        s = _dot(ki_rows(r0, SEL_TILE), qi_cat)
        sc = jnp.zeros((SEL_TILE, QB), F32)
        for h in range(IDX_HEADS):
            sc = sc + wi[h:h + 1, :] * jnp.maximum(s[:, h * QB:(h + 1) * QB], 0.0)
        bits = lax.bitcast_convert_type(sc + 0.0, I32)
        key = jnp.where(bits >= 0, bits, bits ^ 0x7FFFFFFF)
        rows = r0 + sel_iota
        adm = (rows >= kstart) & (rows < kend)
        key = jnp.where(adm, key, INT_MIN)
        key_scr[pl.ds(r0, SEL_TILE), :] = key
        hi_scr[pl.ds(r0, SEL_TILE), :] = (key >> 16).astype(I16)
        lo_scr[pl.ds(r0, SEL_TILE), :] = ((key & 0xFFFF) + I16_MIN).astype(I16)

    pair_rows = 2 * SEL_TILE

    def score_pair(t, carry):
        r0 = pl.multiple_of(t * pair_rows, pair_rows)
        score_rows(r0)
        score_rows(r0 + SEL_TILE)
        return carry

    lax.fori_loop(0, (nst + 1) // 2, score_pair, 0)

    pair_vregs = pair_rows // PACK16

    def as_packed(v):
        return jnp.broadcast_to(v, (PACK16, QB)).astype(I16)

    def threshold(n_pair):
        def count16(src_scr, cand):
            c16 = as_packed(cand)[None]
            acc = jnp.zeros((PACK16, QB), BF16)
            for t in range(n_pair):
                v = src_scr[t * pair_rows:(t + 1) * pair_rows, :].reshape(pair_vregs, PACK16, QB)
                ind = jnp.where(v >= c16, jnp.ones((), BF16), jnp.zeros((), BF16))
                parts = [ind[i] for i in range(pair_vregs)]
                while len(parts) > 1:
                    parts = [parts[i] + parts[i + 1] for i in range(0, len(parts), 2)]
                acc = acc + parts[0]
            return acc.astype(F32).sum(axis=0, keepdims=True)

        def search16(src_scr, base, c_start):
            def step(b, carry):
                t_acc, c_acc = carry
                cand = t_acc + lax.shift_left(jnp.int32(1), 15 - b)
                c = base + count16(src_scr, cand)
                ok = c >= k_sel
                return jnp.where(ok, cand, t_acc), jnp.where(ok, c, c_acc)

            return lax.fori_loop(0, 16, step, (jnp.full((1, QB), I16_MIN, I32), c_start))

        thr_hi, c_hi = search16(hi_scr, 0.0, jnp.full((1, QB), n_pair * pair_rows, F32))
        above = jnp.where(thr_hi == -I16_MIN - 1, 0.0, count16(hi_scr, thr_hi + 1))
        hi16 = as_packed(thr_hi)[None]
        for t in range(n_pair):
            rows = slice(t * pair_rows, (t + 1) * pair_rows)
            hi = hi_scr[rows, :].reshape(pair_vregs, PACK16, QB)
            lo = lo_scr[rows, :].reshape(pair_vregs, PACK16, QB)
            lo2_scr[rows, :] = jnp.where(hi == hi16, lo, jnp.full((), I16_MIN, I16)).reshape(pair_rows, QB)
        thr_lo, c_ge = search16(lo2_scr, above, c_hi)
        return thr_hi * 65536 + (thr_lo - I16_MIN), c_ge

    def count(pred):
        def body(t, acc):
            r0 = sel_start(t)
            ind = jnp.where(pred(key_scr[pl.ds(r0, SEL_TILE), :], r0 + sel_iota), 1, 0)
            return acc + ind.reshape(SEL_TILE // SUBLANES, SUBLANES, QB).sum(axis=0)
        acc = lax.fori_loop(0, nst, body, jnp.zeros((SUBLANES, QB), I32))
        return acc.sum(axis=0, keepdims=True)

    max_pairs = hi_scr.shape[0] // pair_rows
    thr, c_ge = lax.switch((nst + 1) // 2 - 1, [functools.partial(threshold, n) for n in range(1, max_pairs + 1)])

    tied = (c_ge > k_sel) & (thr > INT_MIN)
    big = jnp.full((1, QB), 2 ** row_bits, I32)

    def tie_limit():
        need = k_sel - count(lambda kt, rows: kt > thr)

        def lim_step(b, lim):
            cand = lim + lax.shift_left(jnp.int32(1), row_bits - 1 - b)
            c = count(lambda kt, rows: (kt == thr) & (rows < cand))
            return jnp.where(c < need, cand, lim)

        lim = lax.fori_loop(0, row_bits, lim_step, jnp.zeros((1, QB), I32))
        return jnp.where(tied, lim, big)

    rlim = lax.cond(jnp.max(tied.astype(I32)) > 0, tie_limit, lambda: big)
    rlim = jnp.where(thr > INT_MIN, rlim, -1)

    gq = GROUP * QB
    qg = [jnp.concatenate([qt_ref[0, (g * GROUP + r) * HEAD_DIM:(g * GROUP + r + 1) * HEAD_DIM, :]
                           for r in range(GROUP)], axis=1) for g in range(N_KV_HEADS)]

    def logits_rows(r0, rows, m8, bias_of_head):
        kt = key_scr[pl.ds(r0, rows), :]
        keep = (kt > thr) | ((kt == thr) & (sel_iota[:rows] <= rlim - r0))
        new_m8 = []
        for g in range(N_KV_HEADS):
            lg = _dot(k_rows(g, r0, rows), qg[g])
            parts = []
            for r in range(GROUP):
                part = (lg[:, r * QB:(r + 1) * QB] + bias_of_head(g * GROUP + r)) * LOG2E
                parts.append(jnp.where(keep, part, NEG_INF))
            lg = jnp.concatenate(parts, axis=1)
            ls_scr[pl.ds(r0, rows), g * gq:(g + 1) * gq] = lg
            new_m8.append(jnp.maximum(m8[g], lg.reshape(rows // SUBLANES, SUBLANES, gq).max(axis=0)))
        return tuple(new_m8)

    far_bias = lambda h: tab_ref[FAR_BUCKET, h]
    m8 = tuple(jnp.full((SUBLANES, gq), NEG_INF, F32) for _ in range(N_KV_HEADS))
    n_far = diag - 1 if has_prev else diag
    def far_pair(t, m8):
        r0 = pl.multiple_of(t * pair_rows, pair_rows)
        return logits_rows(r0 + SEL_TILE, SEL_TILE, logits_rows(r0, SEL_TILE, m8, far_bias), far_bias)

    m8 = lax.fori_loop(0, n_far // 4, far_pair, m8)
    m8 = lax.cond(n_far % 4 >= 2, lambda m: logits_rows(sel_start(n_far // 4 * 2), SEL_TILE, m, far_bias),
                  lambda m: m, m8)
    m8 = lax.cond(n_far % 2 == 1, lambda m: logits_rows(tile_start(n_far - 1), QB, m, far_bias), lambda m: m, m8)
    if has_prev:
        m8 = logits_rows(tile_start(diag - 1), 2 * QB, m8, lambda h: tb_ref[h])
    else:
        m8 = logits_rows(tile_start(diag), QB, m8, lambda h: tb_ref[h, QB:, :])
    m = [jnp.max(m8[g], axis=0, keepdims=True) for g in range(N_KV_HEADS)]

    @pl.when(nt % 2 == 1)
    def _():
        ls_scr[pl.ds(tile_start(nt), QB), :] = jnp.full((QB, N_HEADS * QB), NEG_INF, F32)

    oacc_scr[...] = jnp.zeros(oacc_scr.shape, F32)

    def pv_rows(r0):
        for g in range(N_KV_HEADS):
            p = jnp.exp2(ls_scr[pl.ds(r0, SEL_TILE), g * gq:(g + 1) * gq] - m[g])
            oacc_scr[g] += _dot(v_cols(g, r0, SEL_TILE), p.astype(BF16))

    def pv_pair(t, carry):
        r0 = pl.multiple_of(t * pair_rows, pair_rows)
        pv_rows(r0)
        pv_rows(r0 + SEL_TILE)
        return carry

    lax.fori_loop(0, nst // 2, pv_pair, 0)

    @pl.when(nst % 2 == 1)
    def _():
        pv_rows(sel_start(nst - 1))
    out_rows = []
    for g in range(N_KV_HEADS):
        o_g = oacc_scr[g, :HEAD_DIM] / oacc_scr[g, HEAD_DIM:HEAD_DIM + 1]
        out_rows.extend(o_g[:, r * QB:(r + 1) * QB] for r in range(GROUP))
    o_ref[...] = jnp.concatenate(out_rows, axis=0).T.astype(BF16)


def _attn_call(kend, qt, qit, wit, frames, tb, table, *, n_seq, build, n_out_rows, diag_off, kstart, k_sel, name):
    n_qb = kend.shape[0]
    kv_dim = N_KV_HEADS * HEAD_DIM
    if build:
        seq = frames[0].shape[0] // n_seq
        tkf = -(-(QB + seq) // (2 * SEL_TILE)) * 2 * SEL_TILE
        frame_specs = [pl.BlockSpec((seq, kv_dim), lambda s, i: (s, 0)),
                       pl.BlockSpec((seq // QB, kv_dim, QB), lambda s, i: (s, 0, 0)),
                       pl.BlockSpec((seq, IDX_DIM), lambda s, i: (s, 0))] + [_const_spec(a.shape) for a in frames[3:]]
        frame_scratch = [pltpu.VMEM((N_KV_HEADS, tkf, HEAD_DIM), BF16), pltpu.VMEM((N_KV_HEADS * V_ROWS, tkf), BF16),
                         pltpu.VMEM((tkf, IDX_DIM), BF16)]
    else:
        kf, vtf, kif = frames
        tkf = kf.shape[2]
        assert vtf.shape[2] == tkf and kif.shape[1] == tkf
        frame_specs = [pl.BlockSpec((1, N_KV_HEADS, tkf, HEAD_DIM), lambda s, i: (s, 0, 0, 0)),
                       pl.BlockSpec((1, N_KV_HEADS * V_ROWS, tkf), lambda s, i: (s, 0, 0)),
                       pl.BlockSpec((1, tkf, IDX_DIM), lambda s, i: (s, 0, 0))]
        frame_scratch = []
    assert tkf % (2 * SEL_TILE) == 0
    assert tkf // PACK16 <= 256, "per-slot bf16 counts must stay exact"
    has_prev = diag_off >= 1
    row_bits = max(1, (tkf - 1).bit_length())
    feat = lambda a: pl.BlockSpec((1, a.shape[1], QB), lambda s, i: (s * n_qb + i, 0, 0))
    return pl.pallas_call(
        functools.partial(_attn_kernel, diag_off=diag_off, kstart=kstart, k_sel=k_sel, has_prev=has_prev,
                          row_bits=row_bits, build=build),
        grid=(n_seq, n_qb),
        in_specs=[pl.BlockSpec((1, 1, QB), lambda s, i: (i, 0, 0)), feat(qt), feat(qit), feat(wit)] + frame_specs
                 + [_const_spec(tb.shape), pl.BlockSpec(memory_space=pltpu.SMEM)],
        out_specs=pl.BlockSpec((QB, N_HEADS * HEAD_DIM), lambda s, i: (s * n_qb + i, 0)),
        out_shape=jax.ShapeDtypeStruct((n_out_rows, N_HEADS * HEAD_DIM), BF16),
        scratch_shapes=[pltpu.VMEM((tkf, QB), I32)] + [pltpu.VMEM((tkf, QB), I16)] * 3
                       + [pltpu.VMEM((tkf, N_HEADS * QB), F32), pltpu.VMEM((N_KV_HEADS, V_ROWS, GROUP * QB), F32)]
                       + frame_scratch,
        compiler_params=_cparams(("arbitrary", "arbitrary")),
        name=name,
    )(kend, qt, qit, wit, *frames, tb, table)


def _softplus(x):
    return jnp.maximum(x, 0.0) + jnp.log1p(jnp.exp(-jnp.abs(x)))


def _mixb_kernel(x_ref, h0_ref, buf_ref, g_ref, wy_ref, by_ref, wx_ref, bx_ref, cw_ref, cb_ref, wr_ref, br_ref,
                 wi_ref, bi_ref, lam_ref, *rest, tm):
    pre_ref, hlast_ref, tail_ref, xe_scr, h_scr = rest[-5:]

    @pl.when(pl.program_id(1) == 0)
    def _():
        xe_scr[0:SUBLANES, :] = buf_ref[0]
        h_scr[...] = h0_ref[0]

    xn = _rmsnorm(x_ref[...], g_ref[...]).astype(BF16)
    y = jax.nn.gelu(_dot(xn, wy_ref[...]) + by_ref[...])
    xe_scr[SUBLANES:SUBLANES + tm, :] = _dot(xn, wx_ref[...]) + bx_ref[...]
    xc = cb_ref[...] + cw_ref[0:1, :] * xe_scr[SUBLANES - 3:SUBLANES - 3 + tm, :]
    for j in range(1, CONV_B):
        xc = xc + cw_ref[j:j + 1, :] * xe_scr[SUBLANES - 3 + j:SUBLANES - 3 + j + tm, :]
    xcb = xc.astype(BF16)
    d_rnn = xc.shape[1]
    blk = d_rnn // LRU_BLOCKS
    r_pre = jnp.concatenate([_dot(xcb[:, n * blk:(n + 1) * blk], wr_ref[n]) for n in range(LRU_BLOCKS)], axis=1)
    i_pre = jnp.concatenate([_dot(xcb[:, n * blk:(n + 1) * blk], wi_ref[n]) for n in range(LRU_BLOCKS)], axis=1)
    r = jax.nn.sigmoid(r_pre + br_ref[...])
    ig = jax.nn.sigmoid(i_pre + bi_ref[...])
    log_a = -LRU_C * r * _softplus(-lam_ref[...])
    a = jnp.exp(log_a)
    u = jnp.sqrt(jnp.tanh(-log_a) * (1.0 + a * a)) * (ig * xc)
    a = a.reshape(tm // SUBLANES, SUBLANES, d_rnn)
    u = u.reshape(tm // SUBLANES, SUBLANES, d_rnn)
    row_in_group = lax.broadcasted_iota(I32, a.shape, 1)
    s = 1
    while s < SUBLANES:
        a_sh = jnp.where(row_in_group >= s, pltpu.roll(a, s, 1), 1.0)
        u_sh = jnp.where(row_in_group >= s, pltpu.roll(u, s, 1), 0.0)
        u = a * u_sh + u
        a = a * a_sh
        s *= 2
    carry = h_scr[...]
    h_groups = []
    for grp in range(tm // SUBLANES):
        h_grp = a[grp] * carry + u[grp]
        carry = h_grp[SUBLANES - 1:SUBLANES, :]
        h_groups.append(h_grp)
    h = jnp.concatenate(h_groups, axis=0)
    pre_ref[...] = (h * y).astype(BF16)
    h_scr[...] = carry
    hlast_ref[0] = carry
    tail = xe_scr[tm:tm + SUBLANES, :]
    xe_scr[0:SUBLANES, :] = tail
    tail_ref[0] = tail


def _mixc_kernel(x_ref, buf_ref, g_ref, win_ref, cw_ref, *rest, tm):
    pre_ref, tail_ref, pe_scr = rest[-3:]

    @pl.when(pl.program_id(1) == 0)
    def _():
        pe_scr[0:SUBLANES, :] = buf_ref[0]

    d = x_ref.shape[1]
    xn = _rmsnorm(x_ref[...], g_ref[...]).astype(BF16)
    z = _dot(xn, win_ref[...])
    pe_scr[SUBLANES:SUBLANES + tm, :] = z[:, d:2 * d] * z[:, 2 * d:]
    conv = cw_ref[0:1, :] * pe_scr[SUBLANES - 2:SUBLANES - 2 + tm, :]
    for j in range(1, CONV_C):
        conv = conv + cw_ref[j:j + 1, :] * pe_scr[SUBLANES - 2 + j:SUBLANES - 2 + j + tm, :]
    pre_ref[...] = (z[:, :d] * conv).astype(BF16)
    tail = pe_scr[tm:tm + SUBLANES, :]
    pe_scr[0:SUBLANES, :] = tail
    tail_ref[0] = tail


def _shared_out(buf, index):
    if buf is None:
        return {"specs": [], "args": (), "aliases": {}}
    return {"specs": [pl.BlockSpec(memory_space=pl.ANY)], "args": (buf,), "aliases": {index: 0}}


def _seq_specs(n_seq, t_len, row_off, d, tm_max):
    tm = tm_max if t_len % tm_max == 0 else min(TM_SEQ, t_len)
    n_tt = t_len // tm
    off = row_off // tm
    xrow = pl.BlockSpec((tm, d), lambda s, j: (off + s * n_tt + j, 0))
    state = lambda rows: pl.BlockSpec((1, rows, d), lambda s, j: (s, 0, 0))
    return tm, n_tt, xrow, state


def _mixb_call(x, pre_buf, h0, buf, g, wy, by, wx, bx, cw, cb, wr, br, wi, bi, lam, *, n_seq, t_len, row_off,
               name):
    d = x.shape[1]
    tm, n_tt, xrow, state = _seq_specs(n_seq, t_len, row_off, d, TM_SEQ)
    consts = (g, wy, by, wx, bx, cw, cb, wr, br, wi, bi, lam)
    shared = _shared_out(pre_buf, 3 + len(consts))
    return pl.pallas_call(
        functools.partial(_mixb_kernel, tm=tm),
        grid=(n_seq, n_tt),
        in_specs=[xrow, state(1), state(SUBLANES)] + [_const_spec(c.shape) for c in consts] + shared["specs"],
        out_specs=[xrow, state(1), state(SUBLANES)],
        out_shape=[jax.ShapeDtypeStruct((x.shape[0], d), BF16), jax.ShapeDtypeStruct((n_seq, 1, d), F32),
                   jax.ShapeDtypeStruct((n_seq, SUBLANES, d), F32)],
        scratch_shapes=[pltpu.VMEM((tm + SUBLANES, d), F32), pltpu.VMEM((1, d), F32)],
        input_output_aliases=shared["aliases"],
        compiler_params=_cparams(("arbitrary", "arbitrary")),
        name=name,
    )(x, h0, buf, *consts, *shared["args"])


def _mixc_call(x, pre_buf, buf, g, win, cw, *, n_seq, t_len, row_off, name):
    d = x.shape[1]
    tm, n_tt, xrow, state = _seq_specs(n_seq, t_len, row_off, d, TM_CONV)
    consts = (g, win, cw)
    shared = _shared_out(pre_buf, 2 + len(consts))
    return pl.pallas_call(
        functools.partial(_mixc_kernel, tm=tm),
        grid=(n_seq, n_tt),
        in_specs=[xrow, state(SUBLANES)] + [_const_spec(c.shape) for c in consts] + shared["specs"],
        out_specs=[xrow, state(SUBLANES)],
        out_shape=[jax.ShapeDtypeStruct((x.shape[0], d), BF16), jax.ShapeDtypeStruct((n_seq, SUBLANES, d), F32)],
        scratch_shapes=[pltpu.VMEM((tm + SUBLANES, d), F32)],
        input_output_aliases=shared["aliases"],
        compiler_params=_cparams(("arbitrary", "arbitrary")),
        name=name,
    )(x, buf, *consts, *shared["args"])


def _front_pad_rows(a, rows):
    return jnp.pad(a, ((0, 0), (rows - a.shape[1], 0), (0, 0)))


def _frame(parts, axis):
    f = jnp.concatenate(parts, axis=axis).astype(BF16)
    pad = [(0, 0)] * f.ndim
    pad[axis] = (0, -f.shape[axis] % (2 * SEL_TILE))
    return jnp.pad(f, pad)


def _k_frame(parts):
    f = _frame(parts, 1)
    n_seq, rows = f.shape[0], f.shape[1]
    return f.reshape(n_seq, rows, N_KV_HEADS, HEAD_DIM).transpose(0, 2, 1, 3)


def _v_frame(parts):
    f = _frame(parts, 2)
    n_seq, tkf = f.shape[0], f.shape[2]
    extra = jnp.zeros((n_seq, N_KV_HEADS, V_ROWS - HEAD_DIM, tkf), BF16).at[:, :, 0].set(1)
    f = jnp.concatenate([f.reshape(n_seq, N_KV_HEADS, HEAD_DIM, tkf), extra], axis=2)
    return f.reshape(n_seq, N_KV_HEADS * V_ROWS, tkf)


def _cols(a, start, n):
    b0, b1 = start // QB, -(-(start + n) // QB)
    c = a[b0:b1].transpose(1, 0, 2).reshape(a.shape[1], (b1 - b0) * QB)
    return c[:, start - b0 * QB:start - b0 * QB + n]


def _seq_major(a, n_seq, q_len):
    return a.reshape(a.shape[0], n_seq, q_len).transpose(1, 0, 2)


def _lane_pad_cols(a, n_seq, q_len):
    return jnp.pad(_seq_major(a, n_seq, q_len), ((0, 0), (0, 0), (0, QB - q_len)))


def kernel(x_prompt, x_sample, cache_k, cache_v, cache_kidx, state_h, state_conv_b, state_conv_c, meta_tokens,
           rel_bias, norm_mix, norm_ffn, norm_final, a_wq, a_wk, a_wv, a_wo, a_wiq, a_wik, a_wiw, b_wy, b_by, b_wx,
           b_bx, b_conv_w, b_conv_b, b_wr, b_br, b_wi, b_bi, b_lam, b_wo, b_bo, c_win, c_conv_w, c_wo, f_wg, f_wu,
           f_wd):
    bp, seq, d = x_prompt.shape
    bs, dec = x_sample.shape[0], x_sample.shape[1]
    past = cache_k.shape[2]
    n_meta = meta_tokens.shape[0]
    depth = norm_mix.shape[0]
    assert seq % QB == 0 and seq % TM_SEQ == 0 and past % QB == 0 and dec <= QB and n_meta <= QB
    assert dec % PACK16 == 0 and n_meta % PACK16 == 0 and (bs * dec) % n_meta == 0

    n_main, n_samp = bp * seq, bs * dec
    n_tail = -(-(n_samp + n_meta) // TM_PROJ) * TM_PROJ
    x_main = x_prompt.reshape(n_main, d)
    x_tail = jnp.concatenate([x_sample.reshape(n_samp, d), meta_tokens.astype(F32),
                              jnp.zeros((n_tail - n_samp - n_meta, d), F32)], axis=0)
    tail_fill = jnp.zeros((n_tail - n_samp - n_meta, d), BF16)

    k_sel_p = min(TOP_K_MAX, seq // 4)
    k_sel_s = min(TOP_K_MAX, (past + dec) // 4)
    n_qb = seq // QB
    kv_dim = N_KV_HEADS * HEAD_DIM

    tau = jnp.arange(seq, dtype=I32)
    kend_main = (QB + CHUNK * (tau // CHUNK + 1)).reshape(n_qb, 1, QB)
    lane = jnp.arange(QB, dtype=I32)
    kend_samp = jnp.minimum(past + dec, CHUNK * ((past + jnp.minimum(lane, dec - 1)) // CHUNK + 1)).reshape(1, 1, QB)
    kend_meta = jnp.full((1, 1, QB), n_meta, I32)

    tb = _bias_call(rel_bias.astype(F32))
    zeros_bias = jnp.zeros((1, d), F32)
    gfin = norm_final.reshape(1, d).astype(F32)

    new = {name: [] for name in ("k_p", "v_p", "ki_p", "h_p", "cb_p", "cc_p", "k_s", "v_s", "ki_s", "h_s", "cb_s",
                                 "cc_s")}
    for l in range(depth):
        s = l // N_MIXERS
        g_mix = norm_mix[l].reshape(1, d)
        if l % N_MIXERS == 0:
            wrow = jnp.concatenate([a_wk[s], a_wv[s], a_wik[s]], axis=1).astype(BF16)
            wt = jnp.concatenate([a_wq[s].T * HEAD_DIM ** -0.5, a_wiq[s].T * IDX_DIM ** -0.5, a_wv[s].T, a_wiw[s].T,
                                  jnp.zeros((SUBLANES, d), F32)], axis=0).astype(BF16)
            k, v, ki, kb, kib, qt, qit, vt, wit = _proj_call(x_main, g_mix, wrow, wt)
            k_t, v_t, ki_t, _, _, qt_t, qit_t, vt_t, wit_t = _proj_call(x_tail, g_mix, wrow, wt)

            k_meta, v_meta, ki_meta = (a[n_samp:n_samp + n_meta] for a in (k_t, v_t, ki_t))
            k_main, v_main, ki_main = (a.reshape(bp, seq, -1) for a in (k, v, ki))
            k_samp, v_samp, ki_samp = (a[:n_samp].reshape(bs, dec, -1) for a in (k_t, v_t, ki_t))
            bc = lambda a: jnp.broadcast_to(a[None], (bp,) + a.shape)
            new["k_p"].append(jnp.concatenate([bc(k_meta), k_main], axis=1).reshape(bp, n_meta + seq, N_KV_HEADS,
                                                                                   HEAD_DIM))
            new["v_p"].append(jnp.concatenate([bc(v_meta), v_main], axis=1).reshape(bp, n_meta + seq, N_KV_HEADS,
                                                                                   HEAD_DIM))
            new["ki_p"].append(jnp.concatenate([bc(ki_meta), ki_main], axis=1))
            new["k_s"].append(k_samp.reshape(bs, dec, N_KV_HEADS, HEAD_DIM))
            new["v_s"].append(v_samp.reshape(bs, dec, N_KV_HEADS, HEAD_DIM))
            new["ki_s"].append(ki_samp)

            samp_cols = lambda a: _cols(a, 0, n_samp)
            meta_cols = lambda a: _cols(a, n_samp, n_meta)
            vt_meta = meta_cols(vt_t)
            lead = lambda a, axis: jnp.pad(a, [(QB - n_meta, 0) if ax == axis else (0, 0) for ax in range(2)]).astype(BF16)
            pre_main = _attn_call(
                kend_main, qt, qit, wit, (kb, vt, kib, lead(k_meta, 0), lead(vt_meta, 1), lead(ki_meta, 0)), tb,
                rel_bias, n_seq=bp, build=True,
                n_out_rows=n_main, diag_off=1, kstart=QB - n_meta, k_sel=k_sel_p, name="dsa_attn_prompt")
            vtf_samp = _v_frame([cache_v[s].reshape(bs, past, kv_dim).transpose(0, 2, 1),
                                 _seq_major(samp_cols(vt_t), bs, dec)])
            o_samp = _attn_call(
                kend_samp, _lane_pad_cols(samp_cols(qt_t), bs, dec),
                _lane_pad_cols(samp_cols(qit_t), bs, dec), _lane_pad_cols(samp_cols(wit_t), bs, dec),
                (_k_frame([cache_k[s].reshape(bs, past, kv_dim), k_samp]), vtf_samp,
                 _frame([cache_kidx[s], ki_samp], 1)), tb, rel_bias, n_seq=bs, build=False,
                n_out_rows=bs * QB, diag_off=past // QB, kstart=0, k_sel=k_sel_s, name="dsa_attn_sample")
            o_meta = _attn_call(
                kend_meta, _lane_pad_cols(meta_cols(qt_t), 1, n_meta),
                _lane_pad_cols(meta_cols(qit_t), 1, n_meta), _lane_pad_cols(meta_cols(wit_t), 1, n_meta),
                (_k_frame([k_meta[None]]), _v_frame([vt_meta[None]]), _frame([ki_meta[None]], 1)), tb, rel_bias,
                n_seq=1, build=False, n_out_rows=QB, diag_off=0, kstart=0, k_sel=k_sel_p, name="dsa_attn_meta")
            pre_tail = jnp.concatenate([o_samp.reshape(bs, QB, d)[:, :dec].reshape(n_samp, d), o_meta[:n_meta],
                                        tail_fill], axis=0)
            wo, bo = a_wo[s].astype(BF16), zeros_bias
        elif l % N_MIXERS == 1:
            consts = (g_mix, b_wy[s].astype(BF16), b_by[s].reshape(1, d), b_wx[s].astype(BF16), b_bx[s].reshape(1, d),
                      jnp.pad(b_conv_w[s], ((0, SUBLANES - CONV_B), (0, 0))), b_conv_b[s].reshape(1, d),
                      b_wr[s].astype(BF16), b_br[s].reshape(1, d), b_wi[s].astype(BF16), b_bi[s].reshape(1, d),
                      b_lam[s].reshape(1, d))
            pre_tail = jnp.zeros((n_tail, d), BF16)
            pre_tail, h_m, tail_m = _mixb_call(x_tail, pre_tail, jnp.zeros((1, 1, d), F32),
                                               jnp.zeros((1, SUBLANES, d), F32), *consts, n_seq=1, t_len=n_meta,
                                               row_off=n_samp, name="rglru_meta")
            pre_tail, h_s, tail_s = _mixb_call(x_tail, pre_tail, state_h[s].reshape(bs, 1, d),
                                               _front_pad_rows(state_conv_b[s], SUBLANES), *consts, n_seq=bs,
                                               t_len=dec, row_off=0, name="rglru_sample")
            pre_main, h_p, tail_p = _mixb_call(x_main, None, jnp.broadcast_to(h_m, (bp, 1, d)),
                                               jnp.broadcast_to(tail_m, (bp, SUBLANES, d)), *consts, n_seq=bp,
                                               t_len=seq, row_off=0, name="rglru_prompt")
            new["h_p"].append(h_p.reshape(bp, d))
            new["cb_p"].append(tail_p[:, SUBLANES - (CONV_B - 1):])
            new["h_s"].append(h_s.reshape(bs, d))
            new["cb_s"].append(tail_s[:, SUBLANES - (CONV_B - 1):])
            wo, bo = b_wo[s].astype(BF16), b_bo[s].reshape(1, d)
        else:
            consts = (g_mix, c_win[s].astype(BF16), jnp.pad(c_conv_w[s], ((0, SUBLANES - CONV_C), (0, 0))))
            pre_tail = jnp.zeros((n_tail, d), BF16)
            pre_tail, tail_m = _mixc_call(x_tail, pre_tail, jnp.zeros((1, SUBLANES, d), F32), *consts, n_seq=1,
                                          t_len=n_meta, row_off=n_samp, name="sconv_meta")
            pre_tail, tail_s = _mixc_call(x_tail, pre_tail, _front_pad_rows(state_conv_c[s], SUBLANES), *consts,
                                          n_seq=bs, t_len=dec, row_off=0, name="sconv_sample")
            pre_main, tail_p = _mixc_call(x_main, None, jnp.broadcast_to(tail_m, (bp, SUBLANES, d)), *consts,
                                          n_seq=bp, t_len=seq, row_off=0, name="sconv_prompt")
            new["cc_p"].append(tail_p[:, SUBLANES - (CONV_C - 1):])
            new["cc_s"].append(tail_s[:, SUBLANES - (CONV_C - 1):])
            wo, bo = c_wo[s].astype(BF16), zeros_bias
        ffn = (wo, bo, norm_ffn[l].reshape(1, d), f_wg[l].astype(BF16), f_wu[l].astype(BF16), f_wd[l].astype(BF16),
               gfin, l == depth - 1)
        x_main = _ffn_call(x_main, pre_main, *ffn)
        x_tail = _ffn_call(x_tail, pre_tail, *ffn)

    y_prompt = x_main.reshape(bp, seq, d)
    y_sample = x_tail[:n_samp].reshape(bs, dec, d)
    st = lambda name: jnp.stack(new[name])
    return (y_prompt, y_sample, st("k_p"), st("v_p"), st("ki_p"), st("h_p"), st("cb_p"), st("cc_p"),
            st("k_s"), st("v_s"), st("ki_s"), st("h_s"), st("cb_s"), st("cc_s"))
```

```python
import functools

import jax
import jax.numpy as jnp
from jax import lax
from jax.experimental import pallas as pl
from jax.experimental.pallas import tpu as pltpu

F32 = jnp.float32
BF16 = jnp.bfloat16
I32 = jnp.int32
I16 = jnp.int16

CHUNK = 64
N_MIXERS = 3
N_HEADS = 16
N_KV_HEADS = 4
GROUP = N_HEADS // N_KV_HEADS
HEAD_DIM = 64
IDX_HEADS = 8
IDX_DIM = 64
TOP_K_MAX = 256
NEG_INF = -1e30
N_BUCKETS = 32
LRU_BLOCKS = 4
CONV_B = 4
CONV_C = 3
LRU_C = 8.0
RMS_EPS = 1e-6
BUCKET_STEPS = (12, 16, 23, 32, 46, 64, 91)
FAR_BUCKET = N_BUCKETS // 2 - 1

LANES = 128
SUBLANES = 8
TM_ROWS = 512
TM_TAIL = 256
TM_SEQ = 256
TM_CONV = 512
QB = 128
SEL_TILE = 2 * QB
INT_MIN = -2 ** 31
I16_MIN = -2 ** 15
PACK16 = 2 * SUBLANES
V_ROWS = HEAD_DIM + PACK16
LOG2E = 1.4426950408889634
VMEM_LIMIT = 56 * 1024 * 1024


def _cparams(sem):
    return pltpu.CompilerParams(dimension_semantics=sem, vmem_limit_bytes=VMEM_LIMIT)


def _const_spec(shape):
    nd = len(shape)
    return pl.BlockSpec(shape, lambda *_: (0,) * nd, pipeline_mode=pl.Buffered(1))


def _rmsnorm(x, g):
    ms = jnp.mean(x * x, axis=-1, keepdims=True)
    return x * lax.rsqrt(ms + RMS_EPS) * g


def _dot(a, b):
    return jnp.dot(a, b, preferred_element_type=F32)


def _dot_nt(a, b):
    return lax.dot_general(a, b, (((1,), (1,)), ((), ())), preferred_element_type=F32)


def _ffn_kernel(x_ref, pre_ref, wo_ref, bo_ref, gf_ref, wg_ref, wu_ref, wd_ref, gfin_ref, out_ref, *,
                n_chunks, fc, final_norm):
    x1 = x_ref[...] + _dot(pre_ref[...], wo_ref[...]) + bo_ref[...]
    xn = _rmsnorm(x1, gf_ref[...]).astype(BF16)
    acc = x1
    for c in range(n_chunks):
        gt = _dot(xn, wg_ref[:, c * fc:(c + 1) * fc])
        up = _dot(xn, wu_ref[:, c * fc:(c + 1) * fc])
        hm = (gt * jax.nn.sigmoid(gt) * up).astype(BF16)
        acc = acc + _dot(hm, wd_ref[c * fc:(c + 1) * fc, :])
    if final_norm:
        acc = _rmsnorm(acc, gfin_ref[...])
    out_ref[...] = acc


def _ffn_call(x, pre, wo, bo, gf, wg, wu, wd, gfin, final_norm):
    n, d = x.shape
    dff = wg.shape[1]
    n_chunks = 2 if dff % (2 * LANES) == 0 else 1
    fc = dff // n_chunks
    row = lambda i: (i, 0)
    tm = TM_ROWS if n % TM_ROWS == 0 else TM_TAIL
    return pl.pallas_call(
        functools.partial(_ffn_kernel, n_chunks=n_chunks, fc=fc, final_norm=final_norm),
        grid=(n // tm,),
        in_specs=[pl.BlockSpec((tm, d), row), pl.BlockSpec((tm, pre.shape[1]), row),
                  _const_spec(wo.shape), _const_spec(bo.shape), _const_spec(gf.shape),
                  _const_spec(wg.shape), _const_spec(wu.shape), _const_spec(wd.shape), _const_spec(gfin.shape)],
        out_specs=pl.BlockSpec((tm, d), row),
        out_shape=jax.ShapeDtypeStruct((n, d), F32),
        compiler_params=_cparams(("parallel",)),
        name="outproj_swiglu",
    )(x, pre, wo, bo, gf, wg, wu, wd, gfin)


def _proj_kernel(x_ref, g_ref, wrow_ref, wt_ref, k_ref, v_ref, ki_ref, kb_ref, kib_ref, qt_ref, qit_ref, vt_ref,
                 wit_ref, *, n_kv, n_q, n_iq):
    xn = _rmsnorm(x_ref[...], g_ref[...]).astype(BF16)
    row = _dot(xn, wrow_ref[...])
    k_ref[...] = row[:, :n_kv]
    v_ref[...] = row[:, n_kv:2 * n_kv]
    ki_ref[...] = row[:, 2 * n_kv:]
    kb_ref[...] = row[:, :n_kv].astype(BF16)
    kib_ref[...] = row[:, 2 * n_kv:].astype(BF16)
    tt = _dot_nt(wt_ref[...], xn)
    for b in range(x_ref.shape[0] // QB):
        blk = tt[:, b * QB:(b + 1) * QB]
        qt_ref[b] = blk[:n_q].astype(BF16)
        qit_ref[b] = blk[n_q:n_q + n_iq].astype(BF16)
        vt_ref[b] = blk[n_q + n_iq:n_q + n_iq + n_kv].astype(BF16)
        wit_ref[b] = blk[n_q + n_iq + n_kv:n_q + n_iq + n_kv + IDX_HEADS] * (IDX_HEADS ** -0.5)


def _proj_call(x, g, wrow, wt):
    n, d = x.shape
    n_kv = N_KV_HEADS * HEAD_DIM
    n_q = N_HEADS * HEAD_DIM
    n_iq = IDX_HEADS * IDX_DIM
    row = lambda i: (i, 0)
    tm = TM_ROWS if n % TM_ROWS == 0 else TM_TAIL
    feat_spec = lambda f: pl.BlockSpec((tm // QB, f, QB), lambda i: (i, 0, 0))
    feat_shape = lambda f, dt: jax.ShapeDtypeStruct((n // QB, f, QB), dt)
    return pl.pallas_call(
        functools.partial(_proj_kernel, n_kv=n_kv, n_q=n_q, n_iq=n_iq),
        grid=(n // tm,),
        in_specs=[pl.BlockSpec((tm, d), row), _const_spec(g.shape), _const_spec(wrow.shape),
                  _const_spec(wt.shape)],
        out_specs=[pl.BlockSpec((tm, n_kv), row), pl.BlockSpec((tm, n_kv), row),
                   pl.BlockSpec((tm, IDX_DIM), row), pl.BlockSpec((tm, n_kv), row),
                   pl.BlockSpec((tm, IDX_DIM), row), feat_spec(n_q), feat_spec(n_iq), feat_spec(n_kv),
                   feat_spec(IDX_HEADS)],
        out_shape=[jax.ShapeDtypeStruct((n, n_kv), F32), jax.ShapeDtypeStruct((n, n_kv), F32),
                   jax.ShapeDtypeStruct((n, IDX_DIM), F32), jax.ShapeDtypeStruct((n, n_kv), BF16),
                   jax.ShapeDtypeStruct((n, IDX_DIM), BF16), feat_shape(n_q, BF16), feat_shape(n_iq, BF16),
                   feat_shape(n_kv, BF16), feat_shape(IDX_HEADS, F32)],
        compiler_params=_cparams(("parallel",)),
        name="attn_proj",
    )(x, g, wrow, wt)


def _bias_kernel(tab_ref, tb_ref):
    h = pl.program_id(0)
    d = pl.program_id(1)
    kj = lax.broadcasted_iota(I32, (QB, QB), 0)
    qi = lax.broadcasted_iota(I32, (QB, QB), 1)
    rel = (d - 1) * QB + kj - qi
    n = jnp.abs(rel)
    large = jnp.full((QB, QB), N_BUCKETS // 4, I32)
    for s in BUCKET_STEPS:
        large = large + jnp.where(n >= s, 1, 0)
    bucket = jnp.where(rel > 0, N_BUCKETS // 2, 0) + jnp.where(n < N_BUCKETS // 4, n, large)
    val = jnp.zeros((QB, QB), F32)
    for b in range(N_BUCKETS):
        val = jnp.where(bucket == b, tab_ref[b, h], val)
    tb_ref[0] = val


def _bias_call(table):
    return pl.pallas_call(
        _bias_kernel,
        grid=(N_HEADS, 2),
        in_specs=[pl.BlockSpec(memory_space=pltpu.SMEM)],
        out_specs=pl.BlockSpec((1, QB, QB), lambda h, d: (h, d, 0)),
        out_shape=jax.ShapeDtypeStruct((N_HEADS, 2 * QB, QB), F32),
        compiler_params=_cparams(("arbitrary", "arbitrary")),
        name="rel_bias_tiles",
    )(table)


def _attn_kernel(kend_ref, qt_ref, qit_ref, wit_ref, *refs, diag_off, kstart, k_sel, has_prev, row_bits, build):
    if build:
        (kb_ref, vt_ref, kib_ref, k0_ref, v0_ref, ki0_ref, tb_ref, tab_ref, o_ref, key_scr, hi_scr, lo_scr, lo2_scr,
         ls_scr, oacc_scr, kf_scr, vtf_scr, kif_scr) = refs
        seq = kb_ref.shape[0]
        tkf = kf_scr.shape[1]

        @pl.when(pl.program_id(1) == 0)
        def _():
            kif_scr[0:QB, :] = ki0_ref[...]
            kif_scr[QB:QB + seq, :] = kib_ref[...]
            kif_scr[QB + seq:tkf, :] = jnp.zeros((tkf - QB - seq, IDX_DIM), BF16)
            for g in range(N_KV_HEADS):
                cols = slice(g * HEAD_DIM, (g + 1) * HEAD_DIM)
                kf_scr[g, 0:QB, :] = k0_ref[:, cols]
                for c in range(seq // SEL_TILE):
                    kf_scr[g, QB + c * SEL_TILE:QB + (c + 1) * SEL_TILE, :] = kb_ref[c * SEL_TILE:(c + 1) * SEL_TILE,
                                                                                     cols]
                kf_scr[g, QB + seq:tkf, :] = jnp.zeros((tkf - QB - seq, HEAD_DIM), BF16)
                vrows = slice(g * V_ROWS, g * V_ROWS + HEAD_DIM)
                vtf_scr[vrows, 0:QB] = v0_ref[cols, :]
                for j in range(seq // QB):
                    vtf_scr[vrows, QB + j * QB:QB + (j + 1) * QB] = vt_ref[j, cols, :]
                vtf_scr[vrows, QB + seq:tkf] = jnp.zeros((HEAD_DIM, tkf - QB - seq), BF16)
                extra = lax.broadcasted_iota(I32, (V_ROWS - HEAD_DIM, tkf), 0) == 0
                vtf_scr[g * V_ROWS + HEAD_DIM:(g + 1) * V_ROWS, :] = jnp.where(extra, 1.0, 0.0).astype(BF16)

        k_rows = lambda g, r0, rows: kf_scr[g, pl.ds(r0, rows), :]
        v_cols = lambda g, r0, rows: vtf_scr[g * V_ROWS:(g + 1) * V_ROWS, pl.ds(r0, rows)]
        ki_rows = lambda r0, rows: kif_scr[pl.ds(r0, rows), :]
    else:
        (kf_ref, vtf_ref, kif_ref, tb_ref, tab_ref, o_ref, key_scr, hi_scr, lo_scr, lo2_scr, ls_scr,
         oacc_scr) = refs
        k_rows = lambda g, r0, rows: kf_ref[0, g, pl.ds(r0, rows), :]
        v_cols = lambda g, r0, rows: vtf_ref[0, g * V_ROWS:(g + 1) * V_ROWS, pl.ds(r0, rows)]
        ki_rows = lambda r0, rows: kif_ref[0, pl.ds(r0, rows), :]
    diag = pl.program_id(1) + diag_off
    nt = diag + 1
    nst = (nt + SEL_TILE // QB - 1) // (SEL_TILE // QB)
    kend = kend_ref[0]
    qi_cat = jnp.concatenate([qit_ref[0, h * IDX_DIM:(h + 1) * IDX_DIM, :] for h in range(IDX_HEADS)], axis=1)
    wi = wit_ref[0]
    sel_iota = lax.broadcasted_iota(I32, (SEL_TILE, QB), 0)

    def tile_start(t):
        return pl.multiple_of(t * QB, QB)

    def sel_start(t):
        return pl.multiple_of(t * SEL_TILE, SEL_TILE)

    def score_rows(r0):
        s = _dot(ki_rows(r0, SEL_TILE), qi_cat)
        sc = jnp.zeros((SEL_TILE, QB), F32)
        for h in range(IDX_HEADS):
            sc = sc + wi[h:h + 1, :] * jnp.maximum(s[:, h * QB:(h + 1) * QB], 0.0)
        bits = lax.bitcast_convert_type(sc + 0.0, I32)
        key = jnp.where(bits >= 0, bits, bits ^ 0x7FFFFFFF)
        rows = r0 + sel_iota
        adm = (rows >= kstart) & (rows < kend)
        key = jnp.where(adm, key, INT_MIN)
        key_scr[pl.ds(r0, SEL_TILE), :] = key
        hi_scr[pl.ds(r0, SEL_TILE), :] = (key >> 16).astype(I16)
        lo_scr[pl.ds(r0, SEL_TILE), :] = ((key & 0xFFFF) + I16_MIN).astype(I16)

    pair_rows = 2 * SEL_TILE

    def score_pair(t, carry):
        r0 = pl.multiple_of(t * pair_rows, pair_rows)
        score_rows(r0)
        score_rows(r0 + SEL_TILE)
        return carry

    lax.fori_loop(0, (nst + 1) // 2, score_pair, 0)

    pair_vregs = pair_rows // PACK16

    def as_packed(v):
        return jnp.broadcast_to(v, (PACK16, QB)).astype(I16)

    def threshold(n_pair):
        def count16(src_scr, cand):
            c16 = as_packed(cand)[None]
            acc = jnp.zeros((PACK16, QB), BF16)
            for t in range(n_pair):
                v = src_scr[t * pair_rows:(t + 1) * pair_rows, :].reshape(pair_vregs, PACK16, QB)
                ind = jnp.where(v >= c16, jnp.ones((), BF16), jnp.zeros((), BF16))
                parts = [ind[i] for i in range(pair_vregs)]
                while len(parts) > 1:
                    parts = [parts[i] + parts[i + 1] for i in range(0, len(parts), 2)]
                acc = acc + parts[0]
            return acc.astype(F32).sum(axis=0, keepdims=True)

        def search16(src_scr, base, c_start):
            def step(b, carry):
                t_acc, c_acc = carry
                cand = t_acc + lax.shift_left(jnp.int32(1), 15 - b)
                c = base + count16(src_scr, cand)
                ok = c >= k_sel
                return jnp.where(ok, cand, t_acc), jnp.where(ok, c, c_acc)

            return lax.fori_loop(0, 16, step, (jnp.full((1, QB), I16_MIN, I32), c_start))

        thr_hi, c_hi = search16(hi_scr, 0.0, jnp.full((1, QB), n_pair * pair_rows, F32))
        above = jnp.where(thr_hi == -I16_MIN - 1, 0.0, count16(hi_scr, thr_hi + 1))
        hi16 = as_packed(thr_hi)[None]
        for t in range(n_pair):
            rows = slice(t * pair_rows, (t + 1) * pair_rows)
            hi = hi_scr[rows, :].reshape(pair_vregs, PACK16, QB)
            lo = lo_scr[rows, :].reshape(pair_vregs, PACK16, QB)
            lo2_scr[rows, :] = jnp.where(hi == hi16, lo, jnp.full((), I16_MIN, I16)).reshape(pair_rows, QB)
        thr_lo, c_ge = search16(lo2_scr, above, c_hi)
        return thr_hi * 65536 + (thr_lo - I16_MIN), c_ge

    def count(pred):
        def body(t, acc):
            r0 = sel_start(t)
            ind = jnp.where(pred(key_scr[pl.ds(r0, SEL_TILE), :], r0 + sel_iota), 1, 0)
            return acc + ind.reshape(SEL_TILE // SUBLANES, SUBLANES, QB).sum(axis=0)
        acc = lax.fori_loop(0, nst, body, jnp.zeros((SUBLANES, QB), I32))
        return acc.sum(axis=0, keepdims=True)

    max_pairs = hi_scr.shape[0] // pair_rows
    thr, c_ge = lax.switch((nst + 1) // 2 - 1, [functools.partial(threshold, n) for n in range(1, max_pairs + 1)])

    tied = (c_ge > k_sel) & (thr > INT_MIN)
    big = jnp.full((1, QB), 2 ** row_bits, I32)

    def tie_limit():
        need = k_sel - count(lambda kt, rows: kt > thr)

        def lim_step(b, lim):
            cand = lim + lax.shift_left(jnp.int32(1), row_bits - 1 - b)
            c = count(lambda kt, rows: (kt == thr) & (rows < cand))
            return jnp.where(c < need, cand, lim)

        lim = lax.fori_loop(0, row_bits, lim_step, jnp.zeros((1, QB), I32))
        return jnp.where(tied, lim, big)

    rlim = lax.cond(jnp.max(tied.astype(I32)) > 0, tie_limit, lambda: big)
    rlim = jnp.where(thr > INT_MIN, rlim, -1)

    gq = GROUP * QB
    qg = [jnp.concatenate([qt_ref[0, (g * GROUP + r) * HEAD_DIM:(g * GROUP + r + 1) * HEAD_DIM, :]
                           for r in range(GROUP)], axis=1) for g in range(N_KV_HEADS)]

    def logits_rows(r0, rows, m8, bias_of_head):
        kt = key_scr[pl.ds(r0, rows), :]
        keep = (kt > thr) | ((kt == thr) & (sel_iota[:rows] <= rlim - r0))
        new_m8 = []
        for g in range(N_KV_HEADS):
            lg = _dot(k_rows(g, r0, rows), qg[g])
            parts = []
            for r in range(GROUP):
                part = (lg[:, r * QB:(r + 1) * QB] + bias_of_head(g * GROUP + r)) * LOG2E
                parts.append(jnp.where(keep, part, NEG_INF))
            lg = jnp.concatenate(parts, axis=1)
            ls_scr[pl.ds(r0, rows), g * gq:(g + 1) * gq] = lg
            new_m8.append(jnp.maximum(m8[g], lg.reshape(rows // SUBLANES, SUBLANES, gq).max(axis=0)))
        return tuple(new_m8)

    far_bias = lambda h: tab_ref[FAR_BUCKET, h]
    m8 = tuple(jnp.full((SUBLANES, gq), NEG_INF, F32) for _ in range(N_KV_HEADS))
    n_far = diag - 1 if has_prev else diag
    def far_pair(t, m8):
        r0 = pl.multiple_of(t * pair_rows, pair_rows)
        return logits_rows(r0 + SEL_TILE, SEL_TILE, logits_rows(r0, SEL_TILE, m8, far_bias), far_bias)

    m8 = lax.fori_loop(0, n_far // 4, far_pair, m8)
    m8 = lax.cond(n_far % 4 >= 2, lambda m: logits_rows(sel_start(n_far // 4 * 2), SEL_TILE, m, far_bias),
                  lambda m: m, m8)
    m8 = lax.cond(n_far % 2 == 1, lambda m: logits_rows(tile_start(n_far - 1), QB, m, far_bias), lambda m: m, m8)
    if has_prev:
        m8 = logits_rows(tile_start(diag - 1), 2 * QB, m8, lambda h: tb_ref[h])
    else:
        m8 = logits_rows(tile_start(diag), QB, m8, lambda h: tb_ref[h, QB:, :])
    m = [jnp.max(m8[g], axis=0, keepdims=True) for g in range(N_KV_HEADS)]

    @pl.when(nt % 2 == 1)
    def _():
        ls_scr[pl.ds(tile_start(nt), QB), :] = jnp.full((QB, N_HEADS * QB), NEG_INF, F32)

    oacc_scr[...] = jnp.zeros(oacc_scr.shape, F32)

    def pv_rows(r0):
        for g in range(N_KV_HEADS):
            p = jnp.exp2(ls_scr[pl.ds(r0, SEL_TILE), g * gq:(g + 1) * gq] - m[g])
            oacc_scr[g] += _dot(v_cols(g, r0, SEL_TILE), p.astype(BF16))

    def pv_pair(t, carry):
        r0 = pl.multiple_of(t * pair_rows, pair_rows)
        pv_rows(r0)
        pv_rows(r0 + SEL_TILE)
        return carry

    lax.fori_loop(0, nst // 2, pv_pair, 0)

    @pl.when(nst % 2 == 1)
    def _():
        pv_rows(sel_start(nst - 1))
    out_rows = []
    for g in range(N_KV_HEADS):
        o_g = oacc_scr[g, :HEAD_DIM] / oacc_scr[g, HEAD_DIM:HEAD_DIM + 1]
        out_rows.extend(o_g[:, r * QB:(r + 1) * QB] for r in range(GROUP))
    o_ref[...] = jnp.concatenate(out_rows, axis=0).T.astype(BF16)


def _attn_call(kend, qt, qit, wit, frames, tb, table, *, n_seq, build, n_out_rows, diag_off, kstart, k_sel, name):
    n_qb = kend.shape[0]
    kv_dim = N_KV_HEADS * HEAD_DIM
    if build:
        seq = frames[0].shape[0] // n_seq
        tkf = -(-(QB + seq) // (2 * SEL_TILE)) * 2 * SEL_TILE
        frame_specs = [pl.BlockSpec((seq, kv_dim), lambda s, i: (s, 0)),
                       pl.BlockSpec((seq // QB, kv_dim, QB), lambda s, i: (s, 0, 0)),
                       pl.BlockSpec((seq, IDX_DIM), lambda s, i: (s, 0))] + [_const_spec(a.shape) for a in frames[3:]]
        frame_scratch = [pltpu.VMEM((N_KV_HEADS, tkf, HEAD_DIM), BF16), pltpu.VMEM((N_KV_HEADS * V_ROWS, tkf), BF16),
                         pltpu.VMEM((tkf, IDX_DIM), BF16)]
    else:
        kf, vtf, kif = frames
        tkf = kf.shape[2]
        assert vtf.shape[2] == tkf and kif.shape[1] == tkf
        frame_specs = [pl.BlockSpec((1, N_KV_HEADS, tkf, HEAD_DIM), lambda s, i: (s, 0, 0, 0)),
                       pl.BlockSpec((1, N_KV_HEADS * V_ROWS, tkf), lambda s, i: (s, 0, 0)),
                       pl.BlockSpec((1, tkf, IDX_DIM), lambda s, i: (s, 0, 0))]
        frame_scratch = []
    assert tkf % (2 * SEL_TILE) == 0
    assert tkf // PACK16 <= 256, "per-slot bf16 counts must stay exact"
    has_prev = diag_off >= 1
    row_bits = max(1, (tkf - 1).bit_length())
    feat = lambda a: pl.BlockSpec((1, a.shape[1], QB), lambda s, i: (s * n_qb + i, 0, 0))
    return pl.pallas_call(
        functools.partial(_attn_kernel, diag_off=diag_off, kstart=kstart, k_sel=k_sel, has_prev=has_prev,
                          row_bits=row_bits, build=build),
        grid=(n_seq, n_qb),
        in_specs=[pl.BlockSpec((1, 1, QB), lambda s, i: (i, 0, 0)), feat(qt), feat(qit), feat(wit)] + frame_specs
                 + [_const_spec(tb.shape), pl.BlockSpec(memory_space=pltpu.SMEM)],
        out_specs=pl.BlockSpec((QB, N_HEADS * HEAD_DIM), lambda s, i: (s * n_qb + i, 0)),
        out_shape=jax.ShapeDtypeStruct((n_out_rows, N_HEADS * HEAD_DIM), BF16),
        scratch_shapes=[pltpu.VMEM((tkf, QB), I32)] + [pltpu.VMEM((tkf, QB), I16)] * 3
                       + [pltpu.VMEM((tkf, N_HEADS * QB), F32), pltpu.VMEM((N_KV_HEADS, V_ROWS, GROUP * QB), F32)]
                       + frame_scratch,
        compiler_params=_cparams(("arbitrary", "arbitrary")),
        name=name,
    )(kend, qt, qit, wit, *frames, tb, table)


def _softplus(x):
    return jnp.maximum(x, 0.0) + jnp.log1p(jnp.exp(-jnp.abs(x)))


def _mixb_kernel(x_ref, h0_ref, buf_ref, g_ref, wy_ref, by_ref, wx_ref, bx_ref, cw_ref, cb_ref, wr_ref, br_ref,
                 wi_ref, bi_ref, lam_ref, *rest, tm):
    pre_ref, hlast_ref, tail_ref, xe_scr, h_scr = rest[-5:]

    @pl.when(pl.program_id(1) == 0)
    def _():
        xe_scr[0:SUBLANES, :] = buf_ref[0]
        h_scr[...] = h0_ref[0]

    xn = _rmsnorm(x_ref[...], g_ref[...]).astype(BF16)
    y = jax.nn.gelu(_dot(xn, wy_ref[...]) + by_ref[...])
    xe_scr[SUBLANES:SUBLANES + tm, :] = _dot(xn, wx_ref[...]) + bx_ref[...]
    xc = cb_ref[...] + cw_ref[0:1, :] * xe_scr[SUBLANES - 3:SUBLANES - 3 + tm, :]
    for j in range(1, CONV_B):
        xc = xc + cw_ref[j:j + 1, :] * xe_scr[SUBLANES - 3 + j:SUBLANES - 3 + j + tm, :]
    xcb = xc.astype(BF16)
    d_rnn = xc.shape[1]
    blk = d_rnn // LRU_BLOCKS
    r_pre = jnp.concatenate([_dot(xcb[:, n * blk:(n + 1) * blk], wr_ref[n]) for n in range(LRU_BLOCKS)], axis=1)
    i_pre = jnp.concatenate([_dot(xcb[:, n * blk:(n + 1) * blk], wi_ref[n]) for n in range(LRU_BLOCKS)], axis=1)
    r = jax.nn.sigmoid(r_pre + br_ref[...])
    ig = jax.nn.sigmoid(i_pre + bi_ref[...])
    log_a = -LRU_C * r * _softplus(-lam_ref[...])
    a = jnp.exp(log_a)
    u = jnp.sqrt(jnp.tanh(-log_a) * (1.0 + a * a)) * (ig * xc)
    a = a.reshape(tm // SUBLANES, SUBLANES, d_rnn)
    u = u.reshape(tm // SUBLANES, SUBLANES, d_rnn)
    row_in_group = lax.broadcasted_iota(I32, a.shape, 1)
    s = 1
    while s < SUBLANES:
        a_sh = jnp.where(row_in_group >= s, pltpu.roll(a, s, 1), 1.0)
        u_sh = jnp.where(row_in_group >= s, pltpu.roll(u, s, 1), 0.0)
        u = a * u_sh + u
        a = a * a_sh
        s *= 2
    carry = h_scr[...]
    h_groups = []
    for grp in range(tm // SUBLANES):
        h_grp = a[grp] * carry + u[grp]
        carry = h_grp[SUBLANES - 1:SUBLANES, :]
        h_groups.append(h_grp)
    h = jnp.concatenate(h_groups, axis=0)
    pre_ref[...] = (h * y).astype(BF16)
    h_scr[...] = carry
    hlast_ref[0] = carry
    tail = xe_scr[tm:tm + SUBLANES, :]
    xe_scr[0:SUBLANES, :] = tail
    tail_ref[0] = tail


def _mixc_kernel(x_ref, buf_ref, g_ref, win_ref, cw_ref, *rest, tm):
    pre_ref, tail_ref, pe_scr = rest[-3:]

    @pl.when(pl.program_id(1) == 0)
    def _():
        pe_scr[0:SUBLANES, :] = buf_ref[0]

    d = x_ref.shape[1]
    xn = _rmsnorm(x_ref[...], g_ref[...]).astype(BF16)
    z = _dot(xn, win_ref[...])
    pe_scr[SUBLANES:SUBLANES + tm, :] = z[:, d:2 * d] * z[:, 2 * d:]
    conv = cw_ref[0:1, :] * pe_scr[SUBLANES - 2:SUBLANES - 2 + tm, :]
    for j in range(1, CONV_C):
        conv = conv + cw_ref[j:j + 1, :] * pe_scr[SUBLANES - 2 + j:SUBLANES - 2 + j + tm, :]
    pre_ref[...] = (z[:, :d] * conv).astype(BF16)
    tail = pe_scr[tm:tm + SUBLANES, :]
    pe_scr[0:SUBLANES, :] = tail
    tail_ref[0] = tail


def _shared_out(buf, index):
    if buf is None:
        return {"specs": [], "args": (), "aliases": {}}
    return {"specs": [pl.BlockSpec(memory_space=pl.ANY)], "args": (buf,), "aliases": {index: 0}}


def _seq_specs(n_seq, t_len, row_off, d, tm_max):
    tm = tm_max if t_len % tm_max == 0 else min(TM_SEQ, t_len)
    n_tt = t_len // tm
    off = row_off // tm
    xrow = pl.BlockSpec((tm, d), lambda s, j: (off + s * n_tt + j, 0))
    state = lambda rows: pl.BlockSpec((1, rows, d), lambda s, j: (s, 0, 0))
    return tm, n_tt, xrow, state


def _mixb_call(x, pre_buf, h0, buf, g, wy, by, wx, bx, cw, cb, wr, br, wi, bi, lam, *, n_seq, t_len, row_off,
               name):
    d = x.shape[1]
    tm, n_tt, xrow, state = _seq_specs(n_seq, t_len, row_off, d, TM_SEQ)
    consts = (g, wy, by, wx, bx, cw, cb, wr, br, wi, bi, lam)
    shared = _shared_out(pre_buf, 3 + len(consts))
    return pl.pallas_call(
        functools.partial(_mixb_kernel, tm=tm),
        grid=(n_seq, n_tt),
        in_specs=[xrow, state(1), state(SUBLANES)] + [_const_spec(c.shape) for c in consts] + shared["specs"],
        out_specs=[xrow, state(1), state(SUBLANES)],
        out_shape=[jax.ShapeDtypeStruct((x.shape[0], d), BF16), jax.ShapeDtypeStruct((n_seq, 1, d), F32),
                   jax.ShapeDtypeStruct((n_seq, SUBLANES, d), F32)],
        scratch_shapes=[pltpu.VMEM((tm + SUBLANES, d), F32), pltpu.VMEM((1, d), F32)],
        input_output_aliases=shared["aliases"],
        compiler_params=_cparams(("arbitrary", "arbitrary")),
        name=name,
    )(x, h0, buf, *consts, *shared["args"])


def _mixc_call(x, pre_buf, buf, g, win, cw, *, n_seq, t_len, row_off, name):
    d = x.shape[1]
    tm, n_tt, xrow, state = _seq_specs(n_seq, t_len, row_off, d, TM_CONV)
    consts = (g, win, cw)
    shared = _shared_out(pre_buf, 2 + len(consts))
    return pl.pallas_call(
        functools.partial(_mixc_kernel, tm=tm),
        grid=(n_seq, n_tt),
        in_specs=[xrow, state(SUBLANES)] + [_const_spec(c.shape) for c in consts] + shared["specs"],
        out_specs=[xrow, state(SUBLANES)],
        out_shape=[jax.ShapeDtypeStruct((x.shape[0], d), BF16), jax.ShapeDtypeStruct((n_seq, SUBLANES, d), F32)],
        scratch_shapes=[pltpu.VMEM((tm + SUBLANES, d), F32)],
        input_output_aliases=shared["aliases"],
        compiler_params=_cparams(("arbitrary", "arbitrary")),
        name=name,
    )(x, buf, *consts, *shared["args"])


def _front_pad_rows(a, rows):
    return jnp.pad(a, ((0, 0), (rows - a.shape[1], 0), (0, 0)))


def _frame(parts, axis):
    f = jnp.concatenate(parts, axis=axis).astype(BF16)
    pad = [(0, 0)] * f.ndim
    pad[axis] = (0, -f.shape[axis] % (2 * SEL_TILE))
    return jnp.pad(f, pad)


def _k_frame(parts):
    f = _frame(parts, 1)
    n_seq, rows = f.shape[0], f.shape[1]
    return f.reshape(n_seq, rows, N_KV_HEADS, HEAD_DIM).transpose(0, 2, 1, 3)


def _v_frame(parts):
    f = _frame(parts, 2)
    n_seq, tkf = f.shape[0], f.shape[2]
    extra = jnp.zeros((n_seq, N_KV_HEADS, V_ROWS - HEAD_DIM, tkf), BF16).at[:, :, 0].set(1)
    f = jnp.concatenate([f.reshape(n_seq, N_KV_HEADS, HEAD_DIM, tkf), extra], axis=2)
    return f.reshape(n_seq, N_KV_HEADS * V_ROWS, tkf)


def _cols(a, start, n):
    b0, b1 = start // QB, -(-(start + n) // QB)
    c = a[b0:b1].transpose(1, 0, 2).reshape(a.shape[1], (b1 - b0) * QB)
    return c[:, start - b0 * QB:start - b0 * QB + n]


def _seq_major(a, n_seq, q_len):
    return a.reshape(a.shape[0], n_seq, q_len).transpose(1, 0, 2)


def _lane_pad_cols(a, n_seq, q_len):
    return jnp.pad(_seq_major(a, n_seq, q_len), ((0, 0), (0, 0), (0, QB - q_len)))


def kernel(x_prompt, x_sample, cache_k, cache_v, cache_kidx, state_h, state_conv_b, state_conv_c, meta_tokens,
           rel_bias, norm_mix, norm_ffn, norm_final, a_wq, a_wk, a_wv, a_wo, a_wiq, a_wik, a_wiw, b_wy, b_by, b_wx,
           b_bx, b_conv_w, b_conv_b, b_wr, b_br, b_wi, b_bi, b_lam, b_wo, b_bo, c_win, c_conv_w, c_wo, f_wg, f_wu,
           f_wd):
    bp, seq, d = x_prompt.shape
    bs, dec = x_sample.shape[0], x_sample.shape[1]
    past = cache_k.shape[2]
    n_meta = meta_tokens.shape[0]
    depth = norm_mix.shape[0]
    assert seq % QB == 0 and seq % TM_SEQ == 0 and past % QB == 0 and dec <= QB and n_meta <= QB
    assert dec % PACK16 == 0 and n_meta % PACK16 == 0 and (bs * dec) % n_meta == 0

    n_main, n_samp = bp * seq, bs * dec
    n_tail = -(-(n_samp + n_meta) // TM_TAIL) * TM_TAIL
    x_main = x_prompt.reshape(n_main, d)
    x_tail = jnp.concatenate([x_sample.reshape(n_samp, d), meta_tokens.astype(F32),
                              jnp.zeros((n_tail - n_samp - n_meta, d), F32)], axis=0)
    tail_fill = jnp.zeros((n_tail - n_samp - n_meta, d), BF16)

    k_sel_p = min(TOP_K_MAX, seq // 4)
    k_sel_s = min(TOP_K_MAX, (past + dec) // 4)
    n_qb = seq // QB
    kv_dim = N_KV_HEADS * HEAD_DIM

    tau = jnp.arange(seq, dtype=I32)
    kend_main = (QB + CHUNK * (tau // CHUNK + 1)).reshape(n_qb, 1, QB)
    lane = jnp.arange(QB, dtype=I32)
    kend_samp = jnp.minimum(past + dec, CHUNK * ((past + jnp.minimum(lane, dec - 1)) // CHUNK + 1)).reshape(1, 1, QB)
    kend_meta = jnp.full((1, 1, QB), n_meta, I32)

    tb = _bias_call(rel_bias.astype(F32))
    zeros_bias = jnp.zeros((1, d), F32)
    gfin = norm_final.reshape(1, d).astype(F32)

    new = {name: [] for name in ("k_p", "v_p", "ki_p", "h_p", "cb_p", "cc_p", "k_s", "v_s", "ki_s", "h_s", "cb_s",
                                 "cc_s")}
    for l in range(depth):
        s = l // N_MIXERS
        g_mix = norm_mix[l].reshape(1, d)
        if l % N_MIXERS == 0:
            wrow = jnp.concatenate([a_wk[s], a_wv[s], a_wik[s]], axis=1).astype(BF16)
            wt = jnp.concatenate([a_wq[s].T * HEAD_DIM ** -0.5, a_wiq[s].T * IDX_DIM ** -0.5, a_wv[s].T, a_wiw[s].T,
                                  jnp.zeros((SUBLANES, d), F32)], axis=0).astype(BF16)
            k, v, ki, kb, kib, qt, qit, vt, wit = _proj_call(x_main, g_mix, wrow, wt)
            k_t, v_t, ki_t, _, _, qt_t, qit_t, vt_t, wit_t = _proj_call(x_tail, g_mix, wrow, wt)

            k_meta, v_meta, ki_meta = (a[n_samp:n_samp + n_meta] for a in (k_t, v_t, ki_t))
            k_main, v_main, ki_main = (a.reshape(bp, seq, -1) for a in (k, v, ki))
            k_samp, v_samp, ki_samp = (a[:n_samp].reshape(bs, dec, -1) for a in (k_t, v_t, ki_t))
            bc = lambda a: jnp.broadcast_to(a[None], (bp,) + a.shape)
            new["k_p"].append(jnp.concatenate([bc(k_meta), k_main], axis=1).reshape(bp, n_meta + seq, N_KV_HEADS,
                                                                                   HEAD_DIM))
            new["v_p"].append(jnp.concatenate([bc(v_meta), v_main], axis=1).reshape(bp, n_meta + seq, N_KV_HEADS,
                                                                                   HEAD_DIM))
            new["ki_p"].append(jnp.concatenate([bc(ki_meta), ki_main], axis=1))
            new["k_s"].append(k_samp.reshape(bs, dec, N_KV_HEADS, HEAD_DIM))
            new["v_s"].append(v_samp.reshape(bs, dec, N_KV_HEADS, HEAD_DIM))
            new["ki_s"].append(ki_samp)

            samp_cols = lambda a: _cols(a, 0, n_samp)
            meta_cols = lambda a: _cols(a, n_samp, n_meta)
            vt_meta = meta_cols(vt_t)
            lead = lambda a, axis: jnp.pad(a, [(QB - n_meta, 0) if ax == axis else (0, 0)
                                               for ax in range(2)]).astype(BF16)
            pre_main = _attn_call(
                kend_main, qt, qit, wit, (kb, vt, kib, lead(k_meta, 0), lead(vt_meta, 1), lead(ki_meta, 0)), tb,
                rel_bias, n_seq=bp, build=True,
                n_out_rows=n_main, diag_off=1, kstart=QB - n_meta, k_sel=k_sel_p, name="dsa_attn_prompt")
            vtf_samp = _v_frame([cache_v[s].reshape(bs, past, kv_dim).transpose(0, 2, 1),
                                 _seq_major(samp_cols(vt_t), bs, dec)])
            o_samp = _attn_call(
                kend_samp, _lane_pad_cols(samp_cols(qt_t), bs, dec),
                _lane_pad_cols(samp_cols(qit_t), bs, dec), _lane_pad_cols(samp_cols(wit_t), bs, dec),
                (_k_frame([cache_k[s].reshape(bs, past, kv_dim), k_samp]), vtf_samp,
                 _frame([cache_kidx[s], ki_samp], 1)), tb, rel_bias, n_seq=bs, build=False,
                n_out_rows=bs * QB, diag_off=past // QB, kstart=0, k_sel=k_sel_s, name="dsa_attn_sample")
            o_meta = _attn_call(
                kend_meta, _lane_pad_cols(meta_cols(qt_t), 1, n_meta),
                _lane_pad_cols(meta_cols(qit_t), 1, n_meta), _lane_pad_cols(meta_cols(wit_t), 1, n_meta),
                (_k_frame([k_meta[None]]), _v_frame([vt_meta[None]]), _frame([ki_meta[None]], 1)), tb, rel_bias,
                n_seq=1, build=False, n_out_rows=QB, diag_off=0, kstart=0, k_sel=k_sel_p, name="dsa_attn_meta")
            pre_tail = jnp.concatenate([o_samp.reshape(bs, QB, d)[:, :dec].reshape(n_samp, d), o_meta[:n_meta],
                                        tail_fill], axis=0)
            wo, bo = a_wo[s].astype(BF16), zeros_bias
        elif l % N_MIXERS == 1:
            consts = (g_mix, b_wy[s].astype(BF16), b_by[s].reshape(1, d), b_wx[s].astype(BF16), b_bx[s].reshape(1, d),
                      jnp.pad(b_conv_w[s], ((0, SUBLANES - CONV_B), (0, 0))), b_conv_b[s].reshape(1, d),
                      b_wr[s].astype(BF16), b_br[s].reshape(1, d), b_wi[s].astype(BF16), b_bi[s].reshape(1, d),
                      b_lam[s].reshape(1, d))
            pre_tail = jnp.zeros((n_tail, d), BF16)
            pre_tail, h_m, tail_m = _mixb_call(x_tail, pre_tail, jnp.zeros((1, 1, d), F32),
                                               jnp.zeros((1, SUBLANES, d), F32), *consts, n_seq=1, t_len=n_meta,
                                               row_off=n_samp, name="rglru_meta")
            pre_tail, h_s, tail_s = _mixb_call(x_tail, pre_tail, state_h[s].reshape(bs, 1, d),
                                               _front_pad_rows(state_conv_b[s], SUBLANES), *consts, n_seq=bs,
                                               t_len=dec, row_off=0, name="rglru_sample")
            pre_main, h_p, tail_p = _mixb_call(x_main, None, jnp.broadcast_to(h_m, (bp, 1, d)),
                                               jnp.broadcast_to(tail_m, (bp, SUBLANES, d)), *consts, n_seq=bp,
                                               t_len=seq, row_off=0, name="rglru_prompt")
            new["h_p"].append(h_p.reshape(bp, d))
            new["cb_p"].append(tail_p[:, SUBLANES - (CONV_B - 1):])
            new["h_s"].append(h_s.reshape(bs, d))
            new["cb_s"].append(tail_s[:, SUBLANES - (CONV_B - 1):])
            wo, bo = b_wo[s].astype(BF16), b_bo[s].reshape(1, d)
        else:
            consts = (g_mix, c_win[s].astype(BF16), jnp.pad(c_conv_w[s], ((0, SUBLANES - CONV_C), (0, 0))))
            pre_tail = jnp.zeros((n_tail, d), BF16)
            pre_tail, tail_m = _mixc_call(x_tail, pre_tail, jnp.zeros((1, SUBLANES, d), F32), *consts, n_seq=1,
                                          t_len=n_meta, row_off=n_samp, name="sconv_meta")
            pre_tail, tail_s = _mixc_call(x_tail, pre_tail, _front_pad_rows(state_conv_c[s], SUBLANES), *consts,
                                          n_seq=bs, t_len=dec, row_off=0, name="sconv_sample")
            pre_main, tail_p = _mixc_call(x_main, None, jnp.broadcast_to(tail_m, (bp, SUBLANES, d)), *consts,
                                          n_seq=bp, t_len=seq, row_off=0, name="sconv_prompt")
            new["cc_p"].append(tail_p[:, SUBLANES - (CONV_C - 1):])
            new["cc_s"].append(tail_s[:, SUBLANES - (CONV_C - 1):])
            wo, bo = c_wo[s].astype(BF16), zeros_bias
        ffn = (wo, bo, norm_ffn[l].reshape(1, d), f_wg[l].astype(BF16), f_wu[l].astype(BF16), f_wd[l].astype(BF16),
               gfin, l == depth - 1)
        x_main = _ffn_call(x_main, pre_main, *ffn)
        x_tail = _ffn_call(x_tail, pre_tail, *ffn)

    y_prompt = x_main.reshape(bp, seq, d)
    y_sample = x_tail[:n_samp].reshape(bs, dec, d)
    st = lambda name: jnp.stack(new[name])
    return (y_prompt, y_sample, st("k_p"), st("v_p"), st("ki_p"), st("h_p"), st("cb_p"), st("cc_p"),
            st("k_s"), st("v_s"), st("ki_s"), st("h_s"), st("cb_s"), st("cc_s"))
```

```python
import functools

import jax
import jax.numpy as jnp
from jax import lax
from jax.experimental import pallas as pl
from jax.experimental.pallas import tpu as pltpu

F32 = jnp.float32
BF16 = jnp.bfloat16
I32 = jnp.int32
I16 = jnp.int16

CHUNK = 64
N_MIXERS = 3
N_HEADS = 16
N_KV_HEADS = 4
GROUP = N_HEADS // N_KV_HEADS
HEAD_DIM = 64
IDX_HEADS = 8
IDX_DIM = 64
TOP_K_MAX = 256
NEG_INF = -1e30
N_BUCKETS = 32
LRU_BLOCKS = 4
CONV_B = 4
CONV_C = 3
LRU_C = 8.0
RMS_EPS = 1e-6
BUCKET_STEPS = (12, 16, 23, 32, 46, 64, 91)
FAR_BUCKET = N_BUCKETS // 2 - 1

LANES = 128
SUBLANES = 8
TM_ROWS = 512
TM_TAIL = 256
TM_SEQ = 256
TM_CONV = 512
QB = 128
SEL_TILE = 2 * QB
INT_MIN = -2 ** 31
I16_MIN = -2 ** 15
PACK16 = 2 * SUBLANES
V_ROWS = HEAD_DIM + PACK16
VMEM_LIMIT = 56 * 1024 * 1024


def _cparams(sem):
    return pltpu.CompilerParams(dimension_semantics=sem, vmem_limit_bytes=VMEM_LIMIT)


def _const_spec(shape):
    nd = len(shape)
    return pl.BlockSpec(shape, lambda *_: (0,) * nd, pipeline_mode=pl.Buffered(1))


def _rmsnorm(x, g):
    ms = jnp.mean(x * x, axis=-1, keepdims=True)
    return x * lax.rsqrt(ms + RMS_EPS) * g


def _dot(a, b):
    return jnp.dot(a, b, preferred_element_type=F32)


def _dot_nt(a, b):
    return lax.dot_general(a, b, (((1,), (1,)), ((), ())), preferred_element_type=F32)


def _ffn_kernel(x_ref, pre_ref, wo_ref, bo_ref, gf_ref, wg_ref, wu_ref, wd_ref, gfin_ref, out_ref, *,
                n_chunks, fc, final_norm):
    x1 = x_ref[...] + _dot(pre_ref[...], wo_ref[...]) + bo_ref[...]
    xn = _rmsnorm(x1, gf_ref[...]).astype(BF16)
    acc = x1
    for c in range(n_chunks):
        gt = _dot(xn, wg_ref[:, c * fc:(c + 1) * fc])
        up = _dot(xn, wu_ref[:, c * fc:(c + 1) * fc])
        hm = (gt * jax.nn.sigmoid(gt) * up).astype(BF16)
        acc = acc + _dot(hm, wd_ref[c * fc:(c + 1) * fc, :])
    if final_norm:
        acc = _rmsnorm(acc, gfin_ref[...])
    out_ref[...] = acc


def _ffn_call(x, pre, wo, bo, gf, wg, wu, wd, gfin, final_norm):
    n, d = x.shape
    dff = wg.shape[1]
    n_chunks = 2 if dff % (2 * LANES) == 0 else 1
    fc = dff // n_chunks
    row = lambda i: (i, 0)
    tm = TM_ROWS if n % TM_ROWS == 0 else TM_TAIL
    return pl.pallas_call(
        functools.partial(_ffn_kernel, n_chunks=n_chunks, fc=fc, final_norm=final_norm),
        grid=(n // tm,),
        in_specs=[pl.BlockSpec((tm, d), row), pl.BlockSpec((tm, pre.shape[1]), row),
                  _const_spec(wo.shape), _const_spec(bo.shape), _const_spec(gf.shape),
                  _const_spec(wg.shape), _const_spec(wu.shape), _const_spec(wd.shape), _const_spec(gfin.shape)],
        out_specs=pl.BlockSpec((tm, d), row),
        out_shape=jax.ShapeDtypeStruct((n, d), F32),
        compiler_params=_cparams(("parallel",)),
        name="outproj_swiglu",
    )(x, pre, wo, bo, gf, wg, wu, wd, gfin)


def _proj_kernel(x_ref, g_ref, wrow_ref, wt_ref, k_ref, v_ref, ki_ref, kb_ref, kib_ref, qt_ref, qit_ref, vt_ref,
                 wit_ref, *, n_kv, n_q, n_iq):
    xn = _rmsnorm(x_ref[...], g_ref[...]).astype(BF16)
    row = _dot(xn, wrow_ref[...])
    k_ref[...] = row[:, :n_kv]
    v_ref[...] = row[:, n_kv:2 * n_kv]
    ki_ref[...] = row[:, 2 * n_kv:]
    kb_ref[...] = row[:, :n_kv].astype(BF16)
    kib_ref[...] = row[:, 2 * n_kv:].astype(BF16)
    tt = _dot_nt(wt_ref[...], xn)
    for b in range(x_ref.shape[0] // QB):
        blk = tt[:, b * QB:(b + 1) * QB]
        qt_ref[b] = blk[:n_q].astype(BF16)
        qit_ref[b] = blk[n_q:n_q + n_iq].astype(BF16)
        vt_ref[b] = blk[n_q + n_iq:n_q + n_iq + n_kv].astype(BF16)
        wit_ref[b] = blk[n_q + n_iq + n_kv:n_q + n_iq + n_kv + IDX_HEADS] * (IDX_HEADS ** -0.5)


def _proj_call(x, g, wrow, wt):
    n, d = x.shape
    n_kv = N_KV_HEADS * HEAD_DIM
    n_q = N_HEADS * HEAD_DIM
    n_iq = IDX_HEADS * IDX_DIM
    row = lambda i: (i, 0)
    tm = TM_ROWS if n % TM_ROWS == 0 else TM_TAIL
    feat_spec = lambda f: pl.BlockSpec((tm // QB, f, QB), lambda i: (i, 0, 0))
    feat_shape = lambda f, dt: jax.ShapeDtypeStruct((n // QB, f, QB), dt)
    return pl.pallas_call(
        functools.partial(_proj_kernel, n_kv=n_kv, n_q=n_q, n_iq=n_iq),
        grid=(n // tm,),
        in_specs=[pl.BlockSpec((tm, d), row), _const_spec(g.shape), _const_spec(wrow.shape),
                  _const_spec(wt.shape)],
        out_specs=[pl.BlockSpec((tm, n_kv), row), pl.BlockSpec((tm, n_kv), row),
                   pl.BlockSpec((tm, IDX_DIM), row), pl.BlockSpec((tm, n_kv), row),
                   pl.BlockSpec((tm, IDX_DIM), row), feat_spec(n_q), feat_spec(n_iq), feat_spec(n_kv),
                   feat_spec(IDX_HEADS)],
        out_shape=[jax.ShapeDtypeStruct((n, n_kv), F32), jax.ShapeDtypeStruct((n, n_kv), F32),
                   jax.ShapeDtypeStruct((n, IDX_DIM), F32), jax.ShapeDtypeStruct((n, n_kv), BF16),
                   jax.ShapeDtypeStruct((n, IDX_DIM), BF16), feat_shape(n_q, BF16), feat_shape(n_iq, BF16),
                   feat_shape(n_kv, BF16), feat_shape(IDX_HEADS, F32)],
        compiler_params=_cparams(("parallel",)),
        name="attn_proj",
    )(x, g, wrow, wt)


def _bias_kernel(tab_ref, tb_ref):
    h = pl.program_id(0)
    d = pl.program_id(1)
    kj = lax.broadcasted_iota(I32, (QB, QB), 0)
    qi = lax.broadcasted_iota(I32, (QB, QB), 1)
    rel = (d - 1) * QB + kj - qi
    n = jnp.abs(rel)
    large = jnp.full((QB, QB), N_BUCKETS // 4, I32)
    for s in BUCKET_STEPS:
        large = large + jnp.where(n >= s, 1, 0)
    bucket = jnp.where(rel > 0, N_BUCKETS // 2, 0) + jnp.where(n < N_BUCKETS // 4, n, large)
    val = jnp.zeros((QB, QB), F32)
    for b in range(N_BUCKETS):
        val = jnp.where(bucket == b, tab_ref[b, h], val)
    tb_ref[0] = val - tab_ref[FAR_BUCKET, h]


def _bias_call(table):
    return pl.pallas_call(
        _bias_kernel,
        grid=(N_HEADS, 2),
        in_specs=[pl.BlockSpec(memory_space=pltpu.SMEM)],
        out_specs=pl.BlockSpec((1, QB, QB), lambda h, d: (h, d, 0)),
        out_shape=jax.ShapeDtypeStruct((N_HEADS, 2 * QB, QB), F32),
        compiler_params=_cparams(("arbitrary", "arbitrary")),
        name="rel_bias_tiles",
    )(table)


def _attn_kernel(kend_ref, qt_ref, qit_ref, wit_ref, *refs, diag_off, kstart, k_sel, has_prev, row_bits, build):
    if build:
        (kb_ref, vt_ref, kib_ref, k0_ref, v0_ref, ki0_ref, tb_ref, o_ref, key_scr, hi_scr, lo_scr, lo2_scr,
         ls_scr, oacc_scr, kf_scr, vtf_scr, kif_scr) = refs
        seq = kb_ref.shape[0]
        tkf = kf_scr.shape[1]

        @pl.when(pl.program_id(1) == 0)
        def _():
            kif_scr[0:QB, :] = ki0_ref[...]
            kif_scr[QB:QB + seq, :] = kib_ref[...]
            kif_scr[QB + seq:tkf, :] = jnp.zeros((tkf - QB - seq, IDX_DIM), BF16)
            for g in range(N_KV_HEADS):
                cols = slice(g * HEAD_DIM, (g + 1) * HEAD_DIM)
                kf_scr[g, 0:QB, :] = k0_ref[:, cols]
                for c in range(seq // SEL_TILE):
                    kf_scr[g, QB + c * SEL_TILE:QB + (c + 1) * SEL_TILE, :] = kb_ref[c * SEL_TILE:(c + 1) * SEL_TILE,
                                                                                     cols]
                kf_scr[g, QB + seq:tkf, :] = jnp.zeros((tkf - QB - seq, HEAD_DIM), BF16)
                vrows = slice(g * V_ROWS, g * V_ROWS + HEAD_DIM)
                vtf_scr[vrows, 0:QB] = v0_ref[cols, :]
                for j in range(seq // QB):
                    vtf_scr[vrows, QB + j * QB:QB + (j + 1) * QB] = vt_ref[j, cols, :]
                vtf_scr[vrows, QB + seq:tkf] = jnp.zeros((HEAD_DIM, tkf - QB - seq), BF16)
                extra = lax.broadcasted_iota(I32, (V_ROWS - HEAD_DIM, tkf), 0) == 0
                vtf_scr[g * V_ROWS + HEAD_DIM:(g + 1) * V_ROWS, :] = jnp.where(extra, 1.0, 0.0).astype(BF16)

        k_rows = lambda g, r0, rows: kf_scr[g, pl.ds(r0, rows), :]
        v_cols = lambda g, r0, rows: vtf_scr[g * V_ROWS:(g + 1) * V_ROWS, pl.ds(r0, rows)]
        ki_rows = lambda r0, rows: kif_scr[pl.ds(r0, rows), :]
    else:
        (kf_ref, vtf_ref, kif_ref, tb_ref, o_ref, key_scr, hi_scr, lo_scr, lo2_scr, ls_scr,
         oacc_scr) = refs
        k_rows = lambda g, r0, rows: kf_ref[0, g, pl.ds(r0, rows), :]
        v_cols = lambda g, r0, rows: vtf_ref[0, g * V_ROWS:(g + 1) * V_ROWS, pl.ds(r0, rows)]
        ki_rows = lambda r0, rows: kif_ref[0, pl.ds(r0, rows), :]
    diag = pl.program_id(1) + diag_off
    nt = diag + 1
    nst = (nt + SEL_TILE // QB - 1) // (SEL_TILE // QB)
    kend = kend_ref[0]
    qi_cat = jnp.concatenate([qit_ref[0, h * IDX_DIM:(h + 1) * IDX_DIM, :] for h in range(IDX_HEADS)], axis=1)
    wi = wit_ref[0]
    sel_iota = lax.broadcasted_iota(I32, (SEL_TILE, QB), 0)

    def tile_start(t):
        return pl.multiple_of(t * QB, QB)

    def sel_start(t):
        return pl.multiple_of(t * SEL_TILE, SEL_TILE)

    def score_rows(r0):
        s = _dot(ki_rows(r0, SEL_TILE), qi_cat)
        sc = jnp.zeros((SEL_TILE, QB), F32)
        for h in range(IDX_HEADS):
            sc = sc + wi[h:h + 1, :] * jnp.maximum(s[:, h * QB:(h + 1) * QB], 0.0)
        bits = lax.bitcast_convert_type(sc + 0.0, I32)
        key = jnp.where(bits >= 0, bits, bits ^ 0x7FFFFFFF)
        rows = r0 + sel_iota
        adm = (rows >= kstart) & (rows < kend)
        key = jnp.where(adm, key, INT_MIN)
        key_scr[pl.ds(r0, SEL_TILE), :] = key
        hi_scr[pl.ds(r0, SEL_TILE), :] = (key >> 16).astype(I16)
        lo_scr[pl.ds(r0, SEL_TILE), :] = ((key & 0xFFFF) + I16_MIN).astype(I16)

    pair_rows = 2 * SEL_TILE

    def score_pair(t, carry):
        r0 = pl.multiple_of(t * pair_rows, pair_rows)
        score_rows(r0)
        score_rows(r0 + SEL_TILE)
        return carry

    lax.fori_loop(0, (nst + 1) // 2, score_pair, 0)

    pair_vregs = pair_rows // PACK16

    def as_packed(v):
        return jnp.broadcast_to(v, (PACK16, QB)).astype(I16)

    def threshold(n_pair):
        def count16(src_scr, cand):
            c16 = as_packed(cand)[None]
            acc = jnp.zeros((PACK16, QB), BF16)
            for t in range(n_pair):
                v = src_scr[t * pair_rows:(t + 1) * pair_rows, :].reshape(pair_vregs, PACK16, QB)
                ind = jnp.where(v >= c16, jnp.ones((), BF16), jnp.zeros((), BF16))
                parts = [ind[i] for i in range(pair_vregs)]
                while len(parts) > 1:
                    parts = [parts[i] + parts[i + 1] for i in range(0, len(parts), 2)]
                acc = acc + parts[0]
            return acc.astype(F32).sum(axis=0, keepdims=True)

        def search16(src_scr, base, c_start):
            def step(b, carry):
                t_acc, c_acc = carry
                cand = t_acc + lax.shift_left(jnp.int32(1), 15 - b)
                c = base + count16(src_scr, cand)
                ok = c >= k_sel
                return jnp.where(ok, cand, t_acc), jnp.where(ok, c, c_acc)

            return lax.fori_loop(0, 16, step, (jnp.full((1, QB), I16_MIN, I32), c_start))

        thr_hi, c_hi = search16(hi_scr, 0.0, jnp.full((1, QB), n_pair * pair_rows, F32))
        above = jnp.where(thr_hi == -I16_MIN - 1, 0.0, count16(hi_scr, thr_hi + 1))
        hi16 = as_packed(thr_hi)[None]
        for t in range(n_pair):
            rows = slice(t * pair_rows, (t + 1) * pair_rows)
            hi = hi_scr[rows, :].reshape(pair_vregs, PACK16, QB)
            lo = lo_scr[rows, :].reshape(pair_vregs, PACK16, QB)
            lo2_scr[rows, :] = jnp.where(hi == hi16, lo, jnp.full((), I16_MIN, I16)).reshape(pair_rows, QB)
        thr_lo, c_ge = search16(lo2_scr, above, c_hi)
        return thr_hi * 65536 + (thr_lo - I16_MIN), c_ge

    def count(pred):
        def body(t, acc):
            r0 = sel_start(t)
            ind = jnp.where(pred(key_scr[pl.ds(r0, SEL_TILE), :], r0 + sel_iota), 1, 0)
            return acc + ind.reshape(SEL_TILE // SUBLANES, SUBLANES, QB).sum(axis=0)
        acc = lax.fori_loop(0, nst, body, jnp.zeros((SUBLANES, QB), I32))
        return acc.sum(axis=0, keepdims=True)

    max_pairs = hi_scr.shape[0] // pair_rows
    thr, c_ge = lax.switch((nst + 1) // 2 - 1, [functools.partial(threshold, n) for n in range(1, max_pairs + 1)])

    tied = (c_ge > k_sel) & (thr > INT_MIN)
    big = jnp.full((1, QB), 2 ** row_bits, I32)

    def tie_limit():
        need = k_sel - count(lambda kt, rows: kt > thr)

        def lim_step(b, lim):
            cand = lim + lax.shift_left(jnp.int32(1), row_bits - 1 - b)
            c = count(lambda kt, rows: (kt == thr) & (rows < cand))
            return jnp.where(c < need, cand, lim)

        lim = lax.fori_loop(0, row_bits, lim_step, jnp.zeros((1, QB), I32))
        return jnp.where(tied, lim, big)

    rlim = lax.cond(jnp.max(tied.astype(I32)) > 0, tie_limit, lambda: big)
    rlim = jnp.where(thr > INT_MIN, rlim, -1)

    gq = GROUP * QB
    qg = [jnp.concatenate([qt_ref[0, (g * GROUP + r) * HEAD_DIM:(g * GROUP + r + 1) * HEAD_DIM, :]
                           for r in range(GROUP)], axis=1) for g in range(N_KV_HEADS)]

    def logits_rows(r0, rows, m8, bias_of_head):
        kt = key_scr[pl.ds(r0, rows), :]
        keep = (kt > thr) | ((kt == thr) & (sel_iota[:rows] <= rlim - r0))
        new_m8 = []
        for g in range(N_KV_HEADS):
            lg = _dot(k_rows(g, r0, rows), qg[g])
            parts = []
            for r in range(GROUP):
                part = lg[:, r * QB:(r + 1) * QB]
                if bias_of_head is not None:
                    part = part + bias_of_head(g * GROUP + r)
                parts.append(jnp.where(keep, part, NEG_INF))
            lg = jnp.concatenate(parts, axis=1)
            ls_scr[pl.ds(r0, rows), g * gq:(g + 1) * gq] = lg
            new_m8.append(jnp.maximum(m8[g], lg.reshape(rows // SUBLANES, SUBLANES, gq).max(axis=0)))
        return tuple(new_m8)

    far_bias = None
    m8 = tuple(jnp.full((SUBLANES, gq), NEG_INF, F32) for _ in range(N_KV_HEADS))
    n_far = diag - 1 if has_prev else diag

    def far_pair(t, m8):
        r0 = pl.multiple_of(t * pair_rows, pair_rows)
        return logits_rows(r0 + SEL_TILE, SEL_TILE, logits_rows(r0, SEL_TILE, m8, far_bias), far_bias)

    m8 = lax.fori_loop(0, n_far // 4, far_pair, m8)
    m8 = lax.cond(n_far % 4 >= 2, lambda m: logits_rows(sel_start(n_far // 4 * 2), SEL_TILE, m, far_bias),
                  lambda m: m, m8)
    m8 = lax.cond(n_far % 2 == 1, lambda m: logits_rows(tile_start(n_far - 1), QB, m, far_bias), lambda m: m, m8)
    if has_prev:
        m8 = logits_rows(tile_start(diag - 1), 2 * QB, m8, lambda h: tb_ref[h])
    else:
        m8 = logits_rows(tile_start(diag), QB, m8, lambda h: tb_ref[h, QB:, :])
    m = [jnp.max(m8[g], axis=0, keepdims=True) for g in range(N_KV_HEADS)]

    @pl.when(nt % 2 == 1)
    def _():
        ls_scr[pl.ds(tile_start(nt), QB), :] = jnp.full((QB, N_HEADS * QB), NEG_INF, F32)

    oacc_scr[...] = jnp.zeros(oacc_scr.shape, F32)

    def pv_rows(r0):
        for g in range(N_KV_HEADS):
            p = jnp.exp(ls_scr[pl.ds(r0, SEL_TILE), g * gq:(g + 1) * gq] - m[g])
            oacc_scr[g] += _dot(v_cols(g, r0, SEL_TILE), p.astype(BF16))

    def pv_pair(t, carry):
        r0 = pl.multiple_of(t * pair_rows, pair_rows)
        pv_rows(r0)
        pv_rows(r0 + SEL_TILE)
        return carry

    lax.fori_loop(0, nst // 2, pv_pair, 0)

    @pl.when(nst % 2 == 1)
    def _():
        pv_rows(sel_start(nst - 1))
    out_rows = []
    for g in range(N_KV_HEADS):
        o_g = oacc_scr[g, :HEAD_DIM] / oacc_scr[g, HEAD_DIM:HEAD_DIM + 1]
        out_rows.extend(o_g[:, r * QB:(r + 1) * QB] for r in range(GROUP))
    o_ref[...] = jnp.concatenate(out_rows, axis=0).T.astype(BF16)


def _attn_call(kend, qt, qit, wit, frames, tb, *, n_seq, build, n_out_rows, diag_off, kstart, k_sel, name):
    n_qb = kend.shape[0]
    kv_dim = N_KV_HEADS * HEAD_DIM
    if build:
        seq = frames[0].shape[0] // n_seq
        tkf = -(-(QB + seq) // (2 * SEL_TILE)) * 2 * SEL_TILE
        frame_specs = [pl.BlockSpec((seq, kv_dim), lambda s, i: (s, 0)),
                       pl.BlockSpec((seq // QB, kv_dim, QB), lambda s, i: (s, 0, 0)),
                       pl.BlockSpec((seq, IDX_DIM), lambda s, i: (s, 0))] + [_const_spec(a.shape) for a in frames[3:]]
        frame_scratch = [pltpu.VMEM((N_KV_HEADS, tkf, HEAD_DIM), BF16), pltpu.VMEM((N_KV_HEADS * V_ROWS, tkf), BF16),
                         pltpu.VMEM((tkf, IDX_DIM), BF16)]
    else:
        kf, vtf, kif = frames
        tkf = kf.shape[2]
        assert vtf.shape[2] == tkf and kif.shape[1] == tkf
        frame_specs = [pl.BlockSpec((1, N_KV_HEADS, tkf, HEAD_DIM), lambda s, i: (s, 0, 0, 0)),
                       pl.BlockSpec((1, N_KV_HEADS * V_ROWS, tkf), lambda s, i: (s, 0, 0)),
                       pl.BlockSpec((1, tkf, IDX_DIM), lambda s, i: (s, 0, 0))]
        frame_scratch = []
    assert tkf % (2 * SEL_TILE) == 0
    assert tkf // PACK16 <= 256, "per-slot bf16 counts must stay exact"
    has_prev = diag_off >= 1
    row_bits = max(1, (tkf - 1).bit_length())
    feat = lambda a: pl.BlockSpec((1, a.shape[1], QB), lambda s, i: (s * n_qb + i, 0, 0))
    return pl.pallas_call(
        functools.partial(_attn_kernel, diag_off=diag_off, kstart=kstart, k_sel=k_sel, has_prev=has_prev,
                          row_bits=row_bits, build=build),
        grid=(n_seq, n_qb),
        in_specs=[pl.BlockSpec((1, 1, QB), lambda s, i: (i, 0, 0)), feat(qt), feat(qit), feat(wit)] + frame_specs
                 + [_const_spec(tb.shape)],
        out_specs=pl.BlockSpec((QB, N_HEADS * HEAD_DIM), lambda s, i: (s * n_qb + i, 0)),
        out_shape=jax.ShapeDtypeStruct((n_out_rows, N_HEADS * HEAD_DIM), BF16),
        scratch_shapes=[pltpu.VMEM((tkf, QB), I32)] + [pltpu.VMEM((tkf, QB), I16)] * 3
                       + [pltpu.VMEM((tkf, N_HEADS * QB), F32), pltpu.VMEM((N_KV_HEADS, V_ROWS, GROUP * QB), F32)]
                       + frame_scratch,
        compiler_params=_cparams(("arbitrary", "arbitrary")),
        name=name,
    )(kend, qt, qit, wit, *frames, tb)


def _softplus(x):
    return jnp.maximum(x, 0.0) + jnp.log1p(jnp.exp(-jnp.abs(x)))


def _mixb_kernel(x_ref, h0_ref, buf_ref, g_ref, wy_ref, by_ref, wx_ref, bx_ref, cw_ref, cb_ref, wr_ref, br_ref,
                 wi_ref, bi_ref, lam_ref, *rest, tm):
    pre_ref, hlast_ref, tail_ref, xe_scr, h_scr = rest[-5:]

    @pl.when(pl.program_id(1) == 0)
    def _():
        xe_scr[0:SUBLANES, :] = buf_ref[0]
        h_scr[...] = h0_ref[0]

    xn = _rmsnorm(x_ref[...], g_ref[...]).astype(BF16)
    y = jax.nn.gelu(_dot(xn, wy_ref[...]) + by_ref[...])
    xe_scr[SUBLANES:SUBLANES + tm, :] = _dot(xn, wx_ref[...]) + bx_ref[...]
    xc = cb_ref[...] + cw_ref[0:1, :] * xe_scr[SUBLANES - 3:SUBLANES - 3 + tm, :]
    for j in range(1, CONV_B):
        xc = xc + cw_ref[j:j + 1, :] * xe_scr[SUBLANES - 3 + j:SUBLANES - 3 + j + tm, :]
    xcb = xc.astype(BF16)
    d_rnn = xc.shape[1]
    blk = d_rnn // LRU_BLOCKS
    r_pre = jnp.concatenate([_dot(xcb[:, n * blk:(n + 1) * blk], wr_ref[n]) for n in range(LRU_BLOCKS)], axis=1)
    i_pre = jnp.concatenate([_dot(xcb[:, n * blk:(n + 1) * blk], wi_ref[n]) for n in range(LRU_BLOCKS)], axis=1)
    r = jax.nn.sigmoid(r_pre + br_ref[...])
    ig = jax.nn.sigmoid(i_pre + bi_ref[...])
    log_a = -LRU_C * r * _softplus(-lam_ref[...])
    a = jnp.exp(log_a)
    u = jnp.sqrt(jnp.tanh(-log_a) * (1.0 + a * a)) * (ig * xc)
    a = a.reshape(tm // SUBLANES, SUBLANES, d_rnn)
    u = u.reshape(tm // SUBLANES, SUBLANES, d_rnn)
    row_in_group = lax.broadcasted_iota(I32, a.shape, 1)
    s = 1
    while s < SUBLANES:
        a_sh = jnp.where(row_in_group >= s, pltpu.roll(a, s, 1), 1.0)
        u_sh = jnp.where(row_in_group >= s, pltpu.roll(u, s, 1), 0.0)
        u = a * u_sh + u
        a = a * a_sh
        s *= 2
    carry = h_scr[...]
    h_groups = []
    for grp in range(tm // SUBLANES):
        h_grp = a[grp] * carry + u[grp]
        carry = h_grp[SUBLANES - 1:SUBLANES, :]
        h_groups.append(h_grp)
    h = jnp.concatenate(h_groups, axis=0)
    pre_ref[...] = (h * y).astype(BF16)
    h_scr[...] = carry
    hlast_ref[0] = carry
    tail = xe_scr[tm:tm + SUBLANES, :]
    xe_scr[0:SUBLANES, :] = tail
    tail_ref[0] = tail


def _mixc_kernel(x_ref, buf_ref, g_ref, win_ref, cw_ref, *rest, tm):
    pre_ref, tail_ref, pe_scr = rest[-3:]

    @pl.when(pl.program_id(1) == 0)
    def _():
        pe_scr[0:SUBLANES, :] = buf_ref[0]

    d = x_ref.shape[1]
    xn = _rmsnorm(x_ref[...], g_ref[...]).astype(BF16)
    z = _dot(xn, win_ref[...])
    pe_scr[SUBLANES:SUBLANES + tm, :] = z[:, d:2 * d] * z[:, 2 * d:]
    conv = cw_ref[0:1, :] * pe_scr[SUBLANES - 2:SUBLANES - 2 + tm, :]
    for j in range(1, CONV_C):
        conv = conv + cw_ref[j:j + 1, :] * pe_scr[SUBLANES - 2 + j:SUBLANES - 2 + j + tm, :]
    pre_ref[...] = (z[:, :d] * conv).astype(BF16)
    tail = pe_scr[tm:tm + SUBLANES, :]
    pe_scr[0:SUBLANES, :] = tail
    tail_ref[0] = tail


def _shared_out(buf, index):
    if buf is None:
        return {"specs": [], "args": (), "aliases": {}}
    return {"specs": [pl.BlockSpec(memory_space=pl.ANY)], "args": (buf,), "aliases": {index: 0}}


def _seq_specs(n_seq, t_len, row_off, d, tm_max):
    tm = tm_max if t_len % tm_max == 0 else min(TM_SEQ, t_len)
    n_tt = t_len // tm
    off = row_off // tm
    xrow = pl.BlockSpec((tm, d), lambda s, j: (off + s * n_tt + j, 0))
    state = lambda rows: pl.BlockSpec((1, rows, d), lambda s, j: (s, 0, 0))
    return tm, n_tt, xrow, state


def _mixb_call(x, pre_buf, h0, buf, g, wy, by, wx, bx, cw, cb, wr, br, wi, bi, lam, *, n_seq, t_len, row_off,
               name):
    d = x.shape[1]
    tm, n_tt, xrow, state = _seq_specs(n_seq, t_len, row_off, d, TM_SEQ)
    consts = (g, wy, by, wx, bx, cw, cb, wr, br, wi, bi, lam)
    shared = _shared_out(pre_buf, 3 + len(consts))
    return pl.pallas_call(
        functools.partial(_mixb_kernel, tm=tm),
        grid=(n_seq, n_tt),
        in_specs=[xrow, state(1), state(SUBLANES)] + [_const_spec(c.shape) for c in consts] + shared["specs"],
        out_specs=[xrow, state(1), state(SUBLANES)],
        out_shape=[jax.ShapeDtypeStruct((x.shape[0], d), BF16), jax.ShapeDtypeStruct((n_seq, 1, d), F32),
                   jax.ShapeDtypeStruct((n_seq, SUBLANES, d), F32)],
        scratch_shapes=[pltpu.VMEM((tm + SUBLANES, d), F32), pltpu.VMEM((1, d), F32)],
        input_output_aliases=shared["aliases"],
        compiler_params=_cparams(("arbitrary", "arbitrary")),
        name=name,
    )(x, h0, buf, *consts, *shared["args"])


def _mixc_call(x, pre_buf, buf, g, win, cw, *, n_seq, t_len, row_off, name):
    d = x.shape[1]
    tm, n_tt, xrow, state = _seq_specs(n_seq, t_len, row_off, d, TM_CONV)
    consts = (g, win, cw)
    shared = _shared_out(pre_buf, 2 + len(consts))
    return pl.pallas_call(
        functools.partial(_mixc_kernel, tm=tm),
        grid=(n_seq, n_tt),
        in_specs=[xrow, state(SUBLANES)] + [_const_spec(c.shape) for c in consts] + shared["specs"],
        out_specs=[xrow, state(SUBLANES)],
        out_shape=[jax.ShapeDtypeStruct((x.shape[0], d), BF16), jax.ShapeDtypeStruct((n_seq, SUBLANES, d), F32)],
        scratch_shapes=[pltpu.VMEM((tm + SUBLANES, d), F32)],
        input_output_aliases=shared["aliases"],
        compiler_params=_cparams(("arbitrary", "arbitrary")),
        name=name,
    )(x, buf, *consts, *shared["args"])


def _front_pad_rows(a, rows):
    return jnp.pad(a, ((0, 0), (rows - a.shape[1], 0), (0, 0)))


def _frame(parts, axis):
    f = jnp.concatenate(parts, axis=axis).astype(BF16)
    pad = [(0, 0)] * f.ndim
    pad[axis] = (0, -f.shape[axis] % (2 * SEL_TILE))
    return jnp.pad(f, pad)


def _k_frame(parts):
    f = _frame(parts, 1)
    n_seq, rows = f.shape[0], f.shape[1]
    return f.reshape(n_seq, rows, N_KV_HEADS, HEAD_DIM).transpose(0, 2, 1, 3)


def _v_frame(parts):
    f = _frame(parts, 2)
    n_seq, tkf = f.shape[0], f.shape[2]
    extra = jnp.zeros((n_seq, N_KV_HEADS, V_ROWS - HEAD_DIM, tkf), BF16).at[:, :, 0].set(1)
    f = jnp.concatenate([f.reshape(n_seq, N_KV_HEADS, HEAD_DIM, tkf), extra], axis=2)
    return f.reshape(n_seq, N_KV_HEADS * V_ROWS, tkf)


def _cols(a, start, n):
    b0, b1 = start // QB, -(-(start + n) // QB)
    c = a[b0:b1].transpose(1, 0, 2).reshape(a.shape[1], (b1 - b0) * QB)
    return c[:, start - b0 * QB:start - b0 * QB + n]


def _seq_major(a, n_seq, q_len):
    return a.reshape(a.shape[0], n_seq, q_len).transpose(1, 0, 2)


def _lane_pad_cols(a, n_seq, q_len):
    return jnp.pad(_seq_major(a, n_seq, q_len), ((0, 0), (0, 0), (0, QB - q_len)))


def kernel(x_prompt, x_sample, cache_k, cache_v, cache_kidx, state_h, state_conv_b, state_conv_c, meta_tokens,
           rel_bias, norm_mix, norm_ffn, norm_final, a_wq, a_wk, a_wv, a_wo, a_wiq, a_wik, a_wiw, b_wy, b_by, b_wx,
           b_bx, b_conv_w, b_conv_b, b_wr, b_br, b_wi, b_bi, b_lam, b_wo, b_bo, c_win, c_conv_w, c_wo, f_wg, f_wu,
           f_wd):
    bp, seq, d = x_prompt.shape
    bs, dec = x_sample.shape[0], x_sample.shape[1]
    past = cache_k.shape[2]
    n_meta = meta_tokens.shape[0]
    depth = norm_mix.shape[0]
    assert seq % QB == 0 and seq % TM_SEQ == 0 and past % QB == 0 and dec <= QB and n_meta <= QB
    assert dec % PACK16 == 0 and n_meta % PACK16 == 0 and (bs * dec) % n_meta == 0

    n_main, n_samp = bp * seq, bs * dec
    n_tail = -(-(n_samp + n_meta) // TM_TAIL) * TM_TAIL
    x_main = x_prompt.reshape(n_main, d)
    x_tail = jnp.concatenate([x_sample.reshape(n_samp, d), meta_tokens.astype(F32),
                              jnp.zeros((n_tail - n_samp - n_meta, d), F32)], axis=0)
    tail_fill = jnp.zeros((n_tail - n_samp - n_meta, d), BF16)

    k_sel_p = min(TOP_K_MAX, seq // 4)
    k_sel_s = min(TOP_K_MAX, (past + dec) // 4)
    n_qb = seq // QB
    kv_dim = N_KV_HEADS * HEAD_DIM

    tau = jnp.arange(seq, dtype=I32)
    kend_main = (QB + CHUNK * (tau // CHUNK + 1)).reshape(n_qb, 1, QB)
    lane = jnp.arange(QB, dtype=I32)
    kend_samp = jnp.minimum(past + dec, CHUNK * ((past + jnp.minimum(lane, dec - 1)) // CHUNK + 1)).reshape(1, 1, QB)
    kend_meta = jnp.full((1, 1, QB), n_meta, I32)

    tb = _bias_call(rel_bias.astype(F32))
    zeros_bias = jnp.zeros((1, d), F32)
    gfin = norm_final.reshape(1, d).astype(F32)

    new = {name: [] for name in ("k_p", "v_p", "ki_p", "h_p", "cb_p", "cc_p", "k_s", "v_s", "ki_s", "h_s", "cb_s",
                                 "cc_s")}
    for l in range(depth):
        s = l // N_MIXERS
        g_mix = norm_mix[l].reshape(1, d)
        if l % N_MIXERS == 0:
            wrow = jnp.concatenate([a_wk[s], a_wv[s], a_wik[s]], axis=1).astype(BF16)
            wt = jnp.concatenate([a_wq[s].T * HEAD_DIM ** -0.5, a_wiq[s].T * IDX_DIM ** -0.5, a_wv[s].T, a_wiw[s].T,
                                  jnp.zeros((SUBLANES, d), F32)], axis=0).astype(BF16)
            k, v, ki, kb, kib, qt, qit, vt, wit = _proj_call(x_main, g_mix, wrow, wt)
            k_t, v_t, ki_t, _, _, qt_t, qit_t, vt_t, wit_t = _proj_call(x_tail, g_mix, wrow, wt)

            k_meta, v_meta, ki_meta = (a[n_samp:n_samp + n_meta] for a in (k_t, v_t, ki_t))
            k_main, v_main, ki_main = (a.reshape(bp, seq, -1) for a in (k, v, ki))
            k_samp, v_samp, ki_samp = (a[:n_samp].reshape(bs, dec, -1) for a in (k_t, v_t, ki_t))
            bc = lambda a: jnp.broadcast_to(a[None], (bp,) + a.shape)
            new["k_p"].append(jnp.concatenate([bc(k_meta), k_main], axis=1).reshape(bp, n_meta + seq, N_KV_HEADS,
                                                                                   HEAD_DIM))
            new["v_p"].append(jnp.concatenate([bc(v_meta), v_main], axis=1).reshape(bp, n_meta + seq, N_KV_HEADS,
                                                                                   HEAD_DIM))
            new["ki_p"].append(jnp.concatenate([bc(ki_meta), ki_main], axis=1))
            new["k_s"].append(k_samp.reshape(bs, dec, N_KV_HEADS, HEAD_DIM))
            new["v_s"].append(v_samp.reshape(bs, dec, N_KV_HEADS, HEAD_DIM))
            new["ki_s"].append(ki_samp)

            samp_cols = lambda a: _cols(a, 0, n_samp)
            meta_cols = lambda a: _cols(a, n_samp, n_meta)
            vt_meta = meta_cols(vt_t)
            lead = lambda a, axis: jnp.pad(a, [(QB - n_meta, 0) if ax == axis else (0, 0)
                                               for ax in range(2)]).astype(BF16)
            pre_main = _attn_call(
                kend_main, qt, qit, wit, (kb, vt, kib, lead(k_meta, 0), lead(vt_meta, 1), lead(ki_meta, 0)), tb,
                n_seq=bp, build=True,
                n_out_rows=n_main, diag_off=1, kstart=QB - n_meta, k_sel=k_sel_p, name="dsa_attn_prompt")
            vtf_samp = _v_frame([cache_v[s].reshape(bs, past, kv_dim).transpose(0, 2, 1),
                                 _seq_major(samp_cols(vt_t), bs, dec)])
            o_samp = _attn_call(
                kend_samp, _lane_pad_cols(samp_cols(qt_t), bs, dec),
                _lane_pad_cols(samp_cols(qit_t), bs, dec), _lane_pad_cols(samp_cols(wit_t), bs, dec),
                (_k_frame([cache_k[s].reshape(bs, past, kv_dim), k_samp]), vtf_samp,
                 _frame([cache_kidx[s], ki_samp], 1)), tb, n_seq=bs, build=False,
                n_out_rows=bs * QB, diag_off=past // QB, kstart=0, k_sel=k_sel_s, name="dsa_attn_sample")
            o_meta = _attn_call(
                kend_meta, _lane_pad_cols(meta_cols(qt_t), 1, n_meta),
                _lane_pad_cols(meta_cols(qit_t), 1, n_meta), _lane_pad_cols(meta_cols(wit_t), 1, n_meta),
                (_k_frame([k_meta[None]]), _v_frame([vt_meta[None]]), _frame([ki_meta[None]], 1)), tb,
                n_seq=1, build=False, n_out_rows=QB, diag_off=0, kstart=0, k_sel=k_sel_p, name="dsa_attn_meta")
            pre_tail = jnp.concatenate([o_samp.reshape(bs, QB, d)[:, :dec].reshape(n_samp, d), o_meta[:n_meta],
                                        tail_fill], axis=0)
            wo, bo = a_wo[s].astype(BF16), zeros_bias
        elif l % N_MIXERS == 1:
            consts = (g_mix, b_wy[s].astype(BF16), b_by[s].reshape(1, d), b_wx[s].astype(BF16), b_bx[s].reshape(1, d),
                      jnp.pad(b_conv_w[s], ((0, SUBLANES - CONV_B), (0, 0))), b_conv_b[s].reshape(1, d),
                      b_wr[s].astype(BF16), b_br[s].reshape(1, d), b_wi[s].astype(BF16), b_bi[s].reshape(1, d),
                      b_lam[s].reshape(1, d))
            pre_tail = jnp.zeros((n_tail, d), BF16)
            pre_tail, h_m, tail_m = _mixb_call(x_tail, pre_tail, jnp.zeros((1, 1, d), F32),
                                               jnp.zeros((1, SUBLANES, d), F32), *consts, n_seq=1, t_len=n_meta,
                                               row_off=n_samp, name="rglru_meta")
            pre_tail, h_s, tail_s = _mixb_call(x_tail, pre_tail, state_h[s].reshape(bs, 1, d),
                                               _front_pad_rows(state_conv_b[s], SUBLANES), *consts, n_seq=bs,
                                               t_len=dec, row_off=0, name="rglru_sample")
            pre_main, h_p, tail_p = _mixb_call(x_main, None, jnp.broadcast_to(h_m, (bp, 1, d)),
                                               jnp.broadcast_to(tail_m, (bp, SUBLANES, d)), *consts, n_seq=bp,
                                               t_len=seq, row_off=0, name="rglru_prompt")
            new["h_p"].append(h_p.reshape(bp, d))
            new["cb_p"].append(tail_p[:, SUBLANES - (CONV_B - 1):])
            new["h_s"].append(h_s.reshape(bs, d))
            new["cb_s"].append(tail_s[:, SUBLANES - (CONV_B - 1):])
            wo, bo = b_wo[s].astype(BF16), b_bo[s].reshape(1, d)
        else:
            consts = (g_mix, c_win[s].astype(BF16), jnp.pad(c_conv_w[s], ((0, SUBLANES - CONV_C), (0, 0))))
            pre_tail = jnp.zeros((n_tail, d), BF16)
            pre_tail, tail_m = _mixc_call(x_tail, pre_tail, jnp.zeros((1, SUBLANES, d), F32), *consts, n_seq=1,
                                          t_len=n_meta, row_off=n_samp, name="sconv_meta")
            pre_tail, tail_s = _mixc_call(x_tail, pre_tail, _front_pad_rows(state_conv_c[s], SUBLANES), *consts,
                                          n_seq=bs, t_len=dec, row_off=0, name="sconv_sample")
            pre_main, tail_p = _mixc_call(x_main, None, jnp.broadcast_to(tail_m, (bp, SUBLANES, d)), *consts,
                                          n_seq=bp, t_len=seq, row_off=0, name="sconv_prompt")
            new["cc_p"].append(tail_p[:, SUBLANES - (CONV_C - 1):])
            new["cc_s"].append(tail_s[:, SUBLANES - (CONV_C - 1):])
            wo, bo = c_wo[s].astype(BF16), zeros_bias
        ffn = (wo, bo, norm_ffn[l].reshape(1, d), f_wg[l].astype(BF16), f_wu[l].astype(BF16), f_wd[l].astype(BF16),
               gfin, l == depth - 1)
        x_main = _ffn_call(x_main, pre_main, *ffn)
        x_tail = _ffn_call(x_tail, pre_tail, *ffn)

    y_prompt = x_main.reshape(bp, seq, d)
    y_sample = x_tail[:n_samp].reshape(bs, dec, d)
    st = lambda name: jnp.stack(new[name])
    return (y_prompt, y_sample, st("k_p"), st("v_p"), st("ki_p"), st("h_p"), st("cb_p"), st("cc_p"),
            st("k_s"), st("v_s"), st("ki_s"), st("h_s"), st("cb_s"), st("cc_s"))
```

```python
import functools

import jax
import jax.numpy as jnp
from jax import lax
from jax.experimental import pallas as pl
from jax.experimental.pallas import tpu as pltpu

F32 = jnp.float32
BF16 = jnp.bfloat16
I32 = jnp.int32
I16 = jnp.int16

CHUNK = 64
N_MIXERS = 3
N_HEADS = 16
N_KV_HEADS = 4
GROUP = N_HEADS // N_KV_HEADS
HEAD_DIM = 64
IDX_HEADS = 8
IDX_DIM = 64
TOP_K_MAX = 256
NEG_INF = -1e30
N_BUCKETS = 32
LRU_BLOCKS = 4
CONV_B = 4
CONV_C = 3
LRU_C = 8.0
RMS_EPS = 1e-6
BUCKET_STEPS = (12, 16, 23, 32, 46, 64, 91)
FAR_BUCKET = N_BUCKETS // 2 - 1

SUBLANES = 8
TM_ROWS = 512
TM_TAIL = 256
TM_SEQ = 256
TM_CONV = 512
QB = 128
SEL_TILE = 2 * QB
INT_MIN = -2 ** 31
I16_MIN = -2 ** 15
PACK16 = 2 * SUBLANES
V_ROWS = HEAD_DIM + PACK16
VMEM_LIMIT = 56 * 1024 * 1024


def _cparams(sem):
    return pltpu.CompilerParams(dimension_semantics=sem, vmem_limit_bytes=VMEM_LIMIT)


def _const_spec(shape):
    nd = len(shape)
    return pl.BlockSpec(shape, lambda *_: (0,) * nd, pipeline_mode=pl.Buffered(1))


def _rmsnorm(x, g):
    ms = jnp.mean(x * x, axis=-1, keepdims=True)
    return x * lax.rsqrt(ms + RMS_EPS) * g


def _dot(a, b):
    return jnp.dot(a, b, preferred_element_type=F32)


def _dot_nt(a, b):
    return lax.dot_general(a, b, (((1,), (1,)), ((), ())), preferred_element_type=F32)


def _ffn_kernel(x_ref, pre_ref, wo_ref, bo_ref, gf_ref, wg_ref, wu_ref, wd_ref, gfin_ref, out_ref, *, final_norm):
    x1 = x_ref[...] + _dot(pre_ref[...], wo_ref[...]) + bo_ref[...]
    xn = _rmsnorm(x1, gf_ref[...]).astype(BF16)
    gt = _dot(xn, wg_ref[...])
    hm = (gt * jax.nn.sigmoid(gt) * _dot(xn, wu_ref[...])).astype(BF16)
    acc = x1 + _dot(hm, wd_ref[...])
    if final_norm:
        acc = _rmsnorm(acc, gfin_ref[...])
    out_ref[...] = acc


def _ffn_call(x, pre, wo, bo, gf, wg, wu, wd, gfin, final_norm):
    n, d = x.shape
    row = lambda i: (i, 0)
    tm = TM_ROWS if n % TM_ROWS == 0 else TM_TAIL
    return pl.pallas_call(
        functools.partial(_ffn_kernel, final_norm=final_norm),
        grid=(n // tm,),
        in_specs=[pl.BlockSpec((tm, d), row), pl.BlockSpec((tm, pre.shape[1]), row),
                  _const_spec(wo.shape), _const_spec(bo.shape), _const_spec(gf.shape),
                  _const_spec(wg.shape), _const_spec(wu.shape), _const_spec(wd.shape), _const_spec(gfin.shape)],
        out_specs=pl.BlockSpec((tm, d), row),
        out_shape=jax.ShapeDtypeStruct((n, d), F32),
        compiler_params=_cparams(("parallel",)),
        name="outproj_swiglu",
    )(x, pre, wo, bo, gf, wg, wu, wd, gfin)


def _proj_kernel(x_ref, g_ref, wrow_ref, wt_ref, k_ref, v_ref, ki_ref, kb_ref, kib_ref, qt_ref, qit_ref, vt_ref,
                 wit_ref, *, n_kv, n_q, n_iq):
    xn = _rmsnorm(x_ref[...], g_ref[...]).astype(BF16)
    row = _dot(xn, wrow_ref[...])
    k_ref[...] = row[:, :n_kv]
    v_ref[...] = row[:, n_kv:2 * n_kv]
    ki_ref[...] = row[:, 2 * n_kv:]
    kb_ref[...] = row[:, :n_kv].astype(BF16)
    kib_ref[...] = row[:, 2 * n_kv:].astype(BF16)
    tt = _dot_nt(wt_ref[...], xn)
    for b in range(x_ref.shape[0] // QB):
        blk = tt[:, b * QB:(b + 1) * QB]
        qt_ref[b] = blk[:n_q].astype(BF16)
        qit_ref[b] = blk[n_q:n_q + n_iq].astype(BF16)
        vt_ref[b] = blk[n_q + n_iq:n_q + n_iq + n_kv].astype(BF16)
        wit_ref[b] = blk[n_q + n_iq + n_kv:n_q + n_iq + n_kv + IDX_HEADS] * (IDX_HEADS ** -0.5)


def _proj_call(x, g, wrow, wt):
    n, d = x.shape
    n_kv = N_KV_HEADS * HEAD_DIM
    n_q = N_HEADS * HEAD_DIM
    n_iq = IDX_HEADS * IDX_DIM
    row = lambda i: (i, 0)
    tm = TM_ROWS if n % TM_ROWS == 0 else TM_TAIL
    feat_spec = lambda f: pl.BlockSpec((tm // QB, f, QB), lambda i: (i, 0, 0))
    feat_shape = lambda f, dt: jax.ShapeDtypeStruct((n // QB, f, QB), dt)
    return pl.pallas_call(
        functools.partial(_proj_kernel, n_kv=n_kv, n_q=n_q, n_iq=n_iq),
        grid=(n // tm,),
        in_specs=[pl.BlockSpec((tm, d), row), _const_spec(g.shape), _const_spec(wrow.shape),
                  _const_spec(wt.shape)],
        out_specs=[pl.BlockSpec((tm, n_kv), row), pl.BlockSpec((tm, n_kv), row),
                   pl.BlockSpec((tm, IDX_DIM), row), pl.BlockSpec((tm, n_kv), row),
                   pl.BlockSpec((tm, IDX_DIM), row), feat_spec(n_q), feat_spec(n_iq), feat_spec(n_kv),
                   feat_spec(IDX_HEADS)],
        out_shape=[jax.ShapeDtypeStruct((n, n_kv), F32), jax.ShapeDtypeStruct((n, n_kv), F32),
                   jax.ShapeDtypeStruct((n, IDX_DIM), F32), jax.ShapeDtypeStruct((n, n_kv), BF16),
                   jax.ShapeDtypeStruct((n, IDX_DIM), BF16), feat_shape(n_q, BF16), feat_shape(n_iq, BF16),
                   feat_shape(n_kv, BF16), feat_shape(IDX_HEADS, F32)],
        compiler_params=_cparams(("parallel",)),
        name="attn_proj",
    )(x, g, wrow, wt)


def _bias_kernel(tab_ref, tb_ref):
    h = pl.program_id(0)
    d = pl.program_id(1)
    kj = lax.broadcasted_iota(I32, (QB, QB), 0)
    qi = lax.broadcasted_iota(I32, (QB, QB), 1)
    rel = (d - 1) * QB + kj - qi
    n = jnp.abs(rel)
    large = jnp.full((QB, QB), N_BUCKETS // 4, I32)
    for s in BUCKET_STEPS:
        large = large + jnp.where(n >= s, 1, 0)
    bucket = jnp.where(rel > 0, N_BUCKETS // 2, 0) + jnp.where(n < N_BUCKETS // 4, n, large)
    val = jnp.zeros((QB, QB), F32)
    for b in range(N_BUCKETS):
        val = jnp.where(bucket == b, tab_ref[b, h], val)
    tb_ref[0] = val - tab_ref[FAR_BUCKET, h]


def _bias_call(table):
    return pl.pallas_call(
        _bias_kernel,
        grid=(N_HEADS, 2),
        in_specs=[pl.BlockSpec(memory_space=pltpu.SMEM)],
        out_specs=pl.BlockSpec((1, QB, QB), lambda h, d: (h, d, 0)),
        out_shape=jax.ShapeDtypeStruct((N_HEADS, 2 * QB, QB), F32),
        compiler_params=_cparams(("arbitrary", "arbitrary")),
        name="rel_bias_tiles",
    )(table)


def _attn_kernel(kend_ref, qt_ref, qit_ref, wit_ref, *refs, diag_off, kstart, k_sel, has_prev, row_bits, build):
    if build:
        (kb_ref, vt_ref, kib_ref, k0_ref, v0_ref, ki0_ref, tb_ref, o_ref, key_scr, hi_scr, lo_scr, lo2_scr,
         ls_scr, oacc_scr, kf_scr, vtf_scr, kif_scr) = refs
        seq = kb_ref.shape[0]
        tkf = kf_scr.shape[1]

        @pl.when(pl.program_id(1) == 0)
        def _():
            kif_scr[0:QB, :] = ki0_ref[...]
            kif_scr[QB:QB + seq, :] = kib_ref[...]
            kif_scr[QB + seq:tkf, :] = jnp.zeros((tkf - QB - seq, IDX_DIM), BF16)
            for g in range(N_KV_HEADS):
                cols = slice(g * HEAD_DIM, (g + 1) * HEAD_DIM)
                kf_scr[g, 0:QB, :] = k0_ref[:, cols]
                for c in range(seq // SEL_TILE):
                    kf_scr[g, QB + c * SEL_TILE:QB + (c + 1) * SEL_TILE, :] = kb_ref[c * SEL_TILE:(c + 1) * SEL_TILE,
                                                                                     cols]
                kf_scr[g, QB + seq:tkf, :] = jnp.zeros((tkf - QB - seq, HEAD_DIM), BF16)
                vrows = slice(g * V_ROWS, g * V_ROWS + HEAD_DIM)
                vtf_scr[vrows, 0:QB] = v0_ref[cols, :]
                for j in range(seq // QB):
                    vtf_scr[vrows, QB + j * QB:QB + (j + 1) * QB] = vt_ref[j, cols, :]
                vtf_scr[vrows, QB + seq:tkf] = jnp.zeros((HEAD_DIM, tkf - QB - seq), BF16)
                extra = lax.broadcasted_iota(I32, (V_ROWS - HEAD_DIM, tkf), 0) == 0
                vtf_scr[g * V_ROWS + HEAD_DIM:(g + 1) * V_ROWS, :] = jnp.where(extra, 1.0, 0.0).astype(BF16)

        k_rows = lambda g, r0, rows: kf_scr[g, pl.ds(r0, rows), :]
        v_cols = lambda g, r0, rows: vtf_scr[g * V_ROWS:(g + 1) * V_ROWS, pl.ds(r0, rows)]
        ki_rows = lambda r0, rows: kif_scr[pl.ds(r0, rows), :]
    else:
        (kf_ref, vtf_ref, kif_ref, tb_ref, o_ref, key_scr, hi_scr, lo_scr, lo2_scr, ls_scr,
         oacc_scr) = refs
        k_rows = lambda g, r0, rows: kf_ref[0, g, pl.ds(r0, rows), :]
        v_cols = lambda g, r0, rows: vtf_ref[0, g * V_ROWS:(g + 1) * V_ROWS, pl.ds(r0, rows)]
        ki_rows = lambda r0, rows: kif_ref[0, pl.ds(r0, rows), :]
    diag = pl.program_id(1) + diag_off
    nt = diag + 1
    nst = (nt + SEL_TILE // QB - 1) // (SEL_TILE // QB)
    kend = kend_ref[0]
    qi_cat = jnp.concatenate([qit_ref[0, h * IDX_DIM:(h + 1) * IDX_DIM, :] for h in range(IDX_HEADS)], axis=1)
    wi = wit_ref[0]
    sel_iota = lax.broadcasted_iota(I32, (SEL_TILE, QB), 0)

    def tile_start(t):
        return pl.multiple_of(t * QB, QB)

    def sel_start(t):
        return pl.multiple_of(t * SEL_TILE, SEL_TILE)

    def score_rows(r0):
        s = _dot(ki_rows(r0, SEL_TILE), qi_cat)
        sc = jnp.zeros((SEL_TILE, QB), F32)
        for h in range(IDX_HEADS):
            sc = sc + wi[h:h + 1, :] * jnp.maximum(s[:, h * QB:(h + 1) * QB], 0.0)
        bits = lax.bitcast_convert_type(sc + 0.0, I32)
        key = jnp.where(bits >= 0, bits, bits ^ 0x7FFFFFFF)
        rows = r0 + sel_iota
        adm = (rows >= kstart) & (rows < kend)
        key = jnp.where(adm, key, INT_MIN)
        key_scr[pl.ds(r0, SEL_TILE), :] = key
        hi_scr[pl.ds(r0, SEL_TILE), :] = (key >> 16).astype(I16)
        lo_scr[pl.ds(r0, SEL_TILE), :] = ((key & 0xFFFF) + I16_MIN).astype(I16)

    pair_rows = 2 * SEL_TILE

    def score_pair(t, carry):
        r0 = pl.multiple_of(t * pair_rows, pair_rows)
        score_rows(r0)
        score_rows(r0 + SEL_TILE)
        return carry

    lax.fori_loop(0, (nst + 1) // 2, score_pair, 0)

    pair_vregs = pair_rows // PACK16

    def as_packed(v):
        return jnp.broadcast_to(v, (PACK16, QB)).astype(I16)

    def threshold(n_pair):
        def count16(src_scr, cand):
            c16 = as_packed(cand)[None]
            acc = jnp.zeros((PACK16, QB), BF16)
            for t in range(n_pair):
                v = src_scr[t * pair_rows:(t + 1) * pair_rows, :].reshape(pair_vregs, PACK16, QB)
                ind = jnp.where(v >= c16, jnp.ones((), BF16), jnp.zeros((), BF16))
                parts = [ind[i] for i in range(pair_vregs)]
                while len(parts) > 1:
                    parts = [parts[i] + parts[i + 1] for i in range(0, len(parts), 2)]
                acc = acc + parts[0]
            return acc.astype(F32).sum(axis=0, keepdims=True)

        def search16(src_scr, base, c_start):
            def step(b, carry):
                t_acc, c_acc = carry
                cand = t_acc + lax.shift_left(jnp.int32(1), 15 - b)
                c = base + count16(src_scr, cand)
                ok = c >= k_sel
                return jnp.where(ok, cand, t_acc), jnp.where(ok, c, c_acc)

            return lax.fori_loop(0, 16, step, (jnp.full((1, QB), I16_MIN, I32), c_start))

        thr_hi, c_hi = search16(hi_scr, 0.0, jnp.full((1, QB), n_pair * pair_rows, F32))
        above = jnp.where(thr_hi == -I16_MIN - 1, 0.0, count16(hi_scr, thr_hi + 1))
        hi16 = as_packed(thr_hi)[None]
        for t in range(n_pair):
            rows = slice(t * pair_rows, (t + 1) * pair_rows)
            hi = hi_scr[rows, :].reshape(pair_vregs, PACK16, QB)
            lo = lo_scr[rows, :].reshape(pair_vregs, PACK16, QB)
            lo2_scr[rows, :] = jnp.where(hi == hi16, lo, jnp.full((), I16_MIN, I16)).reshape(pair_rows, QB)
        thr_lo, c_ge = search16(lo2_scr, above, c_hi)
        return thr_hi * 65536 + (thr_lo - I16_MIN), c_ge

    def count(pred):
        def body(t, acc):
            r0 = sel_start(t)
            ind = jnp.where(pred(key_scr[pl.ds(r0, SEL_TILE), :], r0 + sel_iota), 1, 0)
            return acc + ind.reshape(SEL_TILE // SUBLANES, SUBLANES, QB).sum(axis=0)
        acc = lax.fori_loop(0, nst, body, jnp.zeros((SUBLANES, QB), I32))
        return acc.sum(axis=0, keepdims=True)

    max_pairs = hi_scr.shape[0] // pair_rows
    thr, c_ge = lax.switch((nst + 1) // 2 - 1, [functools.partial(threshold, n) for n in range(1, max_pairs + 1)])

    tied = (c_ge > k_sel) & (thr > INT_MIN)
    big = jnp.full((1, QB), 2 ** row_bits, I32)

    def tie_limit():
        need = k_sel - count(lambda kt, rows: kt > thr)

        def lim_step(b, lim):
            cand = lim + lax.shift_left(jnp.int32(1), row_bits - 1 - b)
            c = count(lambda kt, rows: (kt == thr) & (rows < cand))
            return jnp.where(c < need, cand, lim)

        lim = lax.fori_loop(0, row_bits, lim_step, jnp.zeros((1, QB), I32))
        return jnp.where(tied, lim, big)

    rlim = lax.cond(jnp.max(tied.astype(I32)) > 0, tie_limit, lambda: big)
    rlim = jnp.where(thr > INT_MIN, rlim, -1)

    gq = GROUP * QB
    qg = [jnp.concatenate([qt_ref[0, (g * GROUP + r) * HEAD_DIM:(g * GROUP + r + 1) * HEAD_DIM, :]
                           for r in range(GROUP)], axis=1) for g in range(N_KV_HEADS)]

    def logits_rows(r0, rows, m8, bias_of_head):
        kt = key_scr[pl.ds(r0, rows), :]
        keep = (kt > thr) | ((kt == thr) & (sel_iota[:rows] <= rlim - r0))
        new_m8 = []
        for g in range(N_KV_HEADS):
            lg = _dot(k_rows(g, r0, rows), qg[g])
            parts = []
            for r in range(GROUP):
                part = lg[:, r * QB:(r + 1) * QB]
                if bias_of_head is not None:
                    part = part + bias_of_head(g * GROUP + r)
                parts.append(jnp.where(keep, part, NEG_INF))
            lg = jnp.concatenate(parts, axis=1)
            ls_scr[pl.ds(r0, rows), g * gq:(g + 1) * gq] = lg
            new_m8.append(jnp.maximum(m8[g], lg.reshape(rows // SUBLANES, SUBLANES, gq).max(axis=0)))
        return tuple(new_m8)

    far_bias = None
    m8 = tuple(jnp.full((SUBLANES, gq), NEG_INF, F32) for _ in range(N_KV_HEADS))
    n_far = diag - 1 if has_prev else diag

    def far_pair(t, m8):
        r0 = pl.multiple_of(t * pair_rows, pair_rows)
        return logits_rows(r0 + SEL_TILE, SEL_TILE, logits_rows(r0, SEL_TILE, m8, far_bias), far_bias)

    m8 = lax.fori_loop(0, n_far // 4, far_pair, m8)
    m8 = lax.cond(n_far % 4 >= 2, lambda m: logits_rows(sel_start(n_far // 4 * 2), SEL_TILE, m, far_bias),
                  lambda m: m, m8)
    m8 = lax.cond(n_far % 2 == 1, lambda m: logits_rows(tile_start(n_far - 1), QB, m, far_bias), lambda m: m, m8)
    if has_prev:
        m8 = logits_rows(tile_start(diag - 1), 2 * QB, m8, lambda h: tb_ref[h])
    else:
        m8 = logits_rows(tile_start(diag), QB, m8, lambda h: tb_ref[h, QB:, :])
    m = [jnp.max(m8[g], axis=0, keepdims=True) for g in range(N_KV_HEADS)]

    @pl.when(nt % 2 == 1)
    def _():
        ls_scr[pl.ds(tile_start(nt), QB), :] = jnp.full((QB, N_HEADS * QB), NEG_INF, F32)

    oacc_scr[...] = jnp.zeros(oacc_scr.shape, F32)

    def pv_rows(r0):
        for g in range(N_KV_HEADS):
            p = jnp.exp(ls_scr[pl.ds(r0, SEL_TILE), g * gq:(g + 1) * gq] - m[g])
            oacc_scr[g] += _dot(v_cols(g, r0, SEL_TILE), p.astype(BF16))

    def pv_pair(t, carry):
        r0 = pl.multiple_of(t * pair_rows, pair_rows)
        pv_rows(r0)
        pv_rows(r0 + SEL_TILE)
        return carry

    lax.fori_loop(0, nst // 2, pv_pair, 0)

    @pl.when(nst % 2 == 1)
    def _():
        pv_rows(sel_start(nst - 1))
    out_rows = []
    for g in range(N_KV_HEADS):
        o_g = oacc_scr[g, :HEAD_DIM] / oacc_scr[g, HEAD_DIM:HEAD_DIM + 1]
        out_rows.extend(o_g[:, r * QB:(r + 1) * QB] for r in range(GROUP))
    o_ref[...] = jnp.concatenate(out_rows, axis=0).T.astype(BF16)


def _attn_call(kend, qt, qit, wit, frames, tb, *, n_seq, build, n_out_rows, diag_off, kstart, k_sel, name):
    n_qb = kend.shape[0]
    kv_dim = N_KV_HEADS * HEAD_DIM
    if build:
        seq = frames[0].shape[0] // n_seq
        tkf = -(-(QB + seq) // (2 * SEL_TILE)) * 2 * SEL_TILE
        frame_specs = [pl.BlockSpec((seq, kv_dim), lambda s, i: (s, 0)),
                       pl.BlockSpec((seq // QB, kv_dim, QB), lambda s, i: (s, 0, 0)),
                       pl.BlockSpec((seq, IDX_DIM), lambda s, i: (s, 0))] + [_const_spec(a.shape) for a in frames[3:]]
        frame_scratch = [pltpu.VMEM((N_KV_HEADS, tkf, HEAD_DIM), BF16), pltpu.VMEM((N_KV_HEADS * V_ROWS, tkf), BF16),
                         pltpu.VMEM((tkf, IDX_DIM), BF16)]
    else:
        kf, vtf, kif = frames
        tkf = kf.shape[2]
        assert vtf.shape[2] == tkf and kif.shape[1] == tkf
        frame_specs = [pl.BlockSpec((1, N_KV_HEADS, tkf, HEAD_DIM), lambda s, i: (s, 0, 0, 0)),
                       pl.BlockSpec((1, N_KV_HEADS * V_ROWS, tkf), lambda s, i: (s, 0, 0)),
                       pl.BlockSpec((1, tkf, IDX_DIM), lambda s, i: (s, 0, 0))]
        frame_scratch = []
    assert tkf % (2 * SEL_TILE) == 0
    assert tkf // PACK16 <= 256, "per-slot bf16 counts must stay exact"
    has_prev = diag_off >= 1
    row_bits = max(1, (tkf - 1).bit_length())
    feat = lambda a: pl.BlockSpec((1, a.shape[1], QB), lambda s, i: (s * n_qb + i, 0, 0))
    return pl.pallas_call(
        functools.partial(_attn_kernel, diag_off=diag_off, kstart=kstart, k_sel=k_sel, has_prev=has_prev,
                          row_bits=row_bits, build=build),
        grid=(n_seq, n_qb),
        in_specs=[pl.BlockSpec((1, 1, QB), lambda s, i: (i, 0, 0)), feat(qt), feat(qit), feat(wit)] + frame_specs
                 + [_const_spec(tb.shape)],
        out_specs=pl.BlockSpec((QB, N_HEADS * HEAD_DIM), lambda s, i: (s * n_qb + i, 0)),
        out_shape=jax.ShapeDtypeStruct((n_out_rows, N_HEADS * HEAD_DIM), BF16),
        scratch_shapes=[pltpu.VMEM((tkf, QB), I32)] + [pltpu.VMEM((tkf, QB), I16)] * 3
                       + [pltpu.VMEM((tkf, N_HEADS * QB), F32), pltpu.VMEM((N_KV_HEADS, V_ROWS, GROUP * QB), F32)]
                       + frame_scratch,
        compiler_params=_cparams(("arbitrary", "arbitrary")),
        name=name,
    )(kend, qt, qit, wit, *frames, tb)


def _softplus(x):
    return jnp.maximum(x, 0.0) + jnp.log1p(jnp.exp(-jnp.abs(x)))


def _mixb_kernel(x_ref, h0_ref, buf_ref, g_ref, wy_ref, by_ref, wx_ref, bx_ref, cw_ref, cb_ref, wr_ref, br_ref,
                 wi_ref, bi_ref, lam_ref, *rest, tm):
    pre_ref, hlast_ref, tail_ref, xe_scr, h_scr = rest[-5:]

    @pl.when(pl.program_id(1) == 0)
    def _():
        xe_scr[0:SUBLANES, :] = buf_ref[0]
        h_scr[...] = h0_ref[0]

    xn = _rmsnorm(x_ref[...], g_ref[...]).astype(BF16)
    y = jax.nn.gelu(_dot(xn, wy_ref[...]) + by_ref[...])
    xe_scr[SUBLANES:SUBLANES + tm, :] = _dot(xn, wx_ref[...]) + bx_ref[...]
    xc = cb_ref[...] + cw_ref[0:1, :] * xe_scr[SUBLANES - 3:SUBLANES - 3 + tm, :]
    for j in range(1, CONV_B):
        xc = xc + cw_ref[j:j + 1, :] * xe_scr[SUBLANES - 3 + j:SUBLANES - 3 + j + tm, :]
    xcb = xc.astype(BF16)
    d_rnn = xc.shape[1]
    blk = d_rnn // LRU_BLOCKS
    r_pre = jnp.concatenate([_dot(xcb[:, n * blk:(n + 1) * blk], wr_ref[n]) for n in range(LRU_BLOCKS)], axis=1)
    i_pre = jnp.concatenate([_dot(xcb[:, n * blk:(n + 1) * blk], wi_ref[n]) for n in range(LRU_BLOCKS)], axis=1)
    r = jax.nn.sigmoid(r_pre + br_ref[...])
    ig = jax.nn.sigmoid(i_pre + bi_ref[...])
    log_a = -LRU_C * r * _softplus(-lam_ref[...])
    a = jnp.exp(log_a)
    u = jnp.sqrt(jnp.tanh(-log_a) * (1.0 + a * a)) * (ig * xc)
    a = a.reshape(tm // SUBLANES, SUBLANES, d_rnn)
    u = u.reshape(tm // SUBLANES, SUBLANES, d_rnn)
    row_in_group = lax.broadcasted_iota(I32, a.shape, 1)
    s = 1
    while s < SUBLANES:
        a_sh = jnp.where(row_in_group >= s, pltpu.roll(a, s, 1), 1.0)
        u_sh = jnp.where(row_in_group >= s, pltpu.roll(u, s, 1), 0.0)
        u = a * u_sh + u
        a = a * a_sh
        s *= 2
    carry = h_scr[...]
    h_groups = []
    for grp in range(tm // SUBLANES):
        h_grp = a[grp] * carry + u[grp]
        carry = h_grp[SUBLANES - 1:SUBLANES, :]
        h_groups.append(h_grp)
    h = jnp.concatenate(h_groups, axis=0)
    pre_ref[...] = (h * y).astype(BF16)
    h_scr[...] = carry
    hlast_ref[0] = carry
    tail = xe_scr[tm:tm + SUBLANES, :]
    xe_scr[0:SUBLANES, :] = tail
    tail_ref[0] = tail


def _mixc_kernel(x_ref, buf_ref, g_ref, win_ref, cw_ref, *rest, tm):
    pre_ref, tail_ref, pe_scr = rest[-3:]

    @pl.when(pl.program_id(1) == 0)
    def _():
        pe_scr[0:SUBLANES, :] = buf_ref[0]

    d = x_ref.shape[1]
    xn = _rmsnorm(x_ref[...], g_ref[...]).astype(BF16)
    z = _dot(xn, win_ref[...])
    pe_scr[SUBLANES:SUBLANES + tm, :] = z[:, d:2 * d] * z[:, 2 * d:]
    conv = cw_ref[0:1, :] * pe_scr[SUBLANES - 2:SUBLANES - 2 + tm, :]
    for j in range(1, CONV_C):
        conv = conv + cw_ref[j:j + 1, :] * pe_scr[SUBLANES - 2 + j:SUBLANES - 2 + j + tm, :]
    pre_ref[...] = (z[:, :d] * conv).astype(BF16)
    tail = pe_scr[tm:tm + SUBLANES, :]
    pe_scr[0:SUBLANES, :] = tail
    tail_ref[0] = tail


def _shared_out(buf, index):
    if buf is None:
        return {"specs": [], "args": (), "aliases": {}}
    return {"specs": [pl.BlockSpec(memory_space=pl.ANY)], "args": (buf,), "aliases": {index: 0}}


def _seq_specs(n_seq, t_len, row_off, d, tm_max):
    tm = tm_max if t_len % tm_max == 0 else min(TM_SEQ, t_len)
    n_tt = t_len // tm
    off = row_off // tm
    xrow = pl.BlockSpec((tm, d), lambda s, j: (off + s * n_tt + j, 0))
    state = lambda rows: pl.BlockSpec((1, rows, d), lambda s, j: (s, 0, 0))
    return tm, n_tt, xrow, state


def _mixb_call(x, pre_buf, h0, buf, g, wy, by, wx, bx, cw, cb, wr, br, wi, bi, lam, *, n_seq, t_len, row_off,
               name):
    d = x.shape[1]
    tm, n_tt, xrow, state = _seq_specs(n_seq, t_len, row_off, d, TM_SEQ)
    consts = (g, wy, by, wx, bx, cw, cb, wr, br, wi, bi, lam)
    shared = _shared_out(pre_buf, 3 + len(consts))
    return pl.pallas_call(
        functools.partial(_mixb_kernel, tm=tm),
        grid=(n_seq, n_tt),
        in_specs=[xrow, state(1), state(SUBLANES)] + [_const_spec(c.shape) for c in consts] + shared["specs"],
        out_specs=[xrow, state(1), state(SUBLANES)],
        out_shape=[jax.ShapeDtypeStruct((x.shape[0], d), BF16), jax.ShapeDtypeStruct((n_seq, 1, d), F32),
                   jax.ShapeDtypeStruct((n_seq, SUBLANES, d), F32)],
        scratch_shapes=[pltpu.VMEM((tm + SUBLANES, d), F32), pltpu.VMEM((1, d), F32)],
        input_output_aliases=shared["aliases"],
        compiler_params=_cparams(("arbitrary", "arbitrary")),
        name=name,
    )(x, h0, buf, *consts, *shared["args"])


def _mixc_call(x, pre_buf, buf, g, win, cw, *, n_seq, t_len, row_off, name):
    d = x.shape[1]
    tm, n_tt, xrow, state = _seq_specs(n_seq, t_len, row_off, d, TM_CONV)
    consts = (g, win, cw)
    shared = _shared_out(pre_buf, 2 + len(consts))
    return pl.pallas_call(
        functools.partial(_mixc_kernel, tm=tm),
        grid=(n_seq, n_tt),
        in_specs=[xrow, state(SUBLANES)] + [_const_spec(c.shape) for c in consts] + shared["specs"],
        out_specs=[xrow, state(SUBLANES)],
        out_shape=[jax.ShapeDtypeStruct((x.shape[0], d), BF16), jax.ShapeDtypeStruct((n_seq, SUBLANES, d), F32)],
        scratch_shapes=[pltpu.VMEM((tm + SUBLANES, d), F32)],
        input_output_aliases=shared["aliases"],
        compiler_params=_cparams(("arbitrary", "arbitrary")),
        name=name,
    )(x, buf, *consts, *shared["args"])


def _front_pad_rows(a, rows):
    return jnp.pad(a, ((0, 0), (rows - a.shape[1], 0), (0, 0)))


def _frame(parts, axis):
    f = jnp.concatenate(parts, axis=axis).astype(BF16)
    pad = [(0, 0)] * f.ndim
    pad[axis] = (0, -f.shape[axis] % (2 * SEL_TILE))
    return jnp.pad(f, pad)


def _k_frame(parts):
    f = _frame(parts, 1)
    n_seq, rows = f.shape[0], f.shape[1]
    return f.reshape(n_seq, rows, N_KV_HEADS, HEAD_DIM).transpose(0, 2, 1, 3)


def _v_frame(parts):
    f = _frame(parts, 2)
    n_seq, tkf = f.shape[0], f.shape[2]
    extra = jnp.zeros((n_seq, N_KV_HEADS, V_ROWS - HEAD_DIM, tkf), BF16).at[:, :, 0].set(1)
    f = jnp.concatenate([f.reshape(n_seq, N_KV_HEADS, HEAD_DIM, tkf), extra], axis=2)
    return f.reshape(n_seq, N_KV_HEADS * V_ROWS, tkf)


def _cols(a, start, n):
    b0, b1 = start // QB, -(-(start + n) // QB)
    c = a[b0:b1].transpose(1, 0, 2).reshape(a.shape[1], (b1 - b0) * QB)
    return c[:, start - b0 * QB:start - b0 * QB + n]


def _seq_major(a, n_seq, q_len):
    return a.reshape(a.shape[0], n_seq, q_len).transpose(1, 0, 2)


def _lane_pad_cols(a, n_seq, q_len):
    return jnp.pad(_seq_major(a, n_seq, q_len), ((0, 0), (0, 0), (0, QB - q_len)))


def kernel(x_prompt, x_sample, cache_k, cache_v, cache_kidx, state_h, state_conv_b, state_conv_c, meta_tokens,
           rel_bias, norm_mix, norm_ffn, norm_final, a_wq, a_wk, a_wv, a_wo, a_wiq, a_wik, a_wiw, b_wy, b_by, b_wx,
           b_bx, b_conv_w, b_conv_b, b_wr, b_br, b_wi, b_bi, b_lam, b_wo, b_bo, c_win, c_conv_w, c_wo, f_wg, f_wu,
           f_wd):
    bp, seq, d = x_prompt.shape
    bs, dec = x_sample.shape[0], x_sample.shape[1]
    past = cache_k.shape[2]
    n_meta = meta_tokens.shape[0]
    depth = norm_mix.shape[0]
    assert seq % QB == 0 and seq % TM_SEQ == 0 and past % QB == 0 and dec <= QB and n_meta <= QB
    assert dec % PACK16 == 0 and n_meta % PACK16 == 0 and (bs * dec) % n_meta == 0

    n_main, n_samp = bp * seq, bs * dec
    n_tail = -(-(n_samp + n_meta) // TM_TAIL) * TM_TAIL
    x_main = x_prompt.reshape(n_main, d)
    x_tail = jnp.concatenate([x_sample.reshape(n_samp, d), meta_tokens.astype(F32),
                              jnp.zeros((n_tail - n_samp - n_meta, d), F32)], axis=0)
    tail_fill = jnp.zeros((n_tail - n_samp - n_meta, d), BF16)

    k_sel_p = min(TOP_K_MAX, seq // 4)
    k_sel_s = min(TOP_K_MAX, (past + dec) // 4)
    n_qb = seq // QB
    kv_dim = N_KV_HEADS * HEAD_DIM

    tau = jnp.arange(seq, dtype=I32)
    kend_main = (QB + CHUNK * (tau // CHUNK + 1)).reshape(n_qb, 1, QB)
    lane = jnp.arange(QB, dtype=I32)
    kend_samp = jnp.minimum(past + dec, CHUNK * ((past + jnp.minimum(lane, dec - 1)) // CHUNK + 1)).reshape(1, 1, QB)
    kend_meta = jnp.full((1, 1, QB), n_meta, I32)

    tb = _bias_call(rel_bias.astype(F32))
    zeros_bias = jnp.zeros((1, d), F32)
    gfin = norm_final.reshape(1, d).astype(F32)

    new = {name: [] for name in ("k_p", "v_p", "ki_p", "h_p", "cb_p", "cc_p", "k_s", "v_s", "ki_s", "h_s", "cb_s",
                                 "cc_s")}
    for l in range(depth):
        s = l // N_MIXERS
        g_mix = norm_mix[l].reshape(1, d)
        if l % N_MIXERS == 0:
            wrow = jnp.concatenate([a_wk[s], a_wv[s], a_wik[s]], axis=1).astype(BF16)
            wt = jnp.concatenate([a_wq[s].T * HEAD_DIM ** -0.5, a_wiq[s].T * IDX_DIM ** -0.5, a_wv[s].T, a_wiw[s].T,
                                  jnp.zeros((SUBLANES, d), F32)], axis=0).astype(BF16)
            k, v, ki, kb, kib, qt, qit, vt, wit = _proj_call(x_main, g_mix, wrow, wt)
            k_t, v_t, ki_t, _, _, qt_t, qit_t, vt_t, wit_t = _proj_call(x_tail, g_mix, wrow, wt)

            k_meta, v_meta, ki_meta = (a[n_samp:n_samp + n_meta] for a in (k_t, v_t, ki_t))
            k_main, v_main, ki_main = (a.reshape(bp, seq, -1) for a in (k, v, ki))
            k_samp, v_samp, ki_samp = (a[:n_samp].reshape(bs, dec, -1) for a in (k_t, v_t, ki_t))
            bc = lambda a: jnp.broadcast_to(a[None], (bp,) + a.shape)
            new["k_p"].append(jnp.concatenate([bc(k_meta), k_main], axis=1).reshape(bp, n_meta + seq, N_KV_HEADS,
                                                                                   HEAD_DIM))
            new["v_p"].append(jnp.concatenate([bc(v_meta), v_main], axis=1).reshape(bp, n_meta + seq, N_KV_HEADS,
                                                                                   HEAD_DIM))
            new["ki_p"].append(jnp.concatenate([bc(ki_meta), ki_main], axis=1))
            new["k_s"].append(k_samp.reshape(bs, dec, N_KV_HEADS, HEAD_DIM))
            new["v_s"].append(v_samp.reshape(bs, dec, N_KV_HEADS, HEAD_DIM))
            new["ki_s"].append(ki_samp)

            samp_cols = lambda a: _cols(a, 0, n_samp)
            meta_cols = lambda a: _cols(a, n_samp, n_meta)
            vt_meta = meta_cols(vt_t)
            lead = lambda a, axis: jnp.pad(a, [(QB - n_meta, 0) if ax == axis else (0, 0)
                                               for ax in range(2)]).astype(BF16)
            pre_main = _attn_call(
                kend_main, qt, qit, wit, (kb, vt, kib, lead(k_meta, 0), lead(vt_meta, 1), lead(ki_meta, 0)), tb,
                n_seq=bp, build=True,
                n_out_rows=n_main, diag_off=1, kstart=QB - n_meta, k_sel=k_sel_p, name="dsa_attn_prompt")
            vtf_samp = _v_frame([cache_v[s].reshape(bs, past, kv_dim).transpose(0, 2, 1),
                                 _seq_major(samp_cols(vt_t), bs, dec)])
            o_samp = _attn_call(
                kend_samp, _lane_pad_cols(samp_cols(qt_t), bs, dec),
                _lane_pad_cols(samp_cols(qit_t), bs, dec), _lane_pad_cols(samp_cols(wit_t), bs, dec),
                (_k_frame([cache_k[s].reshape(bs, past, kv_dim), k_samp]), vtf_samp,
                 _frame([cache_kidx[s], ki_samp], 1)), tb, n_seq=bs, build=False,
                n_out_rows=bs * QB, diag_off=past // QB, kstart=0, k_sel=k_sel_s, name="dsa_attn_sample")
            o_meta = _attn_call(
                kend_meta, _lane_pad_cols(meta_cols(qt_t), 1, n_meta),
                _lane_pad_cols(meta_cols(qit_t), 1, n_meta), _lane_pad_cols(meta_cols(wit_t), 1, n_meta),
                (_k_frame([k_meta[None]]), _v_frame([vt_meta[None]]), _frame([ki_meta[None]], 1)), tb,
                n_seq=1, build=False, n_out_rows=QB, diag_off=0, kstart=0, k_sel=k_sel_p, name="dsa_attn_meta")
            pre_tail = jnp.concatenate([o_samp.reshape(bs, QB, d)[:, :dec].reshape(n_samp, d), o_meta[:n_meta],
                                        tail_fill], axis=0)
            wo, bo = a_wo[s].astype(BF16), zeros_bias
        elif l % N_MIXERS == 1:
            consts = (g_mix, b_wy[s].astype(BF16), b_by[s].reshape(1, d), b_wx[s].astype(BF16), b_bx[s].reshape(1, d),
                      jnp.pad(b_conv_w[s], ((0, SUBLANES - CONV_B), (0, 0))), b_conv_b[s].reshape(1, d),
                      b_wr[s].astype(BF16), b_br[s].reshape(1, d), b_wi[s].astype(BF16), b_bi[s].reshape(1, d),
                      b_lam[s].reshape(1, d))
            pre_tail = jnp.zeros((n_tail, d), BF16)
            pre_tail, h_m, tail_m = _mixb_call(x_tail, pre_tail, jnp.zeros((1, 1, d), F32),
                                               jnp.zeros((1, SUBLANES, d), F32), *consts, n_seq=1, t_len=n_meta,
                                               row_off=n_samp, name="rglru_meta")
            pre_tail, h_s, tail_s = _mixb_call(x_tail, pre_tail, state_h[s].reshape(bs, 1, d),
                                               _front_pad_rows(state_conv_b[s], SUBLANES), *consts, n_seq=bs,
                                               t_len=dec, row_off=0, name="rglru_sample")
            pre_main, h_p, tail_p = _mixb_call(x_main, None, jnp.broadcast_to(h_m, (bp, 1, d)),
                                               jnp.broadcast_to(tail_m, (bp, SUBLANES, d)), *consts, n_seq=bp,
                                               t_len=seq, row_off=0, name="rglru_prompt")
            new["h_p"].append(h_p.reshape(bp, d))
            new["cb_p"].append(tail_p[:, SUBLANES - (CONV_B - 1):])
            new["h_s"].append(h_s.reshape(bs, d))
            new["cb_s"].append(tail_s[:, SUBLANES - (CONV_B - 1):])
            wo, bo = b_wo[s].astype(BF16), b_bo[s].reshape(1, d)
        else:
            consts = (g_mix, c_win[s].astype(BF16), jnp.pad(c_conv_w[s], ((0, SUBLANES - CONV_C), (0, 0))))
            pre_tail = jnp.zeros((n_tail, d), BF16)
            pre_tail, tail_m = _mixc_call(x_tail, pre_tail, jnp.zeros((1, SUBLANES, d), F32), *consts, n_seq=1,
                                          t_len=n_meta, row_off=n_samp, name="sconv_meta")
            pre_tail, tail_s = _mixc_call(x_tail, pre_tail, _front_pad_rows(state_conv_c[s], SUBLANES), *consts,
                                          n_seq=bs, t_len=dec, row_off=0, name="sconv_sample")
            pre_main, tail_p = _mixc_call(x_main, None, jnp.broadcast_to(tail_m, (bp, SUBLANES, d)), *consts,
                                          n_seq=bp, t_len=seq, row_off=0, name="sconv_prompt")
            new["cc_p"].append(tail_p[:, SUBLANES - (CONV_C - 1):])
            new["cc_s"].append(tail_s[:, SUBLANES - (CONV_C - 1):])
            wo, bo = c_wo[s].astype(BF16), zeros_bias
        ffn = (wo, bo, norm_ffn[l].reshape(1, d), f_wg[l].astype(BF16), f_wu[l].astype(BF16), f_wd[l].astype(BF16),
               gfin, l == depth - 1)
        x_main = _ffn_call(x_main, pre_main, *ffn)
        x_tail = _ffn_call(x_tail, pre_tail, *ffn)

    y_prompt = x_main.reshape(bp, seq, d)
    y_sample = x_tail[:n_samp].reshape(bs, dec, d)
    st = lambda name: jnp.stack(new[name])
    return (y_prompt, y_sample, st("k_p"), st("v_p"), st("ki_p"), st("h_p"), st("cb_p"), st("cc_p"),
            st("k_s"), st("v_s"), st("ki_s"), st("h_s"), st("cb_s"), st("cc_s"))
```

```python
import functools

import jax
import jax.numpy as jnp
from jax import lax
from jax.experimental import pallas as pl
from jax.experimental.pallas import tpu as pltpu

F32 = jnp.float32
BF16 = jnp.bfloat16
I32 = jnp.int32
I16 = jnp.int16

CHUNK = 64
N_MIXERS = 3
N_HEADS = 16
N_KV_HEADS = 4
GROUP = N_HEADS // N_KV_HEADS
HEAD_DIM = 64
IDX_HEADS = 8
IDX_DIM = 64
TOP_K_MAX = 256
NEG_INF = -1e30
N_BUCKETS = 32
LRU_BLOCKS = 4
CONV_B = 4
CONV_C = 3
LRU_C = 8.0
RMS_EPS = 1e-6
BUCKET_STEPS = (12, 16, 23, 32, 46, 64, 91)
FAR_BUCKET = N_BUCKETS // 2 - 1

SUBLANES = 8
TM_ROWS = 512
TM_TAIL = 256
TM_SEQ = 256
TM_CONV = 512
QB = 128
ATTN_BLOCKS_PER_STEP = 2
SEL_TILE = 2 * QB
INT_MIN = -2 ** 31
I16_MIN = -2 ** 15
PACK16 = 2 * SUBLANES
V_ROWS = HEAD_DIM + PACK16
VMEM_LIMIT = 56 * 1024 * 1024


def _cparams(sem):
    return pltpu.CompilerParams(dimension_semantics=sem, vmem_limit_bytes=VMEM_LIMIT)


def _const_spec(shape):
    nd = len(shape)
    return pl.BlockSpec(shape, lambda *_: (0,) * nd, pipeline_mode=pl.Buffered(1))


def _rmsnorm(x, g):
    ms = jnp.mean(x * x, axis=-1, keepdims=True)
    return x * lax.rsqrt(ms + RMS_EPS) * g


def _dot(a, b):
    return jnp.dot(a, b, preferred_element_type=F32)


def _dot_nt(a, b):
    return lax.dot_general(a, b, (((1,), (1,)), ((), ())), preferred_element_type=F32)


def _ffn_kernel(x_ref, pre_ref, wo_ref, bo_ref, gf_ref, wg_ref, wu_ref, wd_ref, gfin_ref, out_ref, *, final_norm):
    x1 = x_ref[...] + _dot(pre_ref[...], wo_ref[...]) + bo_ref[...]
    xn = _rmsnorm(x1, gf_ref[...]).astype(BF16)
    gt = _dot(xn, wg_ref[...])
    hm = (gt * jax.nn.sigmoid(gt) * _dot(xn, wu_ref[...])).astype(BF16)
    acc = x1 + _dot(hm, wd_ref[...])
    if final_norm:
        acc = _rmsnorm(acc, gfin_ref[...])
    out_ref[...] = acc


def _ffn_call(x, pre, wo, bo, gf, wg, wu, wd, gfin, final_norm):
    n, d = x.shape
    row = lambda i: (i, 0)
    tm = TM_ROWS if n % TM_ROWS == 0 else TM_TAIL
    return pl.pallas_call(
        functools.partial(_ffn_kernel, final_norm=final_norm),
        grid=(n // tm,),
        in_specs=[pl.BlockSpec((tm, d), row), pl.BlockSpec((tm, pre.shape[1]), row),
                  _const_spec(wo.shape), _const_spec(bo.shape), _const_spec(gf.shape),
                  _const_spec(wg.shape), _const_spec(wu.shape), _const_spec(wd.shape), _const_spec(gfin.shape)],
        out_specs=pl.BlockSpec((tm, d), row),
        out_shape=jax.ShapeDtypeStruct((n, d), F32),
        compiler_params=_cparams(("parallel",)),
        name="outproj_swiglu",
    )(x, pre, wo, bo, gf, wg, wu, wd, gfin)


def _proj_kernel(x_ref, g_ref, wrow_ref, wt_ref, k_ref, v_ref, ki_ref, kb_ref, kib_ref, qt_ref, qit_ref, vt_ref,
                 wit_ref, *, n_kv, n_q, n_iq):
    xn = _rmsnorm(x_ref[...], g_ref[...]).astype(BF16)
    row = _dot(xn, wrow_ref[...])
    k_ref[...] = row[:, :n_kv]
    v_ref[...] = row[:, n_kv:2 * n_kv]
    ki_ref[...] = row[:, 2 * n_kv:]
    kb_ref[...] = row[:, :n_kv].astype(BF16)
    kib_ref[...] = row[:, 2 * n_kv:].astype(BF16)
    tt = _dot_nt(wt_ref[...], xn)
    for b in range(x_ref.shape[0] // QB):
        blk = tt[:, b * QB:(b + 1) * QB]
        qt_ref[b] = blk[:n_q].astype(BF16)
        qit_ref[b] = blk[n_q:n_q + n_iq].astype(BF16)
        vt_ref[b] = blk[n_q + n_iq:n_q + n_iq + n_kv].astype(BF16)
        wit_ref[b] = blk[n_q + n_iq + n_kv:n_q + n_iq + n_kv + IDX_HEADS] * (IDX_HEADS ** -0.5)


def _proj_call(x, g, wrow, wt):
    n, d = x.shape
    n_kv = N_KV_HEADS * HEAD_DIM
    n_q = N_HEADS * HEAD_DIM
    n_iq = IDX_HEADS * IDX_DIM
    row = lambda i: (i, 0)
    tm = TM_ROWS if n % TM_ROWS == 0 else TM_TAIL
    feat_spec = lambda f: pl.BlockSpec((tm // QB, f, QB), lambda i: (i, 0, 0))
    feat_shape = lambda f, dt: jax.ShapeDtypeStruct((n // QB, f, QB), dt)
    return pl.pallas_call(
        functools.partial(_proj_kernel, n_kv=n_kv, n_q=n_q, n_iq=n_iq),
        grid=(n // tm,),
        in_specs=[pl.BlockSpec((tm, d), row), _const_spec(g.shape), _const_spec(wrow.shape),
                  _const_spec(wt.shape)],
        out_specs=[pl.BlockSpec((tm, n_kv), row), pl.BlockSpec((tm, n_kv), row),
                   pl.BlockSpec((tm, IDX_DIM), row), pl.BlockSpec((tm, n_kv), row),
                   pl.BlockSpec((tm, IDX_DIM), row), feat_spec(n_q), feat_spec(n_iq), feat_spec(n_kv),
                   feat_spec(IDX_HEADS)],
        out_shape=[jax.ShapeDtypeStruct((n, n_kv), F32), jax.ShapeDtypeStruct((n, n_kv), F32),
                   jax.ShapeDtypeStruct((n, IDX_DIM), F32), jax.ShapeDtypeStruct((n, n_kv), BF16),
                   jax.ShapeDtypeStruct((n, IDX_DIM), BF16), feat_shape(n_q, BF16), feat_shape(n_iq, BF16),
                   feat_shape(n_kv, BF16), feat_shape(IDX_HEADS, F32)],
        compiler_params=_cparams(("parallel",)),
        name="attn_proj",
    )(x, g, wrow, wt)


def _bias_kernel(tab_ref, tb_ref):
    h = pl.program_id(0)
    d = pl.program_id(1)
    kj = lax.broadcasted_iota(I32, (QB, QB), 0)
    qi = lax.broadcasted_iota(I32, (QB, QB), 1)
    rel = (d - 1) * QB + kj - qi
    n = jnp.abs(rel)
    large = jnp.full((QB, QB), N_BUCKETS // 4, I32)
    for s in BUCKET_STEPS:
        large = large + jnp.where(n >= s, 1, 0)
    bucket = jnp.where(rel > 0, N_BUCKETS // 2, 0) + jnp.where(n < N_BUCKETS // 4, n, large)
    val = jnp.zeros((QB, QB), F32)
    for b in range(N_BUCKETS):
        val = jnp.where(bucket == b, tab_ref[b, h], val)
    tb_ref[0] = val - tab_ref[FAR_BUCKET, h]


def _bias_call(table):
    return pl.pallas_call(
        _bias_kernel,
        grid=(N_HEADS, 2),
        in_specs=[pl.BlockSpec(memory_space=pltpu.SMEM)],
        out_specs=pl.BlockSpec((1, QB, QB), lambda h, d: (h, d, 0)),
        out_shape=jax.ShapeDtypeStruct((N_HEADS, 2 * QB, QB), F32),
        compiler_params=_cparams(("arbitrary", "arbitrary")),
        name="rel_bias_tiles",
    )(table)


def _attn_kernel(*refs, blocks_per_step, **static):
    def one_block(sub, carry):
        _attn_block(sub, blocks_per_step, *refs, **static)
        return carry

    lax.fori_loop(0, blocks_per_step, one_block, 0)


def _attn_block(sub, blocks_per_step, kend_ref, qt_ref, qit_ref, wit_ref, *refs, diag_off, kstart, k_sel, has_prev,
                row_bits, build):
    block = pl.program_id(1) * blocks_per_step + sub
    if build:
        (kb_ref, vt_ref, kib_ref, k0_ref, v0_ref, ki0_ref, tb_ref, o_ref, key_scr, hi_scr, lo_scr, lo2_scr,
         ls_scr, oacc_scr, kf_scr, vtf_scr, kif_scr) = refs
        seq = kb_ref.shape[0]
        tkf = kf_scr.shape[1]

        @pl.when(block == 0)
        def _():
            kif_scr[0:QB, :] = ki0_ref[...]
            kif_scr[QB:QB + seq, :] = kib_ref[...]
            kif_scr[QB + seq:tkf, :] = jnp.zeros((tkf - QB - seq, IDX_DIM), BF16)
            for g in range(N_KV_HEADS):
                cols = slice(g * HEAD_DIM, (g + 1) * HEAD_DIM)
                kf_scr[g, 0:QB, :] = k0_ref[:, cols]
                for c in range(seq // SEL_TILE):
                    kf_scr[g, QB + c * SEL_TILE:QB + (c + 1) * SEL_TILE, :] = kb_ref[c * SEL_TILE:(c + 1) * SEL_TILE,
                                                                                     cols]
                kf_scr[g, QB + seq:tkf, :] = jnp.zeros((tkf - QB - seq, HEAD_DIM), BF16)
                vrows = slice(g * V_ROWS, g * V_ROWS + HEAD_DIM)
                vtf_scr[vrows, 0:QB] = v0_ref[cols, :]
                for j in range(seq // QB):
                    vtf_scr[vrows, QB + j * QB:QB + (j + 1) * QB] = vt_ref[j, cols, :]
                vtf_scr[vrows, QB + seq:tkf] = jnp.zeros((HEAD_DIM, tkf - QB - seq), BF16)
                extra = lax.broadcasted_iota(I32, (V_ROWS - HEAD_DIM, tkf), 0) == 0
                vtf_scr[g * V_ROWS + HEAD_DIM:(g + 1) * V_ROWS, :] = jnp.where(extra, 1.0, 0.0).astype(BF16)

        k_rows = lambda g, r0, rows: kf_scr[g, pl.ds(r0, rows), :]
        v_cols = lambda g, r0, rows: vtf_scr[g * V_ROWS:(g + 1) * V_ROWS, pl.ds(r0, rows)]
        ki_rows = lambda r0, rows: kif_scr[pl.ds(r0, rows), :]
    else:
        (kf_ref, vtf_ref, kif_ref, tb_ref, o_ref, key_scr, hi_scr, lo_scr, lo2_scr, ls_scr,
         oacc_scr) = refs
        k_rows = lambda g, r0, rows: kf_ref[0, g, pl.ds(r0, rows), :]
        v_cols = lambda g, r0, rows: vtf_ref[0, g * V_ROWS:(g + 1) * V_ROWS, pl.ds(r0, rows)]
        ki_rows = lambda r0, rows: kif_ref[0, pl.ds(r0, rows), :]
    diag = block + diag_off
    nt = diag + 1
    nst = (nt + SEL_TILE // QB - 1) // (SEL_TILE // QB)
    kend = kend_ref[sub]
    qi_cat = jnp.concatenate([qit_ref[sub, h * IDX_DIM:(h + 1) * IDX_DIM, :] for h in range(IDX_HEADS)], axis=1)
    wi = wit_ref[sub]
    sel_iota = lax.broadcasted_iota(I32, (SEL_TILE, QB), 0)

    def tile_start(t):
        return pl.multiple_of(t * QB, QB)

    def sel_start(t):
        return pl.multiple_of(t * SEL_TILE, SEL_TILE)

    def score_rows(r0):
        s = _dot(ki_rows(r0, SEL_TILE), qi_cat)
        sc = jnp.zeros((SEL_TILE, QB), F32)
        for h in range(IDX_HEADS):
            sc = sc + wi[h:h + 1, :] * jnp.maximum(s[:, h * QB:(h + 1) * QB], 0.0)
        bits = lax.bitcast_convert_type(sc + 0.0, I32)
        key = jnp.where(bits >= 0, bits, bits ^ 0x7FFFFFFF)
        rows = r0 + sel_iota
        adm = (rows >= kstart) & (rows < kend)
        key = jnp.where(adm, key, INT_MIN)
        key_scr[pl.ds(r0, SEL_TILE), :] = key
        hi_scr[pl.ds(r0, SEL_TILE), :] = (key >> 16).astype(I16)
        lo_scr[pl.ds(r0, SEL_TILE), :] = ((key & 0xFFFF) + I16_MIN).astype(I16)

    pair_rows = 2 * SEL_TILE

    def score_pair(t, carry):
        r0 = pl.multiple_of(t * pair_rows, pair_rows)
        score_rows(r0)
        score_rows(r0 + SEL_TILE)
        return carry

    lax.fori_loop(0, (nst + 1) // 2, score_pair, 0)

    pair_vregs = pair_rows // PACK16

    def as_packed(v):
        return jnp.broadcast_to(v, (PACK16, QB)).astype(I16)

    def threshold(n_pair):
        def count16(src_scr, cand):
            c16 = as_packed(cand)[None]
            acc = jnp.zeros((PACK16, QB), BF16)
            for t in range(n_pair):
                v = src_scr[t * pair_rows:(t + 1) * pair_rows, :].reshape(pair_vregs, PACK16, QB)
                ind = jnp.where(v >= c16, jnp.ones((), BF16), jnp.zeros((), BF16))
                parts = [ind[i] for i in range(pair_vregs)]
                while len(parts) > 1:
                    parts = [parts[i] + parts[i + 1] for i in range(0, len(parts), 2)]
                acc = acc + parts[0]
            return acc.astype(F32).sum(axis=0, keepdims=True)

        def search16(src_scr, base, c_start):
            def step(b, carry):
                t_acc, c_acc = carry
                cand = t_acc + lax.shift_left(jnp.int32(1), 15 - b)
                c = base + count16(src_scr, cand)
                ok = c >= k_sel
                return jnp.where(ok, cand, t_acc), jnp.where(ok, c, c_acc)

            return lax.fori_loop(0, 16, step, (jnp.full((1, QB), I16_MIN, I32), c_start))

        thr_hi, c_hi = search16(hi_scr, 0.0, jnp.full((1, QB), n_pair * pair_rows, F32))
        above = jnp.where(thr_hi == -I16_MIN - 1, 0.0, count16(hi_scr, thr_hi + 1))
        hi16 = as_packed(thr_hi)[None]
        for t in range(n_pair):
            rows = slice(t * pair_rows, (t + 1) * pair_rows)
            hi = hi_scr[rows, :].reshape(pair_vregs, PACK16, QB)
            lo = lo_scr[rows, :].reshape(pair_vregs, PACK16, QB)
            lo2_scr[rows, :] = jnp.where(hi == hi16, lo, jnp.full((), I16_MIN, I16)).reshape(pair_rows, QB)
        thr_lo, c_ge = search16(lo2_scr, above, c_hi)
        return thr_hi * 65536 + (thr_lo - I16_MIN), c_ge

    def count(pred):
        def body(t, acc):
            r0 = sel_start(t)
            ind = jnp.where(pred(key_scr[pl.ds(r0, SEL_TILE), :], r0 + sel_iota), 1, 0)
            return acc + ind.reshape(SEL_TILE // SUBLANES, SUBLANES, QB).sum(axis=0)
        acc = lax.fori_loop(0, nst, body, jnp.zeros((SUBLANES, QB), I32))
        return acc.sum(axis=0, keepdims=True)

    max_pairs = hi_scr.shape[0] // pair_rows
    thr, c_ge = lax.switch((nst + 1) // 2 - 1, [functools.partial(threshold, n) for n in range(1, max_pairs + 1)])

    tied = (c_ge > k_sel) & (thr > INT_MIN)
    big = jnp.full((1, QB), 2 ** row_bits, I32)

    def tie_limit():
        need = k_sel - count(lambda kt, rows: kt > thr)

        def lim_step(b, lim):
            cand = lim + lax.shift_left(jnp.int32(1), row_bits - 1 - b)
            c = count(lambda kt, rows: (kt == thr) & (rows < cand))
            return jnp.where(c < need, cand, lim)

        lim = lax.fori_loop(0, row_bits, lim_step, jnp.zeros((1, QB), I32))
        return jnp.where(tied, lim, big)

    rlim = lax.cond(jnp.max(tied.astype(I32)) > 0, tie_limit, lambda: big)
    rlim = jnp.where(thr > INT_MIN, rlim, -1)

    gq = GROUP * QB
    qg = [jnp.concatenate([qt_ref[sub, (g * GROUP + r) * HEAD_DIM:(g * GROUP + r + 1) * HEAD_DIM, :]
                           for r in range(GROUP)], axis=1) for g in range(N_KV_HEADS)]

    def logits_rows(r0, rows, m8, bias_of_head):
        kt = key_scr[pl.ds(r0, rows), :]
        keep = (kt > thr) | ((kt == thr) & (sel_iota[:rows] <= rlim - r0))
        new_m8 = []
        for g in range(N_KV_HEADS):
            lg = _dot(k_rows(g, r0, rows), qg[g])
            parts = []
            for r in range(GROUP):
                part = lg[:, r * QB:(r + 1) * QB]
                if bias_of_head is not None:
                    part = part + bias_of_head(g * GROUP + r)
                parts.append(jnp.where(keep, part, NEG_INF))
            lg = jnp.concatenate(parts, axis=1)
            ls_scr[pl.ds(r0, rows), g * gq:(g + 1) * gq] = lg
            new_m8.append(jnp.maximum(m8[g], lg.reshape(rows // SUBLANES, SUBLANES, gq).max(axis=0)))
        return tuple(new_m8)

    far_bias = None
    m8 = tuple(jnp.full((SUBLANES, gq), NEG_INF, F32) for _ in range(N_KV_HEADS))
    n_far = diag - 1 if has_prev else diag

    def far_pair(t, m8):
        r0 = pl.multiple_of(t * pair_rows, pair_rows)
        return logits_rows(r0 + SEL_TILE, SEL_TILE, logits_rows(r0, SEL_TILE, m8, far_bias), far_bias)

    m8 = lax.fori_loop(0, n_far // 4, far_pair, m8)
    m8 = lax.cond(n_far % 4 >= 2, lambda m: logits_rows(sel_start(n_far // 4 * 2), SEL_TILE, m, far_bias),
                  lambda m: m, m8)
    m8 = lax.cond(n_far % 2 == 1, lambda m: logits_rows(tile_start(n_far - 1), QB, m, far_bias), lambda m: m, m8)
    if has_prev:
        m8 = logits_rows(tile_start(diag - 1), 2 * QB, m8, lambda h: tb_ref[h])
    else:
        m8 = logits_rows(tile_start(diag), QB, m8, lambda h: tb_ref[h, QB:, :])
    m = [jnp.max(m8[g], axis=0, keepdims=True) for g in range(N_KV_HEADS)]

    @pl.when(nt % 2 == 1)
    def _():
        ls_scr[pl.ds(tile_start(nt), QB), :] = jnp.full((QB, N_HEADS * QB), NEG_INF, F32)

    oacc_scr[...] = jnp.zeros(oacc_scr.shape, F32)

    def pv_rows(r0):
        for g in range(N_KV_HEADS):
            p = jnp.exp(ls_scr[pl.ds(r0, SEL_TILE), g * gq:(g + 1) * gq] - m[g])
            oacc_scr[g] += _dot(v_cols(g, r0, SEL_TILE), p.astype(BF16))

    def pv_pair(t, carry):
        r0 = pl.multiple_of(t * pair_rows, pair_rows)
        pv_rows(r0)
        pv_rows(r0 + SEL_TILE)
        return carry

    lax.fori_loop(0, nst // 2, pv_pair, 0)

    @pl.when(nst % 2 == 1)
    def _():
        pv_rows(sel_start(nst - 1))
    out_rows = []
    for g in range(N_KV_HEADS):
        o_g = oacc_scr[g, :HEAD_DIM] / oacc_scr[g, HEAD_DIM:HEAD_DIM + 1]
        out_rows.extend(o_g[:, r * QB:(r + 1) * QB] for r in range(GROUP))
    o_ref[pl.ds(pl.multiple_of(sub * QB, QB), QB), :] = jnp.concatenate(out_rows, axis=0).T.astype(BF16)


def _attn_call(kend, qt, qit, wit, frames, tb, *, n_seq, build, n_out_rows, diag_off, kstart, k_sel, name):
    n_qb = kend.shape[0]
    kv_dim = N_KV_HEADS * HEAD_DIM
    if build:
        seq = frames[0].shape[0] // n_seq
        tkf = -(-(QB + seq) // (2 * SEL_TILE)) * 2 * SEL_TILE
        frame_specs = [pl.BlockSpec((seq, kv_dim), lambda s, i: (s, 0)),
                       pl.BlockSpec((seq // QB, kv_dim, QB), lambda s, i: (s, 0, 0)),
                       pl.BlockSpec((seq, IDX_DIM), lambda s, i: (s, 0))] + [_const_spec(a.shape) for a in frames[3:]]
        frame_scratch = [pltpu.VMEM((N_KV_HEADS, tkf, HEAD_DIM), BF16), pltpu.VMEM((N_KV_HEADS * V_ROWS, tkf), BF16),
                         pltpu.VMEM((tkf, IDX_DIM), BF16)]
    else:
        kf, vtf, kif = frames
        tkf = kf.shape[2]
        assert vtf.shape[2] == tkf and kif.shape[1] == tkf
        frame_specs = [pl.BlockSpec((1, N_KV_HEADS, tkf, HEAD_DIM), lambda s, i: (s, 0, 0, 0)),
                       pl.BlockSpec((1, N_KV_HEADS * V_ROWS, tkf), lambda s, i: (s, 0, 0)),
                       pl.BlockSpec((1, tkf, IDX_DIM), lambda s, i: (s, 0, 0))]
        frame_scratch = []
    assert tkf % (2 * SEL_TILE) == 0
    assert tkf // PACK16 <= 256, "per-slot bf16 counts must stay exact"
    has_prev = diag_off >= 1
    row_bits = max(1, (tkf - 1).bit_length())
    bps = ATTN_BLOCKS_PER_STEP if n_qb % ATTN_BLOCKS_PER_STEP == 0 else 1
    steps = n_qb // bps
    feat = lambda a: pl.BlockSpec((bps, a.shape[1], QB), lambda s, i: (s * steps + i, 0, 0))
    return pl.pallas_call(
        functools.partial(_attn_kernel, blocks_per_step=bps, diag_off=diag_off, kstart=kstart, k_sel=k_sel,
                          has_prev=has_prev, row_bits=row_bits, build=build),
        grid=(n_seq, steps),
        in_specs=[pl.BlockSpec((bps, 1, QB), lambda s, i: (i, 0, 0)), feat(qt), feat(qit), feat(wit)] + frame_specs
                 + [_const_spec(tb.shape)],
        out_specs=pl.BlockSpec((bps * QB, N_HEADS * HEAD_DIM), lambda s, i: (s * steps + i, 0)),
        out_shape=jax.ShapeDtypeStruct((n_out_rows, N_HEADS * HEAD_DIM), BF16),
        scratch_shapes=[pltpu.VMEM((tkf, QB), I32)] + [pltpu.VMEM((tkf, QB), I16)] * 3
                       + [pltpu.VMEM((tkf, N_HEADS * QB), F32), pltpu.VMEM((N_KV_HEADS, V_ROWS, GROUP * QB), F32)]
                       + frame_scratch,
        compiler_params=_cparams(("arbitrary", "arbitrary")),
        name=name,
    )(kend, qt, qit, wit, *frames, tb)


def _softplus(x):
    return jnp.maximum(x, 0.0) + jnp.log1p(jnp.exp(-jnp.abs(x)))


def _mixb_kernel(x_ref, h0_ref, buf_ref, g_ref, wy_ref, by_ref, wx_ref, bx_ref, cw_ref, cb_ref, wr_ref, br_ref,
                 wi_ref, bi_ref, lam_ref, *rest, tm):
    pre_ref, hlast_ref, tail_ref, xe_scr, h_scr = rest[-5:]

    @pl.when(pl.program_id(1) == 0)
    def _():
        xe_scr[0:SUBLANES, :] = buf_ref[0]
        h_scr[...] = h0_ref[0]

    xn = _rmsnorm(x_ref[...], g_ref[...]).astype(BF16)
    y = jax.nn.gelu(_dot(xn, wy_ref[...]) + by_ref[...])
    xe_scr[SUBLANES:SUBLANES + tm, :] = _dot(xn, wx_ref[...]) + bx_ref[...]
    xc = cb_ref[...] + cw_ref[0:1, :] * xe_scr[SUBLANES - 3:SUBLANES - 3 + tm, :]
    for j in range(1, CONV_B):
        xc = xc + cw_ref[j:j + 1, :] * xe_scr[SUBLANES - 3 + j:SUBLANES - 3 + j + tm, :]
    xcb = xc.astype(BF16)
    d_rnn = xc.shape[1]
    blk = d_rnn // LRU_BLOCKS
    r_pre = jnp.concatenate([_dot(xcb[:, n * blk:(n + 1) * blk], wr_ref[n]) for n in range(LRU_BLOCKS)], axis=1)
    i_pre = jnp.concatenate([_dot(xcb[:, n * blk:(n + 1) * blk], wi_ref[n]) for n in range(LRU_BLOCKS)], axis=1)
    r = jax.nn.sigmoid(r_pre + br_ref[...])
    ig = jax.nn.sigmoid(i_pre + bi_ref[...])
    log_a = -LRU_C * r * _softplus(-lam_ref[...])
    a = jnp.exp(log_a)
    u = jnp.sqrt(jnp.tanh(-log_a) * (1.0 + a * a)) * (ig * xc)
    a = a.reshape(tm // SUBLANES, SUBLANES, d_rnn)
    u = u.reshape(tm // SUBLANES, SUBLANES, d_rnn)
    row_in_group = lax.broadcasted_iota(I32, a.shape, 1)
    s = 1
    while s < SUBLANES:
        a_sh = jnp.where(row_in_group >= s, pltpu.roll(a, s, 1), 1.0)
        u_sh = jnp.where(row_in_group >= s, pltpu.roll(u, s, 1), 0.0)
        u = a * u_sh + u
        a = a * a_sh
        s *= 2
    carry = h_scr[...]
    h_groups = []
    for grp in range(tm // SUBLANES):
        h_grp = a[grp] * carry + u[grp]
        carry = h_grp[SUBLANES - 1:SUBLANES, :]
        h_groups.append(h_grp)
    h = jnp.concatenate(h_groups, axis=0)
    pre_ref[...] = (h * y).astype(BF16)
    h_scr[...] = carry
    hlast_ref[0] = carry
    tail = xe_scr[tm:tm + SUBLANES, :]
    xe_scr[0:SUBLANES, :] = tail
    tail_ref[0] = tail


def _mixc_kernel(x_ref, buf_ref, g_ref, win_ref, cw_ref, *rest, tm):
    pre_ref, tail_ref, pe_scr = rest[-3:]

    @pl.when(pl.program_id(1) == 0)
    def _():
        pe_scr[0:SUBLANES, :] = buf_ref[0]

    d = x_ref.shape[1]
    xn = _rmsnorm(x_ref[...], g_ref[...]).astype(BF16)
    z = _dot(xn, win_ref[...])
    pe_scr[SUBLANES:SUBLANES + tm, :] = z[:, d:2 * d] * z[:, 2 * d:]
    conv = cw_ref[0:1, :] * pe_scr[SUBLANES - 2:SUBLANES - 2 + tm, :]
    for j in range(1, CONV_C):
        conv = conv + cw_ref[j:j + 1, :] * pe_scr[SUBLANES - 2 + j:SUBLANES - 2 + j + tm, :]
    pre_ref[...] = (z[:, :d] * conv).astype(BF16)
    tail = pe_scr[tm:tm + SUBLANES, :]
    pe_scr[0:SUBLANES, :] = tail
    tail_ref[0] = tail


def _shared_out(buf, index):
    if buf is None:
        return {"specs": [], "args": (), "aliases": {}}
    return {"specs": [pl.BlockSpec(memory_space=pl.ANY)], "args": (buf,), "aliases": {index: 0}}


def _seq_specs(n_seq, t_len, row_off, d, tm_max):
    tm = tm_max if t_len % tm_max == 0 else min(TM_SEQ, t_len)
    n_tt = t_len // tm
    off = row_off // tm
    xrow = pl.BlockSpec((tm, d), lambda s, j: (off + s * n_tt + j, 0))
    state = lambda rows: pl.BlockSpec((1, rows, d), lambda s, j: (s, 0, 0))
    return tm, n_tt, xrow, state


def _mixb_call(x, pre_buf, h0, buf, g, wy, by, wx, bx, cw, cb, wr, br, wi, bi, lam, *, n_seq, t_len, row_off,
               name):
    d = x.shape[1]
    tm, n_tt, xrow, state = _seq_specs(n_seq, t_len, row_off, d, TM_SEQ)
    consts = (g, wy, by, wx, bx, cw, cb, wr, br, wi, bi, lam)
    shared = _shared_out(pre_buf, 3 + len(consts))
    return pl.pallas_call(
        functools.partial(_mixb_kernel, tm=tm),
        grid=(n_seq, n_tt),
        in_specs=[xrow, state(1), state(SUBLANES)] + [_const_spec(c.shape) for c in consts] + shared["specs"],
        out_specs=[xrow, state(1), state(SUBLANES)],
        out_shape=[jax.ShapeDtypeStruct((x.shape[0], d), BF16), jax.ShapeDtypeStruct((n_seq, 1, d), F32),
                   jax.ShapeDtypeStruct((n_seq, SUBLANES, d), F32)],
        scratch_shapes=[pltpu.VMEM((tm + SUBLANES, d), F32), pltpu.VMEM((1, d), F32)],
        input_output_aliases=shared["aliases"],
        compiler_params=_cparams(("arbitrary", "arbitrary")),
        name=name,
    )(x, h0, buf, *consts, *shared["args"])


def _mixc_call(x, pre_buf, buf, g, win, cw, *, n_seq, t_len, row_off, name):
    d = x.shape[1]
    tm, n_tt, xrow, state = _seq_specs(n_seq, t_len, row_off, d, TM_CONV)
    consts = (g, win, cw)
    shared = _shared_out(pre_buf, 2 + len(consts))
    return pl.pallas_call(
        functools.partial(_mixc_kernel, tm=tm),
        grid=(n_seq, n_tt),
        in_specs=[xrow, state(SUBLANES)] + [_const_spec(c.shape) for c in consts] + shared["specs"],
        out_specs=[xrow, state(SUBLANES)],
        out_shape=[jax.ShapeDtypeStruct((x.shape[0], d), BF16), jax.ShapeDtypeStruct((n_seq, SUBLANES, d), F32)],
        scratch_shapes=[pltpu.VMEM((tm + SUBLANES, d), F32)],
        input_output_aliases=shared["aliases"],
        compiler_params=_cparams(("arbitrary", "arbitrary")),
        name=name,
    )(x, buf, *consts, *shared["args"])


def _front_pad_rows(a, rows):
    return jnp.pad(a, ((0, 0), (rows - a.shape[1], 0), (0, 0)))


def _frame(parts, axis):
    f = jnp.concatenate(parts, axis=axis).astype(BF16)
    pad = [(0, 0)] * f.ndim
    pad[axis] = (0, -f.shape[axis] % (2 * SEL_TILE))
    return jnp.pad(f, pad)


def _k_frame(parts):
    f = _frame(parts, 1)
    n_seq, rows = f.shape[0], f.shape[1]
    return f.reshape(n_seq, rows, N_KV_HEADS, HEAD_DIM).transpose(0, 2, 1, 3)


def _v_frame(parts):
    f = _frame(parts, 2)
    n_seq, tkf = f.shape[0], f.shape[2]
    extra = jnp.zeros((n_seq, N_KV_HEADS, V_ROWS - HEAD_DIM, tkf), BF16).at[:, :, 0].set(1)
    f = jnp.concatenate([f.reshape(n_seq, N_KV_HEADS, HEAD_DIM, tkf), extra], axis=2)
    return f.reshape(n_seq, N_KV_HEADS * V_ROWS, tkf)


def _cols(a, start, n):
    b0, b1 = start // QB, -(-(start + n) // QB)
    c = a[b0:b1].transpose(1, 0, 2).reshape(a.shape[1], (b1 - b0) * QB)
    return c[:, start - b0 * QB:start - b0 * QB + n]


def _seq_major(a, n_seq, q_len):
    return a.reshape(a.shape[0], n_seq, q_len).transpose(1, 0, 2)


def _lane_pad_cols(a, n_seq, q_len):
    return jnp.pad(_seq_major(a, n_seq, q_len), ((0, 0), (0, 0), (0, QB - q_len)))


def kernel(x_prompt, x_sample, cache_k, cache_v, cache_kidx, state_h, state_conv_b, state_conv_c, meta_tokens,
           rel_bias, norm_mix, norm_ffn, norm_final, a_wq, a_wk, a_wv, a_wo, a_wiq, a_wik, a_wiw, b_wy, b_by, b_wx,
           b_bx, b_conv_w, b_conv_b, b_wr, b_br, b_wi, b_bi, b_lam, b_wo, b_bo, c_win, c_conv_w, c_wo, f_wg, f_wu,
           f_wd):
    bp, seq, d = x_prompt.shape
    bs, dec = x_sample.shape[0], x_sample.shape[1]
    past = cache_k.shape[2]
    n_meta = meta_tokens.shape[0]
    depth = norm_mix.shape[0]
    assert seq % QB == 0 and seq % TM_SEQ == 0 and past % QB == 0 and dec <= QB and n_meta <= QB
    assert dec % PACK16 == 0 and n_meta % PACK16 == 0 and (bs * dec) % n_meta == 0

    n_main, n_samp = bp * seq, bs * dec
    n_tail = -(-(n_samp + n_meta) // TM_TAIL) * TM_TAIL
    x_main = x_prompt.reshape(n_main, d)
    x_tail = jnp.concatenate([x_sample.reshape(n_samp, d), meta_tokens.astype(F32),
                              jnp.zeros((n_tail - n_samp - n_meta, d), F32)], axis=0)
    tail_fill = jnp.zeros((n_tail - n_samp - n_meta, d), BF16)

    k_sel_p = min(TOP_K_MAX, seq // 4)
    k_sel_s = min(TOP_K_MAX, (past + dec) // 4)
    n_qb = seq // QB
    kv_dim = N_KV_HEADS * HEAD_DIM

    tau = jnp.arange(seq, dtype=I32)
    kend_main = (QB + CHUNK * (tau // CHUNK + 1)).reshape(n_qb, 1, QB)
    lane = jnp.arange(QB, dtype=I32)
    kend_samp = jnp.minimum(past + dec, CHUNK * ((past + jnp.minimum(lane, dec - 1)) // CHUNK + 1)).reshape(1, 1, QB)
    kend_meta = jnp.full((1, 1, QB), n_meta, I32)

    tb = _bias_call(rel_bias.astype(F32))
    zeros_bias = jnp.zeros((1, d), F32)
    gfin = norm_final.reshape(1, d).astype(F32)

    new = {name: [] for name in ("k_p", "v_p", "ki_p", "h_p", "cb_p", "cc_p", "k_s", "v_s", "ki_s", "h_s", "cb_s",
                                 "cc_s")}
    for l in range(depth):
        s = l // N_MIXERS
        g_mix = norm_mix[l].reshape(1, d)
        if l % N_MIXERS == 0:
            wrow = jnp.concatenate([a_wk[s], a_wv[s], a_wik[s]], axis=1).astype(BF16)
            wt = jnp.concatenate([a_wq[s].T * HEAD_DIM ** -0.5, a_wiq[s].T * IDX_DIM ** -0.5, a_wv[s].T, a_wiw[s].T,
                                  jnp.zeros((SUBLANES, d), F32)], axis=0).astype(BF16)
            k, v, ki, kb, kib, qt, qit, vt, wit = _proj_call(x_main, g_mix, wrow, wt)
            k_t, v_t, ki_t, _, _, qt_t, qit_t, vt_t, wit_t = _proj_call(x_tail, g_mix, wrow, wt)

            k_meta, v_meta, ki_meta = (a[n_samp:n_samp + n_meta] for a in (k_t, v_t, ki_t))
            k_main, v_main, ki_main = (a.reshape(bp, seq, -1) for a in (k, v, ki))
            k_samp, v_samp, ki_samp = (a[:n_samp].reshape(bs, dec, -1) for a in (k_t, v_t, ki_t))
            bc = lambda a: jnp.broadcast_to(a[None], (bp,) + a.shape)
            new["k_p"].append(jnp.concatenate([bc(k_meta), k_main], axis=1).reshape(bp, n_meta + seq, N_KV_HEADS,
                                                                                   HEAD_DIM))
            new["v_p"].append(jnp.concatenate([bc(v_meta), v_main], axis=1).reshape(bp, n_meta + seq, N_KV_HEADS,
                                                                                   HEAD_DIM))
            new["ki_p"].append(jnp.concatenate([bc(ki_meta), ki_main], axis=1))
            new["k_s"].append(k_samp.reshape(bs, dec, N_KV_HEADS, HEAD_DIM))
            new["v_s"].append(v_samp.reshape(bs, dec, N_KV_HEADS, HEAD_DIM))
            new["ki_s"].append(ki_samp)

            samp_cols = lambda a: _cols(a, 0, n_samp)
            meta_cols = lambda a: _cols(a, n_samp, n_meta)
            vt_meta = meta_cols(vt_t)
            lead = lambda a, axis: jnp.pad(a, [(QB - n_meta, 0) if ax == axis else (0, 0)
                                               for ax in range(2)]).astype(BF16)
            pre_main = _attn_call(
                kend_main, qt, qit, wit, (kb, vt, kib, lead(k_meta, 0), lead(vt_meta, 1), lead(ki_meta, 0)), tb,
                n_seq=bp, build=True,
                n_out_rows=n_main, diag_off=1, kstart=QB - n_meta, k_sel=k_sel_p, name="dsa_attn_prompt")
            vtf_samp = _v_frame([cache_v[s].reshape(bs, past, kv_dim).transpose(0, 2, 1),
                                 _seq_major(samp_cols(vt_t), bs, dec)])
            o_samp = _attn_call(
                kend_samp, _lane_pad_cols(samp_cols(qt_t), bs, dec),
                _lane_pad_cols(samp_cols(qit_t), bs, dec), _lane_pad_cols(samp_cols(wit_t), bs, dec),
                (_k_frame([cache_k[s].reshape(bs, past, kv_dim), k_samp]), vtf_samp,
                 _frame([cache_kidx[s], ki_samp], 1)), tb, n_seq=bs, build=False,
                n_out_rows=bs * QB, diag_off=past // QB, kstart=0, k_sel=k_sel_s, name="dsa_attn_sample")
            o_meta = _attn_call(
                kend_meta, _lane_pad_cols(meta_cols(qt_t), 1, n_meta),
                _lane_pad_cols(meta_cols(qit_t), 1, n_meta), _lane_pad_cols(meta_cols(wit_t), 1, n_meta),
                (_k_frame([k_meta[None]]), _v_frame([vt_meta[None]]), _frame([ki_meta[None]], 1)), tb,
                n_seq=1, build=False, n_out_rows=QB, diag_off=0, kstart=0, k_sel=k_sel_p, name="dsa_attn_meta")
            pre_tail = jnp.concatenate([o_samp.reshape(bs, QB, d)[:, :dec].reshape(n_samp, d), o_meta[:n_meta],
                                        tail_fill], axis=0)
            wo, bo = a_wo[s].astype(BF16), zeros_bias
        elif l % N_MIXERS == 1:
            consts = (g_mix, b_wy[s].astype(BF16), b_by[s].reshape(1, d), b_wx[s].astype(BF16), b_bx[s].reshape(1, d),
                      jnp.pad(b_conv_w[s], ((0, SUBLANES - CONV_B), (0, 0))), b_conv_b[s].reshape(1, d),
                      b_wr[s].astype(BF16), b_br[s].reshape(1, d), b_wi[s].astype(BF16), b_bi[s].reshape(1, d),
                      b_lam[s].reshape(1, d))
            pre_tail = jnp.zeros((n_tail, d), BF16)
            pre_tail, h_m, tail_m = _mixb_call(x_tail, pre_tail, jnp.zeros((1, 1, d), F32),
                                               jnp.zeros((1, SUBLANES, d), F32), *consts, n_seq=1, t_len=n_meta,
                                               row_off=n_samp, name="rglru_meta")
            pre_tail, h_s, tail_s = _mixb_call(x_tail, pre_tail, state_h[s].reshape(bs, 1, d),
                                               _front_pad_rows(state_conv_b[s], SUBLANES), *consts, n_seq=bs,
                                               t_len=dec, row_off=0, name="rglru_sample")
            pre_main, h_p, tail_p = _mixb_call(x_main, None, jnp.broadcast_to(h_m, (bp, 1, d)),
                                               jnp.broadcast_to(tail_m, (bp, SUBLANES, d)), *consts, n_seq=bp,
                                               t_len=seq, row_off=0, name="rglru_prompt")
            new["h_p"].append(h_p.reshape(bp, d))
            new["cb_p"].append(tail_p[:, SUBLANES - (CONV_B - 1):])
            new["h_s"].append(h_s.reshape(bs, d))
            new["cb_s"].append(tail_s[:, SUBLANES - (CONV_B - 1):])
            wo, bo = b_wo[s].astype(BF16), b_bo[s].reshape(1, d)
        else:
            consts = (g_mix, c_win[s].astype(BF16), jnp.pad(c_conv_w[s], ((0, SUBLANES - CONV_C), (0, 0))))
            pre_tail = jnp.zeros((n_tail, d), BF16)
            pre_tail, tail_m = _mixc_call(x_tail, pre_tail, jnp.zeros((1, SUBLANES, d), F32), *consts, n_seq=1,
                                          t_len=n_meta, row_off=n_samp, name="sconv_meta")
            pre_tail, tail_s = _mixc_call(x_tail, pre_tail, _front_pad_rows(state_conv_c[s], SUBLANES), *consts,
                                          n_seq=bs, t_len=dec, row_off=0, name="sconv_sample")
            pre_main, tail_p = _mixc_call(x_main, None, jnp.broadcast_to(tail_m, (bp, SUBLANES, d)), *consts,
                                          n_seq=bp, t_len=seq, row_off=0, name="sconv_prompt")
            new["cc_p"].append(tail_p[:, SUBLANES - (CONV_C - 1):])
            new["cc_s"].append(tail_s[:, SUBLANES - (CONV_C - 1):])
            wo, bo = c_wo[s].astype(BF16), zeros_bias
        ffn = (wo, bo, norm_ffn[l].reshape(1, d), f_wg[l].astype(BF16), f_wu[l].astype(BF16), f_wd[l].astype(BF16),
               gfin, l == depth - 1)
        x_main = _ffn_call(x_main, pre_main, *ffn)
        x_tail = _ffn_call(x_tail, pre_tail, *ffn)

    y_prompt = x_main.reshape(bp, seq, d)
    y_sample = x_tail[:n_samp].reshape(bs, dec, d)
    st = lambda name: jnp.stack(new[name])
    return (y_prompt, y_sample, st("k_p"), st("v_p"), st("ki_p"), st("h_p"), st("cb_p"), st("cc_p"),
            st("k_s"), st("v_s"), st("ki_s"), st("h_s"), st("cb_s"), st("cc_s"))
```

```python
import functools

import jax
import jax.numpy as jnp
from jax import lax
from jax.experimental import pallas as pl
from jax.experimental.pallas import tpu as pltpu

F32 = jnp.float32
BF16 = jnp.bfloat16
I32 = jnp.int32
I16 = jnp.int16

CHUNK = 64
N_MIXERS = 3
N_HEADS = 16
N_KV_HEADS = 4
GROUP = N_HEADS // N_KV_HEADS
HEAD_DIM = 64
IDX_HEADS = 8
IDX_DIM = 64
TOP_K_MAX = 256
NEG_INF = -1e30
N_BUCKETS = 32
LRU_BLOCKS = 4
CONV_B = 4
CONV_C = 3
LRU_C = 8.0
RMS_EPS = 1e-6
BUCKET_STEPS = (12, 16, 23, 32, 46, 64, 91)
FAR_BUCKET = N_BUCKETS // 2 - 1

SUBLANES = 8
TM_ROWS = 512
TM_TAIL = 256
TM_SEQ = 256
TM_CONV = 512
QB = 128
ATTN_BLOCKS_PER_STEP = 2
SEL_TILE = 2 * QB
INT_MIN = -2 ** 31
I16_MIN = -2 ** 15
PACK16 = 2 * SUBLANES
V_ROWS = HEAD_DIM + PACK16
VMEM_LIMIT = 56 * 1024 * 1024


def _cparams(sem):
    return pltpu.CompilerParams(dimension_semantics=sem, vmem_limit_bytes=VMEM_LIMIT)


def _const_spec(shape):
    nd = len(shape)
    return pl.BlockSpec(shape, lambda *_: (0,) * nd, pipeline_mode=pl.Buffered(1))


def _rmsnorm(x, g):
    ms = jnp.mean(x * x, axis=-1, keepdims=True)
    return x * lax.rsqrt(ms + RMS_EPS) * g


def _dot(a, b):
    return jnp.dot(a, b, preferred_element_type=F32)


def _dot_nt(a, b):
    return lax.dot_general(a, b, (((1,), (1,)), ((), ())), preferred_element_type=F32)


def _ffn_kernel(x_ref, pre_ref, wo_ref, bo_ref, gf_ref, wg_ref, wu_ref, wd_ref, gfin_ref, out_ref, *, final_norm):
    x1 = x_ref[...] + _dot(pre_ref[...], wo_ref[...]) + bo_ref[...]
    xn = _rmsnorm(x1, gf_ref[...]).astype(BF16)
    gt = _dot(xn, wg_ref[...])
    hm = (gt * jax.nn.sigmoid(gt) * _dot(xn, wu_ref[...])).astype(BF16)
    acc = x1 + _dot(hm, wd_ref[...])
    if final_norm:
        acc = _rmsnorm(acc, gfin_ref[...])
    out_ref[...] = acc


def _ffn_call(x, pre, wo, bo, gf, wg, wu, wd, gfin, final_norm):
    n, d = x.shape
    row = lambda i: (i, 0)
    tm = TM_ROWS if n % TM_ROWS == 0 else TM_TAIL
    return pl.pallas_call(
        functools.partial(_ffn_kernel, final_norm=final_norm),
        grid=(n // tm,),
        in_specs=[pl.BlockSpec((tm, d), row), pl.BlockSpec((tm, pre.shape[1]), row),
                  _const_spec(wo.shape), _const_spec(bo.shape), _const_spec(gf.shape),
                  _const_spec(wg.shape), _const_spec(wu.shape), _const_spec(wd.shape), _const_spec(gfin.shape)],
        out_specs=pl.BlockSpec((tm, d), row),
        out_shape=jax.ShapeDtypeStruct((n, d), F32),
        compiler_params=_cparams(("parallel",)),
        name="outproj_swiglu",
    )(x, pre, wo, bo, gf, wg, wu, wd, gfin)


def _proj_kernel(x_ref, g_ref, wrow_ref, wt_ref, k_ref, v_ref, ki_ref, kb_ref, kib_ref, qt_ref, qit_ref, vt_ref,
                 wit_ref, *, n_kv, n_q, n_iq):
    xn = _rmsnorm(x_ref[...], g_ref[...]).astype(BF16)
    row = _dot(xn, wrow_ref[...])
    k_ref[...] = row[:, :n_kv]
    v_ref[...] = row[:, n_kv:2 * n_kv]
    ki_ref[...] = row[:, 2 * n_kv:]
    kb_ref[...] = row[:, :n_kv].astype(BF16)
    kib_ref[...] = row[:, 2 * n_kv:].astype(BF16)
    tt = _dot_nt(wt_ref[...], xn)
    for b in range(x_ref.shape[0] // QB):
        blk = tt[:, b * QB:(b + 1) * QB]
        qt_ref[b] = blk[:n_q].astype(BF16)
        qit_ref[b] = blk[n_q:n_q + n_iq].astype(BF16)
        vt_ref[b] = blk[n_q + n_iq:n_q + n_iq + n_kv].astype(BF16)
        wit_ref[b] = blk[n_q + n_iq + n_kv:n_q + n_iq + n_kv + IDX_HEADS] * (IDX_HEADS ** -0.5)


def _proj_call(x, g, wrow, wt):
    n, d = x.shape
    n_kv = N_KV_HEADS * HEAD_DIM
    n_q = N_HEADS * HEAD_DIM
    n_iq = IDX_HEADS * IDX_DIM
    row = lambda i: (i, 0)
    tm = TM_ROWS if n % TM_ROWS == 0 else TM_TAIL
    feat_spec = lambda f: pl.BlockSpec((tm // QB, f, QB), lambda i: (i, 0, 0))
    feat_shape = lambda f, dt: jax.ShapeDtypeStruct((n // QB, f, QB), dt)
    return pl.pallas_call(
        functools.partial(_proj_kernel, n_kv=n_kv, n_q=n_q, n_iq=n_iq),
        grid=(n // tm,),
        in_specs=[pl.BlockSpec((tm, d), row), _const_spec(g.shape), _const_spec(wrow.shape),
                  _const_spec(wt.shape)],
        out_specs=[pl.BlockSpec((tm, n_kv), row), pl.BlockSpec((tm, n_kv), row),
                   pl.BlockSpec((tm, IDX_DIM), row), pl.BlockSpec((tm, n_kv), row),
                   pl.BlockSpec((tm, IDX_DIM), row), feat_spec(n_q), feat_spec(n_iq), feat_spec(n_kv),
                   feat_spec(IDX_HEADS)],
        out_shape=[jax.ShapeDtypeStruct((n, n_kv), F32), jax.ShapeDtypeStruct((n, n_kv), F32),
                   jax.ShapeDtypeStruct((n, IDX_DIM), F32), jax.ShapeDtypeStruct((n, n_kv), BF16),
                   jax.ShapeDtypeStruct((n, IDX_DIM), BF16), feat_shape(n_q, BF16), feat_shape(n_iq, BF16),
                   feat_shape(n_kv, BF16), feat_shape(IDX_HEADS, F32)],
        compiler_params=_cparams(("parallel",)),
        name="attn_proj",
    )(x, g, wrow, wt)


def _bias_kernel(tab_ref, tb_ref):
    h = pl.program_id(0)
    d = pl.program_id(1)
    kj = lax.broadcasted_iota(I32, (QB, QB), 0)
    qi = lax.broadcasted_iota(I32, (QB, QB), 1)
    rel = (d - 1) * QB + kj - qi
    n = jnp.abs(rel)
    large = jnp.full((QB, QB), N_BUCKETS // 4, I32)
    for s in BUCKET_STEPS:
        large = large + jnp.where(n >= s, 1, 0)
    bucket = jnp.where(rel > 0, N_BUCKETS // 2, 0) + jnp.where(n < N_BUCKETS // 4, n, large)
    val = jnp.zeros((QB, QB), F32)
    for b in range(N_BUCKETS):
        val = jnp.where(bucket == b, tab_ref[b, h], val)
    tb_ref[0] = val - tab_ref[FAR_BUCKET, h]


def _bias_call(table):
    return pl.pallas_call(
        _bias_kernel,
        grid=(N_HEADS, 2),
        in_specs=[pl.BlockSpec(memory_space=pltpu.SMEM)],
        out_specs=pl.BlockSpec((1, QB, QB), lambda h, d: (h, d, 0)),
        out_shape=jax.ShapeDtypeStruct((N_HEADS, 2 * QB, QB), F32),
        compiler_params=_cparams(("arbitrary", "arbitrary")),
        name="rel_bias_tiles",
    )(table)


def _attn_kernel(*refs, blocks_per_step, **static):
    def one_block(sub, carry):
        _attn_block(sub, blocks_per_step, *refs, **static)
        return carry

    lax.fori_loop(0, blocks_per_step, one_block, 0)


def _attn_block(sub, blocks_per_step, kend_ref, qt_ref, qit_ref, wit_ref, *refs, diag_off, kstart, k_sel, has_prev,
                row_bits, build):
    block = pl.program_id(1) * blocks_per_step + sub
    if build:
        (kb_ref, vt_ref, kib_ref, k0_ref, v0_ref, ki0_ref, tb_ref, o_ref, key_scr, hi_scr, lo_scr, lo2_scr,
         ls_scr, oacc_scr, kf_scr, vtf_scr, kif_scr) = refs
        seq = kb_ref.shape[0]
        tkf = kf_scr.shape[1]

        @pl.when(block == 0)
        def _():
            kif_scr[0:QB, :] = ki0_ref[...]
            kif_scr[QB:QB + seq, :] = kib_ref[...]
            kif_scr[QB + seq:tkf, :] = jnp.zeros((tkf - QB - seq, IDX_DIM), BF16)
            for g in range(N_KV_HEADS):
                cols = slice(g * HEAD_DIM, (g + 1) * HEAD_DIM)
                kf_scr[g, 0:QB, :] = k0_ref[:, cols]
                for c in range(seq // SEL_TILE):
                    kf_scr[g, QB + c * SEL_TILE:QB + (c + 1) * SEL_TILE, :] = kb_ref[c * SEL_TILE:(c + 1) * SEL_TILE,
                                                                                     cols]
                kf_scr[g, QB + seq:tkf, :] = jnp.zeros((tkf - QB - seq, HEAD_DIM), BF16)
                vrows = slice(g * V_ROWS, g * V_ROWS + HEAD_DIM)
                vtf_scr[vrows, 0:QB] = v0_ref[cols, :]
                for j in range(seq // QB):
                    vtf_scr[vrows, QB + j * QB:QB + (j + 1) * QB] = vt_ref[j, cols, :]
                vtf_scr[vrows, QB + seq:tkf] = jnp.zeros((HEAD_DIM, tkf - QB - seq), BF16)
                extra = lax.broadcasted_iota(I32, (V_ROWS - HEAD_DIM, tkf), 0) == 0
                vtf_scr[g * V_ROWS + HEAD_DIM:(g + 1) * V_ROWS, :] = jnp.where(extra, 1.0, 0.0).astype(BF16)

        k_rows = lambda g, r0, rows: kf_scr[g, pl.ds(r0, rows), :]
        v_cols = lambda g, r0, rows: vtf_scr[g * V_ROWS:(g + 1) * V_ROWS, pl.ds(r0, rows)]
        ki_rows = lambda r0, rows: kif_scr[pl.ds(r0, rows), :]
    else:
        (kf_ref, vtf_ref, kif_ref, tb_ref, o_ref, key_scr, hi_scr, lo_scr, lo2_scr, ls_scr,
         oacc_scr) = refs
        k_rows = lambda g, r0, rows: kf_ref[0, g, pl.ds(r0, rows), :]
        v_cols = lambda g, r0, rows: vtf_ref[0, g * V_ROWS:(g + 1) * V_ROWS, pl.ds(r0, rows)]
        ki_rows = lambda r0, rows: kif_ref[0, pl.ds(r0, rows), :]
    diag = block + diag_off
    nt = diag + 1
    nst = (nt + SEL_TILE // QB - 1) // (SEL_TILE // QB)
    kend = kend_ref[sub]
    qi_cat = jnp.concatenate([qit_ref[sub, h * IDX_DIM:(h + 1) * IDX_DIM, :] for h in range(IDX_HEADS)], axis=1)
    wi = wit_ref[sub]
    sel_iota = lax.broadcasted_iota(I32, (SEL_TILE, QB), 0)

    def tile_start(t):
        return pl.multiple_of(t * QB, QB)

    def sel_start(t):
        return pl.multiple_of(t * SEL_TILE, SEL_TILE)

    def score_rows(r0):
        s = _dot(ki_rows(r0, SEL_TILE), qi_cat)
        sc = jnp.zeros((SEL_TILE, QB), F32)
        for h in range(IDX_HEADS):
            sc = sc + wi[h:h + 1, :] * jnp.maximum(s[:, h * QB:(h + 1) * QB], 0.0)
        bits = lax.bitcast_convert_type(sc + 0.0, I32)
        key = jnp.where(bits >= 0, bits, bits ^ 0x7FFFFFFF)
        rows = r0 + sel_iota
        adm = (rows >= kstart) & (rows < kend)
        key = jnp.where(adm, key, INT_MIN)
        key_scr[pl.ds(r0, SEL_TILE), :] = key
        hi_scr[pl.ds(r0, SEL_TILE), :] = (key >> 16).astype(I16)
        lo_scr[pl.ds(r0, SEL_TILE), :] = ((key & 0xFFFF) + I16_MIN).astype(I16)

    pair_rows = 2 * SEL_TILE

    def score_pair(t, carry):
        r0 = pl.multiple_of(t * pair_rows, pair_rows)
        score_rows(r0)
        score_rows(r0 + SEL_TILE)
        return carry

    lax.fori_loop(0, nst // 2, score_pair, 0)

    @pl.when(nst % 2 == 1)
    def _():
        score_rows(sel_start(nst - 1))
        blank = jnp.full((SEL_TILE, QB), I16_MIN, I16)
        hi_scr[pl.ds(sel_start(nst), SEL_TILE), :] = blank
        lo_scr[pl.ds(sel_start(nst), SEL_TILE), :] = blank

    pair_vregs = pair_rows // PACK16

    def as_packed(v):
        return jnp.broadcast_to(v, (PACK16, QB)).astype(I16)

    def threshold(n_pair):
        def count16(src_scr, cand):
            c16 = as_packed(cand)[None]
            acc = jnp.zeros((PACK16, QB), BF16)
            for t in range(n_pair):
                v = src_scr[t * pair_rows:(t + 1) * pair_rows, :].reshape(pair_vregs, PACK16, QB)
                ind = jnp.where(v >= c16, jnp.ones((), BF16), jnp.zeros((), BF16))
                parts = [ind[i] for i in range(pair_vregs)]
                while len(parts) > 1:
                    parts = [parts[i] + parts[i + 1] for i in range(0, len(parts), 2)]
                acc = acc + parts[0]
            return acc.astype(F32).sum(axis=0, keepdims=True)

        def search16(src_scr, base, c_start):
            def step(b, carry):
                t_acc, c_acc = carry
                cand = t_acc + lax.shift_left(jnp.int32(1), 15 - b)
                c = base + count16(src_scr, cand)
                ok = c >= k_sel
                return jnp.where(ok, cand, t_acc), jnp.where(ok, c, c_acc)

            return lax.fori_loop(0, 16, step, (jnp.full((1, QB), I16_MIN, I32), c_start))

        thr_hi, c_hi = search16(hi_scr, 0.0, jnp.full((1, QB), n_pair * pair_rows, F32))
        above = jnp.where(thr_hi == -I16_MIN - 1, 0.0, count16(hi_scr, thr_hi + 1))
        hi16 = as_packed(thr_hi)[None]
        for t in range(n_pair):
            rows = slice(t * pair_rows, (t + 1) * pair_rows)
            hi = hi_scr[rows, :].reshape(pair_vregs, PACK16, QB)
            lo = lo_scr[rows, :].reshape(pair_vregs, PACK16, QB)
            lo2_scr[rows, :] = jnp.where(hi == hi16, lo, jnp.full((), I16_MIN, I16)).reshape(pair_rows, QB)
        thr_lo, c_ge = search16(lo2_scr, above, c_hi)
        return thr_hi * 65536 + (thr_lo - I16_MIN), c_ge

    def count(pred):
        def body(t, acc):
            r0 = sel_start(t)
            ind = jnp.where(pred(key_scr[pl.ds(r0, SEL_TILE), :], r0 + sel_iota), 1, 0)
            return acc + ind.reshape(SEL_TILE // SUBLANES, SUBLANES, QB).sum(axis=0)
        acc = lax.fori_loop(0, nst, body, jnp.zeros((SUBLANES, QB), I32))
        return acc.sum(axis=0, keepdims=True)

    max_pairs = hi_scr.shape[0] // pair_rows
    thr, c_ge = lax.switch((nst + 1) // 2 - 1, [functools.partial(threshold, n) for n in range(1, max_pairs + 1)])

    tied = (c_ge > k_sel) & (thr > INT_MIN)
    big = jnp.full((1, QB), 2 ** row_bits, I32)

    def tie_limit():
        need = k_sel - count(lambda kt, rows: kt > thr)

        def lim_step(b, lim):
            cand = lim + lax.shift_left(jnp.int32(1), row_bits - 1 - b)
            c = count(lambda kt, rows: (kt == thr) & (rows < cand))
            return jnp.where(c < need, cand, lim)

        lim = lax.fori_loop(0, row_bits, lim_step, jnp.zeros((1, QB), I32))
        return jnp.where(tied, lim, big)

    rlim = lax.cond(jnp.max(tied.astype(I32)) > 0, tie_limit, lambda: big)
    rlim = jnp.where(thr > INT_MIN, rlim, -1)

    gq = GROUP * QB
    qg = [jnp.concatenate([qt_ref[sub, (g * GROUP + r) * HEAD_DIM:(g * GROUP + r + 1) * HEAD_DIM, :]
                           for r in range(GROUP)], axis=1) for g in range(N_KV_HEADS)]

    def logits_rows(r0, rows, m8, bias_of_head):
        kt = key_scr[pl.ds(r0, rows), :]
        keep = (kt > thr) | ((kt == thr) & (sel_iota[:rows] <= rlim - r0))
        new_m8 = []
        for g in range(N_KV_HEADS):
            lg = _dot(k_rows(g, r0, rows), qg[g])
            parts = []
            for r in range(GROUP):
                part = lg[:, r * QB:(r + 1) * QB]
                if bias_of_head is not None:
                    part = part + bias_of_head(g * GROUP + r)
                parts.append(jnp.where(keep, part, NEG_INF))
            lg = jnp.concatenate(parts, axis=1)
            ls_scr[pl.ds(r0, rows), g * gq:(g + 1) * gq] = lg
            new_m8.append(jnp.maximum(m8[g], lg.reshape(rows // SUBLANES, SUBLANES, gq).max(axis=0)))
        return tuple(new_m8)

    far_bias = None
    m8 = tuple(jnp.full((SUBLANES, gq), NEG_INF, F32) for _ in range(N_KV_HEADS))
    n_far = diag - 1 if has_prev else diag

    def far_pair(t, m8):
        r0 = pl.multiple_of(t * pair_rows, pair_rows)
        return logits_rows(r0 + SEL_TILE, SEL_TILE, logits_rows(r0, SEL_TILE, m8, far_bias), far_bias)

    m8 = lax.fori_loop(0, n_far // 4, far_pair, m8)
    m8 = lax.cond(n_far % 4 >= 2, lambda m: logits_rows(sel_start(n_far // 4 * 2), SEL_TILE, m, far_bias),
                  lambda m: m, m8)
    m8 = lax.cond(n_far % 2 == 1, lambda m: logits_rows(tile_start(n_far - 1), QB, m, far_bias), lambda m: m, m8)
    if has_prev:
        m8 = logits_rows(tile_start(diag - 1), 2 * QB, m8, lambda h: tb_ref[h])
    else:
        m8 = logits_rows(tile_start(diag), QB, m8, lambda h: tb_ref[h, QB:, :])
    m = [jnp.max(m8[g], axis=0, keepdims=True) for g in range(N_KV_HEADS)]

    @pl.when(nt % 2 == 1)
    def _():
        ls_scr[pl.ds(tile_start(nt), QB), :] = jnp.full((QB, N_HEADS * QB), NEG_INF, F32)

    oacc_scr[...] = jnp.zeros(oacc_scr.shape, F32)

    def pv_rows(r0):
        for g in range(N_KV_HEADS):
            p = jnp.exp(ls_scr[pl.ds(r0, SEL_TILE), g * gq:(g + 1) * gq] - m[g])
            oacc_scr[g] += _dot(v_cols(g, r0, SEL_TILE), p.astype(BF16))

    def pv_pair(t, carry):
        r0 = pl.multiple_of(t * pair_rows, pair_rows)
        pv_rows(r0)
        pv_rows(r0 + SEL_TILE)
        return carry

    lax.fori_loop(0, nst // 2, pv_pair, 0)

    @pl.when(nst % 2 == 1)
    def _():
        pv_rows(sel_start(nst - 1))
    out_rows = []
    for g in range(N_KV_HEADS):
        o_g = oacc_scr[g, :HEAD_DIM] / oacc_scr[g, HEAD_DIM:HEAD_DIM + 1]
        out_rows.extend(o_g[:, r * QB:(r + 1) * QB] for r in range(GROUP))
    o_ref[pl.ds(pl.multiple_of(sub * QB, QB), QB), :] = jnp.concatenate(out_rows, axis=0).T.astype(BF16)


def _attn_call(kend, qt, qit, wit, frames, tb, *, n_seq, build, n_out_rows, diag_off, kstart, k_sel, name):
    n_qb = kend.shape[0]
    kv_dim = N_KV_HEADS * HEAD_DIM
    if build:
        seq = frames[0].shape[0] // n_seq
        tkf = -(-(QB + seq) // (2 * SEL_TILE)) * 2 * SEL_TILE
        frame_specs = [pl.BlockSpec((seq, kv_dim), lambda s, i: (s, 0)),
                       pl.BlockSpec((seq // QB, kv_dim, QB), lambda s, i: (s, 0, 0)),
                       pl.BlockSpec((seq, IDX_DIM), lambda s, i: (s, 0))] + [_const_spec(a.shape) for a in frames[3:]]
        frame_scratch = [pltpu.VMEM((N_KV_HEADS, tkf, HEAD_DIM), BF16), pltpu.VMEM((N_KV_HEADS * V_ROWS, tkf), BF16),
                         pltpu.VMEM((tkf, IDX_DIM), BF16)]
    else:
        kf, vtf, kif = frames
        tkf = kf.shape[2]
        assert vtf.shape[2] == tkf and kif.shape[1] == tkf
        frame_specs = [pl.BlockSpec((1, N_KV_HEADS, tkf, HEAD_DIM), lambda s, i: (s, 0, 0, 0)),
                       pl.BlockSpec((1, N_KV_HEADS * V_ROWS, tkf), lambda s, i: (s, 0, 0)),
                       pl.BlockSpec((1, tkf, IDX_DIM), lambda s, i: (s, 0, 0))]
        frame_scratch = []
    assert tkf % (2 * SEL_TILE) == 0
    assert tkf // PACK16 <= 256, "per-slot bf16 counts must stay exact"
    has_prev = diag_off >= 1
    row_bits = max(1, (tkf - 1).bit_length())
    bps = ATTN_BLOCKS_PER_STEP if n_qb % ATTN_BLOCKS_PER_STEP == 0 else 1
    steps = n_qb // bps
    feat = lambda a: pl.BlockSpec((bps, a.shape[1], QB), lambda s, i: (s * steps + i, 0, 0))
    return pl.pallas_call(
        functools.partial(_attn_kernel, blocks_per_step=bps, diag_off=diag_off, kstart=kstart, k_sel=k_sel,
                          has_prev=has_prev, row_bits=row_bits, build=build),
        grid=(n_seq, steps),
        in_specs=[pl.BlockSpec((bps, 1, QB), lambda s, i: (i, 0, 0)), feat(qt), feat(qit), feat(wit)] + frame_specs
                 + [_const_spec(tb.shape)],
        out_specs=pl.BlockSpec((bps * QB, N_HEADS * HEAD_DIM), lambda s, i: (s * steps + i, 0)),
        out_shape=jax.ShapeDtypeStruct((n_out_rows, N_HEADS * HEAD_DIM), BF16),
        scratch_shapes=[pltpu.VMEM((tkf, QB), I32)] + [pltpu.VMEM((tkf, QB), I16)] * 3
                       + [pltpu.VMEM((tkf, N_HEADS * QB), F32), pltpu.VMEM((N_KV_HEADS, V_ROWS, GROUP * QB), F32)]
                       + frame_scratch,
        compiler_params=_cparams(("arbitrary", "arbitrary")),
        name=name,
    )(kend, qt, qit, wit, *frames, tb)


def _softplus(x):
    return jnp.maximum(x, 0.0) + jnp.log1p(jnp.exp(-jnp.abs(x)))


def _mixb_kernel(x_ref, h0_ref, buf_ref, g_ref, wy_ref, by_ref, wx_ref, bx_ref, cw_ref, cb_ref, wr_ref, br_ref,
                 wi_ref, bi_ref, lam_ref, *rest, tm):
    pre_ref, hlast_ref, tail_ref, xe_scr, h_scr = rest[-5:]

    @pl.when(pl.program_id(1) == 0)
    def _():
        xe_scr[0:SUBLANES, :] = buf_ref[0]
        h_scr[...] = h0_ref[0]

    xn = _rmsnorm(x_ref[...], g_ref[...]).astype(BF16)
    y = jax.nn.gelu(_dot(xn, wy_ref[...]) + by_ref[...])
    xe_scr[SUBLANES:SUBLANES + tm, :] = _dot(xn, wx_ref[...]) + bx_ref[...]
    xc = cb_ref[...] + cw_ref[0:1, :] * xe_scr[SUBLANES - 3:SUBLANES - 3 + tm, :]
    for j in range(1, CONV_B):
        xc = xc + cw_ref[j:j + 1, :] * xe_scr[SUBLANES - 3 + j:SUBLANES - 3 + j + tm, :]
    xcb = xc.astype(BF16)
    d_rnn = xc.shape[1]
    blk = d_rnn // LRU_BLOCKS
    r_pre = jnp.concatenate([_dot(xcb[:, n * blk:(n + 1) * blk], wr_ref[n]) for n in range(LRU_BLOCKS)], axis=1)
    i_pre = jnp.concatenate([_dot(xcb[:, n * blk:(n + 1) * blk], wi_ref[n]) for n in range(LRU_BLOCKS)], axis=1)
    r = jax.nn.sigmoid(r_pre + br_ref[...])
    ig = jax.nn.sigmoid(i_pre + bi_ref[...])
    log_a = -LRU_C * r * _softplus(-lam_ref[...])
    a = jnp.exp(log_a)
    u = jnp.sqrt(jnp.tanh(-log_a) * (1.0 + a * a)) * (ig * xc)
    a = a.reshape(tm // SUBLANES, SUBLANES, d_rnn)
    u = u.reshape(tm // SUBLANES, SUBLANES, d_rnn)
    row_in_group = lax.broadcasted_iota(I32, a.shape, 1)
    s = 1
    while s < SUBLANES:
        a_sh = jnp.where(row_in_group >= s, pltpu.roll(a, s, 1), 1.0)
        u_sh = jnp.where(row_in_group >= s, pltpu.roll(u, s, 1), 0.0)
        u = a * u_sh + u
        a = a * a_sh
        s *= 2
    carry = h_scr[...]
    h_groups = []
    for grp in range(tm // SUBLANES):
        h_grp = a[grp] * carry + u[grp]
        carry = h_grp[SUBLANES - 1:SUBLANES, :]
        h_groups.append(h_grp)
    h = jnp.concatenate(h_groups, axis=0)
    pre_ref[...] = (h * y).astype(BF16)
    h_scr[...] = carry
    hlast_ref[0] = carry
    tail = xe_scr[tm:tm + SUBLANES, :]
    xe_scr[0:SUBLANES, :] = tail
    tail_ref[0] = tail


def _mixc_kernel(x_ref, buf_ref, g_ref, win_ref, cw_ref, *rest, tm):
    pre_ref, tail_ref, pe_scr = rest[-3:]

    @pl.when(pl.program_id(1) == 0)
    def _():
        pe_scr[0:SUBLANES, :] = buf_ref[0]

    d = x_ref.shape[1]
    xn = _rmsnorm(x_ref[...], g_ref[...]).astype(BF16)
    z = _dot(xn, win_ref[...])
    pe_scr[SUBLANES:SUBLANES + tm, :] = z[:, d:2 * d] * z[:, 2 * d:]
    conv = cw_ref[0:1, :] * pe_scr[SUBLANES - 2:SUBLANES - 2 + tm, :]
    for j in range(1, CONV_C):
        conv = conv + cw_ref[j:j + 1, :] * pe_scr[SUBLANES - 2 + j:SUBLANES - 2 + j + tm, :]
    pre_ref[...] = (z[:, :d] * conv).astype(BF16)
    tail = pe_scr[tm:tm + SUBLANES, :]
    pe_scr[0:SUBLANES, :] = tail
    tail_ref[0] = tail


def _shared_out(buf, index):
    if buf is None:
        return {"specs": [], "args": (), "aliases": {}}
    return {"specs": [pl.BlockSpec(memory_space=pl.ANY)], "args": (buf,), "aliases": {index: 0}}


def _seq_specs(n_seq, t_len, row_off, d, tm_max):
    tm = tm_max if t_len % tm_max == 0 else min(TM_SEQ, t_len)
    n_tt = t_len // tm
    off = row_off // tm
    xrow = pl.BlockSpec((tm, d), lambda s, j: (off + s * n_tt + j, 0))
    state = lambda rows: pl.BlockSpec((1, rows, d), lambda s, j: (s, 0, 0))
    return tm, n_tt, xrow, state


def _mixb_call(x, pre_buf, h0, buf, g, wy, by, wx, bx, cw, cb, wr, br, wi, bi, lam, *, n_seq, t_len, row_off,
               name):
    d = x.shape[1]
    tm, n_tt, xrow, state = _seq_specs(n_seq, t_len, row_off, d, TM_SEQ)
    consts = (g, wy, by, wx, bx, cw, cb, wr, br, wi, bi, lam)
    shared = _shared_out(pre_buf, 3 + len(consts))
    return pl.pallas_call(
        functools.partial(_mixb_kernel, tm=tm),
        grid=(n_seq, n_tt),
        in_specs=[xrow, state(1), state(SUBLANES)] + [_const_spec(c.shape) for c in consts] + shared["specs"],
        out_specs=[xrow, state(1), state(SUBLANES)],
        out_shape=[jax.ShapeDtypeStruct((x.shape[0], d), BF16), jax.ShapeDtypeStruct((n_seq, 1, d), F32),
                   jax.ShapeDtypeStruct((n_seq, SUBLANES, d), F32)],
        scratch_shapes=[pltpu.VMEM((tm + SUBLANES, d), F32), pltpu.VMEM((1, d), F32)],
        input_output_aliases=shared["aliases"],
        compiler_params=_cparams(("arbitrary", "arbitrary")),
        name=name,
    )(x, h0, buf, *consts, *shared["args"])


def _mixc_call(x, pre_buf, buf, g, win, cw, *, n_seq, t_len, row_off, name):
    d = x.shape[1]
    tm, n_tt, xrow, state = _seq_specs(n_seq, t_len, row_off, d, TM_CONV)
    consts = (g, win, cw)
    shared = _shared_out(pre_buf, 2 + len(consts))
    return pl.pallas_call(
        functools.partial(_mixc_kernel, tm=tm),
        grid=(n_seq, n_tt),
        in_specs=[xrow, state(SUBLANES)] + [_const_spec(c.shape) for c in consts] + shared["specs"],
        out_specs=[xrow, state(SUBLANES)],
        out_shape=[jax.ShapeDtypeStruct((x.shape[0], d), BF16), jax.ShapeDtypeStruct((n_seq, SUBLANES, d), F32)],
        scratch_shapes=[pltpu.VMEM((tm + SUBLANES, d), F32)],
        input_output_aliases=shared["aliases"],
        compiler_params=_cparams(("arbitrary", "arbitrary")),
        name=name,
    )(x, buf, *consts, *shared["args"])


def _front_pad_rows(a, rows):
    return jnp.pad(a, ((0, 0), (rows - a.shape[1], 0), (0, 0)))


def _frame(parts, axis):
    f = jnp.concatenate(parts, axis=axis).astype(BF16)
    pad = [(0, 0)] * f.ndim
    pad[axis] = (0, -f.shape[axis] % (2 * SEL_TILE))
    return jnp.pad(f, pad)


def _k_frame(parts):
    f = _frame(parts, 1)
    n_seq, rows = f.shape[0], f.shape[1]
    return f.reshape(n_seq, rows, N_KV_HEADS, HEAD_DIM).transpose(0, 2, 1, 3)


def _v_frame(parts):
    f = _frame(parts, 2)
    n_seq, tkf = f.shape[0], f.shape[2]
    extra = jnp.zeros((n_seq, N_KV_HEADS, V_ROWS - HEAD_DIM, tkf), BF16).at[:, :, 0].set(1)
    f = jnp.concatenate([f.reshape(n_seq, N_KV_HEADS, HEAD_DIM, tkf), extra], axis=2)
    return f.reshape(n_seq, N_KV_HEADS * V_ROWS, tkf)


def _cols(a, start, n):
    b0, b1 = start // QB, -(-(start + n) // QB)
    c = a[b0:b1].transpose(1, 0, 2).reshape(a.shape[1], (b1 - b0) * QB)
    return c[:, start - b0 * QB:start - b0 * QB + n]


def _seq_major(a, n_seq, q_len):
    return a.reshape(a.shape[0], n_seq, q_len).transpose(1, 0, 2)


def _lane_pad_cols(a, n_seq, q_len):
    return jnp.pad(_seq_major(a, n_seq, q_len), ((0, 0), (0, 0), (0, QB - q_len)))


def kernel(x_prompt, x_sample, cache_k, cache_v, cache_kidx, state_h, state_conv_b, state_conv_c, meta_tokens,
           rel_bias, norm_mix, norm_ffn, norm_final, a_wq, a_wk, a_wv, a_wo, a_wiq, a_wik, a_wiw, b_wy, b_by, b_wx,
           b_bx, b_conv_w, b_conv_b, b_wr, b_br, b_wi, b_bi, b_lam, b_wo, b_bo, c_win, c_conv_w, c_wo, f_wg, f_wu,
           f_wd):
    bp, seq, d = x_prompt.shape
    bs, dec = x_sample.shape[0], x_sample.shape[1]
    past = cache_k.shape[2]
    n_meta = meta_tokens.shape[0]
    depth = norm_mix.shape[0]
    assert seq % QB == 0 and seq % TM_SEQ == 0 and past % QB == 0 and dec <= QB and n_meta <= QB
    assert dec % PACK16 == 0 and n_meta % PACK16 == 0 and (bs * dec) % n_meta == 0

    n_main, n_samp = bp * seq, bs * dec
    n_tail = -(-(n_samp + n_meta) // TM_TAIL) * TM_TAIL
    x_main = x_prompt.reshape(n_main, d)
    x_tail = jnp.concatenate([x_sample.reshape(n_samp, d), meta_tokens.astype(F32),
                              jnp.zeros((n_tail - n_samp - n_meta, d), F32)], axis=0)
    tail_fill = jnp.zeros((n_tail - n_samp - n_meta, d), BF16)

    k_sel_p = min(TOP_K_MAX, seq // 4)
    k_sel_s = min(TOP_K_MAX, (past + dec) // 4)
    n_qb = seq // QB
    kv_dim = N_KV_HEADS * HEAD_DIM

    tau = jnp.arange(seq, dtype=I32)
    kend_main = (QB + CHUNK * (tau // CHUNK + 1)).reshape(n_qb, 1, QB)
    lane = jnp.arange(QB, dtype=I32)
    kend_samp = jnp.minimum(past + dec, CHUNK * ((past + jnp.minimum(lane, dec - 1)) // CHUNK + 1)).reshape(1, 1, QB)
    kend_meta = jnp.full((1, 1, QB), n_meta, I32)

    tb = _bias_call(rel_bias.astype(F32))
    zeros_bias = jnp.zeros((1, d), F32)
    gfin = norm_final.reshape(1, d).astype(F32)

    new = {name: [] for name in ("k_p", "v_p", "ki_p", "h_p", "cb_p", "cc_p", "k_s", "v_s", "ki_s", "h_s", "cb_s",
                                 "cc_s")}
    for l in range(depth):
        s = l // N_MIXERS
        g_mix = norm_mix[l].reshape(1, d)
        if l % N_MIXERS == 0:
            wrow = jnp.concatenate([a_wk[s], a_wv[s], a_wik[s]], axis=1).astype(BF16)
            wt = jnp.concatenate([a_wq[s].T * HEAD_DIM ** -0.5, a_wiq[s].T * IDX_DIM ** -0.5, a_wv[s].T, a_wiw[s].T,
                                  jnp.zeros((SUBLANES, d), F32)], axis=0).astype(BF16)
            k, v, ki, kb, kib, qt, qit, vt, wit = _proj_call(x_main, g_mix, wrow, wt)
            k_t, v_t, ki_t, _, _, qt_t, qit_t, vt_t, wit_t = _proj_call(x_tail, g_mix, wrow, wt)

            k_meta, v_meta, ki_meta = (a[n_samp:n_samp + n_meta] for a in (k_t, v_t, ki_t))
            k_main, v_main, ki_main = (a.reshape(bp, seq, -1) for a in (k, v, ki))
            k_samp, v_samp, ki_samp = (a[:n_samp].reshape(bs, dec, -1) for a in (k_t, v_t, ki_t))
            bc = lambda a: jnp.broadcast_to(a[None], (bp,) + a.shape)
            new["k_p"].append(jnp.concatenate([bc(k_meta), k_main], axis=1).reshape(bp, n_meta + seq, N_KV_HEADS,
                                                                                   HEAD_DIM))
            new["v_p"].append(jnp.concatenate([bc(v_meta), v_main], axis=1).reshape(bp, n_meta + seq, N_KV_HEADS,
                                                                                   HEAD_DIM))
            new["ki_p"].append(jnp.concatenate([bc(ki_meta), ki_main], axis=1))
            new["k_s"].append(k_samp.reshape(bs, dec, N_KV_HEADS, HEAD_DIM))
            new["v_s"].append(v_samp.reshape(bs, dec, N_KV_HEADS, HEAD_DIM))
            new["ki_s"].append(ki_samp)

            samp_cols = lambda a: _cols(a, 0, n_samp)
            meta_cols = lambda a: _cols(a, n_samp, n_meta)
            vt_meta = meta_cols(vt_t)
            lead = lambda a, axis: jnp.pad(a, [(QB - n_meta, 0) if ax == axis else (0, 0)
                                               for ax in range(2)]).astype(BF16)
            pre_main = _attn_call(
                kend_main, qt, qit, wit, (kb, vt, kib, lead(k_meta, 0), lead(vt_meta, 1), lead(ki_meta, 0)), tb,
                n_seq=bp, build=True,
                n_out_rows=n_main, diag_off=1, kstart=QB - n_meta, k_sel=k_sel_p, name="dsa_attn_prompt")
            vtf_samp = _v_frame([cache_v[s].reshape(bs, past, kv_dim).transpose(0, 2, 1),
                                 _seq_major(samp_cols(vt_t), bs, dec)])
            o_samp = _attn_call(
                kend_samp, _lane_pad_cols(samp_cols(qt_t), bs, dec),
                _lane_pad_cols(samp_cols(qit_t), bs, dec), _lane_pad_cols(samp_cols(wit_t), bs, dec),
                (_k_frame([cache_k[s].reshape(bs, past, kv_dim), k_samp]), vtf_samp,
                 _frame([cache_kidx[s], ki_samp], 1)), tb, n_seq=bs, build=False,
                n_out_rows=bs * QB, diag_off=past // QB, kstart=0, k_sel=k_sel_s, name="dsa_attn_sample")
            o_meta = _attn_call(
                kend_meta, _lane_pad_cols(meta_cols(qt_t), 1, n_meta),
                _lane_pad_cols(meta_cols(qit_t), 1, n_meta), _lane_pad_cols(meta_cols(wit_t), 1, n_meta),
                (_k_frame([k_meta[None]]), _v_frame([vt_meta[None]]), _frame([ki_meta[None]], 1)), tb,
                n_seq=1, build=False, n_out_rows=QB, diag_off=0, kstart=0, k_sel=k_sel_p, name="dsa_attn_meta")
            pre_tail = jnp.concatenate([o_samp.reshape(bs, QB, d)[:, :dec].reshape(n_samp, d), o_meta[:n_meta],
                                        tail_fill], axis=0)
            wo, bo = a_wo[s].astype(BF16), zeros_bias
        elif l % N_MIXERS == 1:
            consts = (g_mix, b_wy[s].astype(BF16), b_by[s].reshape(1, d), b_wx[s].astype(BF16), b_bx[s].reshape(1, d),
                      jnp.pad(b_conv_w[s], ((0, SUBLANES - CONV_B), (0, 0))), b_conv_b[s].reshape(1, d),
                      b_wr[s].astype(BF16), b_br[s].reshape(1, d), b_wi[s].astype(BF16), b_bi[s].reshape(1, d),
                      b_lam[s].reshape(1, d))
            pre_tail = jnp.zeros((n_tail, d), BF16)
            pre_tail, h_m, tail_m = _mixb_call(x_tail, pre_tail, jnp.zeros((1, 1, d), F32),
                                               jnp.zeros((1, SUBLANES, d), F32), *consts, n_seq=1, t_len=n_meta,
                                               row_off=n_samp, name="rglru_meta")
            pre_tail, h_s, tail_s = _mixb_call(x_tail, pre_tail, state_h[s].reshape(bs, 1, d),
                                               _front_pad_rows(state_conv_b[s], SUBLANES), *consts, n_seq=bs,
                                               t_len=dec, row_off=0, name="rglru_sample")
            pre_main, h_p, tail_p = _mixb_call(x_main, None, jnp.broadcast_to(h_m, (bp, 1, d)),
                                               jnp.broadcast_to(tail_m, (bp, SUBLANES, d)), *consts, n_seq=bp,
                                               t_len=seq, row_off=0, name="rglru_prompt")
            new["h_p"].append(h_p.reshape(bp, d))
            new["cb_p"].append(tail_p[:, SUBLANES - (CONV_B - 1):])
            new["h_s"].append(h_s.reshape(bs, d))
            new["cb_s"].append(tail_s[:, SUBLANES - (CONV_B - 1):])
            wo, bo = b_wo[s].astype(BF16), b_bo[s].reshape(1, d)
        else:
            consts = (g_mix, c_win[s].astype(BF16), jnp.pad(c_conv_w[s], ((0, SUBLANES - CONV_C), (0, 0))))
            pre_tail = jnp.zeros((n_tail, d), BF16)
            pre_tail, tail_m = _mixc_call(x_tail, pre_tail, jnp.zeros((1, SUBLANES, d), F32), *consts, n_seq=1,
                                          t_len=n_meta, row_off=n_samp, name="sconv_meta")
            pre_tail, tail_s = _mixc_call(x_tail, pre_tail, _front_pad_rows(state_conv_c[s], SUBLANES), *consts,
                                          n_seq=bs, t_len=dec, row_off=0, name="sconv_sample")
            pre_main, tail_p = _mixc_call(x_main, None, jnp.broadcast_to(tail_m, (bp, SUBLANES, d)), *consts,
                                          n_seq=bp, t_len=seq, row_off=0, name="sconv_prompt")
            new["cc_p"].append(tail_p[:, SUBLANES - (CONV_C - 1):])
            new["cc_s"].append(tail_s[:, SUBLANES - (CONV_C - 1):])
            wo, bo = c_wo[s].astype(BF16), zeros_bias
        ffn = (wo, bo, norm_ffn[l].reshape(1, d), f_wg[l].astype(BF16), f_wu[l].astype(BF16), f_wd[l].astype(BF16),
               gfin, l == depth - 1)
        x_main = _ffn_call(x_main, pre_main, *ffn)
        x_tail = _ffn_call(x_tail, pre_tail, *ffn)

    y_prompt = x_main.reshape(bp, seq, d)
    y_sample = x_tail[:n_samp].reshape(bs, dec, d)
    st = lambda name: jnp.stack(new[name])
    return (y_prompt, y_sample, st("k_p"), st("v_p"), st("ki_p"), st("h_p"), st("cb_p"), st("cc_p"),
            st("k_s"), st("v_s"), st("ki_s"), st("h_s"), st("cb_s"), st("cc_s"))
```

```python
import functools

import jax
import jax.numpy as jnp
from jax import lax
from jax.experimental import pallas as pl
from jax.experimental.pallas import tpu as pltpu

F32 = jnp.float32
BF16 = jnp.bfloat16
I32 = jnp.int32
I16 = jnp.int16

CHUNK = 64
N_MIXERS = 3
N_HEADS = 16
N_KV_HEADS = 4
GROUP = N_HEADS // N_KV_HEADS
HEAD_DIM = 64
IDX_HEADS = 8
IDX_DIM = 64
TOP_K_MAX = 256
NEG_INF = -1e30
N_BUCKETS = 32
LRU_BLOCKS = 4
CONV_B = 4
CONV_C = 3
LRU_C = 8.0
RMS_EPS = 1e-6
BUCKET_STEPS = (12, 16, 23, 32, 46, 64, 91)
FAR_BUCKET = N_BUCKETS // 2 - 1

SUBLANES = 8
TM_ROWS = 512
TM_TAIL = 256
TM_SEQ = 256
TM_CONV = 512
QB = 128
ATTN_BLOCKS_PER_STEP = 4
SEL_TILE = 2 * QB
INT_MIN = -2 ** 31
I16_MIN = -2 ** 15
PACK16 = 2 * SUBLANES
V_ROWS = HEAD_DIM + PACK16
VMEM_LIMIT = 56 * 1024 * 1024


def _cparams(sem):
    return pltpu.CompilerParams(dimension_semantics=sem, vmem_limit_bytes=VMEM_LIMIT)


def _const_spec(shape):
    nd = len(shape)
    return pl.BlockSpec(shape, lambda *_: (0,) * nd, pipeline_mode=pl.Buffered(1))


def _rmsnorm(x, g):
    ms = jnp.mean(x * x, axis=-1, keepdims=True)
    return x * lax.rsqrt(ms + RMS_EPS) * g


def _dot(a, b):
    return jnp.dot(a, b, preferred_element_type=F32)


def _dot_nt(a, b):
    return lax.dot_general(a, b, (((1,), (1,)), ((), ())), preferred_element_type=F32)


def _ffn_kernel(x_ref, pre_ref, wo_ref, bo_ref, gf_ref, wg_ref, wu_ref, wd_ref, gfin_ref, out_ref, *, final_norm):
    x1 = x_ref[...] + _dot(pre_ref[...], wo_ref[...]) + bo_ref[...]
    xn = _rmsnorm(x1, gf_ref[...]).astype(BF16)
    gt = _dot(xn, wg_ref[...])
    hm = (gt * jax.nn.sigmoid(gt) * _dot(xn, wu_ref[...])).astype(BF16)
    acc = x1 + _dot(hm, wd_ref[...])
    if final_norm:
        acc = _rmsnorm(acc, gfin_ref[...])
    out_ref[...] = acc


def _ffn_call(x, pre, wo, bo, gf, wg, wu, wd, gfin, final_norm):
    n, d = x.shape
    row = lambda i: (i, 0)
    tm = TM_ROWS if n % TM_ROWS == 0 else TM_TAIL
    return pl.pallas_call(
        functools.partial(_ffn_kernel, final_norm=final_norm),
        grid=(n // tm,),
        in_specs=[pl.BlockSpec((tm, d), row), pl.BlockSpec((tm, pre.shape[1]), row),
                  _const_spec(wo.shape), _const_spec(bo.shape), _const_spec(gf.shape),
                  _const_spec(wg.shape), _const_spec(wu.shape), _const_spec(wd.shape), _const_spec(gfin.shape)],
        out_specs=pl.BlockSpec((tm, d), row),
        out_shape=jax.ShapeDtypeStruct((n, d), F32),
        compiler_params=_cparams(("parallel",)),
        name="outproj_swiglu",
    )(x, pre, wo, bo, gf, wg, wu, wd, gfin)


def _proj_kernel(x_ref, g_ref, wrow_ref, wt_ref, k_ref, v_ref, ki_ref, kb_ref, kib_ref, qt_ref, qit_ref, vt_ref,
                 wit_ref, *, n_kv, n_q, n_iq):
    xn = _rmsnorm(x_ref[...], g_ref[...]).astype(BF16)
    row = _dot(xn, wrow_ref[...])
    k_ref[...] = row[:, :n_kv]
    v_ref[...] = row[:, n_kv:2 * n_kv]
    ki_ref[...] = row[:, 2 * n_kv:]
    kb_ref[...] = row[:, :n_kv].astype(BF16)
    kib_ref[...] = row[:, 2 * n_kv:].astype(BF16)
    tt = _dot_nt(wt_ref[...], xn)
    for b in range(x_ref.shape[0] // QB):
        blk = tt[:, b * QB:(b + 1) * QB]
        qt_ref[b] = blk[:n_q].astype(BF16)
        qit_ref[b] = blk[n_q:n_q + n_iq].astype(BF16)
        vt_ref[b] = blk[n_q + n_iq:n_q + n_iq + n_kv].astype(BF16)
        wit_ref[b] = blk[n_q + n_iq + n_kv:n_q + n_iq + n_kv + IDX_HEADS] * (IDX_HEADS ** -0.5)


def _proj_call(x, g, wrow, wt):
    n, d = x.shape
    n_kv = N_KV_HEADS * HEAD_DIM
    n_q = N_HEADS * HEAD_DIM
    n_iq = IDX_HEADS * IDX_DIM
    row = lambda i: (i, 0)
    tm = TM_ROWS if n % TM_ROWS == 0 else TM_TAIL
    feat_spec = lambda f: pl.BlockSpec((tm // QB, f, QB), lambda i: (i, 0, 0))
    feat_shape = lambda f, dt: jax.ShapeDtypeStruct((n // QB, f, QB), dt)
    return pl.pallas_call(
        functools.partial(_proj_kernel, n_kv=n_kv, n_q=n_q, n_iq=n_iq),
        grid=(n // tm,),
        in_specs=[pl.BlockSpec((tm, d), row), _const_spec(g.shape), _const_spec(wrow.shape),
                  _const_spec(wt.shape)],
        out_specs=[pl.BlockSpec((tm, n_kv), row), pl.BlockSpec((tm, n_kv), row),
                   pl.BlockSpec((tm, IDX_DIM), row), pl.BlockSpec((tm, n_kv), row),
                   pl.BlockSpec((tm, IDX_DIM), row), feat_spec(n_q), feat_spec(n_iq), feat_spec(n_kv),
                   feat_spec(IDX_HEADS)],
        out_shape=[jax.ShapeDtypeStruct((n, n_kv), F32), jax.ShapeDtypeStruct((n, n_kv), F32),
                   jax.ShapeDtypeStruct((n, IDX_DIM), F32), jax.ShapeDtypeStruct((n, n_kv), BF16),
                   jax.ShapeDtypeStruct((n, IDX_DIM), BF16), feat_shape(n_q, BF16), feat_shape(n_iq, BF16),
                   feat_shape(n_kv, BF16), feat_shape(IDX_HEADS, F32)],
        compiler_params=_cparams(("parallel",)),
        name="attn_proj",
    )(x, g, wrow, wt)


def _bias_kernel(tab_ref, tb_ref):
    h = pl.program_id(0)
    d = pl.program_id(1)
    kj = lax.broadcasted_iota(I32, (QB, QB), 0)
    qi = lax.broadcasted_iota(I32, (QB, QB), 1)
    rel = (d - 1) * QB + kj - qi
    n = jnp.abs(rel)
    large = jnp.full((QB, QB), N_BUCKETS // 4, I32)
    for s in BUCKET_STEPS:
        large = large + jnp.where(n >= s, 1, 0)
    bucket = jnp.where(rel > 0, N_BUCKETS // 2, 0) + jnp.where(n < N_BUCKETS // 4, n, large)
    val = jnp.zeros((QB, QB), F32)
    for b in range(N_BUCKETS):
        val = jnp.where(bucket == b, tab_ref[b, h], val)
    tb_ref[0] = val - tab_ref[FAR_BUCKET, h]


def _bias_call(table):
    return pl.pallas_call(
        _bias_kernel,
        grid=(N_HEADS, 2),
        in_specs=[pl.BlockSpec(memory_space=pltpu.SMEM)],
        out_specs=pl.BlockSpec((1, QB, QB), lambda h, d: (h, d, 0)),
        out_shape=jax.ShapeDtypeStruct((N_HEADS, 2 * QB, QB), F32),
        compiler_params=_cparams(("arbitrary", "arbitrary")),
        name="rel_bias_tiles",
    )(table)


def _attn_kernel(*refs, blocks_per_step, **static):
    def one_block(sub, carry):
        _attn_block(sub, blocks_per_step, *refs, **static)
        return carry

    lax.fori_loop(0, blocks_per_step, one_block, 0)


def _attn_block(sub, blocks_per_step, kend_ref, qt_ref, qit_ref, wit_ref, *refs, diag_off, kstart, k_sel, has_prev,
                row_bits, build):
    block = pl.program_id(1) * blocks_per_step + sub
    if build:
        (kb_ref, vt_ref, kib_ref, k0_ref, v0_ref, ki0_ref, tb_ref, o_ref, key_scr, hi_scr, lo_scr, lo2_scr,
         ls_scr, oacc_scr, kf_scr, vtf_scr, kif_scr) = refs
        seq = kb_ref.shape[0]
        tkf = kf_scr.shape[1]

        @pl.when(block == 0)
        def _():
            kif_scr[0:QB, :] = ki0_ref[...]
            kif_scr[QB:QB + seq, :] = kib_ref[...]
            kif_scr[QB + seq:tkf, :] = jnp.zeros((tkf - QB - seq, IDX_DIM), BF16)
            for g in range(N_KV_HEADS):
                cols = slice(g * HEAD_DIM, (g + 1) * HEAD_DIM)
                kf_scr[g, 0:QB, :] = k0_ref[:, cols]
                for c in range(seq // SEL_TILE):
                    kf_scr[g, QB + c * SEL_TILE:QB + (c + 1) * SEL_TILE, :] = kb_ref[c * SEL_TILE:(c + 1) * SEL_TILE,
                                                                                     cols]
                kf_scr[g, QB + seq:tkf, :] = jnp.zeros((tkf - QB - seq, HEAD_DIM), BF16)
                vrows = slice(g * V_ROWS, g * V_ROWS + HEAD_DIM)
                vtf_scr[vrows, 0:QB] = v0_ref[cols, :]
                for j in range(seq // QB):
                    vtf_scr[vrows, QB + j * QB:QB + (j + 1) * QB] = vt_ref[j, cols, :]
                vtf_scr[vrows, QB + seq:tkf] = jnp.zeros((HEAD_DIM, tkf - QB - seq), BF16)
                extra = lax.broadcasted_iota(I32, (V_ROWS - HEAD_DIM, tkf), 0) == 0
                vtf_scr[g * V_ROWS + HEAD_DIM:(g + 1) * V_ROWS, :] = jnp.where(extra, 1.0, 0.0).astype(BF16)

        k_rows = lambda g, r0, rows: kf_scr[g, pl.ds(r0, rows), :]
        v_cols = lambda g, r0, rows: vtf_scr[g * V_ROWS:(g + 1) * V_ROWS, pl.ds(r0, rows)]
        ki_rows = lambda r0, rows: kif_scr[pl.ds(r0, rows), :]
    else:
        (kf_ref, vtf_ref, kif_ref, tb_ref, o_ref, key_scr, hi_scr, lo_scr, lo2_scr, ls_scr,
         oacc_scr) = refs
        k_rows = lambda g, r0, rows: kf_ref[0, g, pl.ds(r0, rows), :]
        v_cols = lambda g, r0, rows: vtf_ref[0, g * V_ROWS:(g + 1) * V_ROWS, pl.ds(r0, rows)]
        ki_rows = lambda r0, rows: kif_ref[0, pl.ds(r0, rows), :]
    diag = block + diag_off
    nt = diag + 1
    nst = (nt + SEL_TILE // QB - 1) // (SEL_TILE // QB)
    kend = kend_ref[sub]
    qi_cat = jnp.concatenate([qit_ref[sub, h * IDX_DIM:(h + 1) * IDX_DIM, :] for h in range(IDX_HEADS)], axis=1)
    wi = wit_ref[sub]
    sel_iota = lax.broadcasted_iota(I32, (SEL_TILE, QB), 0)

    def tile_start(t):
        return pl.multiple_of(t * QB, QB)

    def sel_start(t):
        return pl.multiple_of(t * SEL_TILE, SEL_TILE)

    def score_rows(r0):
        s = _dot(ki_rows(r0, SEL_TILE), qi_cat)
        sc = jnp.zeros((SEL_TILE, QB), F32)
        for h in range(IDX_HEADS):
            sc = sc + wi[h:h + 1, :] * jnp.maximum(s[:, h * QB:(h + 1) * QB], 0.0)
        bits = lax.bitcast_convert_type(sc + 0.0, I32)
        key = jnp.where(bits >= 0, bits, bits ^ 0x7FFFFFFF)
        rows = r0 + sel_iota
        adm = (rows >= kstart) & (rows < kend)
        key = jnp.where(adm, key, INT_MIN)
        key_scr[pl.ds(r0, SEL_TILE), :] = key
        hi_scr[pl.ds(r0, SEL_TILE), :] = (key >> 16).astype(I16)
        lo_scr[pl.ds(r0, SEL_TILE), :] = ((key & 0xFFFF) + I16_MIN).astype(I16)

    pair_rows = 2 * SEL_TILE

    def score_pair(t, carry):
        r0 = pl.multiple_of(t * pair_rows, pair_rows)
        score_rows(r0)
        score_rows(r0 + SEL_TILE)
        return carry

    lax.fori_loop(0, nst // 2, score_pair, 0)

    @pl.when(nst % 2 == 1)
    def _():
        score_rows(sel_start(nst - 1))
        blank = jnp.full((SEL_TILE, QB), I16_MIN, I16)
        hi_scr[pl.ds(sel_start(nst), SEL_TILE), :] = blank
        lo_scr[pl.ds(sel_start(nst), SEL_TILE), :] = blank

    pair_vregs = pair_rows // PACK16

    def as_packed(v):
        return jnp.broadcast_to(v, (PACK16, QB)).astype(I16)

    def threshold(n_pair):
        def count16(src_scr, cand):
            c16 = as_packed(cand)[None]
            acc = jnp.zeros((PACK16, QB), BF16)
            for t in range(n_pair):
                v = src_scr[t * pair_rows:(t + 1) * pair_rows, :].reshape(pair_vregs, PACK16, QB)
                ind = jnp.where(v >= c16, jnp.ones((), BF16), jnp.zeros((), BF16))
                parts = [ind[i] for i in range(pair_vregs)]
                while len(parts) > 1:
                    parts = [parts[i] + parts[i + 1] for i in range(0, len(parts), 2)]
                acc = acc + parts[0]
            return acc.astype(F32).sum(axis=0, keepdims=True)

        def search16(src_scr, base, c_start):
            def step(b, carry):
                t_acc, c_acc = carry
                cand = t_acc + lax.shift_left(jnp.int32(1), 15 - b)
                c = base + count16(src_scr, cand)
                ok = c >= k_sel
                return jnp.where(ok, cand, t_acc), jnp.where(ok, c, c_acc)

            return lax.fori_loop(0, 16, step, (jnp.full((1, QB), I16_MIN, I32), c_start))

        thr_hi, c_hi = search16(hi_scr, 0.0, jnp.full((1, QB), n_pair * pair_rows, F32))
        above = jnp.where(thr_hi == -I16_MIN - 1, 0.0, count16(hi_scr, thr_hi + 1))
        hi16 = as_packed(thr_hi)[None]
        for t in range(n_pair):
            rows = slice(t * pair_rows, (t + 1) * pair_rows)
            hi = hi_scr[rows, :].reshape(pair_vregs, PACK16, QB)
            lo = lo_scr[rows, :].reshape(pair_vregs, PACK16, QB)
            lo2_scr[rows, :] = jnp.where(hi == hi16, lo, jnp.full((), I16_MIN, I16)).reshape(pair_rows, QB)
        thr_lo, c_ge = search16(lo2_scr, above, c_hi)
        return thr_hi * 65536 + (thr_lo - I16_MIN), c_ge

    def count(pred):
        def body(t, acc):
            r0 = sel_start(t)
            ind = jnp.where(pred(key_scr[pl.ds(r0, SEL_TILE), :], r0 + sel_iota), 1, 0)
            return acc + ind.reshape(SEL_TILE // SUBLANES, SUBLANES, QB).sum(axis=0)
        acc = lax.fori_loop(0, nst, body, jnp.zeros((SUBLANES, QB), I32))
        return acc.sum(axis=0, keepdims=True)

    max_pairs = hi_scr.shape[0] // pair_rows
    thr, c_ge = lax.switch((nst + 1) // 2 - 1, [functools.partial(threshold, n) for n in range(1, max_pairs + 1)])

    tied = (c_ge > k_sel) & (thr > INT_MIN)
    big = jnp.full((1, QB), 2 ** row_bits, I32)

    def tie_limit():
        need = k_sel - count(lambda kt, rows: kt > thr)

        def lim_step(b, lim):
            cand = lim + lax.shift_left(jnp.int32(1), row_bits - 1 - b)
            c = count(lambda kt, rows: (kt == thr) & (rows < cand))
            return jnp.where(c < need, cand, lim)

        lim = lax.fori_loop(0, row_bits, lim_step, jnp.zeros((1, QB), I32))
        return jnp.where(tied, lim, big)

    rlim = lax.cond(jnp.max(tied.astype(I32)) > 0, tie_limit, lambda: big)
    rlim = jnp.where(thr > INT_MIN, rlim, -1)

    gq = GROUP * QB
    qg = [jnp.concatenate([qt_ref[sub, (g * GROUP + r) * HEAD_DIM:(g * GROUP + r + 1) * HEAD_DIM, :]
                           for r in range(GROUP)], axis=1) for g in range(N_KV_HEADS)]

    def logits_rows(r0, rows, m8, bias_of_head):
        kt = key_scr[pl.ds(r0, rows), :]
        keep = (kt > thr) | ((kt == thr) & (sel_iota[:rows] <= rlim - r0))
        new_m8 = []
        for g in range(N_KV_HEADS):
            lg = _dot(k_rows(g, r0, rows), qg[g])
            parts = []
            for r in range(GROUP):
                part = lg[:, r * QB:(r + 1) * QB]
                if bias_of_head is not None:
                    part = part + bias_of_head(g * GROUP + r)
                parts.append(jnp.where(keep, part, NEG_INF))
            lg = jnp.concatenate(parts, axis=1)
            ls_scr[pl.ds(r0, rows), g * gq:(g + 1) * gq] = lg
            new_m8.append(jnp.maximum(m8[g], lg.reshape(rows // SUBLANES, SUBLANES, gq).max(axis=0)))
        return tuple(new_m8)

    far_bias = None
    m8 = tuple(jnp.full((SUBLANES, gq), NEG_INF, F32) for _ in range(N_KV_HEADS))
    n_far = diag - 1 if has_prev else diag

    def far_pair(t, m8):
        r0 = pl.multiple_of(t * pair_rows, pair_rows)
        return logits_rows(r0 + SEL_TILE, SEL_TILE, logits_rows(r0, SEL_TILE, m8, far_bias), far_bias)

    m8 = lax.fori_loop(0, n_far // 4, far_pair, m8)
    m8 = lax.cond(n_far % 4 >= 2, lambda m: logits_rows(sel_start(n_far // 4 * 2), SEL_TILE, m, far_bias),
                  lambda m: m, m8)
    m8 = lax.cond(n_far % 2 == 1, lambda m: logits_rows(tile_start(n_far - 1), QB, m, far_bias), lambda m: m, m8)
    if has_prev:
        m8 = logits_rows(tile_start(diag - 1), 2 * QB, m8, lambda h: tb_ref[h])
    else:
        m8 = logits_rows(tile_start(diag), QB, m8, lambda h: tb_ref[h, QB:, :])
    m = [jnp.max(m8[g], axis=0, keepdims=True) for g in range(N_KV_HEADS)]

    @pl.when(nt % 2 == 1)
    def _():
        ls_scr[pl.ds(tile_start(nt), QB), :] = jnp.full((QB, N_HEADS * QB), NEG_INF, F32)

    oacc_scr[...] = jnp.zeros(oacc_scr.shape, F32)

    def pv_rows(r0):
        for g in range(N_KV_HEADS):
            p = jnp.exp(ls_scr[pl.ds(r0, SEL_TILE), g * gq:(g + 1) * gq] - m[g])
            oacc_scr[g] += _dot(v_cols(g, r0, SEL_TILE), p.astype(BF16))

    def pv_pair(t, carry):
        r0 = pl.multiple_of(t * pair_rows, pair_rows)
        pv_rows(r0)
        pv_rows(r0 + SEL_TILE)
        return carry

    lax.fori_loop(0, nst // 2, pv_pair, 0)

    @pl.when(nst % 2 == 1)
    def _():
        pv_rows(sel_start(nst - 1))
    out_rows = []
    for g in range(N_KV_HEADS):
        o_g = oacc_scr[g, :HEAD_DIM] / oacc_scr[g, HEAD_DIM:HEAD_DIM + 1]
        out_rows.extend(o_g[:, r * QB:(r + 1) * QB] for r in range(GROUP))
    o_ref[pl.ds(pl.multiple_of(sub * QB, QB), QB), :] = jnp.concatenate(out_rows, axis=0).T.astype(BF16)


def _attn_call(kend, qt, qit, wit, frames, tb, *, n_seq, build, n_out_rows, diag_off, kstart, k_sel, name):
    n_qb = kend.shape[0]
    kv_dim = N_KV_HEADS * HEAD_DIM
    if build:
        seq = frames[0].shape[0] // n_seq
        tkf = -(-(QB + seq) // (2 * SEL_TILE)) * 2 * SEL_TILE
        frame_specs = [pl.BlockSpec((seq, kv_dim), lambda s, i: (s, 0)),
                       pl.BlockSpec((seq // QB, kv_dim, QB), lambda s, i: (s, 0, 0)),
                       pl.BlockSpec((seq, IDX_DIM), lambda s, i: (s, 0))] + [_const_spec(a.shape) for a in frames[3:]]
        frame_scratch = [pltpu.VMEM((N_KV_HEADS, tkf, HEAD_DIM), BF16), pltpu.VMEM((N_KV_HEADS * V_ROWS, tkf), BF16),
                         pltpu.VMEM((tkf, IDX_DIM), BF16)]
    else:
        kf, vtf, kif = frames
        tkf = kf.shape[2]
        assert vtf.shape[2] == tkf and kif.shape[1] == tkf
        frame_specs = [pl.BlockSpec((1, N_KV_HEADS, tkf, HEAD_DIM), lambda s, i: (s, 0, 0, 0)),
                       pl.BlockSpec((1, N_KV_HEADS * V_ROWS, tkf), lambda s, i: (s, 0, 0)),
                       pl.BlockSpec((1, tkf, IDX_DIM), lambda s, i: (s, 0, 0))]
        frame_scratch = []
    assert tkf % (2 * SEL_TILE) == 0
    assert tkf // PACK16 <= 256, "per-slot bf16 counts must stay exact"
    has_prev = diag_off >= 1
    row_bits = max(1, (tkf - 1).bit_length())
    bps = ATTN_BLOCKS_PER_STEP if n_qb % ATTN_BLOCKS_PER_STEP == 0 else 1
    steps = n_qb // bps
    feat = lambda a: pl.BlockSpec((bps, a.shape[1], QB), lambda s, i: (s * steps + i, 0, 0))
    return pl.pallas_call(
        functools.partial(_attn_kernel, blocks_per_step=bps, diag_off=diag_off, kstart=kstart, k_sel=k_sel,
                          has_prev=has_prev, row_bits=row_bits, build=build),
        grid=(n_seq, steps),
        in_specs=[pl.BlockSpec((bps, 1, QB), lambda s, i: (i, 0, 0)), feat(qt), feat(qit), feat(wit)] + frame_specs
                 + [_const_spec(tb.shape)],
        out_specs=pl.BlockSpec((bps * QB, N_HEADS * HEAD_DIM), lambda s, i: (s * steps + i, 0)),
        out_shape=jax.ShapeDtypeStruct((n_out_rows, N_HEADS * HEAD_DIM), BF16),
        scratch_shapes=[pltpu.VMEM((tkf, QB), I32)] + [pltpu.VMEM((tkf, QB), I16)] * 3
                       + [pltpu.VMEM((tkf, N_HEADS * QB), F32), pltpu.VMEM((N_KV_HEADS, V_ROWS, GROUP * QB), F32)]
                       + frame_scratch,
        compiler_params=_cparams(("arbitrary", "arbitrary")),
        name=name,
    )(kend, qt, qit, wit, *frames, tb)


def _sigmoid(x):
    return 0.5 * jnp.tanh(0.5 * x) + 0.5


def _softplus(x):
    return jnp.maximum(x, 0.0) + jnp.log1p(jnp.exp(-jnp.abs(x)))


def _mixb_kernel(x_ref, h0_ref, buf_ref, g_ref, wy_ref, by_ref, wx_ref, bx_ref, cw_ref, cb_ref, wr_ref, br_ref,
                 wi_ref, bi_ref, lam_ref, *rest, tm):
    pre_ref, hlast_ref, tail_ref, xe_scr, h_scr = rest[-5:]

    @pl.when(pl.program_id(1) == 0)
    def _():
        xe_scr[0:SUBLANES, :] = buf_ref[0]
        h_scr[...] = h0_ref[0]

    xn = _rmsnorm(x_ref[...], g_ref[...]).astype(BF16)
    y = jax.nn.gelu(_dot(xn, wy_ref[...]) + by_ref[...])
    xe_scr[SUBLANES:SUBLANES + tm, :] = _dot(xn, wx_ref[...]) + bx_ref[...]
    xc = cb_ref[...] + cw_ref[0:1, :] * xe_scr[SUBLANES - 3:SUBLANES - 3 + tm, :]
    for j in range(1, CONV_B):
        xc = xc + cw_ref[j:j + 1, :] * xe_scr[SUBLANES - 3 + j:SUBLANES - 3 + j + tm, :]
    xcb = xc.astype(BF16)
    d_rnn = xc.shape[1]
    blk = d_rnn // LRU_BLOCKS
    r_pre = jnp.concatenate([_dot(xcb[:, n * blk:(n + 1) * blk], wr_ref[n]) for n in range(LRU_BLOCKS)], axis=1)
    i_pre = jnp.concatenate([_dot(xcb[:, n * blk:(n + 1) * blk], wi_ref[n]) for n in range(LRU_BLOCKS)], axis=1)
    r = _sigmoid(r_pre + br_ref[...])
    ig = _sigmoid(i_pre + bi_ref[...])
    log_a = -LRU_C * r * _softplus(-lam_ref[...])
    a = jnp.exp(log_a)
    u = jnp.sqrt(jnp.tanh(-log_a) * (1.0 + a * a)) * (ig * xc)
    a = a.reshape(tm // SUBLANES, SUBLANES, d_rnn)
    u = u.reshape(tm // SUBLANES, SUBLANES, d_rnn)
    row_in_group = lax.broadcasted_iota(I32, a.shape, 1)
    s = 1
    while s < SUBLANES:
        a_sh = jnp.where(row_in_group >= s, pltpu.roll(a, s, 1), 1.0)
        u_sh = jnp.where(row_in_group >= s, pltpu.roll(u, s, 1), 0.0)
        u = a * u_sh + u
        a = a * a_sh
        s *= 2
    carry = h_scr[...]
    h_groups = []
    for grp in range(tm // SUBLANES):
        h_grp = a[grp] * carry + u[grp]
        carry = h_grp[SUBLANES - 1:SUBLANES, :]
        h_groups.append(h_grp)
    h = jnp.concatenate(h_groups, axis=0)
    pre_ref[...] = (h * y).astype(BF16)
    h_scr[...] = carry
    hlast_ref[0] = carry
    tail = xe_scr[tm:tm + SUBLANES, :]
    xe_scr[0:SUBLANES, :] = tail
    tail_ref[0] = tail


def _mixc_kernel(x_ref, buf_ref, g_ref, win_ref, cw_ref, *rest, tm):
    pre_ref, tail_ref, pe_scr = rest[-3:]

    @pl.when(pl.program_id(1) == 0)
    def _():
        pe_scr[0:SUBLANES, :] = buf_ref[0]

    d = x_ref.shape[1]
    xn = _rmsnorm(x_ref[...], g_ref[...]).astype(BF16)
    z = _dot(xn, win_ref[...])
    pe_scr[SUBLANES:SUBLANES + tm, :] = z[:, d:2 * d] * z[:, 2 * d:]
    conv = cw_ref[0:1, :] * pe_scr[SUBLANES - 2:SUBLANES - 2 + tm, :]
    for j in range(1, CONV_C):
        conv = conv + cw_ref[j:j + 1, :] * pe_scr[SUBLANES - 2 + j:SUBLANES - 2 + j + tm, :]
    pre_ref[...] = (z[:, :d] * conv).astype(BF16)
    tail = pe_scr[tm:tm + SUBLANES, :]
    pe_scr[0:SUBLANES, :] = tail
    tail_ref[0] = tail


def _shared_out(buf, index):
    if buf is None:
        return {"specs": [], "args": (), "aliases": {}}
    return {"specs": [pl.BlockSpec(memory_space=pl.ANY)], "args": (buf,), "aliases": {index: 0}}


def _seq_specs(n_seq, t_len, row_off, d, tm_max):
    tm = tm_max if t_len % tm_max == 0 else min(TM_SEQ, t_len)
    n_tt = t_len // tm
    off = row_off // tm
    xrow = pl.BlockSpec((tm, d), lambda s, j: (off + s * n_tt + j, 0))
    state = lambda rows: pl.BlockSpec((1, rows, d), lambda s, j: (s, 0, 0))
    return tm, n_tt, xrow, state


def _mixb_call(x, pre_buf, h0, buf, g, wy, by, wx, bx, cw, cb, wr, br, wi, bi, lam, *, n_seq, t_len, row_off,
               name):
    d = x.shape[1]
    tm, n_tt, xrow, state = _seq_specs(n_seq, t_len, row_off, d, TM_SEQ)
    consts = (g, wy, by, wx, bx, cw, cb, wr, br, wi, bi, lam)
    shared = _shared_out(pre_buf, 3 + len(consts))
    return pl.pallas_call(
        functools.partial(_mixb_kernel, tm=tm),
        grid=(n_seq, n_tt),
        in_specs=[xrow, state(1), state(SUBLANES)] + [_const_spec(c.shape) for c in consts] + shared["specs"],
        out_specs=[xrow, state(1), state(SUBLANES)],
        out_shape=[jax.ShapeDtypeStruct((x.shape[0], d), BF16), jax.ShapeDtypeStruct((n_seq, 1, d), F32),
                   jax.ShapeDtypeStruct((n_seq, SUBLANES, d), F32)],
        scratch_shapes=[pltpu.VMEM((tm + SUBLANES, d), F32), pltpu.VMEM((1, d), F32)],
        input_output_aliases=shared["aliases"],
        compiler_params=_cparams(("arbitrary", "arbitrary")),
        name=name,
    )(x, h0, buf, *consts, *shared["args"])


def _mixc_call(x, pre_buf, buf, g, win, cw, *, n_seq, t_len, row_off, name):
    d = x.shape[1]
    tm, n_tt, xrow, state = _seq_specs(n_seq, t_len, row_off, d, TM_CONV)
    consts = (g, win, cw)
    shared = _shared_out(pre_buf, 2 + len(consts))
    return pl.pallas_call(
        functools.partial(_mixc_kernel, tm=tm),
        grid=(n_seq, n_tt),
        in_specs=[xrow, state(SUBLANES)] + [_const_spec(c.shape) for c in consts] + shared["specs"],
        out_specs=[xrow, state(SUBLANES)],
        out_shape=[jax.ShapeDtypeStruct((x.shape[0], d), BF16), jax.ShapeDtypeStruct((n_seq, SUBLANES, d), F32)],
        scratch_shapes=[pltpu.VMEM((tm + SUBLANES, d), F32)],
        input_output_aliases=shared["aliases"],
        compiler_params=_cparams(("arbitrary", "arbitrary")),
        name=name,
    )(x, buf, *consts, *shared["args"])


def _front_pad_rows(a, rows):
    return jnp.pad(a, ((0, 0), (rows - a.shape[1], 0), (0, 0)))


def _frame(parts, axis):
    f = jnp.concatenate(parts, axis=axis).astype(BF16)
    pad = [(0, 0)] * f.ndim
    pad[axis] = (0, -f.shape[axis] % (2 * SEL_TILE))
    return jnp.pad(f, pad)


def _k_frame(parts):
    f = _frame(parts, 1)
    n_seq, rows = f.shape[0], f.shape[1]
    return f.reshape(n_seq, rows, N_KV_HEADS, HEAD_DIM).transpose(0, 2, 1, 3)


def _v_frame(parts):
    f = _frame(parts, 2)
    n_seq, tkf = f.shape[0], f.shape[2]
    extra = jnp.zeros((n_seq, N_KV_HEADS, V_ROWS - HEAD_DIM, tkf), BF16).at[:, :, 0].set(1)
    f = jnp.concatenate([f.reshape(n_seq, N_KV_HEADS, HEAD_DIM, tkf), extra], axis=2)
    return f.reshape(n_seq, N_KV_HEADS * V_ROWS, tkf)


def _cols(a, start, n):
    b0, b1 = start // QB, -(-(start + n) // QB)
    c = a[b0:b1].transpose(1, 0, 2).reshape(a.shape[1], (b1 - b0) * QB)
    return c[:, start - b0 * QB:start - b0 * QB + n]


def _seq_major(a, n_seq, q_len):
    return a.reshape(a.shape[0], n_seq, q_len).transpose(1, 0, 2)


def _lane_pad_cols(a, n_seq, q_len):
    return jnp.pad(_seq_major(a, n_seq, q_len), ((0, 0), (0, 0), (0, QB - q_len)))


def kernel(x_prompt, x_sample, cache_k, cache_v, cache_kidx, state_h, state_conv_b, state_conv_c, meta_tokens,
           rel_bias, norm_mix, norm_ffn, norm_final, a_wq, a_wk, a_wv, a_wo, a_wiq, a_wik, a_wiw, b_wy, b_by, b_wx,
           b_bx, b_conv_w, b_conv_b, b_wr, b_br, b_wi, b_bi, b_lam, b_wo, b_bo, c_win, c_conv_w, c_wo, f_wg, f_wu,
           f_wd):
    bp, seq, d = x_prompt.shape
    bs, dec = x_sample.shape[0], x_sample.shape[1]
    past = cache_k.shape[2]
    n_meta = meta_tokens.shape[0]
    depth = norm_mix.shape[0]
    assert seq % QB == 0 and seq % TM_SEQ == 0 and past % QB == 0 and dec <= QB and n_meta <= QB
    assert dec % PACK16 == 0 and n_meta % PACK16 == 0 and (bs * dec) % n_meta == 0

    n_main, n_samp = bp * seq, bs * dec
    n_tail = -(-(n_samp + n_meta) // TM_TAIL) * TM_TAIL
    x_main = x_prompt.reshape(n_main, d)
    x_tail = jnp.concatenate([x_sample.reshape(n_samp, d), meta_tokens.astype(F32),
                              jnp.zeros((n_tail - n_samp - n_meta, d), F32)], axis=0)
    tail_fill = jnp.zeros((n_tail - n_samp - n_meta, d), BF16)

    k_sel_p = min(TOP_K_MAX, seq // 4)
    k_sel_s = min(TOP_K_MAX, (past + dec) // 4)
    n_qb = seq // QB
    kv_dim = N_KV_HEADS * HEAD_DIM

    tau = jnp.arange(seq, dtype=I32)
    kend_main = (QB + CHUNK * (tau // CHUNK + 1)).reshape(n_qb, 1, QB)
    lane = jnp.arange(QB, dtype=I32)
    kend_samp = jnp.minimum(past + dec, CHUNK * ((past + jnp.minimum(lane, dec - 1)) // CHUNK + 1)).reshape(1, 1, QB)
    kend_meta = jnp.full((1, 1, QB), n_meta, I32)

    tb = _bias_call(rel_bias.astype(F32))
    zeros_bias = jnp.zeros((1, d), F32)
    gfin = norm_final.reshape(1, d).astype(F32)

    new = {name: [] for name in ("k_p", "v_p", "ki_p", "h_p", "cb_p", "cc_p", "k_s", "v_s", "ki_s", "h_s", "cb_s",
                                 "cc_s")}
    for l in range(depth):
        s = l // N_MIXERS
        g_mix = norm_mix[l].reshape(1, d)
        if l % N_MIXERS == 0:
            wrow = jnp.concatenate([a_wk[s], a_wv[s], a_wik[s]], axis=1).astype(BF16)
            wt = jnp.concatenate([a_wq[s].T * HEAD_DIM ** -0.5, a_wiq[s].T * IDX_DIM ** -0.5, a_wv[s].T, a_wiw[s].T,
                                  jnp.zeros((SUBLANES, d), F32)], axis=0).astype(BF16)
            k, v, ki, kb, kib, qt, qit, vt, wit = _proj_call(x_main, g_mix, wrow, wt)
            k_t, v_t, ki_t, _, _, qt_t, qit_t, vt_t, wit_t = _proj_call(x_tail, g_mix, wrow, wt)

            k_meta, v_meta, ki_meta = (a[n_samp:n_samp + n_meta] for a in (k_t, v_t, ki_t))
            k_main, v_main, ki_main = (a.reshape(bp, seq, -1) for a in (k, v, ki))
            k_samp, v_samp, ki_samp = (a[:n_samp].reshape(bs, dec, -1) for a in (k_t, v_t, ki_t))
            bc = lambda a: jnp.broadcast_to(a[None], (bp,) + a.shape)
            new["k_p"].append(jnp.concatenate([bc(k_meta), k_main], axis=1).reshape(bp, n_meta + seq, N_KV_HEADS,
                                                                                   HEAD_DIM))
            new["v_p"].append(jnp.concatenate([bc(v_meta), v_main], axis=1).reshape(bp, n_meta + seq, N_KV_HEADS,
                                                                                   HEAD_DIM))
            new["ki_p"].append(jnp.concatenate([bc(ki_meta), ki_main], axis=1))
            new["k_s"].append(k_samp.reshape(bs, dec, N_KV_HEADS, HEAD_DIM))
            new["v_s"].append(v_samp.reshape(bs, dec, N_KV_HEADS, HEAD_DIM))
            new["ki_s"].append(ki_samp)

            samp_cols = lambda a: _cols(a, 0, n_samp)
            meta_cols = lambda a: _cols(a, n_samp, n_meta)
            vt_meta = meta_cols(vt_t)
            lead = lambda a, axis: jnp.pad(a, [(QB - n_meta, 0) if ax == axis else (0, 0)
                                               for ax in range(2)]).astype(BF16)
            pre_main = _attn_call(
                kend_main, qt, qit, wit, (kb, vt, kib, lead(k_meta, 0), lead(vt_meta, 1), lead(ki_meta, 0)), tb,
                n_seq=bp, build=True,
                n_out_rows=n_main, diag_off=1, kstart=QB - n_meta, k_sel=k_sel_p, name="dsa_attn_prompt")
            vtf_samp = _v_frame([cache_v[s].reshape(bs, past, kv_dim).transpose(0, 2, 1),
                                 _seq_major(samp_cols(vt_t), bs, dec)])
            o_samp = _attn_call(
                kend_samp, _lane_pad_cols(samp_cols(qt_t), bs, dec),
                _lane_pad_cols(samp_cols(qit_t), bs, dec), _lane_pad_cols(samp_cols(wit_t), bs, dec),
                (_k_frame([cache_k[s].reshape(bs, past, kv_dim), k_samp]), vtf_samp,
                 _frame([cache_kidx[s], ki_samp], 1)), tb, n_seq=bs, build=False,
                n_out_rows=bs * QB, diag_off=past // QB, kstart=0, k_sel=k_sel_s, name="dsa_attn_sample")
            o_meta = _attn_call(
                kend_meta, _lane_pad_cols(meta_cols(qt_t), 1, n_meta),
                _lane_pad_cols(meta_cols(qit_t), 1, n_meta), _lane_pad_cols(meta_cols(wit_t), 1, n_meta),
                (_k_frame([k_meta[None]]), _v_frame([vt_meta[None]]), _frame([ki_meta[None]], 1)), tb,
                n_seq=1, build=False, n_out_rows=QB, diag_off=0, kstart=0, k_sel=k_sel_p, name="dsa_attn_meta")
            pre_tail = jnp.concatenate([o_samp.reshape(bs, QB, d)[:, :dec].reshape(n_samp, d), o_meta[:n_meta],
                                        tail_fill], axis=0)
            wo, bo = a_wo[s].astype(BF16), zeros_bias
        elif l % N_MIXERS == 1:
            consts = (g_mix, b_wy[s].astype(BF16), b_by[s].reshape(1, d), b_wx[s].astype(BF16), b_bx[s].reshape(1, d),
                      jnp.pad(b_conv_w[s], ((0, SUBLANES - CONV_B), (0, 0))), b_conv_b[s].reshape(1, d),
                      b_wr[s].astype(BF16), b_br[s].reshape(1, d), b_wi[s].astype(BF16), b_bi[s].reshape(1, d),
                      b_lam[s].reshape(1, d))
            pre_tail = jnp.zeros((n_tail, d), BF16)
            pre_tail, h_m, tail_m = _mixb_call(x_tail, pre_tail, jnp.zeros((1, 1, d), F32),
                                               jnp.zeros((1, SUBLANES, d), F32), *consts, n_seq=1, t_len=n_meta,
                                               row_off=n_samp, name="rglru_meta")
            pre_tail, h_s, tail_s = _mixb_call(x_tail, pre_tail, state_h[s].reshape(bs, 1, d),
                                               _front_pad_rows(state_conv_b[s], SUBLANES), *consts, n_seq=bs,
                                               t_len=dec, row_off=0, name="rglru_sample")
            pre_main, h_p, tail_p = _mixb_call(x_main, None, jnp.broadcast_to(h_m, (bp, 1, d)),
                                               jnp.broadcast_to(tail_m, (bp, SUBLANES, d)), *consts, n_seq=bp,
                                               t_len=seq, row_off=0, name="rglru_prompt")
            new["h_p"].append(h_p.reshape(bp, d))
            new["cb_p"].append(tail_p[:, SUBLANES - (CONV_B - 1):])
            new["h_s"].append(h_s.reshape(bs, d))
            new["cb_s"].append(tail_s[:, SUBLANES - (CONV_B - 1):])
            wo, bo = b_wo[s].astype(BF16), b_bo[s].reshape(1, d)
        else:
            consts = (g_mix, c_win[s].astype(BF16), jnp.pad(c_conv_w[s], ((0, SUBLANES - CONV_C), (0, 0))))
            pre_tail = jnp.zeros((n_tail, d), BF16)
            pre_tail, tail_m = _mixc_call(x_tail, pre_tail, jnp.zeros((1, SUBLANES, d), F32), *consts, n_seq=1,
                                          t_len=n_meta, row_off=n_samp, name="sconv_meta")
            pre_tail, tail_s = _mixc_call(x_tail, pre_tail, _front_pad_rows(state_conv_c[s], SUBLANES), *consts,
                                          n_seq=bs, t_len=dec, row_off=0, name="sconv_sample")
            pre_main, tail_p = _mixc_call(x_main, None, jnp.broadcast_to(tail_m, (bp, SUBLANES, d)), *consts,
                                          n_seq=bp, t_len=seq, row_off=0, name="sconv_prompt")
            new["cc_p"].append(tail_p[:, SUBLANES - (CONV_C - 1):])
            new["cc_s"].append(tail_s[:, SUBLANES - (CONV_C - 1):])
            wo, bo = c_wo[s].astype(BF16), zeros_bias
        ffn = (wo, bo, norm_ffn[l].reshape(1, d), f_wg[l].astype(BF16), f_wu[l].astype(BF16), f_wd[l].astype(BF16),
               gfin, l == depth - 1)
        x_main = _ffn_call(x_main, pre_main, *ffn)
        x_tail = _ffn_call(x_tail, pre_tail, *ffn)

    y_prompt = x_main.reshape(bp, seq, d)
    y_sample = x_tail[:n_samp].reshape(bs, dec, d)
    st = lambda name: jnp.stack(new[name])
    return (y_prompt, y_sample, st("k_p"), st("v_p"), st("ki_p"), st("h_p"), st("cb_p"), st("cc_p"),
            st("k_s"), st("v_s"), st("ki_s"), st("h_s"), st("cb_s"), st("cc_s"))
```
